```python
import math
import jax
import jax.numpy as jnp
from jax import lax
import numpy as np

D_MODEL = 1024
BATCH = 1
SEQ = 16384
DEPTH = 1

GRID_W = 64
CTX_LEN = 256
MIX_W = D_MODEL
CONV_W = MIX_W // 2
CONV_HEADS = 8
SSM_W = MIX_W - CONV_W
SSM_GROUP = 16
SSM_GROUPS = SSM_W // SSM_GROUP
SSM_STATE = 64
IN_W = 3 * CONV_W + SSM_W
D_FF = 4 * D_MODEL
N_MOD = 6
RMS_EPS = 1e-6
DT_MIN = 1e-3
DT_MAX = 1e-1

kernel_name = 'hybrid_conv_s5_prefix_block'


def rms_norm(x, g):
    xf = x.astype(jnp.float32)
    y = xf * lax.rsqrt(jnp.mean(xf * xf, axis=-1, keepdims=True) + RMS_EPS)
    return (y * g.astype(jnp.float32)).astype(x.dtype)


def head_rms_norm(y, n_heads, g):
    shp = y.shape
    yh = y.reshape(shp[:-1] + (n_heads, shp[-1] // n_heads)).astype(jnp.float32)
    yh = yh * lax.rsqrt(jnp.mean(yh * yh, axis=-1, keepdims=True) + RMS_EPS)
    return (yh.reshape(shp) * g.astype(jnp.float32)).astype(y.dtype)


def short_conv(z, w):
    pad = [(0, 0)] * (z.ndim - 2) + [(1, 1), (0, 0)]
    zp = jnp.pad(z, pad)
    n = z.shape[-2]
    return w[0] * zp[..., 0:n, :] + w[1] * zp[..., 1:n + 1, :] + w[2] * zp[..., 2:n + 2, :]


def zoh(a_re, a_im, log_dt, b_re, b_im):
    dt = jnp.exp(log_dt)[:, None]
    lr = jnp.minimum(a_re, -1e-4)
    li = a_im
    mag = jnp.exp(lr * dt)
    ab_re = mag * jnp.cos(li * dt)
    ab_im = mag * jnp.sin(li * dt)
    nr = ab_re - 1.0
    den = lr * lr + li * li
    f_re = ((nr * lr + ab_im * li) / den)[..., None]
    f_im = ((ab_im * lr - nr * li) / den)[..., None]
    bb_re = f_re * b_re - f_im * b_im
    bb_im = f_re * b_im + f_im * b_re
    return ab_re, ab_im, bb_re, bb_im


def complex_affine_combine(left, right):
    a1r, a1i, b1r, b1i = left
    a2r, a2i, b2r, b2i = right
    return (a2r * a1r - a2i * a1i,
            a2r * a1i + a2i * a1r,
            a2r * b1r - a2i * b1i + b2r,
            a2r * b1i + a2i * b1r + b2i)


def ssm_states(u, ab_re, ab_im, bb_re, bb_im, reverse, h0):
    bu_re = jnp.einsum('bngj,gpj->bngp', u, bb_re)
    bu_im = jnp.einsum('bngj,gpj->bngp', u, bb_im)
    if h0 is not None:
        h0_re, h0_im = h0
        first = -1 if reverse else 0
        bu_re = bu_re.at[:, first].add(ab_re * h0_re - ab_im * h0_im)
        bu_im = bu_im.at[:, first].add(ab_re * h0_im + ab_im * h0_re)
    a_re = jnp.broadcast_to(ab_re, bu_re.shape)
    a_im = jnp.broadcast_to(ab_im, bu_im.shape)
    _, _, h_re, h_im = lax.associative_scan(
        complex_affine_combine, (a_re, a_im, bu_re, bu_im), reverse=reverse, axis=1)
    return h_re, h_im


def ssm_readout(h_re, h_im, c_re, c_im):
    return (jnp.einsum('bngp,gjp->bngj', h_re, c_re)
            - jnp.einsum('bngp,gjp->bngj', h_im, c_im))


def mixer_merge(conv_y, ssm_y, u, ssm_d, w_glu, g_conv_out, g_ssm_out, w_out):
    s = jax.nn.gelu(ssm_y + ssm_d * u)
    s = s * jax.nn.sigmoid(s @ w_glu)
    merged = jnp.concatenate([head_rms_norm(conv_y, CONV_HEADS, g_conv_out),
                              head_rms_norm(s, SSM_GROUPS, g_ssm_out)], axis=-1)
    return merged @ w_out


def sq_relu_mlp(h, w1, w2):
    return jnp.square(jax.nn.relu(h @ w1)) @ w2


def setup_inputs(seed: int = 0) -> dict:
    key = jax.random.key(seed)
    ks = jax.random.split(key, 32)
    nrm = jax.random.normal
    f32 = jnp.float32
    G, P, J = SSM_GROUPS, SSM_STATE, SSM_GROUP
    x = nrm(ks[0], (BATCH, SEQ, D_MODEL), f32)
    c = nrm(ks[1], (BATCH, D_MODEL), f32)
    ctx = nrm(ks[2], (BATCH, CTX_LEN, D_MODEL), f32)
    c_ctx = nrm(ks[3], (D_MODEL,), f32)
    w_mod = nrm(ks[4], (DEPTH, D_MODEL, N_MOD * D_MODEL), f32) * (0.5 * D_MODEL ** -0.5)
    b_mod = 0.01 * nrm(ks[5], (DEPTH, N_MOD * D_MODEL), f32)
    g_norm1 = 1.0 + 0.01 * nrm(ks[6], (DEPTH, D_MODEL), f32)
    w_in = nrm(ks[7], (DEPTH, D_MODEL, IN_W), f32) * D_MODEL ** -0.5
    conv_w = nrm(ks[8], (DEPTH, 3, CONV_W), f32) * 3 ** -0.5
    ssm_a_re = -0.5 + 0.01 * nrm(ks[9], (DEPTH, 2, G, P), f32)
    ssm_a_im = jnp.pi * jnp.arange(P, dtype=f32) + 0.01 * nrm(ks[10], (DEPTH, 2, G, P), f32)
    ssm_log_dt = jax.random.uniform(ks[11], (DEPTH, 2, G), f32,
                                    minval=math.log(DT_MIN), maxval=math.log(DT_MAX))
    ssm_b_re = nrm(ks[12], (DEPTH, 2, G, P, J), f32) * (2 * J) ** -0.5
    ssm_b_im = nrm(ks[13], (DEPTH, 2, G, P, J), f32) * (2 * J) ** -0.5
    ssm_c_re = 0.5 * nrm(ks[14], (DEPTH, 2, G, J, P), f32)
    ssm_c_im = 0.5 * nrm(ks[15], (DEPTH, 2, G, J, P), f32)
    ssm_d = nrm(ks[16], (DEPTH, SSM_W), f32)
    w_glu = nrm(ks[17], (DEPTH, SSM_W, SSM_W), f32) * SSM_W ** -0.5
    g_conv_out = 1.0 + 0.01 * nrm(ks[18], (DEPTH, CONV_W), f32)
    g_ssm_out = 1.0 + 0.01 * nrm(ks[19], (DEPTH, SSM_W), f32)
    w_out = nrm(ks[20], (DEPTH, MIX_W, D_MODEL), f32) * MIX_W ** -0.5
    g_norm2 = 1.0 + 0.01 * nrm(ks[21], (DEPTH, D_MODEL), f32)
    w_mlp1 = nrm(ks[22], (DEPTH, D_MODEL, D_FF), f32) * D_MODEL ** -0.5
    w_mlp2 = nrm(ks[23], (DEPTH, D_FF, D_MODEL), f32) * D_FF ** -0.5
    g_final = 1.0 + 0.01 * nrm(ks[24], (D_MODEL,), f32)
    return {'x': x, 'c': c, 'ctx': ctx, 'c_ctx': c_ctx, 'w_mod': w_mod, 'b_mod': b_mod,
            'g_norm1': g_norm1, 'w_in': w_in, 'conv_w': conv_w,
            'ssm_a_re': ssm_a_re, 'ssm_a_im': ssm_a_im, 'ssm_log_dt': ssm_log_dt,
            'ssm_b_re': ssm_b_re, 'ssm_b_im': ssm_b_im, 'ssm_c_re': ssm_c_re, 'ssm_c_im': ssm_c_im,
            'ssm_d': ssm_d, 'w_glu': w_glu, 'g_conv_out': g_conv_out, 'g_ssm_out': g_ssm_out,
            'w_out': w_out, 'g_norm2': g_norm2, 'w_mlp1': w_mlp1, 'w_mlp2': w_mlp2,
            'g_final': g_final}


def reference(x, c, ctx, c_ctx, w_mod, b_mod, g_norm1, w_in, conv_w,
              ssm_a_re, ssm_a_im, ssm_log_dt, ssm_b_re, ssm_b_im, ssm_c_re, ssm_c_im,
              ssm_d, w_glu, g_conv_out, g_ssm_out, w_out, g_norm2, w_mlp1, w_mlp2, g_final):
    bsz, n_lat, _ = x.shape
    rows = n_lat // GRID_W
    n_ctx = ctx.shape[1]
    split_at = [CONV_W, 2 * CONV_W, 3 * CONV_W]
    for layer in range(DEPTH):
        last = layer == DEPTH - 1
        mod = jax.nn.silu(c) @ w_mod[layer] + b_mod[layer]
        sh1, sc1, gt1, sh2, sc2, gt2 = jnp.split(mod[:, None, :], N_MOD, axis=-1)
        mod_c = jax.nn.silu(c_ctx) @ w_mod[layer] + b_mod[layer]
        csh1, csc1, cgt1, csh2, csc2, cgt2 = jnp.split(mod_c, N_MOD)

        h_lat = rms_norm(x, g_norm1[layer]) * (1.0 + sc1) + sh1
        h_ctx = rms_norm(ctx, g_norm1[layer]) * (1.0 + csc1) + csh1
        z_lat = h_lat @ w_in[layer]
        z_ctx = h_ctx @ w_in[layer]
        bl, cl, vl, ul = jnp.split(z_lat, split_at, axis=-1)
        bc, cc, vc, uc = jnp.split(z_ctx, split_at, axis=-1)

        conv_lat = bl * short_conv((cl * vl).reshape(bsz, rows, GRID_W, CONV_W),
                                   conv_w[layer]).reshape(bsz, n_lat, CONV_W)

        ul_g = ul.reshape(bsz, n_lat, SSM_GROUPS, SSM_GROUP)
        uc_g = uc.reshape(bsz, n_ctx, SSM_GROUPS, SSM_GROUP)
        y_lat_dirs = []
        y_ctx_dirs = []
        for direction in range(2):
            reverse = direction == 1
            ab_re, ab_im, bb_re, bb_im = zoh(ssm_a_re[layer, direction], ssm_a_im[layer, direction],
                                             ssm_log_dt[layer, direction],
                                             ssm_b_re[layer, direction], ssm_b_im[layer, direction])
            hc_re, hc_im = ssm_states(uc_g, ab_re, ab_im, bb_re, bb_im, reverse, None)
            end = 0 if reverse else -1
            hl_re, hl_im = ssm_states(ul_g, ab_re, ab_im, bb_re, bb_im, reverse,
                                      (hc_re[:, end], hc_im[:, end]))
            y_lat_dirs.append(ssm_readout(hl_re, hl_im, ssm_c_re[layer, direction],
                                          ssm_c_im[layer, direction]))
            if not last:
                y_ctx_dirs.append(ssm_readout(hc_re, hc_im, ssm_c_re[layer, direction],
                                              ssm_c_im[layer, direction]))
        ssm_lat = (y_lat_dirs[0] + y_lat_dirs[1]).reshape(bsz, n_lat, SSM_W)

        mix_lat = mixer_merge(conv_lat, ssm_lat, ul, ssm_d[layer], w_glu[layer],
                              g_conv_out[layer], g_ssm_out[layer], w_out[layer])
        x_mid = x + gt1 * mix_lat
        x = x_mid + gt2 * sq_relu_mlp(rms_norm(x_mid, g_norm2[layer]) * (1.0 + sc2) + sh2,
                                      w_mlp1[layer], w_mlp2[layer])

        if not last:
            conv_ctx = bc * short_conv(cc * vc, conv_w[layer])
            ssm_ctx = (y_ctx_dirs[0] + y_ctx_dirs[1]).reshape(bsz, n_ctx, SSM_W)
            ctx_mid = ctx + cgt1 * mixer_merge(conv_ctx, ssm_ctx, uc, ssm_d[layer], w_glu[layer],
                                               g_conv_out[layer], g_ssm_out[layer], w_out[layer])
            ctx = ctx_mid + cgt2 * sq_relu_mlp(
                rms_norm(ctx_mid, g_norm2[layer]) * (1.0 + csc2) + csh2,
                w_mlp1[layer], w_mlp2[layer])
    return rms_norm(x, g_final)
```

```python
import functools
import math

import jax
import jax.numpy as jnp
from jax import lax
from jax.experimental import pallas as pl
from jax.experimental.pallas import tpu as pltpu

D_MODEL = 1024
GRID_W = 64
CONV_W = 512
CONV_HEADS = 8
SSM_W = 512
SSM_GROUP = 16
SSM_GROUPS = 32
SSM_STATE = 64
IN_W = 3 * CONV_W + SSM_W
D_FF = 4 * D_MODEL
N_MOD = 6
RMS_EPS = 1e-6

CHUNK = 16
CHUNK_W = CHUNK * SSM_GROUP
PAIR = 2
PAIR_STATE = PAIR * SSM_STATE
N_COMP = 4
SUBLANES = 8
VMEM_LIMIT = 56 * 1024 * 1024

_BF16 = jnp.bfloat16
_F32 = jnp.float32


def _dot(a, b):
    return jnp.dot(a, b, preferred_element_type=_F32)


def _mod_kernel(s_ref, w_ref, b_ref, o_ref):
    s = s_ref[...]
    act = s * jax.nn.sigmoid(s)
    o_ref[...] = jnp.dot(act, w_ref[...], preferred_element_type=_F32,
                         precision=lax.Precision.HIGHEST) + b_ref[...]


def _modulation(cond_rows, w_mod, b_mod):
    n_out = w_mod.shape[1]
    tn = 1024
    return pl.pallas_call(
        _mod_kernel,
        grid=(n_out // tn,),
        in_specs=[pl.BlockSpec((SUBLANES, D_MODEL), lambda j: (0, 0)),
                  pl.BlockSpec((D_MODEL, tn), lambda j: (0, j)),
                  pl.BlockSpec((1, tn), lambda j: (0, j))],
        out_specs=pl.BlockSpec((SUBLANES, tn), lambda j: (0, j)),
        out_shape=jax.ShapeDtypeStruct((SUBLANES, n_out), _F32),
        compiler_params=pltpu.CompilerParams(dimension_semantics=("arbitrary",),
                                             vmem_limit_bytes=VMEM_LIMIT),
        name="mod",
    )(cond_rows, w_mod, b_mod)


def _rms_rows(x, g):
    ms = jnp.mean(x * x, axis=-1, keepdims=True)
    return x * lax.rsqrt(ms + RMS_EPS) * g


def _in_kernel(mod_row, x_ref, mods_ref, g1_ref, w_in_ref, convw_ref, gconv_ref, ones_ref,
               conv_ref, u_ref):
    x = x_ref[...]
    tm = x.shape[0]
    sh1 = mods_ref[mod_row:mod_row + 1, 0:D_MODEL]
    sc1 = mods_ref[mod_row:mod_row + 1, D_MODEL:2 * D_MODEL]
    h = _rms_rows(x, g1_ref[...]) * (1.0 + sc1) + sh1
    z = _dot(h.astype(_BF16), w_in_ref[...])
    b = z[:, 0:CONV_W]
    cv = z[:, CONV_W:2 * CONV_W] * z[:, 2 * CONV_W:3 * CONV_W]
    u_ref[...] = z[:, 3 * CONV_W:]
    pos = lax.broadcasted_iota(jnp.int32, (tm, CONV_W), 0) % GRID_W
    prev = jnp.where(pos == 0, 0.0, pltpu.roll(cv, 1, axis=0))
    nxt = jnp.where(pos == GRID_W - 1, 0.0, pltpu.roll(cv, tm - 1, axis=0))
    cw = convw_ref[...]
    y = b * (cw[0:1] * prev + cw[1:2] * cv + cw[2:3] * nxt)
    ssq = _dot((y * y).astype(_BF16), ones_ref[...])
    yn = y * lax.rsqrt(ssq * (1.0 / (CONV_W // CONV_HEADS)) + RMS_EPS) * gconv_ref[...]
    conv_ref[...] = yn.astype(_BF16)


def _in_proj(x2d, mods, mod_row, g1, w_in_bf, conv_w, g_conv, ones_head, tm):
    n = x2d.shape[0]
    const = lambda i: (0, 0)
    return pl.pallas_call(
        functools.partial(_in_kernel, mod_row),
        grid=(n // tm,),
        in_specs=[pl.BlockSpec((tm, D_MODEL), lambda i: (i, 0)),
                  pl.BlockSpec(mods.shape, const),
                  pl.BlockSpec((1, D_MODEL), const),
                  pl.BlockSpec((D_MODEL, IN_W), const),
                  pl.BlockSpec((3, CONV_W), const),
                  pl.BlockSpec((1, CONV_W), const),
                  pl.BlockSpec((CONV_W, CONV_W), const)],
        out_specs=[pl.BlockSpec((tm, CONV_W), lambda i: (i, 0)),
                   pl.BlockSpec((tm, SSM_W), lambda i: (i, 0))],
        out_shape=[jax.ShapeDtypeStruct((n, CONV_W), _BF16),
                   jax.ShapeDtypeStruct((n, SSM_W), _F32)],
        compiler_params=pltpu.CompilerParams(dimension_semantics=("arbitrary",),
                                             vmem_limit_bytes=VMEM_LIMIT),
        name="in_proj",
    )(x2d, mods, g1, w_in_bf, conv_w, g_conv, ones_head)


def _scan_steps(n_rows):
    return max(1, math.ceil(math.log2(n_rows)))


def _ssm_kernel(n_lat, n_ctx, u_ref, uc_ref, m_ref, wst_ref, wout_ref, apow_ref, y_ref, buf_a, buf_b):
    n_rows = n_lat + n_ctx
    pad = SUBLANES
    wst = wst_ref[0]
    s_lat = _dot(u_ref[0], wst[0:CHUNK_W]) + _dot(u_ref[1], wst[CHUNK_W:])
    s_ctx = _dot(uc_ref[0], wst[0:CHUNK_W]) + _dot(uc_ref[1], wst[CHUNK_W:])

    zeros_pad = jnp.zeros((pad, PAIR_STATE), _F32)
    for buf in (buf_a, buf_b):
        for comp in range(N_COMP):
            buf[comp, 0:pad] = zeros_pad
            buf[comp, pad + n_rows:pad + n_rows + pad] = zeros_pad
    for comp in range(2):
        lanes = slice(comp * PAIR_STATE, (comp + 1) * PAIR_STATE)
        buf_a[comp, pad:pad + n_ctx] = s_ctx[:, lanes]
        buf_a[comp, pad + n_ctx:pad + n_rows] = s_lat[:, lanes]
    for comp in range(2, 4):
        lanes = slice(comp * PAIR_STATE, (comp + 1) * PAIR_STATE)
        buf_a[comp, pad:pad + n_lat] = s_lat[:, lanes]
        buf_a[comp, pad + n_lat:pad + n_rows] = s_ctx[:, lanes]

    src, dst = buf_a, buf_b
    for k in range(_scan_steps(n_rows)):
        sh = 1 << k
        coef = [apow_ref[0, k:k + 1, c * PAIR_STATE:(c + 1) * PAIR_STATE] for c in range(N_COMP)]
        if sh < SUBLANES:
            lo, hi = pad, pad + n_rows
        else:
            lo, hi = pad + sh, pad + n_rows
        xr, xi = src[0, lo - sh:hi - sh], src[1, lo - sh:hi - sh]
        dst[0, lo:hi] = src[0, lo:hi] + coef[0] * xr - coef[1] * xi
        dst[1, lo:hi] = src[1, lo:hi] + coef[0] * xi + coef[1] * xr
        if sh < SUBLANES:
            blo, bhi = pad, pad + n_rows
        else:
            blo, bhi = pad, pad + n_rows - sh
        xr, xi = src[2, blo + sh:bhi + sh], src[3, blo + sh:bhi + sh]
        dst[2, blo:bhi] = src[2, blo:bhi] + coef[2] * xr - coef[3] * xi
        dst[3, blo:bhi] = src[3, blo:bhi] + coef[2] * xi + coef[3] * xr
        if sh >= SUBLANES:
            for comp in range(2):
                dst[comp, pad:pad + sh] = src[comp, pad:pad + sh]
            for comp in range(2, 4):
                dst[comp, pad + n_rows - sh:pad + n_rows] = src[comp, pad + n_rows - sh:pad + n_rows]
        src, dst = dst, src

    f0 = pad + n_ctx - 1
    b0 = pad + 1
    hcat = jnp.concatenate([src[0, f0:f0 + n_lat], src[1, f0:f0 + n_lat],
                            src[2, b0:b0 + n_lat], src[3, b0:b0 + n_lat]], axis=1).astype(_BF16)
    wout = wout_ref[0]
    y_ref[0] = _dot(u_ref[0], m_ref[0]) + _dot(hcat, wout[:, 0:CHUNK_W])
    y_ref[1] = _dot(u_ref[1], m_ref[1]) + _dot(hcat, wout[:, CHUNK_W:])


def _ssm(u_g, uc_g, m_toep, wst, wout, apow):
    n_groups, n_lat, _ = u_g.shape
    n_ctx = uc_g.shape[1]
    n_rows = n_lat + n_ctx
    n_pairs = n_groups // PAIR
    buf = pltpu.VMEM((N_COMP, n_rows + 2 * SUBLANES, PAIR_STATE), _F32)
    return pl.pallas_call(
        functools.partial(_ssm_kernel, n_lat, n_ctx),
        grid=(n_pairs,),
        in_specs=[pl.BlockSpec((PAIR, n_lat, CHUNK_W), lambda q: (q, 0, 0)),
                  pl.BlockSpec((PAIR, n_ctx, CHUNK_W), lambda q: (q, 0, 0)),
                  pl.BlockSpec((PAIR, CHUNK_W, CHUNK_W), lambda q: (q, 0, 0)),
                  pl.BlockSpec((1, PAIR * CHUNK_W, N_COMP * PAIR_STATE), lambda q: (q, 0, 0)),
                  pl.BlockSpec((1, N_COMP * PAIR_STATE, PAIR * CHUNK_W), lambda q: (q, 0, 0)),
                  pl.BlockSpec((1,) + apow.shape[1:], lambda q: (q, 0, 0))],
        out_specs=pl.BlockSpec((PAIR, n_lat, CHUNK_W), lambda q: (q, 0, 0)),
        out_shape=jax.ShapeDtypeStruct((n_groups, n_lat, CHUNK_W), _F32),
        scratch_shapes=[buf, buf],
        compiler_params=pltpu.CompilerParams(dimension_semantics=("arbitrary",),
                                             vmem_limit_bytes=VMEM_LIMIT),
        name="ssm",
    )(u_g, uc_g, m_toep, wst, wout, apow)


def _out_kernel(x_ref, conv_ref, u_ref, y_ref, mods_ref, ssmd_ref, gssm_ref, g2_ref, gfin_ref,
                wglu_ref, wout_ref, w1_ref, w2_ref, ones_ref, o_ref):
    x = x_ref[...]
    gt1 = mods_ref[0:1, 2 * D_MODEL:3 * D_MODEL]
    sh2 = mods_ref[0:1, 3 * D_MODEL:4 * D_MODEL]
    sc2 = mods_ref[0:1, 4 * D_MODEL:5 * D_MODEL]
    gt2 = mods_ref[0:1, 5 * D_MODEL:6 * D_MODEL]
    s = jax.nn.gelu(y_ref[...] + ssmd_ref[...] * u_ref[...])
    s = s * jax.nn.sigmoid(_dot(s.astype(_BF16), wglu_ref[...]))
    ssq = _dot((s * s).astype(_BF16), ones_ref[...])
    sn = s * lax.rsqrt(ssq * (1.0 / SSM_GROUP) + RMS_EPS) * gssm_ref[...]
    mix = _dot(conv_ref[...], wout_ref[0:CONV_W]) + _dot(sn.astype(_BF16), wout_ref[CONV_W:])
    x_mid = x + gt1 * mix
    h2 = (_rms_rows(x_mid, g2_ref[...]) * (1.0 + sc2) + sh2).astype(_BF16)
    n_split = 4
    ff = D_FF // n_split
    acc = None
    for j in range(n_split):
        a = jnp.maximum(_dot(h2, w1_ref[:, j * ff:(j + 1) * ff]), 0.0)
        part = _dot((a * a).astype(_BF16), w2_ref[j * ff:(j + 1) * ff])
        acc = part if acc is None else acc + part
    x_out = x_mid + gt2 * acc
    o_ref[...] = _rms_rows(x_out, gfin_ref[...])


def _out_proj(x2d, conv_n, u_tok, y_tok, mods, ssm_d, g_ssm, g2, g_fin, wglu, wout, w1, w2,
              ones_grp, tm):
    n = x2d.shape[0]
    const = lambda i: (0, 0)
    row = lambda i: (i, 0)
    resident = lambda shape: pl.BlockSpec(shape, const, pipeline_mode=pl.Buffered(1))
    return pl.pallas_call(
        _out_kernel,
        grid=(n // tm,),
        in_specs=[pl.BlockSpec((tm, D_MODEL), row),
                  pl.BlockSpec((tm, CONV_W), row),
                  pl.BlockSpec((tm, SSM_W), row),
                  pl.BlockSpec((tm, SSM_W), row),
                  pl.BlockSpec(mods.shape, const),
                  pl.BlockSpec((1, SSM_W), const),
                  pl.BlockSpec((1, SSM_W), const),
                  pl.BlockSpec((1, D_MODEL), const),
                  pl.BlockSpec((1, D_MODEL), const),
                  resident((SSM_W, SSM_W)),
                  resident((D_MODEL, D_MODEL)),
                  resident((D_MODEL, D_FF)),
                  resident((D_FF, D_MODEL)),
                  resident((SSM_W, SSM_W))],
        out_specs=pl.BlockSpec((tm, D_MODEL), row),
        out_shape=jax.ShapeDtypeStruct((n, D_MODEL), _F32),
        compiler_params=pltpu.CompilerParams(dimension_semantics=("arbitrary",),
                                             vmem_limit_bytes=VMEM_LIMIT),
        name="out_proj",
    )(x2d, conv_n, u_tok, y_tok, mods, ssm_d, g_ssm, g2, g_fin, wglu, wout, w1, w2, ones_grp)


def _cmul(ar, ai, br, bi):
    return ar * br - ai * bi, ar * bi + ai * br


def _discretize(a_re, a_im, log_dt, b_re, b_im):
    dt = jnp.exp(log_dt)[:, None]
    lr = jnp.minimum(a_re, -1e-4)
    li = a_im
    mag = jnp.exp(lr * dt)
    ab_re = mag * jnp.cos(li * dt)
    ab_im = mag * jnp.sin(li * dt)
    nr = ab_re - 1.0
    den = lr * lr + li * li
    f_re = ((nr * lr + ab_im * li) / den)[..., None]
    f_im = ((ab_im * lr - nr * li) / den)[..., None]
    bb_re = f_re * b_re - f_im * b_im
    bb_im = f_re * b_im + f_im * b_re
    return ab_re, ab_im, bb_re, bb_im


def _powers(ar, ai, n):
    pr, pi = [jnp.ones_like(ar)], [jnp.zeros_like(ai)]
    for _ in range(n):
        nr, ni = _cmul(pr[-1], pi[-1], ar, ai)
        pr.append(nr)
        pi.append(ni)
    return jnp.stack(pr), jnp.stack(pi)


def _pair_block_diag(w, row_axis, col_axis):
    eye = jnp.eye(PAIR, dtype=w.dtype)
    w = jnp.expand_dims(w, col_axis)
    shape = [1] * w.ndim
    shape[row_axis] = PAIR
    shape[col_axis] = PAIR
    return w * eye.reshape(shape)


def _ssm_matrices(a_re, a_im, log_dt, b_re, b_im, c_re, c_im, n_steps):
    G, P, J, T = SSM_GROUPS, SSM_STATE, SSM_GROUP, CHUNK
    toep, wst_cols, wout_rows, apow_cols = 0.0, [], [], []
    t_idx = jnp.arange(T)
    for d in range(2):
        ab_re, ab_im, bb_re, bb_im = _discretize(a_re[d], a_im[d], log_dt[d], b_re[d], b_im[d])
        pw_re, pw_im = _powers(ab_re, ab_im, T)
        cr, ci = c_re[d], c_im[d]
        cp_re, cp_im = _cmul(cr[None], ci[None], pw_re[:T, :, None, :], pw_im[:T, :, None, :])
        taps = (jnp.einsum('kgjp,gpi->kgji', cp_re, bb_re)
                - jnp.einsum('kgjp,gpi->kgji', cp_im, bb_im))
        lag = (t_idx[None, :] - t_idx[:, None]) if d == 0 else (t_idx[:, None] - t_idx[None, :])
        blocks = jnp.where((lag >= 0)[:, :, None, None, None], taps[jnp.clip(lag, 0, T - 1)], 0.0)
        toep = toep + blocks
        sel = (T - 1 - t_idx) if d == 0 else t_idx
        w_re, w_im = _cmul(pw_re[sel][..., None], pw_im[sel][..., None], bb_re[None], bb_im[None])
        for w in (w_re, w_im):
            wst_cols.append(jnp.transpose(w, (1, 0, 3, 2)).reshape(G, T * J, P))
        sel = (t_idx + 1) if d == 0 else (T - t_idx)
        o_re, o_im = _cmul(cr[None], ci[None], pw_re[sel][:, :, None, :], pw_im[sel][:, :, None, :])
        for w in (o_re, -o_im):
            wout_rows.append(jnp.transpose(w, (1, 3, 0, 2)).reshape(G, P, T * J))
        sr, si = pw_re[T], pw_im[T]
        steps_re, steps_im = [], []
        for _ in range(n_steps):
            steps_re.append(sr)
            steps_im.append(si)
            sr, si = _cmul(sr, si, sr, si)
        apow_cols += [jnp.stack(steps_re, 1), jnp.stack(steps_im, 1)]

    n_pairs = G // PAIR
    m_toep = jnp.transpose(toep, (2, 0, 4, 1, 3)).reshape(G, T * J, T * J)
    wst = jnp.stack(wst_cols, 2).reshape(n_pairs, PAIR, T * J, N_COMP, P)
    wst = _pair_block_diag(wst, 1, 4).reshape(n_pairs, PAIR * T * J, N_COMP * PAIR * P)
    wout = jnp.stack(wout_rows, 1).reshape(n_pairs, PAIR, N_COMP, P, T * J)
    wout = jnp.transpose(wout, (0, 2, 1, 3, 4))
    wout = _pair_block_diag(wout, 2, 4).reshape(n_pairs, N_COMP * PAIR * P, PAIR * T * J)
    apow = jnp.stack(apow_cols, 2).reshape(n_pairs, PAIR, n_steps, N_COMP, P)
    apow = jnp.transpose(apow, (0, 2, 3, 1, 4)).reshape(n_pairs, n_steps, N_COMP * PAIR * P)
    pad_steps = -n_steps % SUBLANES
    apow = jnp.pad(apow, ((0, 0), (0, pad_steps), (0, 0)))
    return m_toep.astype(_BF16), wst.astype(_BF16), wout.astype(_BF16), apow.astype(_F32)


def _group_ones(width, group):
    idx = jnp.arange(width) // group
    return (idx[:, None] == idx[None, :]).astype(_BF16)


def _to_group_major(u_tok):
    n = u_tok.shape[0]
    u4 = u_tok.reshape(n // CHUNK, CHUNK, SSM_GROUPS, SSM_GROUP)
    return jnp.transpose(u4, (2, 0, 1, 3)).reshape(SSM_GROUPS, n // CHUNK, CHUNK_W)


def _to_token_major(y_g):
    n_chunks = y_g.shape[1]
    y4 = y_g.reshape(SSM_GROUPS, n_chunks, CHUNK, SSM_GROUP)
    return jnp.transpose(y4, (1, 2, 0, 3)).reshape(n_chunks * CHUNK, SSM_W)


def kernel(x, c, ctx, c_ctx, w_mod, b_mod, g_norm1, w_in, conv_w, ssm_a_re, ssm_a_im, ssm_log_dt,
           ssm_b_re, ssm_b_im, ssm_c_re, ssm_c_im, ssm_d, w_glu, g_conv_out, g_ssm_out, w_out,
           g_norm2, w_mlp1, w_mlp2, g_final):
    bsz, n_lat, d_model = x.shape
    n_ctx = ctx.shape[1]
    assert bsz == 1 and d_model == D_MODEL and w_mod.shape[0] == 1
    assert n_lat % (CHUNK * SUBLANES) == 0 and n_ctx % (CHUNK * SUBLANES) == 0 and n_lat % GRID_W == 0
    layer = 0
    x2d = x[0]
    ctx2d = ctx[0]

    cond_rows = jnp.zeros((SUBLANES, D_MODEL), _F32).at[0].set(c[0]).at[1].set(c_ctx)
    mods = _modulation(cond_rows, w_mod[layer], b_mod[layer][None, :])

    g1 = g_norm1[layer][None, :]
    w_in_bf = w_in[layer].astype(_BF16)
    ones_head = _group_ones(CONV_W, CONV_W // CONV_HEADS)
    g_conv = g_conv_out[layer][None, :]
    conv_n, u_lat = _in_proj(x2d, mods, 0, g1, w_in_bf, conv_w[layer], g_conv, ones_head, tm=512)
    _, u_ctx = _in_proj(ctx2d, mods, 1, g1, w_in_bf, conv_w[layer], g_conv, ones_head, tm=n_ctx)

    n_rows = (n_lat + n_ctx) // CHUNK
    m_toep, wst, wout_s, apow = _ssm_matrices(
        ssm_a_re[layer], ssm_a_im[layer], ssm_log_dt[layer], ssm_b_re[layer], ssm_b_im[layer],
        ssm_c_re[layer], ssm_c_im[layer], _scan_steps(n_rows))
    y_g = _ssm(_to_group_major(u_lat).astype(_BF16), _to_group_major(u_ctx).astype(_BF16),
               m_toep, wst, wout_s, apow)
    y_tok = _to_token_major(y_g)

    out = _out_proj(x2d, conv_n, u_lat, y_tok, mods, ssm_d[layer][None, :],
                    g_ssm_out[layer][None, :], g_norm2[layer][None, :], g_final[None, :],
                    w_glu[layer].astype(_BF16), w_out[layer].astype(_BF16),
                    w_mlp1[layer].astype(_BF16), w_mlp2[layer].astype(_BF16),
                    _group_ones(SSM_W, SSM_GROUP), tm=512)
    return out[None]
```

```python
import functools
import math

import jax
import jax.numpy as jnp
from jax import lax
from jax.experimental import pallas as pl
from jax.experimental.pallas import tpu as pltpu

D_MODEL = 1024
GRID_W = 64
CONV_W = 512
CONV_HEADS = 8
SSM_W = 512
SSM_GROUP = 16
SSM_GROUPS = 32
SSM_STATE = 64
IN_W = 3 * CONV_W + SSM_W
D_FF = 4 * D_MODEL
N_MOD = 6
RMS_EPS = 1e-6

LANES = 128
SUBLANES = 8
CHUNK = 16
CHUNK_W = CHUNK * SSM_GROUP
PAIR = 2
PAIR_STATE = PAIR * SSM_STATE
N_COMP = 4
BLOCKS = LANES // SSM_GROUP
N_SLABS = SSM_W // LANES
VMEM_LIMIT = 56 * 1024 * 1024

_BF16 = jnp.bfloat16
_F32 = jnp.float32


def _dot(a, b):
    return jnp.dot(a, b, preferred_element_type=_F32)


def _lane_block_transpose(vs):
    blk = lax.broadcasted_iota(jnp.int32, vs[0].shape, 1) // SSM_GROUP
    dist = BLOCKS // 2
    while dist:
        shift = dist * SSM_GROUP
        upper = (blk & dist) != 0
        out = [None] * BLOCKS
        for a in range(BLOCKS):
            if a & dist:
                continue
            lo, hi = vs[a], vs[a + dist]
            out[a] = jnp.where(upper, pltpu.roll(hi, shift, axis=1), lo)
            out[a + dist] = jnp.where(upper, hi, pltpu.roll(lo, LANES - shift, axis=1))
        vs = out
        dist //= 2
    return vs


def _mod_kernel(s_ref, w_ref, b_ref, o_ref):
    s = s_ref[...]
    act = s * jax.nn.sigmoid(s)
    o_ref[...] = jnp.dot(act, w_ref[...], preferred_element_type=_F32,
                         precision=lax.Precision.HIGHEST) + b_ref[...]


def _modulation(cond_rows, w_mod, b_mod):
    n_out = w_mod.shape[1]
    tn = 1024
    return pl.pallas_call(
        _mod_kernel,
        grid=(n_out // tn,),
        in_specs=[pl.BlockSpec((SUBLANES, D_MODEL), lambda j: (0, 0)),
                  pl.BlockSpec((D_MODEL, tn), lambda j: (0, j)),
                  pl.BlockSpec((1, tn), lambda j: (0, j))],
        out_specs=pl.BlockSpec((SUBLANES, tn), lambda j: (0, j)),
        out_shape=jax.ShapeDtypeStruct((SUBLANES, n_out), _F32),
        compiler_params=pltpu.CompilerParams(dimension_semantics=("arbitrary",),
                                             vmem_limit_bytes=VMEM_LIMIT),
        name="mod",
    )(cond_rows, w_mod, b_mod)


def _rms_rows(x, g):
    ms = jnp.mean(x * x, axis=-1, keepdims=True)
    return x * lax.rsqrt(ms + RMS_EPS) * g


def _in_kernel(mod_row, x_ref, mods_ref, g1_ref, w_in_ref, convw_ref, gconv_ref, ones_ref,
               conv_ref, ug_ref, u_scr):
    x = x_ref[...]
    tm = x.shape[0]
    n_chunks = tm // CHUNK
    sh1 = mods_ref[mod_row:mod_row + 1, 0:D_MODEL]
    sc1 = mods_ref[mod_row:mod_row + 1, D_MODEL:2 * D_MODEL]
    h = _rms_rows(x, g1_ref[...]) * (1.0 + sc1) + sh1
    z = _dot(h.astype(_BF16), w_in_ref[...])
    b = z[:, 0:CONV_W]
    cv = z[:, CONV_W:2 * CONV_W] * z[:, 2 * CONV_W:3 * CONV_W]
    for q in range(N_SLABS):
        u_scr[q] = z[:, 3 * CONV_W + q * LANES:3 * CONV_W + (q + 1) * LANES]
    pos = lax.broadcasted_iota(jnp.int32, (tm, CONV_W), 0) % GRID_W
    prev = jnp.where(pos == 0, 0.0, pltpu.roll(cv, 1, axis=0))
    nxt = jnp.where(pos == GRID_W - 1, 0.0, pltpu.roll(cv, tm - 1, axis=0))
    cw = convw_ref[...]
    y = b * (cw[0:1] * prev + cw[1:2] * cv + cw[2:3] * nxt)
    ssq = _dot((y * y).astype(_BF16), ones_ref[...])
    yn = y * lax.rsqrt(ssq * (1.0 / (CONV_W // CONV_HEADS)) + RMS_EPS) * gconv_ref[...]
    conv_ref[...] = yn.astype(_BF16)
    for q in range(N_SLABS):
        for half in range(CHUNK // BLOCKS):
            rows = [u_scr[q, pl.ds(half * BLOCKS + t, n_chunks, stride=CHUNK), :] for t in range(BLOCKS)]
            cols = _lane_block_transpose(rows)
            for g8 in range(BLOCKS):
                ug_ref[q * BLOCKS + g8, :, half * LANES:(half + 1) * LANES] = cols[g8].astype(_BF16)


def _in_proj(x2d, mods, mod_row, g1, w_in_bf, conv_w, g_conv, ones_head, tm):
    n = x2d.shape[0]
    const = lambda i: (0, 0)
    return pl.pallas_call(
        functools.partial(_in_kernel, mod_row),
        grid=(n // tm,),
        in_specs=[pl.BlockSpec((tm, D_MODEL), lambda i: (i, 0)),
                  pl.BlockSpec(mods.shape, const),
                  pl.BlockSpec((1, D_MODEL), const),
                  pl.BlockSpec((D_MODEL, IN_W), const),
                  pl.BlockSpec((3, CONV_W), const),
                  pl.BlockSpec((1, CONV_W), const),
                  pl.BlockSpec((CONV_W, CONV_W), const)],
        out_specs=[pl.BlockSpec((tm, CONV_W), lambda i: (i, 0)),
                   pl.BlockSpec((SSM_GROUPS, tm // CHUNK, CHUNK_W), lambda i: (0, i, 0))],
        out_shape=[jax.ShapeDtypeStruct((n, CONV_W), _BF16),
                   jax.ShapeDtypeStruct((SSM_GROUPS, n // CHUNK, CHUNK_W), _BF16)],
        scratch_shapes=[pltpu.VMEM((N_SLABS, tm, LANES), _F32)],
        compiler_params=pltpu.CompilerParams(dimension_semantics=("arbitrary",),
                                             vmem_limit_bytes=VMEM_LIMIT),
        name="in_proj",
    )(x2d, mods, g1, w_in_bf, conv_w, g_conv, ones_head)


def _scan_steps(n_rows):
    return max(1, math.ceil(math.log2(n_rows)))


def _ssm_kernel(n_lat, n_ctx, u_ref, uc_ref, wst_ref, wout_ref, cmat_ref, dmat_ref, apow_ref, y_ref,
                buf_a, buf_b, m_scr):
    n_rows = n_lat + n_ctx
    pad = SUBLANES
    wst = wst_ref[0]
    s_lat = _dot(u_ref[0], wst[0:CHUNK_W]) + _dot(u_ref[1], wst[CHUNK_W:])
    s_ctx = _dot(uc_ref[0], wst[0:CHUNK_W]) + _dot(uc_ref[1], wst[CHUNK_W:])

    cm = cmat_ref[0]
    half = PAIR * SSM_STATE * 2
    taps_f = _dot(wst[:, 0:half], cm[0:half])
    taps_b = _dot(wst[:, half:], cm[half:])
    blk = lax.broadcasted_iota(jnp.int32, (SSM_GROUP, CHUNK_W), 1) // SSM_GROUP
    for e in range(PAIR):
        base = e * CHUNK_W
        for s in range(CHUNK):
            acc = jnp.zeros((SSM_GROUP, CHUNK_W), _F32)
            for t in range(CHUNK):
                val = None
                if s >= t:
                    r0 = base + (s - t) * SSM_GROUP
                    val = taps_b[r0:r0 + SSM_GROUP]
                if s <= t:
                    r0 = base + (CHUNK - 1 - t + s) * SSM_GROUP
                    v2 = taps_f[r0:r0 + SSM_GROUP]
                    val = v2 if val is None else val + v2
                if s == t:
                    val = val + dmat_ref[0, e]
                acc = jnp.where(blk == t, val, acc)
            m_scr[e, s * SSM_GROUP:(s + 1) * SSM_GROUP, :] = acc.astype(_BF16)

    zeros_pad = jnp.zeros((pad, PAIR_STATE), _F32)
    for buf in (buf_a, buf_b):
        for comp in range(N_COMP):
            buf[comp, 0:pad] = zeros_pad
            buf[comp, pad + n_rows:pad + n_rows + pad] = zeros_pad
    for comp in range(2):
        lanes = slice(comp * PAIR_STATE, (comp + 1) * PAIR_STATE)
        buf_a[comp, pad:pad + n_ctx] = s_ctx[:, lanes]
        buf_a[comp, pad + n_ctx:pad + n_rows] = s_lat[:, lanes]
    for comp in range(2, 4):
        lanes = slice(comp * PAIR_STATE, (comp + 1) * PAIR_STATE)
        buf_a[comp, pad:pad + n_lat] = s_lat[:, lanes]
        buf_a[comp, pad + n_lat:pad + n_rows] = s_ctx[:, lanes]

    src, dst = buf_a, buf_b
    for k in range(_scan_steps(n_rows)):
        sh = 1 << k
        coef = [apow_ref[0, k:k + 1, c * PAIR_STATE:(c + 1) * PAIR_STATE] for c in range(N_COMP)]
        if sh < SUBLANES:
            lo, hi = pad, pad + n_rows
        else:
            lo, hi = pad + sh, pad + n_rows
        xr, xi = src[0, lo - sh:hi - sh], src[1, lo - sh:hi - sh]
        dst[0, lo:hi] = src[0, lo:hi] + coef[0] * xr - coef[1] * xi
        dst[1, lo:hi] = src[1, lo:hi] + coef[0] * xi + coef[1] * xr
        if sh < SUBLANES:
            blo, bhi = pad, pad + n_rows
        else:
            blo, bhi = pad, pad + n_rows - sh
        xr, xi = src[2, blo + sh:bhi + sh], src[3, blo + sh:bhi + sh]
        dst[2, blo:bhi] = src[2, blo:bhi] + coef[2] * xr - coef[3] * xi
        dst[3, blo:bhi] = src[3, blo:bhi] + coef[2] * xi + coef[3] * xr
        if sh >= SUBLANES:
            for comp in range(2):
                dst[comp, pad:pad + sh] = src[comp, pad:pad + sh]
            for comp in range(2, 4):
                dst[comp, pad + n_rows - sh:pad + n_rows] = src[comp, pad + n_rows - sh:pad + n_rows]
        src, dst = dst, src

    f0 = pad + n_ctx - 1
    b0 = pad + 1
    hcat = jnp.concatenate([src[0, f0:f0 + n_lat], src[1, f0:f0 + n_lat],
                            src[2, b0:b0 + n_lat], src[3, b0:b0 + n_lat]], axis=1).astype(_BF16)
    wout = wout_ref[0]
    y_ref[0] = _dot(u_ref[0], m_scr[0]) + _dot(hcat, wout[:, 0:CHUNK_W])
    y_ref[1] = _dot(u_ref[1], m_scr[1]) + _dot(hcat, wout[:, CHUNK_W:])


def _ssm(u_g, uc_g, wst, wout, cmat, dmat, apow):
    n_groups, n_lat, _ = u_g.shape
    n_ctx = uc_g.shape[1]
    n_rows = n_lat + n_ctx
    n_pairs = n_groups // PAIR
    buf = pltpu.VMEM((N_COMP, n_rows + 2 * SUBLANES, PAIR_STATE), _F32)
    pair3 = lambda q: (q, 0, 0)
    return pl.pallas_call(
        functools.partial(_ssm_kernel, n_lat, n_ctx),
        grid=(n_pairs,),
        in_specs=[pl.BlockSpec((PAIR, n_lat, CHUNK_W), pair3),
                  pl.BlockSpec((PAIR, n_ctx, CHUNK_W), pair3),
                  pl.BlockSpec((1,) + wst.shape[1:], pair3),
                  pl.BlockSpec((1,) + wout.shape[1:], pair3),
                  pl.BlockSpec((1,) + cmat.shape[1:], pair3),
                  pl.BlockSpec((1,) + dmat.shape[1:], lambda q: (q, 0, 0, 0)),
                  pl.BlockSpec((1,) + apow.shape[1:], pair3)],
        out_specs=pl.BlockSpec((PAIR, n_lat, CHUNK_W), pair3),
        out_shape=jax.ShapeDtypeStruct((n_groups, n_lat, CHUNK_W), _F32),
        scratch_shapes=[buf, buf, pltpu.VMEM((PAIR, CHUNK_W, CHUNK_W), _BF16)],
        compiler_params=pltpu.CompilerParams(dimension_semantics=("arbitrary",),
                                             vmem_limit_bytes=VMEM_LIMIT),
        name="ssm",
    )(u_g, uc_g, wst, wout, cmat, dmat, apow)


def _out_kernel(x_ref, conv_ref, yg_ref, mods_ref, gssm_ref, g2_ref, gfin_ref,
                wglu_ref, wout_ref, w1_ref, w2_ref, ones_ref, o_ref, y_scr):
    x = x_ref[...]
    n_chunks = x.shape[0] // CHUNK
    gt1 = mods_ref[0:1, 2 * D_MODEL:3 * D_MODEL]
    sh2 = mods_ref[0:1, 3 * D_MODEL:4 * D_MODEL]
    sc2 = mods_ref[0:1, 4 * D_MODEL:5 * D_MODEL]
    gt2 = mods_ref[0:1, 5 * D_MODEL:6 * D_MODEL]
    for q in range(N_SLABS):
        for half in range(CHUNK // BLOCKS):
            cols = [yg_ref[q * BLOCKS + g8, :, half * LANES:(half + 1) * LANES] for g8 in range(BLOCKS)]
            rows = _lane_block_transpose(cols)
            for t in range(BLOCKS):
                y_scr[q, pl.ds(half * BLOCKS + t, n_chunks, stride=CHUNK), :] = rows[t]
    s = jax.nn.gelu(jnp.concatenate([y_scr[q] for q in range(N_SLABS)], axis=1))
    s = s * jax.nn.sigmoid(_dot(s.astype(_BF16), wglu_ref[...]))
    ssq = _dot((s * s).astype(_BF16), ones_ref[...])
    sn = s * lax.rsqrt(ssq * (1.0 / SSM_GROUP) + RMS_EPS) * gssm_ref[...]
    mix = _dot(conv_ref[...], wout_ref[0:CONV_W]) + _dot(sn.astype(_BF16), wout_ref[CONV_W:])
    x_mid = x + gt1 * mix
    h2 = (_rms_rows(x_mid, g2_ref[...]) * (1.0 + sc2) + sh2).astype(_BF16)
    n_split = 4
    ff = D_FF // n_split
    acc = None
    for j in range(n_split):
        a = jnp.maximum(_dot(h2, w1_ref[:, j * ff:(j + 1) * ff]), 0.0)
        part = _dot((a * a).astype(_BF16), w2_ref[j * ff:(j + 1) * ff])
        acc = part if acc is None else acc + part
    x_out = x_mid + gt2 * acc
    o_ref[...] = _rms_rows(x_out, gfin_ref[...])


def _out_proj(x2d, conv_n, y_g, mods, g_ssm, g2, g_fin, wglu, wout, w1, w2, ones_grp, tm):
    n = x2d.shape[0]
    const = lambda i: (0, 0)
    row = lambda i: (i, 0)
    resident = lambda shape: pl.BlockSpec(shape, const, pipeline_mode=pl.Buffered(1))
    return pl.pallas_call(
        _out_kernel,
        grid=(n // tm,),
        in_specs=[pl.BlockSpec((tm, D_MODEL), row),
                  pl.BlockSpec((tm, CONV_W), row),
                  pl.BlockSpec((SSM_GROUPS, tm // CHUNK, CHUNK_W), lambda i: (0, i, 0)),
                  pl.BlockSpec(mods.shape, const),
                  pl.BlockSpec((1, SSM_W), const),
                  pl.BlockSpec((1, D_MODEL), const),
                  pl.BlockSpec((1, D_MODEL), const),
                  resident((SSM_W, SSM_W)),
                  resident((D_MODEL, D_MODEL)),
                  resident((D_MODEL, D_FF)),
                  resident((D_FF, D_MODEL)),
                  resident((SSM_W, SSM_W))],
        out_specs=pl.BlockSpec((tm, D_MODEL), row),
        out_shape=jax.ShapeDtypeStruct((n, D_MODEL), _F32),
        scratch_shapes=[pltpu.VMEM((N_SLABS, tm, LANES), _F32)],
        compiler_params=pltpu.CompilerParams(dimension_semantics=("arbitrary",),
                                             vmem_limit_bytes=VMEM_LIMIT),
        name="out_proj",
    )(x2d, conv_n, y_g, mods, g_ssm, g2, g_fin, wglu, wout, w1, w2, ones_grp)


def _cmul(ar, ai, br, bi):
    return ar * br - ai * bi, ar * bi + ai * br


def _discretize(a_re, a_im, log_dt, b_re, b_im):
    dt = jnp.exp(log_dt)[..., None]
    lr = jnp.minimum(a_re, -1e-4)
    li = a_im
    mag = jnp.exp(lr * dt)
    ab_re = mag * jnp.cos(li * dt)
    ab_im = mag * jnp.sin(li * dt)
    nr = ab_re - 1.0
    den = lr * lr + li * li
    f_re = ((nr * lr + ab_im * li) / den)[..., None]
    f_im = ((ab_im * lr - nr * li) / den)[..., None]
    bb_re = f_re * b_re - f_im * b_im
    bb_im = f_re * b_im + f_im * b_re
    return ab_re, ab_im, bb_re, bb_im


def _pair_block_diag(w, row_axis, col_axis):
    eye = jnp.eye(PAIR, dtype=w.dtype)
    w = jnp.expand_dims(w, col_axis)
    shape = [1] * w.ndim
    shape[row_axis] = PAIR
    shape[col_axis] = PAIR
    return w * eye.reshape(shape)


def _ssm_matrices(a_re, a_im, log_dt, b_re, b_im, c_re, c_im, ssm_d, n_steps):
    G, P, J, T = SSM_GROUPS, SSM_STATE, SSM_GROUP, CHUNK
    n_pairs = G // PAIR
    ab_re, ab_im, bb_re, bb_im = _discretize(a_re, a_im, log_dt, b_re, b_im)
    pr, pi = ab_re[None], ab_im[None]
    while pr.shape[0] < T:
        nr, ni = _cmul(pr, pi, pr[-1:], pi[-1:])
        pr, pi = jnp.concatenate([pr, nr]), jnp.concatenate([pi, ni])
    pw_re = jnp.concatenate([jnp.ones_like(pr[:1]), pr])
    pw_im = jnp.concatenate([jnp.zeros_like(pi[:1]), pi])

    def by_dir(p, fwd_sel, bwd_sel):
        return jnp.transpose(jnp.stack([p[fwd_sel, 0], p[bwd_sel, 1]], axis=1), (2, 0, 1, 3))
    rev = slice(T - 1, None, -1)
    sel_re = by_dir(pw_re, rev, slice(0, T))[:, :, None]
    sel_im = by_dir(pw_im, rev, slice(0, T))[:, :, None]
    bt_re = jnp.transpose(bb_re, (1, 3, 0, 2))[:, None]
    bt_im = jnp.transpose(bb_im, (1, 3, 0, 2))[:, None]
    w_re, w_im = _cmul(sel_re, sel_im, bt_re, bt_im)
    wst = jnp.stack([w_re, w_im], axis=4).reshape(n_pairs, PAIR, T * J, N_COMP, P)
    wst = _pair_block_diag(wst, 1, 4).reshape(n_pairs, PAIR * T * J, N_COMP * PAIR * P)

    def by_dir_o(p):
        o = jnp.stack([p[1:T + 1, 0], p[T:0:-1, 1]], axis=1)
        return jnp.transpose(o, (2, 1, 3, 0))[..., None]
    ct_re = jnp.transpose(c_re, (1, 0, 3, 2))[:, :, :, None, :]
    ct_im = jnp.transpose(c_im, (1, 0, 3, 2))[:, :, :, None, :]
    o_re, o_im = _cmul(ct_re, ct_im, by_dir_o(pw_re), by_dir_o(pw_im))

    def rows_by_comp(re_part, im_part):
        w = jnp.stack([re_part, -im_part], axis=2).reshape(n_pairs, PAIR, N_COMP, P, T * J)
        return jnp.transpose(w, (0, 2, 1, 3, 4))
    wout = _pair_block_diag(rows_by_comp(o_re, o_im), 2, 4).reshape(
        n_pairs, N_COMP * PAIR * P, PAIR * T * J)
    tile_t = lambda v: jnp.broadcast_to(v, (G, 2, P, T, J))
    cmat = rows_by_comp(tile_t(ct_re), tile_t(ct_im)).reshape(n_pairs, N_COMP * PAIR * P, T * J)

    sr, si = pw_re[T], pw_im[T]
    steps = []
    for _ in range(n_steps):
        steps.append(jnp.stack([sr, si], axis=1))
        sr, si = _cmul(sr, si, sr, si)
    apow = jnp.stack(steps).reshape(n_steps, N_COMP, n_pairs, PAIR * P)
    apow = jnp.transpose(apow, (2, 0, 1, 3)).reshape(n_pairs, n_steps, N_COMP * PAIR * P)
    apow = jnp.pad(apow, ((0, 0), (0, -n_steps % SUBLANES), (0, 0)))

    dmat = ssm_d.reshape(G, J, 1, 1) * jnp.eye(J, dtype=_F32)[None, :, None, :]
    dmat = jnp.broadcast_to(dmat, (G, J, T, J)).reshape(n_pairs, PAIR, J, T * J)
    return wst.astype(_BF16), wout.astype(_BF16), cmat.astype(_BF16), dmat, apow


def _group_ones(width, group):
    idx = jnp.arange(width) // group
    return (idx[:, None] == idx[None, :]).astype(_BF16)


def kernel(x, c, ctx, c_ctx, w_mod, b_mod, g_norm1, w_in, conv_w, ssm_a_re, ssm_a_im, ssm_log_dt,
           ssm_b_re, ssm_b_im, ssm_c_re, ssm_c_im, ssm_d, w_glu, g_conv_out, g_ssm_out, w_out,
           g_norm2, w_mlp1, w_mlp2, g_final):
    bsz, n_lat, d_model = x.shape
    n_ctx = ctx.shape[1]
    assert bsz == 1 and d_model == D_MODEL and w_mod.shape[0] == 1
    assert n_lat % (CHUNK * SUBLANES) == 0 and n_ctx % (CHUNK * CHUNK) == 0 and n_lat % GRID_W == 0
    layer = 0
    x2d = x[0]
    ctx2d = ctx[0]

    cond_rows = jnp.zeros((SUBLANES, D_MODEL), _F32).at[0].set(c[0]).at[1].set(c_ctx)
    mods = _modulation(cond_rows, w_mod[layer], b_mod[layer][None, :])

    g1 = g_norm1[layer][None, :]
    w_in_bf = w_in[layer].astype(_BF16)
    ones_head = _group_ones(CONV_W, CONV_W // CONV_HEADS)
    g_conv = g_conv_out[layer][None, :]
    conv_n, u_g = _in_proj(x2d, mods, 0, g1, w_in_bf, conv_w[layer], g_conv, ones_head, tm=512)
    _, uc_g = _in_proj(ctx2d, mods, 1, g1, w_in_bf, conv_w[layer], g_conv, ones_head, tm=n_ctx)

    n_rows = (n_lat + n_ctx) // CHUNK
    wst, wout_s, cmat, dmat, apow = _ssm_matrices(
        ssm_a_re[layer], ssm_a_im[layer], ssm_log_dt[layer], ssm_b_re[layer], ssm_b_im[layer],
        ssm_c_re[layer], ssm_c_im[layer], ssm_d[layer], _scan_steps(n_rows))
    y_g = _ssm(u_g, uc_g, wst, wout_s, cmat, dmat, apow)

    out = _out_proj(x2d, conv_n, y_g, mods, g_ssm_out[layer][None, :], g_norm2[layer][None, :],
                    g_final[None, :], w_glu[layer].astype(_BF16), w_out[layer].astype(_BF16),
                    w_mlp1[layer].astype(_BF16), w_mlp2[layer].astype(_BF16),
                    _group_ones(SSM_W, SSM_GROUP), tm=512)
    return out[None]
```

```python
import functools
import math

import jax
import jax.numpy as jnp
from jax import lax
from jax.experimental import pallas as pl
from jax.experimental.pallas import tpu as pltpu

D_MODEL = 1024
GRID_W = 64
CONV_W = 512
CONV_HEADS = 8
SSM_W = 512
SSM_GROUP = 16
SSM_GROUPS = 32
SSM_STATE = 64
IN_W = 3 * CONV_W + SSM_W
D_FF = 4 * D_MODEL
N_MOD = 6
RMS_EPS = 1e-6

LANES = 128
SUBLANES = 8
CHUNK = 16
CHUNK_W = CHUNK * SSM_GROUP
PAIR = 2
PAIR_STATE = PAIR * SSM_STATE
N_COMP = 4
BLOCKS = LANES // SSM_GROUP
N_SLABS = SSM_W // LANES
VMEM_LIMIT = 56 * 1024 * 1024

_BF16 = jnp.bfloat16
_F32 = jnp.float32


def _dot(a, b):
    return jnp.dot(a, b, preferred_element_type=_F32)


def _lane_block_transpose(vs):
    blk = lax.broadcasted_iota(jnp.int32, vs[0].shape, 1) // SSM_GROUP
    dist = BLOCKS // 2
    while dist:
        shift = dist * SSM_GROUP
        upper = (blk & dist) != 0
        out = [None] * BLOCKS
        for a in range(BLOCKS):
            if a & dist:
                continue
            lo, hi = vs[a], vs[a + dist]
            out[a] = jnp.where(upper, pltpu.roll(hi, shift, axis=1), lo)
            out[a + dist] = jnp.where(upper, hi, pltpu.roll(lo, LANES - shift, axis=1))
        vs = out
        dist //= 2
    return vs


def _mod_kernel(s_ref, w_ref, b_ref, o_ref):
    s = s_ref[...]
    act = s * jax.nn.sigmoid(s)
    o_ref[...] = jnp.dot(act, w_ref[...], preferred_element_type=_F32,
                         precision=lax.Precision.HIGHEST) + b_ref[...]


def _modulation(cond_rows, w_mod, b_mod):
    n_out = w_mod.shape[1]
    tn = 1024
    return pl.pallas_call(
        _mod_kernel,
        grid=(n_out // tn,),
        in_specs=[pl.BlockSpec((SUBLANES, D_MODEL), lambda j: (0, 0)),
                  pl.BlockSpec((D_MODEL, tn), lambda j: (0, j)),
                  pl.BlockSpec((1, tn), lambda j: (0, j))],
        out_specs=pl.BlockSpec((SUBLANES, tn), lambda j: (0, j)),
        out_shape=jax.ShapeDtypeStruct((SUBLANES, n_out), _F32),
        compiler_params=pltpu.CompilerParams(dimension_semantics=("arbitrary",),
                                             vmem_limit_bytes=VMEM_LIMIT),
        name="mod",
    )(cond_rows, w_mod, b_mod)


def _rms_rows(x, g):
    ms = jnp.mean(x * x, axis=-1, keepdims=True)
    return x * lax.rsqrt(ms + RMS_EPS) * g


def _in_kernel(mod_row, x_ref, mods_ref, g1_ref, w_in_ref, convw_ref, gconv_ref, ones_ref,
               conv_ref, ug_ref, u_scr):
    x = x_ref[...]
    tm = x.shape[0]
    n_chunks = tm // CHUNK
    sh1 = mods_ref[mod_row:mod_row + 1, 0:D_MODEL]
    sc1 = mods_ref[mod_row:mod_row + 1, D_MODEL:2 * D_MODEL]
    h = _rms_rows(x, g1_ref[...]) * (1.0 + sc1) + sh1
    z = _dot(h.astype(_BF16), w_in_ref[...])
    b = z[:, 0:CONV_W]
    cv = z[:, CONV_W:2 * CONV_W] * z[:, 2 * CONV_W:3 * CONV_W]
    for q in range(N_SLABS):
        u_scr[q] = z[:, 3 * CONV_W + q * LANES:3 * CONV_W + (q + 1) * LANES]
    pos = lax.broadcasted_iota(jnp.int32, (tm, CONV_W), 0) % GRID_W
    prev = jnp.where(pos == 0, 0.0, pltpu.roll(cv, 1, axis=0))
    nxt = jnp.where(pos == GRID_W - 1, 0.0, pltpu.roll(cv, tm - 1, axis=0))
    cw = convw_ref[...]
    y = b * (cw[0:1] * prev + cw[1:2] * cv + cw[2:3] * nxt)
    ssq = _dot((y * y).astype(_BF16), ones_ref[...])
    yn = y * lax.rsqrt(ssq * (1.0 / (CONV_W // CONV_HEADS)) + RMS_EPS) * gconv_ref[...]
    conv_ref[...] = yn.astype(_BF16)
    for q in range(N_SLABS):
        for half in range(CHUNK // BLOCKS):
            rows = [u_scr[q, pl.ds(half * BLOCKS + t, n_chunks, stride=CHUNK), :] for t in range(BLOCKS)]
            cols = _lane_block_transpose(rows)
            for g8 in range(BLOCKS):
                ug_ref[q * BLOCKS + g8, :, half * LANES:(half + 1) * LANES] = cols[g8].astype(_BF16)


def _in_proj(x2d, mods, mod_row, g1, w_in_bf, conv_w, g_conv, ones_head, tm):
    n = x2d.shape[0]
    const = lambda i: (0, 0)
    return pl.pallas_call(
        functools.partial(_in_kernel, mod_row),
        grid=(n // tm,),
        in_specs=[pl.BlockSpec((tm, D_MODEL), lambda i: (i, 0)),
                  pl.BlockSpec(mods.shape, const),
                  pl.BlockSpec((1, D_MODEL), const),
                  pl.BlockSpec((D_MODEL, IN_W), const),
                  pl.BlockSpec((3, CONV_W), const),
                  pl.BlockSpec((1, CONV_W), const),
                  pl.BlockSpec((CONV_W, CONV_W), const)],
        out_specs=[pl.BlockSpec((tm, CONV_W), lambda i: (i, 0)),
                   pl.BlockSpec((SSM_GROUPS, tm // CHUNK, CHUNK_W), lambda i: (0, i, 0))],
        out_shape=[jax.ShapeDtypeStruct((n, CONV_W), _BF16),
                   jax.ShapeDtypeStruct((SSM_GROUPS, n // CHUNK, CHUNK_W), _BF16)],
        scratch_shapes=[pltpu.VMEM((N_SLABS, tm, LANES), _F32)],
        compiler_params=pltpu.CompilerParams(dimension_semantics=("arbitrary",),
                                             vmem_limit_bytes=VMEM_LIMIT),
        name="in_proj",
    )(x2d, mods, g1, w_in_bf, conv_w, g_conv, ones_head)


def _scan_steps(n_rows):
    return max(1, math.ceil(math.log2(n_rows)))


def _ssm_kernel(n_lat, n_ctx, u_ref, uc_ref, wst_ref, wout_ref, cmat_ref, dmat_ref, apow_ref, y_ref,
                buf_a, buf_b, m_scr):
    n_rows = n_lat + n_ctx
    pad = SUBLANES

    blk = lax.broadcasted_iota(jnp.int32, (SSM_GROUP, CHUNK_W), 1) // SSM_GROUP
    for e in range(PAIR):
        taps = _dot(wst_ref[e], cmat_ref[e])
        taps_f = taps[:, 0:CHUNK_W]
        taps_b = taps[:, CHUNK_W:]
        for s in range(CHUNK):
            acc = jnp.zeros((SSM_GROUP, CHUNK_W), _F32)
            for t in range(CHUNK):
                val = None
                if s >= t:
                    r0 = (s - t) * SSM_GROUP
                    val = taps_b[r0:r0 + SSM_GROUP]
                if s <= t:
                    r0 = (CHUNK - 1 - t + s) * SSM_GROUP
                    v2 = taps_f[r0:r0 + SSM_GROUP]
                    val = v2 if val is None else val + v2
                if s == t:
                    val = val + dmat_ref[0, e]
                acc = jnp.where(blk == t, val, acc)
            m_scr[e, s * SSM_GROUP:(s + 1) * SSM_GROUP, :] = acc.astype(_BF16)

    def paired(m0, m1):
        left = lax.broadcasted_iota(jnp.int32, (m0.shape[0], PAIR_STATE), 1) < SSM_STATE
        re0, im0 = m0[:, 0:PAIR_STATE], m0[:, PAIR_STATE:]
        re1, im1 = m1[:, 0:PAIR_STATE], m1[:, PAIR_STATE:]
        return [jnp.where(left, re0, re1), jnp.where(left, im0, im1),
                jnp.where(left, re1, re0), jnp.where(left, im1, im0)]

    s_lat = paired(_dot(u_ref[0], wst_ref[0]), _dot(u_ref[1], wst_ref[1]))
    s_ctx = paired(_dot(uc_ref[0], wst_ref[0]), _dot(uc_ref[1], wst_ref[1]))

    zeros_pad = jnp.zeros((pad, PAIR_STATE), _F32)
    for buf in (buf_a, buf_b):
        for comp in range(N_COMP):
            buf[comp, 0:pad] = zeros_pad
            buf[comp, pad + n_rows:pad + n_rows + pad] = zeros_pad
    for comp in range(2):
        buf_a[comp, pad:pad + n_ctx] = s_ctx[comp]
        buf_a[comp, pad + n_ctx:pad + n_rows] = s_lat[comp]
    for comp in range(2, 4):
        buf_a[comp, pad:pad + n_lat] = s_lat[comp]
        buf_a[comp, pad + n_lat:pad + n_rows] = s_ctx[comp]

    src, dst = buf_a, buf_b
    for k in range(_scan_steps(n_rows)):
        sh = 1 << k
        coef = [apow_ref[0, k:k + 1, c * PAIR_STATE:(c + 1) * PAIR_STATE] for c in range(N_COMP)]
        if sh < SUBLANES:
            lo, hi = pad, pad + n_rows
        else:
            lo, hi = pad + sh, pad + n_rows
        xr, xi = src[0, lo - sh:hi - sh], src[1, lo - sh:hi - sh]
        dst[0, lo:hi] = src[0, lo:hi] + coef[0] * xr - coef[1] * xi
        dst[1, lo:hi] = src[1, lo:hi] + coef[0] * xi + coef[1] * xr
        if sh < SUBLANES:
            blo, bhi = pad, pad + n_rows
        else:
            blo, bhi = pad, pad + n_rows - sh
        xr, xi = src[2, blo + sh:bhi + sh], src[3, blo + sh:bhi + sh]
        dst[2, blo:bhi] = src[2, blo:bhi] + coef[2] * xr - coef[3] * xi
        dst[3, blo:bhi] = src[3, blo:bhi] + coef[2] * xi + coef[3] * xr
        if sh >= SUBLANES:
            for comp in range(2):
                dst[comp, pad:pad + sh] = src[comp, pad:pad + sh]
            for comp in range(2, 4):
                dst[comp, pad + n_rows - sh:pad + n_rows] = src[comp, pad + n_rows - sh:pad + n_rows]
        src, dst = dst, src

    f0 = pad + n_ctx - 1
    b0 = pad + 1
    f_re, f_im = src[0, f0:f0 + n_lat], src[1, f0:f0 + n_lat]
    b_re, b_im = src[2, b0:b0 + n_lat], src[3, b0:b0 + n_lat]
    left = lax.broadcasted_iota(jnp.int32, (n_lat, PAIR_STATE), 1) < SSM_STATE
    h0 = jnp.concatenate([jnp.where(left, f_re, b_re), jnp.where(left, f_im, b_im)], axis=1)
    h1 = jnp.concatenate([jnp.where(left, b_re, f_re), jnp.where(left, b_im, f_im)], axis=1)
    y_ref[0] = _dot(u_ref[0], m_scr[0]) + _dot(h0.astype(_BF16), wout_ref[0])
    y_ref[1] = _dot(u_ref[1], m_scr[1]) + _dot(h1.astype(_BF16), wout_ref[1])


def _ssm(u_g, uc_g, wst, wout, cmat, dmat, apow):
    n_groups, n_lat, _ = u_g.shape
    n_ctx = uc_g.shape[1]
    n_rows = n_lat + n_ctx
    n_pairs = n_groups // PAIR
    buf = pltpu.VMEM((N_COMP, n_rows + 2 * SUBLANES, PAIR_STATE), _F32)
    pair3 = lambda q: (q, 0, 0)
    return pl.pallas_call(
        functools.partial(_ssm_kernel, n_lat, n_ctx),
        grid=(n_pairs,),
        in_specs=[pl.BlockSpec((PAIR, n_lat, CHUNK_W), pair3),
                  pl.BlockSpec((PAIR, n_ctx, CHUNK_W), pair3),
                  pl.BlockSpec((PAIR,) + wst.shape[1:], pair3),
                  pl.BlockSpec((PAIR,) + wout.shape[1:], pair3),
                  pl.BlockSpec((PAIR,) + cmat.shape[1:], pair3),
                  pl.BlockSpec((1,) + dmat.shape[1:], lambda q: (q, 0, 0, 0)),
                  pl.BlockSpec((1,) + apow.shape[1:], pair3)],
        out_specs=pl.BlockSpec((PAIR, n_lat, CHUNK_W), pair3),
        out_shape=jax.ShapeDtypeStruct((n_groups, n_lat, CHUNK_W), _F32),
        scratch_shapes=[buf, buf, pltpu.VMEM((PAIR, CHUNK_W, CHUNK_W), _BF16)],
        compiler_params=pltpu.CompilerParams(dimension_semantics=("arbitrary",),
                                             vmem_limit_bytes=VMEM_LIMIT),
        name="ssm",
    )(u_g, uc_g, wst, wout, cmat, dmat, apow)


def _out_kernel(x_ref, conv_ref, yg_ref, mods_ref, gssm_ref, g2_ref, gfin_ref,
                wglu_ref, wout_ref, w1_ref, w2_ref, ones_ref, o_ref, y_scr):
    x = x_ref[...]
    n_chunks = x.shape[0] // CHUNK
    gt1 = mods_ref[0:1, 2 * D_MODEL:3 * D_MODEL]
    sh2 = mods_ref[0:1, 3 * D_MODEL:4 * D_MODEL]
    sc2 = mods_ref[0:1, 4 * D_MODEL:5 * D_MODEL]
    gt2 = mods_ref[0:1, 5 * D_MODEL:6 * D_MODEL]
    for q in range(N_SLABS):
        for half in range(CHUNK // BLOCKS):
            cols = [yg_ref[q * BLOCKS + g8, :, half * LANES:(half + 1) * LANES] for g8 in range(BLOCKS)]
            rows = _lane_block_transpose(cols)
            for t in range(BLOCKS):
                y_scr[q, pl.ds(half * BLOCKS + t, n_chunks, stride=CHUNK), :] = rows[t]
    s = jax.nn.gelu(jnp.concatenate([y_scr[q] for q in range(N_SLABS)], axis=1))
    s = s * jax.nn.sigmoid(_dot(s.astype(_BF16), wglu_ref[...]))
    ssq = _dot((s * s).astype(_BF16), ones_ref[...])
    sn = s * lax.rsqrt(ssq * (1.0 / SSM_GROUP) + RMS_EPS) * gssm_ref[...]
    mix = _dot(conv_ref[...], wout_ref[0:CONV_W]) + _dot(sn.astype(_BF16), wout_ref[CONV_W:])
    x_mid = x + gt1 * mix
    h2 = (_rms_rows(x_mid, g2_ref[...]) * (1.0 + sc2) + sh2).astype(_BF16)
    n_split = 4
    ff = D_FF // n_split
    acc = None
    for j in range(n_split):
        a = jnp.maximum(_dot(h2, w1_ref[:, j * ff:(j + 1) * ff]), 0.0)
        part = _dot((a * a).astype(_BF16), w2_ref[j * ff:(j + 1) * ff])
        acc = part if acc is None else acc + part
    x_out = x_mid + gt2 * acc
    o_ref[...] = _rms_rows(x_out, gfin_ref[...])


def _out_proj(x2d, conv_n, y_g, mods, g_ssm, g2, g_fin, wglu, wout, w1, w2, ones_grp, tm):
    n = x2d.shape[0]
    const = lambda i: (0, 0)
    row = lambda i: (i, 0)
    resident = lambda shape: pl.BlockSpec(shape, const, pipeline_mode=pl.Buffered(1))
    return pl.pallas_call(
        _out_kernel,
        grid=(n // tm,),
        in_specs=[pl.BlockSpec((tm, D_MODEL), row),
                  pl.BlockSpec((tm, CONV_W), row),
                  pl.BlockSpec((SSM_GROUPS, tm // CHUNK, CHUNK_W), lambda i: (0, i, 0)),
                  pl.BlockSpec(mods.shape, const),
                  pl.BlockSpec((1, SSM_W), const),
                  pl.BlockSpec((1, D_MODEL), const),
                  pl.BlockSpec((1, D_MODEL), const),
                  resident((SSM_W, SSM_W)),
                  resident((D_MODEL, D_MODEL)),
                  resident((D_MODEL, D_FF)),
                  resident((D_FF, D_MODEL)),
                  resident((SSM_W, SSM_W))],
        out_specs=pl.BlockSpec((tm, D_MODEL), row),
        out_shape=jax.ShapeDtypeStruct((n, D_MODEL), _F32),
        scratch_shapes=[pltpu.VMEM((N_SLABS, tm, LANES), _F32)],
        compiler_params=pltpu.CompilerParams(dimension_semantics=("arbitrary",),
                                             vmem_limit_bytes=VMEM_LIMIT),
        name="out_proj",
    )(x2d, conv_n, y_g, mods, g_ssm, g2, g_fin, wglu, wout, w1, w2, ones_grp)


def _cmul(ar, ai, br, bi):
    return ar * br - ai * bi, ar * bi + ai * br


def _discretize(a_re, a_im, log_dt, b_re, b_im):
    dt = jnp.exp(log_dt)[..., None]
    lr = jnp.minimum(a_re, -1e-4)
    li = a_im
    mag = jnp.exp(lr * dt)
    ab_re = mag * jnp.cos(li * dt)
    ab_im = mag * jnp.sin(li * dt)
    nr = ab_re - 1.0
    den = lr * lr + li * li
    f_re = ((nr * lr + ab_im * li) / den)[..., None]
    f_im = ((ab_im * lr - nr * li) / den)[..., None]
    bb_re = f_re * b_re - f_im * b_im
    bb_im = f_re * b_im + f_im * b_re
    return ab_re, ab_im, bb_re, bb_im


def _member_slots(w, dir_axis):
    return jnp.stack([w[:, 0], jnp.flip(w[:, 1], axis=dir_axis - 1)], axis=1)


def _ssm_matrices(a_re, a_im, log_dt, b_re, b_im, c_re, c_im, ssm_d, n_steps):
    G, P, J, T = SSM_GROUPS, SSM_STATE, SSM_GROUP, CHUNK
    n_pairs = G // PAIR
    ab_re, ab_im, bb_re, bb_im = _discretize(a_re, a_im, log_dt, b_re, b_im)
    expo = jnp.arange(T + 1).reshape(T + 1, 1, 1, 1)
    pw_re, pw_im = jnp.ones((T + 1, 2, G, P), _F32), jnp.zeros((T + 1, 2, G, P), _F32)
    sq_re, sq_im = ab_re, ab_im
    for bit in range(T.bit_length()):
        m_re, m_im = _cmul(pw_re, pw_im, sq_re[None], sq_im[None])
        take = ((expo >> bit) & 1) == 1
        pw_re, pw_im = jnp.where(take, m_re, pw_re), jnp.where(take, m_im, pw_im)
        sq_re, sq_im = _cmul(sq_re, sq_im, sq_re, sq_im)

    def by_dir(p):
        sel = jnp.stack([p[T - 1::-1, 0], p[0:T, 1]], axis=1)
        return jnp.transpose(sel, (2, 0, 1, 3))[:, :, None]
    bt_re = jnp.transpose(bb_re, (1, 3, 0, 2))[:, None]
    bt_im = jnp.transpose(bb_im, (1, 3, 0, 2))[:, None]
    w_re, w_im = _cmul(by_dir(pw_re), by_dir(pw_im), bt_re, bt_im)
    wst = jnp.stack([w_re, w_im], axis=3).reshape(n_pairs, PAIR, T * J, 2, 2, P)
    wst = _member_slots(wst, 4).reshape(G, T * J, N_COMP * P)

    def by_dir_o(p):
        o = jnp.stack([p[1:T + 1, 0], p[T:0:-1, 1]], axis=1)
        return jnp.transpose(o, (2, 1, 3, 0))[..., None]
    ct_re = jnp.transpose(c_re, (1, 0, 3, 2))[:, :, :, None, :]
    ct_im = jnp.transpose(c_im, (1, 0, 3, 2))[:, :, :, None, :]
    o_re, o_im = _cmul(ct_re, ct_im, by_dir_o(pw_re), by_dir_o(pw_im))
    wout = jnp.stack([o_re, -o_im], axis=1).reshape(n_pairs, PAIR, 2, 2, P, T * J)
    wout = _member_slots(wout, 3).reshape(G, N_COMP * P, T * J)

    tile_t = lambda v: jnp.broadcast_to(v, (G, 2, P, T, J)).reshape(G, 1, 2, P, 1, T * J)
    cmat = jnp.concatenate([tile_t(ct_re), -tile_t(ct_im)], axis=1)
    cmat = cmat * jnp.eye(2, dtype=_F32).reshape(1, 1, 2, 1, 2, 1)
    cmat = _member_slots(cmat.reshape(n_pairs, PAIR, 2, 2, P, 2 * T * J), 3).reshape(
        G, N_COMP * P, 2 * T * J)

    sr, si = pw_re[T].reshape(2, n_pairs, PAIR, P), pw_im[T].reshape(2, n_pairs, PAIR, P)
    steps = []
    for _ in range(n_steps):
        rows = [sr[0], si[0], sr[1, :, ::-1], si[1, :, ::-1]]
        steps.append(jnp.stack([r.reshape(n_pairs, PAIR * P) for r in rows], axis=1))
        sr, si = _cmul(sr, si, sr, si)
    apow = jnp.stack(steps, axis=1).reshape(n_pairs, n_steps, N_COMP * PAIR * P)
    apow = jnp.pad(apow, ((0, 0), (0, -n_steps % SUBLANES), (0, 0)))

    dmat = ssm_d.reshape(G, J, 1, 1) * jnp.eye(J, dtype=_F32)[None, :, None, :]
    dmat = jnp.broadcast_to(dmat, (G, J, T, J)).reshape(n_pairs, PAIR, J, T * J)
    return wst.astype(_BF16), wout.astype(_BF16), cmat.astype(_BF16), dmat, apow


def _group_ones(width, group):
    idx = jnp.arange(width) // group
    return (idx[:, None] == idx[None, :]).astype(_BF16)


def kernel(x, c, ctx, c_ctx, w_mod, b_mod, g_norm1, w_in, conv_w, ssm_a_re, ssm_a_im, ssm_log_dt,
           ssm_b_re, ssm_b_im, ssm_c_re, ssm_c_im, ssm_d, w_glu, g_conv_out, g_ssm_out, w_out,
           g_norm2, w_mlp1, w_mlp2, g_final):
    bsz, n_lat, d_model = x.shape
    n_ctx = ctx.shape[1]
    assert bsz == 1 and d_model == D_MODEL and w_mod.shape[0] == 1
    assert n_lat % (CHUNK * SUBLANES) == 0 and n_ctx % (CHUNK * CHUNK) == 0 and n_lat % GRID_W == 0
    layer = 0
    x2d = x[0]
    ctx2d = ctx[0]

    cond_rows = jnp.zeros((SUBLANES, D_MODEL), _F32).at[0].set(c[0]).at[1].set(c_ctx)
    mods = _modulation(cond_rows, w_mod[layer], b_mod[layer][None, :])

    g1 = g_norm1[layer][None, :]
    w_in_bf = w_in[layer].astype(_BF16)
    ones_head = _group_ones(CONV_W, CONV_W // CONV_HEADS)
    g_conv = g_conv_out[layer][None, :]
    conv_n, u_g = _in_proj(x2d, mods, 0, g1, w_in_bf, conv_w[layer], g_conv, ones_head, tm=512)
    _, uc_g = _in_proj(ctx2d, mods, 1, g1, w_in_bf, conv_w[layer], g_conv, ones_head, tm=n_ctx)

    n_rows = (n_lat + n_ctx) // CHUNK
    wst, wout_s, cmat, dmat, apow = _ssm_matrices(
        ssm_a_re[layer], ssm_a_im[layer], ssm_log_dt[layer], ssm_b_re[layer], ssm_b_im[layer],
        ssm_c_re[layer], ssm_c_im[layer], ssm_d[layer], _scan_steps(n_rows))
    y_g = _ssm(u_g, uc_g, wst, wout_s, cmat, dmat, apow)

    out = _out_proj(x2d, conv_n, y_g, mods, g_ssm_out[layer][None, :], g_norm2[layer][None, :],
                    g_final[None, :], w_glu[layer].astype(_BF16), w_out[layer].astype(_BF16),
                    w_mlp1[layer].astype(_BF16), w_mlp2[layer].astype(_BF16),
                    _group_ones(SSM_W, SSM_GROUP), tm=512)
    return out[None]
```

```python
import functools
import math

import jax
import jax.numpy as jnp
from jax import lax
from jax.experimental import pallas as pl
from jax.experimental.pallas import tpu as pltpu

D_MODEL = 1024
GRID_W = 64
CONV_W = 512
CONV_HEADS = 8
SSM_W = 512
SSM_GROUP = 16
SSM_GROUPS = 32
SSM_STATE = 64
IN_W = 3 * CONV_W + SSM_W
D_FF = 4 * D_MODEL
N_MOD = 6
RMS_EPS = 1e-6

LANES = 128
SUBLANES = 8
CHUNK = 16
CHUNK_W = CHUNK * SSM_GROUP
PAIR = 2
PAIR_STATE = PAIR * SSM_STATE
N_COMP = 4
BLOCKS = LANES // SSM_GROUP
N_SLABS = SSM_W // LANES
VMEM_LIMIT = 56 * 1024 * 1024

_BF16 = jnp.bfloat16
_F32 = jnp.float32


def _dot(a, b):
    return jnp.dot(a, b, preferred_element_type=_F32)


def _lane_block_transpose(vs):
    blk = lax.broadcasted_iota(jnp.int32, vs[0].shape, 1) // SSM_GROUP
    dist = BLOCKS // 2
    while dist:
        shift = dist * SSM_GROUP
        upper = (blk & dist) != 0
        out = [None] * BLOCKS
        for a in range(BLOCKS):
            if a & dist:
                continue
            lo, hi = vs[a], vs[a + dist]
            out[a] = jnp.where(upper, pltpu.roll(hi, shift, axis=1), lo)
            out[a + dist] = jnp.where(upper, hi, pltpu.roll(lo, LANES - shift, axis=1))
        vs = out
        dist //= 2
    return vs


def _mod_kernel(s_ref, w_ref, b_ref, o_ref):
    s = s_ref[...]
    act = s * jax.nn.sigmoid(s)
    o_ref[...] = jnp.dot(act, w_ref[...], preferred_element_type=_F32,
                         precision=lax.Precision.HIGHEST) + b_ref[...]


def _modulation(cond_rows, w_mod, b_mod):
    n_out = w_mod.shape[1]
    tn = 1024
    return pl.pallas_call(
        _mod_kernel,
        grid=(n_out // tn,),
        in_specs=[pl.BlockSpec((SUBLANES, D_MODEL), lambda j: (0, 0)),
                  pl.BlockSpec((D_MODEL, tn), lambda j: (0, j)),
                  pl.BlockSpec((1, tn), lambda j: (0, j))],
        out_specs=pl.BlockSpec((SUBLANES, tn), lambda j: (0, j)),
        out_shape=jax.ShapeDtypeStruct((SUBLANES, n_out), _F32),
        compiler_params=pltpu.CompilerParams(dimension_semantics=("arbitrary",),
                                             vmem_limit_bytes=VMEM_LIMIT),
        name="mod",
    )(cond_rows, w_mod, b_mod)


def _rms_rows(x, g):
    ms = jnp.mean(x * x, axis=-1, keepdims=True)
    return x * lax.rsqrt(ms + RMS_EPS) * g


def _in_kernel(mod_row, x_ref, mods_ref, g1_ref, w_in_ref, convw_ref, gconv_ref, ones_ref,
               conv_ref, ug_ref, u_scr):
    x = x_ref[...]
    tm = x.shape[0]
    n_chunks = tm // CHUNK
    sh1 = mods_ref[mod_row:mod_row + 1, 0:D_MODEL]
    sc1 = mods_ref[mod_row:mod_row + 1, D_MODEL:2 * D_MODEL]
    h = _rms_rows(x, g1_ref[...]) * (1.0 + sc1) + sh1
    z = _dot(h.astype(_BF16), w_in_ref[...])
    b = z[:, 0:CONV_W]
    cv = z[:, CONV_W:2 * CONV_W] * z[:, 2 * CONV_W:3 * CONV_W]
    for q in range(N_SLABS):
        u_scr[q] = z[:, 3 * CONV_W + q * LANES:3 * CONV_W + (q + 1) * LANES]
    pos = lax.broadcasted_iota(jnp.int32, (tm, CONV_W), 0) % GRID_W
    prev = jnp.where(pos == 0, 0.0, pltpu.roll(cv, 1, axis=0))
    nxt = jnp.where(pos == GRID_W - 1, 0.0, pltpu.roll(cv, tm - 1, axis=0))
    cw = convw_ref[...]
    y = b * (cw[0:1] * prev + cw[1:2] * cv + cw[2:3] * nxt)
    ssq = _dot((y * y).astype(_BF16), ones_ref[...])
    yn = y * lax.rsqrt(ssq * (1.0 / (CONV_W // CONV_HEADS)) + RMS_EPS) * gconv_ref[...]
    conv_ref[...] = yn.astype(_BF16)
    for q in range(N_SLABS):
        for half in range(CHUNK // BLOCKS):
            rows = [u_scr[q, pl.ds(half * BLOCKS + t, n_chunks, stride=CHUNK), :] for t in range(BLOCKS)]
            cols = _lane_block_transpose(rows)
            for g8 in range(BLOCKS):
                ug_ref[q * BLOCKS + g8, :, half * LANES:(half + 1) * LANES] = cols[g8].astype(_BF16)


def _in_proj(x2d, mods, mod_row, g1, w_in_bf, conv_w, g_conv, ones_head, tm):
    n = x2d.shape[0]
    const = lambda i: (0, 0)
    return pl.pallas_call(
        functools.partial(_in_kernel, mod_row),
        grid=(n // tm,),
        in_specs=[pl.BlockSpec((tm, D_MODEL), lambda i: (i, 0)),
                  pl.BlockSpec(mods.shape, const),
                  pl.BlockSpec((1, D_MODEL), const),
                  pl.BlockSpec((D_MODEL, IN_W), const),
                  pl.BlockSpec((3, CONV_W), const),
                  pl.BlockSpec((1, CONV_W), const),
                  pl.BlockSpec((CONV_W, CONV_W), const)],
        out_specs=[pl.BlockSpec((tm, CONV_W), lambda i: (i, 0)),
                   pl.BlockSpec((SSM_GROUPS, tm // CHUNK, CHUNK_W), lambda i: (0, i, 0))],
        out_shape=[jax.ShapeDtypeStruct((n, CONV_W), _BF16),
                   jax.ShapeDtypeStruct((SSM_GROUPS, n // CHUNK, CHUNK_W), _BF16)],
        scratch_shapes=[pltpu.VMEM((N_SLABS, tm, LANES), _F32)],
        compiler_params=pltpu.CompilerParams(dimension_semantics=("arbitrary",),
                                             vmem_limit_bytes=VMEM_LIMIT),
        name="in_proj",
    )(x2d, mods, g1, w_in_bf, conv_w, g_conv, ones_head)


def _scan_steps(n_rows):
    return max(1, math.ceil(math.log2(n_rows)))


def _cmul(ar, ai, br, bi):
    return ar * br - ai * bi, ar * bi + ai * br


def _ssm_prep(n_steps, par_ref, btr_ref, bti_ref, ctr_ref, cti_ref, wst_ref, wout_ref, cmat_ref):
    T = CHUNK
    left8 = lax.broadcasted_iota(jnp.int32, (SUBLANES, LANES), 1) < SSM_STATE
    row0 = lax.broadcasted_iota(jnp.int32, (SUBLANES, LANES), 0) == 0

    def member_rows(vf, vb):
        f8 = jnp.broadcast_to(vf, (SUBLANES, LANES))
        b8 = pltpu.roll(jnp.broadcast_to(vb, (SUBLANES, LANES)), SSM_STATE, axis=1)
        return jnp.where(row0, jnp.where(left8, f8, b8), jnp.where(left8, b8, f8))

    pf, pb = par_ref[0, 0], par_ref[1, 0]
    lr = jnp.minimum(member_rows(pf[0:1], pb[0:1]), -1e-4)
    li = member_rows(pf[1:2], pb[1:2])
    dt = jnp.exp(member_rows(pf[2:3], pb[2:3]))
    mag = jnp.exp(lr * dt)
    ab_re = mag * jnp.cos(li * dt)
    ab_im = mag * jnp.sin(li * dt)
    nr = ab_re - 1.0
    den = lr * lr + li * li
    f_re = (nr * lr + ab_im * li) / den
    f_im = (ab_im * lr - nr * li) / den
    pt_re, pt_im = [jnp.ones_like(ab_re)], [jnp.zeros_like(ab_im)]
    for _ in range(T):
        nxt = _cmul(pt_re[-1], pt_im[-1], ab_re, ab_im)
        pt_re.append(nxt[0])
        pt_im.append(nxt[1])

    cols = jnp.concatenate([ab_re, ab_im, jnp.zeros((LANES - 2 * SUBLANES, LANES), _F32)], axis=0).T
    left16 = lax.broadcasted_iota(jnp.int32, (SSM_GROUP, LANES), 1) < SSM_STATE
    left1 = left16[0:1]
    rows = lax.broadcasted_iota(jnp.int32, (LANES, CHUNK_W), 0)
    tcol = lax.broadcasted_iota(jnp.int32, (LANES, CHUNK_W), 1) // SSM_GROUP
    zero_half = jnp.zeros((SSM_STATE, CHUNK_W), _BF16)
    for e in range(PAIR):
        fwd_first = e == 0
        bf_re, bf_im = btr_ref[0, 0], bti_ref[0, 0]
        bb_re = pltpu.roll(btr_ref[1, 0], SSM_STATE, axis=1)
        bb_im = pltpu.roll(bti_ref[1, 0], SSM_STATE, axis=1)
        pick = left16 if fwd_first else jnp.logical_not(left16)
        b_re, b_im = jnp.where(pick, bf_re, bb_re), jnp.where(pick, bf_im, bb_im)
        bbar_re, bbar_im = _cmul(f_re[e:e + 1], f_im[e:e + 1], b_re, b_im)
        fwd_lane = left1 if fwd_first else jnp.logical_not(left1)
        for s in range(T):
            pr = jnp.where(fwd_lane, pt_re[T - 1 - s][e:e + 1], pt_re[s][e:e + 1])
            pi = jnp.where(fwd_lane, pt_im[T - 1 - s][e:e + 1], pt_im[s][e:e + 1])
            w_re, w_im = _cmul(pr, pi, bbar_re, bbar_im)
            wst_ref[e, s * SSM_GROUP:(s + 1) * SSM_GROUP, 0:LANES] = w_re.astype(_BF16)
            wst_ref[e, s * SSM_GROUP:(s + 1) * SSM_GROUP, LANES:] = w_im.astype(_BF16)
        fwd_row = (rows < SSM_STATE) if fwd_first else (rows >= SSM_STATE)
        expo = jnp.where(fwd_row, tcol + 1, T - tcol)
        q_re = jnp.ones((LANES, CHUNK_W), _F32)
        q_im = jnp.zeros((LANES, CHUNK_W), _F32)
        s_re, s_im = cols[:, e:e + 1], cols[:, SUBLANES + e:SUBLANES + e + 1]
        for bit in range(T.bit_length()):
            m_re, m_im = _cmul(q_re, q_im, jnp.broadcast_to(s_re, q_re.shape), jnp.broadcast_to(s_im, q_re.shape))
            take = ((expo >> bit) & 1) == 1
            q_re, q_im = jnp.where(take, m_re, q_re), jnp.where(take, m_im, q_im)
            s_re, s_im = _cmul(s_re, s_im, s_re, s_im)
        d0, d1 = (0, 1) if fwd_first else (1, 0)
        c_re = jnp.concatenate([ctr_ref[e, d0], ctr_ref[e, d1]], axis=0)
        c_im = jnp.concatenate([cti_ref[e, d0], cti_ref[e, d1]], axis=0)
        o_re, o_im = _cmul(c_re, c_im, q_re, q_im)
        wout_ref[e, 0:LANES] = o_re.astype(_BF16)
        wout_ref[e, LANES:] = (-o_im).astype(_BF16)
        for ri, src in enumerate((c_re, -c_im)):
            for slot, d in enumerate((d0, d1)):
                r0 = ri * LANES + slot * SSM_STATE
                block = src[slot * SSM_STATE:(slot + 1) * SSM_STATE].astype(_BF16)
                cmat_ref[e, r0:r0 + SSM_STATE, d * CHUNK_W:(d + 1) * CHUNK_W] = block
                cmat_ref[e, r0:r0 + SSM_STATE, (1 - d) * CHUNK_W:(2 - d) * CHUNK_W] = zero_half

    m0_re, m1_re = pt_re[T][0:1], pt_re[T][1:2]
    m0_im, m1_im = pt_im[T][0:1], pt_im[T][1:2]
    cf = (jnp.where(left1, m0_re, m1_re), jnp.where(left1, m0_im, m1_im))
    cb = (jnp.where(left1, m1_re, m0_re), jnp.where(left1, m1_im, m0_im))
    coefs = []
    for _ in range(n_steps):
        coefs.append([cf[0], cf[1], cb[0], cb[1]])
        cf = _cmul(cf[0], cf[1], cf[0], cf[1])
        cb = _cmul(cb[0], cb[1], cb[0], cb[1])
    return coefs


def _ssm_kernel(n_lat, n_ctx, u_ref, uc_ref, par_ref, btr_ref, bti_ref, ctr_ref, cti_ref, dmat_ref,
                y_ref, buf_a, buf_b, wst_ref, wout_ref, cmat_ref, m_scr):
    n_rows = n_lat + n_ctx
    pad = SUBLANES
    coefs = _ssm_prep(_scan_steps(n_rows), par_ref, btr_ref, bti_ref, ctr_ref, cti_ref,
                      wst_ref, wout_ref, cmat_ref)

    blk = lax.broadcasted_iota(jnp.int32, (SSM_GROUP, CHUNK_W), 1) // SSM_GROUP
    for e in range(PAIR):
        taps = _dot(wst_ref[e], cmat_ref[e])
        taps_f = taps[:, 0:CHUNK_W]
        taps_b = taps[:, CHUNK_W:]
        for s in range(CHUNK):
            acc = jnp.zeros((SSM_GROUP, CHUNK_W), _F32)
            for t in range(CHUNK):
                val = None
                if s >= t:
                    r0 = (s - t) * SSM_GROUP
                    val = taps_b[r0:r0 + SSM_GROUP]
                if s <= t:
                    r0 = (CHUNK - 1 - t + s) * SSM_GROUP
                    v2 = taps_f[r0:r0 + SSM_GROUP]
                    val = v2 if val is None else val + v2
                if s == t:
                    val = val + dmat_ref[0, e]
                acc = jnp.where(blk == t, val, acc)
            m_scr[e, s * SSM_GROUP:(s + 1) * SSM_GROUP, :] = acc.astype(_BF16)

    def paired(m0, m1):
        left = lax.broadcasted_iota(jnp.int32, (m0.shape[0], PAIR_STATE), 1) < SSM_STATE
        re0, im0 = m0[:, 0:PAIR_STATE], m0[:, PAIR_STATE:]
        re1, im1 = m1[:, 0:PAIR_STATE], m1[:, PAIR_STATE:]
        return [jnp.where(left, re0, re1), jnp.where(left, im0, im1),
                jnp.where(left, re1, re0), jnp.where(left, im1, im0)]

    s_lat = paired(_dot(u_ref[0], wst_ref[0]), _dot(u_ref[1], wst_ref[1]))
    s_ctx = paired(_dot(uc_ref[0], wst_ref[0]), _dot(uc_ref[1], wst_ref[1]))

    zeros_pad = jnp.zeros((pad, PAIR_STATE), _F32)
    for buf in (buf_a, buf_b):
        for comp in range(N_COMP):
            buf[comp, 0:pad] = zeros_pad
            buf[comp, pad + n_rows:pad + n_rows + pad] = zeros_pad
    for comp in range(2):
        buf_a[comp, pad:pad + n_ctx] = s_ctx[comp]
        buf_a[comp, pad + n_ctx:pad + n_rows] = s_lat[comp]
    for comp in range(2, 4):
        buf_a[comp, pad:pad + n_lat] = s_lat[comp]
        buf_a[comp, pad + n_lat:pad + n_rows] = s_ctx[comp]

    src, dst = buf_a, buf_b
    for k in range(_scan_steps(n_rows)):
        sh = 1 << k
        coef = coefs[k]
        if sh < SUBLANES:
            lo, hi = pad, pad + n_rows
        else:
            lo, hi = pad + sh, pad + n_rows
        xr, xi = src[0, lo - sh:hi - sh], src[1, lo - sh:hi - sh]
        dst[0, lo:hi] = src[0, lo:hi] + coef[0] * xr - coef[1] * xi
        dst[1, lo:hi] = src[1, lo:hi] + coef[0] * xi + coef[1] * xr
        if sh < SUBLANES:
            blo, bhi = pad, pad + n_rows
        else:
            blo, bhi = pad, pad + n_rows - sh
        xr, xi = src[2, blo + sh:bhi + sh], src[3, blo + sh:bhi + sh]
        dst[2, blo:bhi] = src[2, blo:bhi] + coef[2] * xr - coef[3] * xi
        dst[3, blo:bhi] = src[3, blo:bhi] + coef[2] * xi + coef[3] * xr
        if sh >= SUBLANES:
            for comp in range(2):
                dst[comp, pad:pad + sh] = src[comp, pad:pad + sh]
            for comp in range(2, 4):
                dst[comp, pad + n_rows - sh:pad + n_rows] = src[comp, pad + n_rows - sh:pad + n_rows]
        src, dst = dst, src

    f0 = pad + n_ctx - 1
    b0 = pad + 1
    f_re, f_im = src[0, f0:f0 + n_lat], src[1, f0:f0 + n_lat]
    b_re, b_im = src[2, b0:b0 + n_lat], src[3, b0:b0 + n_lat]
    left = lax.broadcasted_iota(jnp.int32, (n_lat, PAIR_STATE), 1) < SSM_STATE
    h0 = jnp.concatenate([jnp.where(left, f_re, b_re), jnp.where(left, f_im, b_im)], axis=1)
    h1 = jnp.concatenate([jnp.where(left, b_re, f_re), jnp.where(left, b_im, f_im)], axis=1)
    y_ref[0] = _dot(u_ref[0], m_scr[0]) + _dot(h0.astype(_BF16), wout_ref[0])
    y_ref[1] = _dot(u_ref[1], m_scr[1]) + _dot(h1.astype(_BF16), wout_ref[1])


def _ssm(u_g, uc_g, par, bt_re, bt_im, ct_re, ct_im, dmat):
    n_groups, n_lat, _ = u_g.shape
    n_ctx = uc_g.shape[1]
    n_rows = n_lat + n_ctx
    n_pairs = n_groups // PAIR
    buf = pltpu.VMEM((N_COMP, n_rows + 2 * SUBLANES, PAIR_STATE), _F32)
    mat = pltpu.VMEM((PAIR, CHUNK_W, CHUNK_W), _BF16)
    pair3 = lambda q: (q, 0, 0)
    pair4 = lambda q: (q, 0, 0, 0)
    by_dir = lambda q: (0, q, 0, 0)
    return pl.pallas_call(
        functools.partial(_ssm_kernel, n_lat, n_ctx),
        grid=(n_pairs,),
        in_specs=[pl.BlockSpec((PAIR, n_lat, CHUNK_W), pair3),
                  pl.BlockSpec((PAIR, n_ctx, CHUNK_W), pair3),
                  pl.BlockSpec((2, 1) + par.shape[2:], by_dir),
                  pl.BlockSpec((2, 1) + bt_re.shape[2:], by_dir),
                  pl.BlockSpec((2, 1) + bt_im.shape[2:], by_dir),
                  pl.BlockSpec((PAIR,) + ct_re.shape[1:], pair4),
                  pl.BlockSpec((PAIR,) + ct_im.shape[1:], pair4),
                  pl.BlockSpec((1,) + dmat.shape[1:], pair4)],
        out_specs=pl.BlockSpec((PAIR, n_lat, CHUNK_W), pair3),
        out_shape=jax.ShapeDtypeStruct((n_groups, n_lat, CHUNK_W), _F32),
        scratch_shapes=[buf, buf, mat, mat, pltpu.VMEM((PAIR, CHUNK_W, 2 * CHUNK_W), _BF16), mat],
        compiler_params=pltpu.CompilerParams(dimension_semantics=("arbitrary",),
                                             vmem_limit_bytes=VMEM_LIMIT),
        name="ssm",
    )(u_g, uc_g, par, bt_re, bt_im, ct_re, ct_im, dmat)


def _out_kernel(x_ref, conv_ref, yg_ref, mods_ref, gssm_ref, g2_ref, gfin_ref,
                wglu_ref, wout_ref, w1_ref, w2_ref, ones_ref, o_ref, y_scr):
    x = x_ref[...]
    n_chunks = x.shape[0] // CHUNK
    gt1 = mods_ref[0:1, 2 * D_MODEL:3 * D_MODEL]
    sh2 = mods_ref[0:1, 3 * D_MODEL:4 * D_MODEL]
    sc2 = mods_ref[0:1, 4 * D_MODEL:5 * D_MODEL]
    gt2 = mods_ref[0:1, 5 * D_MODEL:6 * D_MODEL]
    for q in range(N_SLABS):
        for half in range(CHUNK // BLOCKS):
            cols = [yg_ref[q * BLOCKS + g8, :, half * LANES:(half + 1) * LANES] for g8 in range(BLOCKS)]
            rows = _lane_block_transpose(cols)
            for t in range(BLOCKS):
                y_scr[q, pl.ds(half * BLOCKS + t, n_chunks, stride=CHUNK), :] = rows[t]
    s = jax.nn.gelu(jnp.concatenate([y_scr[q] for q in range(N_SLABS)], axis=1))
    s = s * jax.nn.sigmoid(_dot(s.astype(_BF16), wglu_ref[...]))
    ssq = _dot((s * s).astype(_BF16), ones_ref[...])
    sn = s * lax.rsqrt(ssq * (1.0 / SSM_GROUP) + RMS_EPS) * gssm_ref[...]
    mix = _dot(conv_ref[...], wout_ref[0:CONV_W]) + _dot(sn.astype(_BF16), wout_ref[CONV_W:])
    x_mid = x + gt1 * mix
    h2 = (_rms_rows(x_mid, g2_ref[...]) * (1.0 + sc2) + sh2).astype(_BF16)
    n_split = 4
    ff = D_FF // n_split
    acc = None
    for j in range(n_split):
        a = jnp.maximum(_dot(h2, w1_ref[:, j * ff:(j + 1) * ff]), 0.0)
        part = _dot((a * a).astype(_BF16), w2_ref[j * ff:(j + 1) * ff])
        acc = part if acc is None else acc + part
    x_out = x_mid + gt2 * acc
    o_ref[...] = _rms_rows(x_out, gfin_ref[...])


def _out_proj(x2d, conv_n, y_g, mods, g_ssm, g2, g_fin, wglu, wout, w1, w2, ones_grp, tm):
    n = x2d.shape[0]
    const = lambda i: (0, 0)
    row = lambda i: (i, 0)
    resident = lambda shape: pl.BlockSpec(shape, const, pipeline_mode=pl.Buffered(1))
    return pl.pallas_call(
        _out_kernel,
        grid=(n // tm,),
        in_specs=[pl.BlockSpec((tm, D_MODEL), row),
                  pl.BlockSpec((tm, CONV_W), row),
                  pl.BlockSpec((SSM_GROUPS, tm // CHUNK, CHUNK_W), lambda i: (0, i, 0)),
                  pl.BlockSpec(mods.shape, const),
                  pl.BlockSpec((1, SSM_W), const),
                  pl.BlockSpec((1, D_MODEL), const),
                  pl.BlockSpec((1, D_MODEL), const),
                  resident((SSM_W, SSM_W)),
                  resident((D_MODEL, D_MODEL)),
                  resident((D_MODEL, D_FF)),
                  resident((D_FF, D_MODEL)),
                  resident((SSM_W, SSM_W))],
        out_specs=pl.BlockSpec((tm, D_MODEL), row),
        out_shape=jax.ShapeDtypeStruct((n, D_MODEL), _F32),
        scratch_shapes=[pltpu.VMEM((N_SLABS, tm, LANES), _F32)],
        compiler_params=pltpu.CompilerParams(dimension_semantics=("arbitrary",),
                                             vmem_limit_bytes=VMEM_LIMIT),
        name="out_proj",
    )(x2d, conv_n, y_g, mods, g_ssm, g2, g_fin, wglu, wout, w1, w2, ones_grp)


def _ssm_params(a_re, a_im, log_dt, b_re, b_im, c_re, c_im, ssm_d):
    G, P, J, T = SSM_GROUPS, SSM_STATE, SSM_GROUP, CHUNK
    n_pairs = G // PAIR
    pair_lanes = lambda v: v.reshape(2, n_pairs, PAIR * P)
    dt_lanes = jnp.broadcast_to(log_dt[..., None], (2, G, P))
    par = jnp.stack([pair_lanes(a_re), pair_lanes(a_im), pair_lanes(dt_lanes)], axis=2)
    b_lanes = lambda b: jnp.transpose(b.reshape(2, n_pairs, PAIR, P, J), (0, 1, 4, 2, 3)).reshape(
        2, n_pairs, J, PAIR * P)
    c_tiled = lambda c: jnp.broadcast_to(jnp.transpose(c, (1, 0, 3, 2))[:, :, :, None, :],
                                         (G, 2, P, T, J)).reshape(G, 2, P, T * J)
    dmat = ssm_d.reshape(G, J, 1, 1) * jnp.eye(J, dtype=_F32)[None, :, None, :]
    dmat = jnp.broadcast_to(dmat, (G, J, T, J)).reshape(n_pairs, PAIR, J, T * J)
    return par, b_lanes(b_re), b_lanes(b_im), c_tiled(c_re), c_tiled(c_im), dmat


def _group_ones(width, group):
    idx = jnp.arange(width) // group
    return (idx[:, None] == idx[None, :]).astype(_BF16)


def kernel(x, c, ctx, c_ctx, w_mod, b_mod, g_norm1, w_in, conv_w, ssm_a_re, ssm_a_im, ssm_log_dt,
           ssm_b_re, ssm_b_im, ssm_c_re, ssm_c_im, ssm_d, w_glu, g_conv_out, g_ssm_out, w_out,
           g_norm2, w_mlp1, w_mlp2, g_final):
    bsz, n_lat, d_model = x.shape
    n_ctx = ctx.shape[1]
    assert bsz == 1 and d_model == D_MODEL and w_mod.shape[0] == 1
    assert n_lat % (CHUNK * SUBLANES) == 0 and n_ctx % (CHUNK * CHUNK) == 0 and n_lat % GRID_W == 0
    layer = 0
    x2d = x[0]
    ctx2d = ctx[0]

    cond_rows = jnp.zeros((SUBLANES, D_MODEL), _F32).at[0].set(c[0]).at[1].set(c_ctx)
    mods = _modulation(cond_rows, w_mod[layer], b_mod[layer][None, :])

    g1 = g_norm1[layer][None, :]
    w_in_bf = w_in[layer].astype(_BF16)
    ones_head = _group_ones(CONV_W, CONV_W // CONV_HEADS)
    g_conv = g_conv_out[layer][None, :]
    conv_n, u_g = _in_proj(x2d, mods, 0, g1, w_in_bf, conv_w[layer], g_conv, ones_head, tm=512)
    _, uc_g = _in_proj(ctx2d, mods, 1, g1, w_in_bf, conv_w[layer], g_conv, ones_head, tm=n_ctx)

    y_g = _ssm(u_g, uc_g, *_ssm_params(
        ssm_a_re[layer], ssm_a_im[layer], ssm_log_dt[layer], ssm_b_re[layer], ssm_b_im[layer],
        ssm_c_re[layer], ssm_c_im[layer], ssm_d[layer]))

    out = _out_proj(x2d, conv_n, y_g, mods, g_ssm_out[layer][None, :], g_norm2[layer][None, :],
                    g_final[None, :], w_glu[layer].astype(_BF16), w_out[layer].astype(_BF16),
                    w_mlp1[layer].astype(_BF16), w_mlp2[layer].astype(_BF16),
                    _group_ones(SSM_W, SSM_GROUP), tm=512)
    return out[None]
```

```python
import functools
import math

import jax
import jax.numpy as jnp
from jax import lax
from jax.experimental import pallas as pl
from jax.experimental.pallas import tpu as pltpu

D_MODEL = 1024
GRID_W = 64
CONV_W = 512
CONV_HEADS = 8
SSM_W = 512
SSM_GROUP = 16
SSM_GROUPS = 32
SSM_STATE = 64
IN_W = 3 * CONV_W + SSM_W
D_FF = 4 * D_MODEL
N_MOD = 6
RMS_EPS = 1e-6

LANES = 128
MXU_W = 256
SUBLANES = 8
CHUNK = 16
CHUNK_W = CHUNK * SSM_GROUP
PAIR = 2
PAIR_STATE = PAIR * SSM_STATE
N_COMP = 4
BLOCKS = LANES // SSM_GROUP
N_SLABS = SSM_W // LANES
VMEM_LIMIT = 56 * 1024 * 1024

_BF16 = jnp.bfloat16
_F32 = jnp.float32


def _dot(a, b):
    return jnp.dot(a, b, preferred_element_type=_F32)


def _lane_block_transpose(vs):
    blk = lax.broadcasted_iota(jnp.int32, vs[0].shape, 1) // SSM_GROUP
    dist = BLOCKS // 2
    while dist:
        shift = dist * SSM_GROUP
        upper = (blk & dist) != 0
        out = [None] * BLOCKS
        for a in range(BLOCKS):
            if a & dist:
                continue
            lo, hi = vs[a], vs[a + dist]
            out[a] = jnp.where(upper, pltpu.roll(hi, shift, axis=1), lo)
            out[a + dist] = jnp.where(upper, hi, pltpu.roll(lo, LANES - shift, axis=1))
        vs = out
        dist //= 2
    return vs


def _mod_kernel(s_ref, w_ref, b_ref, o_ref):
    s = s_ref[...]
    act = s * jax.nn.sigmoid(s)
    o_ref[...] = jnp.dot(act, w_ref[...], preferred_element_type=_F32,
                         precision=lax.Precision.HIGHEST) + b_ref[...]


def _modulation(cond_rows, w_mod, b_mod):
    n_out = w_mod.shape[1]
    tn = 1024
    return pl.pallas_call(
        _mod_kernel,
        grid=(n_out // tn,),
        in_specs=[pl.BlockSpec((SUBLANES, D_MODEL), lambda j: (0, 0)),
                  pl.BlockSpec((D_MODEL, tn), lambda j: (0, j)),
                  pl.BlockSpec((1, tn), lambda j: (0, j))],
        out_specs=pl.BlockSpec((SUBLANES, tn), lambda j: (0, j)),
        out_shape=jax.ShapeDtypeStruct((SUBLANES, n_out), _F32),
        compiler_params=pltpu.CompilerParams(dimension_semantics=("arbitrary",),
                                             vmem_limit_bytes=VMEM_LIMIT),
        name="mod",
    )(cond_rows, w_mod, b_mod)


def _rms_rows(x, g):
    ms = jnp.mean(x * x, axis=-1, keepdims=True)
    return x * lax.rsqrt(ms + RMS_EPS) * g


def _in_kernel(mod_row, n_sub, x_ref, mods_ref, g1_ref, w_in_ref, convw_ref, gconv_ref, ones_ref,
               conv_ref, ug_ref, u_scr):
    sh1 = mods_ref[mod_row:mod_row + 1, 0:D_MODEL]
    sc1 = mods_ref[mod_row:mod_row + 1, D_MODEL:2 * D_MODEL]
    ts = x_ref.shape[0] // n_sub
    n_chunks = ts // CHUNK
    pos = lax.broadcasted_iota(jnp.int32, (ts, MXU_W), 0) % GRID_W
    for sb in range(n_sub):
        r0 = sb * ts
        h = _rms_rows(x_ref[r0:r0 + ts, :], g1_ref[...]) * (1.0 + sc1) + sh1
        z = _dot(h.astype(_BF16), w_in_ref[...])
        for q in range(N_SLABS):
            u_scr[q, r0:r0 + ts] = z[:, q * LANES:(q + 1) * LANES]
        for k in range(CONV_W // MXU_W):
            c0 = SSM_W + 3 * k * MXU_W
            lanes = slice(k * MXU_W, (k + 1) * MXU_W)
            b = z[:, c0:c0 + MXU_W]
            cv = z[:, c0 + MXU_W:c0 + 2 * MXU_W] * z[:, c0 + 2 * MXU_W:c0 + 3 * MXU_W]
            prev = jnp.where(pos == 0, 0.0, pltpu.roll(cv, 1, axis=0))
            nxt = jnp.where(pos == GRID_W - 1, 0.0, pltpu.roll(cv, ts - 1, axis=0))
            y = b * (convw_ref[0:1, lanes] * prev + convw_ref[1:2, lanes] * cv + convw_ref[2:3, lanes] * nxt)
            ssq = _dot((y * y).astype(_BF16), ones_ref[...])
            yn = y * lax.rsqrt(ssq * (1.0 / (CONV_W // CONV_HEADS)) + RMS_EPS) * gconv_ref[:, lanes]
            conv_ref[r0:r0 + ts, lanes] = yn.astype(_BF16)
        ch0 = sb * n_chunks
        for q in range(N_SLABS):
            for half in range(CHUNK // BLOCKS):
                rows = [u_scr[q, pl.ds(r0 + half * BLOCKS + t, n_chunks, stride=CHUNK), :]
                        for t in range(BLOCKS)]
                cols = _lane_block_transpose(rows)
                for g8 in range(BLOCKS):
                    ug_ref[q * BLOCKS + g8, ch0:ch0 + n_chunks, half * LANES:(half + 1) * LANES] = (
                        cols[g8].astype(_BF16))


def _in_proj(x2d, mods, mod_row, g1, w_in_bf, conv_w, g_conv, ones_head, tm, n_sub):
    n = x2d.shape[0]
    const = lambda i: (0, 0)
    return pl.pallas_call(
        functools.partial(_in_kernel, mod_row, n_sub),
        grid=(n // tm,),
        in_specs=[pl.BlockSpec((tm, D_MODEL), lambda i: (i, 0)),
                  pl.BlockSpec(mods.shape, const),
                  pl.BlockSpec((1, D_MODEL), const),
                  pl.BlockSpec((D_MODEL, IN_W), const),
                  pl.BlockSpec((3, CONV_W), const),
                  pl.BlockSpec((1, CONV_W), const),
                  pl.BlockSpec((MXU_W, MXU_W), const)],
        out_specs=[pl.BlockSpec((tm, CONV_W), lambda i: (i, 0)),
                   pl.BlockSpec((SSM_GROUPS, tm // CHUNK, CHUNK_W), lambda i: (0, i, 0))],
        out_shape=[jax.ShapeDtypeStruct((n, CONV_W), _BF16),
                   jax.ShapeDtypeStruct((SSM_GROUPS, n // CHUNK, CHUNK_W), _BF16)],
        scratch_shapes=[pltpu.VMEM((N_SLABS, tm, LANES), _F32)],
        compiler_params=pltpu.CompilerParams(dimension_semantics=("arbitrary",),
                                             vmem_limit_bytes=VMEM_LIMIT),
        name="in_proj",
    )(x2d, mods, g1, w_in_bf, conv_w, g_conv, ones_head)


def _scan_steps(n_rows):
    return max(1, math.ceil(math.log2(n_rows)))


def _cmul(ar, ai, br, bi):
    return ar * br - ai * bi, ar * bi + ai * br


def _ssm_prep(n_steps, par_ref, btr_ref, bti_ref, ctr_ref, cti_ref, wst_ref, wout_ref, cmat_ref):
    T = CHUNK
    left8 = lax.broadcasted_iota(jnp.int32, (SUBLANES, LANES), 1) < SSM_STATE
    row0 = lax.broadcasted_iota(jnp.int32, (SUBLANES, LANES), 0) == 0

    def member_rows(vf, vb):
        f8 = jnp.broadcast_to(vf, (SUBLANES, LANES))
        b8 = pltpu.roll(jnp.broadcast_to(vb, (SUBLANES, LANES)), SSM_STATE, axis=1)
        return jnp.where(row0, jnp.where(left8, f8, b8), jnp.where(left8, b8, f8))

    pf, pb = par_ref[0, 0], par_ref[1, 0]
    lr = jnp.minimum(member_rows(pf[0:1], pb[0:1]), -1e-4)
    li = member_rows(pf[1:2], pb[1:2])
    dt = jnp.exp(member_rows(pf[2:3], pb[2:3]))
    mag = jnp.exp(lr * dt)
    ab_re = mag * jnp.cos(li * dt)
    ab_im = mag * jnp.sin(li * dt)
    nr = ab_re - 1.0
    den = lr * lr + li * li
    f_re = (nr * lr + ab_im * li) / den
    f_im = (ab_im * lr - nr * li) / den
    pt_re, pt_im = [jnp.ones_like(ab_re)], [jnp.zeros_like(ab_im)]
    for _ in range(T):
        nxt = _cmul(pt_re[-1], pt_im[-1], ab_re, ab_im)
        pt_re.append(nxt[0])
        pt_im.append(nxt[1])

    cols = jnp.concatenate([ab_re, ab_im, jnp.zeros((LANES - 2 * SUBLANES, LANES), _F32)], axis=0).T
    left16 = lax.broadcasted_iota(jnp.int32, (SSM_GROUP, LANES), 1) < SSM_STATE
    left1 = left16[0:1]
    rows = lax.broadcasted_iota(jnp.int32, (LANES, CHUNK_W), 0)
    tcol = lax.broadcasted_iota(jnp.int32, (LANES, CHUNK_W), 1) // SSM_GROUP
    zero_half = jnp.zeros((SSM_STATE, CHUNK_W), _BF16)
    for e in range(PAIR):
        fwd_first = e == 0
        bf_re, bf_im = btr_ref[0, 0], bti_ref[0, 0]
        bb_re = pltpu.roll(btr_ref[1, 0], SSM_STATE, axis=1)
        bb_im = pltpu.roll(bti_ref[1, 0], SSM_STATE, axis=1)
        pick = left16 if fwd_first else jnp.logical_not(left16)
        b_re, b_im = jnp.where(pick, bf_re, bb_re), jnp.where(pick, bf_im, bb_im)
        bbar_re, bbar_im = _cmul(f_re[e:e + 1], f_im[e:e + 1], b_re, b_im)
        fwd_lane = left1 if fwd_first else jnp.logical_not(left1)
        for s in range(T):
            pr = jnp.where(fwd_lane, pt_re[T - 1 - s][e:e + 1], pt_re[s][e:e + 1])
            pi = jnp.where(fwd_lane, pt_im[T - 1 - s][e:e + 1], pt_im[s][e:e + 1])
            w_re, w_im = _cmul(pr, pi, bbar_re, bbar_im)
            wst_ref[e, s * SSM_GROUP:(s + 1) * SSM_GROUP, 0:LANES] = w_re.astype(_BF16)
            wst_ref[e, s * SSM_GROUP:(s + 1) * SSM_GROUP, LANES:] = w_im.astype(_BF16)
        fwd_row = (rows < SSM_STATE) if fwd_first else (rows >= SSM_STATE)
        expo = jnp.where(fwd_row, tcol + 1, T - tcol)
        q_re = jnp.ones((LANES, CHUNK_W), _F32)
        q_im = jnp.zeros((LANES, CHUNK_W), _F32)
        s_re, s_im = cols[:, e:e + 1], cols[:, SUBLANES + e:SUBLANES + e + 1]
        for bit in range(T.bit_length()):
            m_re, m_im = _cmul(q_re, q_im, jnp.broadcast_to(s_re, q_re.shape), jnp.broadcast_to(s_im, q_re.shape))
            take = ((expo >> bit) & 1) == 1
            q_re, q_im = jnp.where(take, m_re, q_re), jnp.where(take, m_im, q_im)
            s_re, s_im = _cmul(s_re, s_im, s_re, s_im)
        d0, d1 = (0, 1) if fwd_first else (1, 0)
        c_re = jnp.concatenate([ctr_ref[e, d0], ctr_ref[e, d1]], axis=0)
        c_im = jnp.concatenate([cti_ref[e, d0], cti_ref[e, d1]], axis=0)
        o_re, o_im = _cmul(c_re, c_im, q_re, q_im)
        wout_ref[e, 0:LANES] = o_re.astype(_BF16)
        wout_ref[e, LANES:] = (-o_im).astype(_BF16)
        for ri, src in enumerate((c_re, -c_im)):
            for slot, d in enumerate((d0, d1)):
                r0 = ri * LANES + slot * SSM_STATE
                block = src[slot * SSM_STATE:(slot + 1) * SSM_STATE].astype(_BF16)
                cmat_ref[e, r0:r0 + SSM_STATE, d * CHUNK_W:(d + 1) * CHUNK_W] = block
                cmat_ref[e, r0:r0 + SSM_STATE, (1 - d) * CHUNK_W:(2 - d) * CHUNK_W] = zero_half

    m0_re, m1_re = pt_re[T][0:1], pt_re[T][1:2]
    m0_im, m1_im = pt_im[T][0:1], pt_im[T][1:2]
    cf = (jnp.where(left1, m0_re, m1_re), jnp.where(left1, m0_im, m1_im))
    cb = (jnp.where(left1, m1_re, m0_re), jnp.where(left1, m1_im, m0_im))
    coefs = []
    for _ in range(n_steps):
        coefs.append([cf[0], cf[1], cb[0], cb[1]])
        cf = _cmul(cf[0], cf[1], cf[0], cf[1])
        cb = _cmul(cb[0], cb[1], cb[0], cb[1])
    return coefs


def _ssm_kernel(n_lat, n_ctx, u_ref, uc_ref, par_ref, btr_ref, bti_ref, ctr_ref, cti_ref, dmat_ref,
                y_ref, buf_a, buf_b, wst_ref, wout_ref, cmat_ref, m_scr):
    n_rows = n_lat + n_ctx
    pad = SUBLANES
    coefs = _ssm_prep(_scan_steps(n_rows), par_ref, btr_ref, bti_ref, ctr_ref, cti_ref,
                      wst_ref, wout_ref, cmat_ref)

    blk = lax.broadcasted_iota(jnp.int32, (SSM_GROUP, CHUNK_W), 1) // SSM_GROUP
    for e in range(PAIR):
        taps = _dot(wst_ref[e], cmat_ref[e])
        taps_f = taps[:, 0:CHUNK_W]
        taps_b = taps[:, CHUNK_W:]
        for s in range(CHUNK):
            acc = jnp.zeros((SSM_GROUP, CHUNK_W), _F32)
            for t in range(CHUNK):
                val = None
                if s >= t:
                    r0 = (s - t) * SSM_GROUP
                    val = taps_b[r0:r0 + SSM_GROUP]
                if s <= t:
                    r0 = (CHUNK - 1 - t + s) * SSM_GROUP
                    v2 = taps_f[r0:r0 + SSM_GROUP]
                    val = v2 if val is None else val + v2
                if s == t:
                    val = val + dmat_ref[0, e]
                acc = jnp.where(blk == t, val, acc)
            m_scr[e, s * SSM_GROUP:(s + 1) * SSM_GROUP, :] = acc.astype(_BF16)

    def paired(m0, m1):
        left = lax.broadcasted_iota(jnp.int32, (m0.shape[0], PAIR_STATE), 1) < SSM_STATE
        re0, im0 = m0[:, 0:PAIR_STATE], m0[:, PAIR_STATE:]
        re1, im1 = m1[:, 0:PAIR_STATE], m1[:, PAIR_STATE:]
        return [jnp.where(left, re0, re1), jnp.where(left, im0, im1),
                jnp.where(left, re1, re0), jnp.where(left, im1, im0)]

    s_lat = paired(_dot(u_ref[0], wst_ref[0]), _dot(u_ref[1], wst_ref[1]))
    s_ctx = paired(_dot(uc_ref[0], wst_ref[0]), _dot(uc_ref[1], wst_ref[1]))

    zeros_pad = jnp.zeros((pad, PAIR_STATE), _F32)
    for buf in (buf_a, buf_b):
        for comp in range(N_COMP):
            buf[comp, 0:pad] = zeros_pad
            buf[comp, pad + n_rows:pad + n_rows + pad] = zeros_pad
    for comp in range(2):
        buf_a[comp, pad:pad + n_ctx] = s_ctx[comp]
        buf_a[comp, pad + n_ctx:pad + n_rows] = s_lat[comp]
    for comp in range(2, 4):
        buf_a[comp, pad:pad + n_lat] = s_lat[comp]
        buf_a[comp, pad + n_lat:pad + n_rows] = s_ctx[comp]

    src, dst = buf_a, buf_b
    for k in range(_scan_steps(n_rows)):
        sh = 1 << k
        coef = coefs[k]
        if sh < SUBLANES:
            lo, hi = pad, pad + n_rows
        else:
            lo, hi = pad + sh, pad + n_rows
        xr, xi = src[0, lo - sh:hi - sh], src[1, lo - sh:hi - sh]
        dst[0, lo:hi] = src[0, lo:hi] + coef[0] * xr - coef[1] * xi
        dst[1, lo:hi] = src[1, lo:hi] + coef[0] * xi + coef[1] * xr
        if sh < SUBLANES:
            blo, bhi = pad, pad + n_rows
        else:
            blo, bhi = pad, pad + n_rows - sh
        xr, xi = src[2, blo + sh:bhi + sh], src[3, blo + sh:bhi + sh]
        dst[2, blo:bhi] = src[2, blo:bhi] + coef[2] * xr - coef[3] * xi
        dst[3, blo:bhi] = src[3, blo:bhi] + coef[2] * xi + coef[3] * xr
        if sh >= SUBLANES:
            for comp in range(2):
                dst[comp, pad:pad + sh] = src[comp, pad:pad + sh]
            for comp in range(2, 4):
                dst[comp, pad + n_rows - sh:pad + n_rows] = src[comp, pad + n_rows - sh:pad + n_rows]
        src, dst = dst, src

    f0 = pad + n_ctx - 1
    b0 = pad + 1
    f_re, f_im = src[0, f0:f0 + n_lat], src[1, f0:f0 + n_lat]
    b_re, b_im = src[2, b0:b0 + n_lat], src[3, b0:b0 + n_lat]
    left = lax.broadcasted_iota(jnp.int32, (n_lat, PAIR_STATE), 1) < SSM_STATE
    h0 = jnp.concatenate([jnp.where(left, f_re, b_re), jnp.where(left, f_im, b_im)], axis=1)
    h1 = jnp.concatenate([jnp.where(left, b_re, f_re), jnp.where(left, b_im, f_im)], axis=1)
    y_ref[0] = _dot(u_ref[0], m_scr[0]) + _dot(h0.astype(_BF16), wout_ref[0])
    y_ref[1] = _dot(u_ref[1], m_scr[1]) + _dot(h1.astype(_BF16), wout_ref[1])


def _ssm(u_g, uc_g, par, bt_re, bt_im, ct_re, ct_im, dmat):
    n_groups, n_lat, _ = u_g.shape
    n_ctx = uc_g.shape[1]
    n_rows = n_lat + n_ctx
    n_pairs = n_groups // PAIR
    buf = pltpu.VMEM((N_COMP, n_rows + 2 * SUBLANES, PAIR_STATE), _F32)
    mat = pltpu.VMEM((PAIR, CHUNK_W, CHUNK_W), _BF16)
    pair3 = lambda q: (q, 0, 0)
    pair4 = lambda q: (q, 0, 0, 0)
    by_dir = lambda q: (0, q, 0, 0)
    return pl.pallas_call(
        functools.partial(_ssm_kernel, n_lat, n_ctx),
        grid=(n_pairs,),
        in_specs=[pl.BlockSpec((PAIR, n_lat, CHUNK_W), pair3),
                  pl.BlockSpec((PAIR, n_ctx, CHUNK_W), pair3),
                  pl.BlockSpec((2, 1) + par.shape[2:], by_dir),
                  pl.BlockSpec((2, 1) + bt_re.shape[2:], by_dir),
                  pl.BlockSpec((2, 1) + bt_im.shape[2:], by_dir),
                  pl.BlockSpec((PAIR,) + ct_re.shape[1:], pair4),
                  pl.BlockSpec((PAIR,) + ct_im.shape[1:], pair4),
                  pl.BlockSpec((1,) + dmat.shape[1:], pair4)],
        out_specs=pl.BlockSpec((PAIR, n_lat, CHUNK_W), pair3),
        out_shape=jax.ShapeDtypeStruct((n_groups, n_lat, CHUNK_W), _F32),
        scratch_shapes=[buf, buf, mat, mat, pltpu.VMEM((PAIR, CHUNK_W, 2 * CHUNK_W), _BF16), mat],
        compiler_params=pltpu.CompilerParams(dimension_semantics=("arbitrary",),
                                             vmem_limit_bytes=VMEM_LIMIT),
        name="ssm",
    )(u_g, uc_g, par, bt_re, bt_im, ct_re, ct_im, dmat)


def _out_kernel(x_ref, conv_ref, yg_ref, mods_ref, gssm_ref, g2_ref, gfin_ref,
                wglu_ref, wout_ref, w1_ref, w2_ref, ones_ref, o_ref, y_scr, mix_scr):
    x = x_ref[...]
    n_chunks = x.shape[0] // CHUNK
    gt1 = mods_ref[0:1, 2 * D_MODEL:3 * D_MODEL]
    sh2 = mods_ref[0:1, 3 * D_MODEL:4 * D_MODEL]
    sc2 = mods_ref[0:1, 4 * D_MODEL:5 * D_MODEL]
    gt2 = mods_ref[0:1, 5 * D_MODEL:6 * D_MODEL]
    mix_scr[...] = _dot(conv_ref[...], wout_ref[0:CONV_W])
    for q in range(N_SLABS):
        for half in range(CHUNK // BLOCKS):
            cols = [yg_ref[q * BLOCKS + g8, :, half * LANES:(half + 1) * LANES] for g8 in range(BLOCKS)]
            rows = _lane_block_transpose(cols)
            for t in range(BLOCKS):
                y_scr[q, pl.ds(half * BLOCKS + t, n_chunks, stride=CHUNK), :] = rows[t]
    s = jax.nn.gelu(jnp.concatenate([y_scr[q] for q in range(N_SLABS)], axis=1))
    s = s * jax.nn.sigmoid(_dot(s.astype(_BF16), wglu_ref[...]))
    sq = (s * s).astype(_BF16)
    ssq = jnp.concatenate([_dot(sq[:, k * MXU_W:(k + 1) * MXU_W], ones_ref[...])
                           for k in range(SSM_W // MXU_W)], axis=1)
    sn = s * lax.rsqrt(ssq * (1.0 / SSM_GROUP) + RMS_EPS) * gssm_ref[...]
    mix = mix_scr[...] + _dot(sn.astype(_BF16), wout_ref[CONV_W:])
    x_mid = x + gt1 * mix
    h2 = (_rms_rows(x_mid, g2_ref[...]) * (1.0 + sc2) + sh2).astype(_BF16)
    n_split = 4
    ff = D_FF // n_split
    acc = None
    for j in range(n_split):
        a = jnp.maximum(_dot(h2, w1_ref[:, j * ff:(j + 1) * ff]), 0.0)
        part = _dot((a * a).astype(_BF16), w2_ref[j * ff:(j + 1) * ff])
        acc = part if acc is None else acc + part
    x_out = x_mid + gt2 * acc
    o_ref[...] = _rms_rows(x_out, gfin_ref[...])


def _out_proj(x2d, conv_n, y_g, mods, g_ssm, g2, g_fin, wglu, wout, w1, w2, ones_grp, tm):
    n = x2d.shape[0]
    const = lambda i: (0, 0)
    row = lambda i: (i, 0)
    resident = lambda shape: pl.BlockSpec(shape, const, pipeline_mode=pl.Buffered(1))
    return pl.pallas_call(
        _out_kernel,
        grid=(n // tm,),
        in_specs=[pl.BlockSpec((tm, D_MODEL), row),
                  pl.BlockSpec((tm, CONV_W), row),
                  pl.BlockSpec((SSM_GROUPS, tm // CHUNK, CHUNK_W), lambda i: (0, i, 0)),
                  pl.BlockSpec(mods.shape, const),
                  pl.BlockSpec((1, SSM_W), const),
                  pl.BlockSpec((1, D_MODEL), const),
                  pl.BlockSpec((1, D_MODEL), const),
                  resident((SSM_W, SSM_W)),
                  resident((D_MODEL, D_MODEL)),
                  resident((D_MODEL, D_FF)),
                  resident((D_FF, D_MODEL)),
                  resident((MXU_W, MXU_W))],
        out_specs=pl.BlockSpec((tm, D_MODEL), row),
        out_shape=jax.ShapeDtypeStruct((n, D_MODEL), _F32),
        scratch_shapes=[pltpu.VMEM((N_SLABS, tm, LANES), _F32), pltpu.VMEM((tm, D_MODEL), _F32)],
        compiler_params=pltpu.CompilerParams(dimension_semantics=("arbitrary",),
                                             vmem_limit_bytes=VMEM_LIMIT),
        name="out_proj",
    )(x2d, conv_n, y_g, mods, g_ssm, g2, g_fin, wglu, wout, w1, w2, ones_grp)


def _ssm_params(a_re, a_im, log_dt, b_re, b_im, c_re, c_im, ssm_d):
    G, P, J, T = SSM_GROUPS, SSM_STATE, SSM_GROUP, CHUNK
    n_pairs = G // PAIR
    pair_lanes = lambda v: v.reshape(2, n_pairs, PAIR * P)
    dt_lanes = jnp.broadcast_to(log_dt[..., None], (2, G, P))
    par = jnp.stack([pair_lanes(a_re), pair_lanes(a_im), pair_lanes(dt_lanes)], axis=2)
    b_lanes = lambda b: jnp.transpose(b.reshape(2, n_pairs, PAIR, P, J), (0, 1, 4, 2, 3)).reshape(
        2, n_pairs, J, PAIR * P)
    c_tiled = lambda c: jnp.broadcast_to(jnp.transpose(c, (1, 0, 3, 2))[:, :, :, None, :],
                                         (G, 2, P, T, J)).reshape(G, 2, P, T * J)
    dmat = ssm_d.reshape(G, J, 1, 1) * jnp.eye(J, dtype=_F32)[None, :, None, :]
    dmat = jnp.broadcast_to(dmat, (G, J, T, J)).reshape(n_pairs, PAIR, J, T * J)
    return par, b_lanes(b_re), b_lanes(b_im), c_tiled(c_re), c_tiled(c_im), dmat


def _group_ones(width, group):
    idx = jnp.arange(width) // group
    return (idx[:, None] == idx[None, :]).astype(_BF16)


def kernel(x, c, ctx, c_ctx, w_mod, b_mod, g_norm1, w_in, conv_w, ssm_a_re, ssm_a_im, ssm_log_dt,
           ssm_b_re, ssm_b_im, ssm_c_re, ssm_c_im, ssm_d, w_glu, g_conv_out, g_ssm_out, w_out,
           g_norm2, w_mlp1, w_mlp2, g_final):
    bsz, n_lat, d_model = x.shape
    n_ctx = ctx.shape[1]
    assert bsz == 1 and d_model == D_MODEL and w_mod.shape[0] == 1
    assert n_lat % (CHUNK * SUBLANES) == 0 and n_ctx % (CHUNK * CHUNK) == 0 and n_lat % GRID_W == 0
    layer = 0
    x2d = x[0]
    ctx2d = ctx[0]

    cond_rows = jnp.zeros((SUBLANES, D_MODEL), _F32).at[0].set(c[0]).at[1].set(c_ctx)
    mods = _modulation(cond_rows, w_mod[layer], b_mod[layer][None, :])

    g1 = g_norm1[layer][None, :]
    w_l = w_in[layer]
    col_order = [w_l[:, 3 * CONV_W:]]
    for k in range(CONV_W // MXU_W):
        col_order += [w_l[:, part * CONV_W + k * MXU_W:part * CONV_W + (k + 1) * MXU_W] for part in range(3)]
    w_in_bf = jnp.concatenate(col_order, axis=1).astype(_BF16)
    ones_head = _group_ones(MXU_W, CONV_W // CONV_HEADS)
    g_conv = g_conv_out[layer][None, :]
    conv_n, u_g = _in_proj(x2d, mods, 0, g1, w_in_bf, conv_w[layer], g_conv, ones_head, tm=512, n_sub=1)
    _, uc_g = _in_proj(ctx2d, mods, 1, g1, w_in_bf, conv_w[layer], g_conv, ones_head, tm=n_ctx, n_sub=1)

    y_g = _ssm(u_g, uc_g, *_ssm_params(
        ssm_a_re[layer], ssm_a_im[layer], ssm_log_dt[layer], ssm_b_re[layer], ssm_b_im[layer],
        ssm_c_re[layer], ssm_c_im[layer], ssm_d[layer]))

    out = _out_proj(x2d, conv_n, y_g, mods, g_ssm_out[layer][None, :], g_norm2[layer][None, :],
                    g_final[None, :], w_glu[layer].astype(_BF16), w_out[layer].astype(_BF16),
                    w_mlp1[layer].astype(_BF16), w_mlp2[layer].astype(_BF16),
                    _group_ones(MXU_W, SSM_GROUP), tm=512)
    return out[None]
```

```python
import functools
import math

import jax
import jax.numpy as jnp
from jax import lax
from jax.experimental import pallas as pl
from jax.experimental.pallas import tpu as pltpu

D_MODEL = 1024
GRID_W = 64
CONV_W = 512
CONV_HEADS = 8
SSM_W = 512
SSM_GROUP = 16
SSM_GROUPS = 32
SSM_STATE = 64
IN_W = 3 * CONV_W + SSM_W
D_FF = 4 * D_MODEL
N_MOD = 6
RMS_EPS = 1e-6

LANES = 128
MXU_W = 256
SUBLANES = 8
CHUNK = 16
CHUNK_W = CHUNK * SSM_GROUP
PAIR = 2
PAIR_STATE = PAIR * SSM_STATE
N_COMP = 4
BLOCKS = LANES // SSM_GROUP
N_SLABS = SSM_W // LANES
VMEM_LIMIT = 56 * 1024 * 1024

_BF16 = jnp.bfloat16
_F32 = jnp.float32


def _dot(a, b):
    return jnp.dot(a, b, preferred_element_type=_F32)


def _lane_block_transpose(vs):
    blk = lax.broadcasted_iota(jnp.int32, vs[0].shape, 1) // SSM_GROUP
    dist = BLOCKS // 2
    while dist:
        shift = dist * SSM_GROUP
        upper = (blk & dist) != 0
        out = [None] * BLOCKS
        for a in range(BLOCKS):
            if a & dist:
                continue
            lo, hi = vs[a], vs[a + dist]
            out[a] = jnp.where(upper, pltpu.roll(hi, shift, axis=1), lo)
            out[a + dist] = jnp.where(upper, hi, pltpu.roll(lo, LANES - shift, axis=1))
        vs = out
        dist //= 2
    return vs


def _mod_kernel(c_ref, cctx_ref, w_ref, b_ref, o_ref):
    first = lax.broadcasted_iota(jnp.int32, (SUBLANES, D_MODEL), 0) == 0
    s = jnp.where(first, jnp.broadcast_to(c_ref[...], first.shape), jnp.broadcast_to(cctx_ref[...], first.shape))
    act = s * jax.nn.sigmoid(s)
    o_ref[...] = jnp.dot(act, w_ref[...], preferred_element_type=_F32,
                         precision=lax.Precision.HIGHEST) + b_ref[...]


def _modulation(c_row, cctx_row, w_mod, b_mod):
    n_out = w_mod.shape[1]
    tn = 1024
    return pl.pallas_call(
        _mod_kernel,
        grid=(n_out // tn,),
        in_specs=[pl.BlockSpec((1, D_MODEL), lambda j: (0, 0)),
                  pl.BlockSpec((1, D_MODEL), lambda j: (0, 0)),
                  pl.BlockSpec((D_MODEL, tn), lambda j: (0, j)),
                  pl.BlockSpec((1, tn), lambda j: (0, j))],
        out_specs=pl.BlockSpec((SUBLANES, tn), lambda j: (0, j)),
        out_shape=jax.ShapeDtypeStruct((SUBLANES, n_out), _F32),
        compiler_params=pltpu.CompilerParams(dimension_semantics=("arbitrary",),
                                             vmem_limit_bytes=VMEM_LIMIT),
        name="mod",
    )(c_row, cctx_row, w_mod, b_mod)


def _rms_rows(x, g):
    ms = jnp.mean(x * x, axis=-1, keepdims=True)
    return x * lax.rsqrt(ms + RMS_EPS) * g


def _group_ones(width, group):
    rows = lax.broadcasted_iota(jnp.int32, (width, width), 0) // group
    cols = lax.broadcasted_iota(jnp.int32, (width, width), 1) // group
    return jnp.where(rows == cols, 1.0, 0.0).astype(_BF16)


def _in_kernel(mod_row, x_ref, mods_ref, g1_ref, w_in_ref, convw_ref, gconv_ref,
               conv_ref, ug_ref, u_scr, w_scr, ones_scr):
    @pl.when(pl.program_id(0) == 0)
    def _():
        w_scr[:, 0:SSM_W] = w_in_ref[:, 3 * CONV_W:].astype(_BF16)
        for k in range(CONV_W // MXU_W):
            for part in range(3):
                dst = SSM_W + (3 * k + part) * MXU_W
                src = part * CONV_W + k * MXU_W
                w_scr[:, dst:dst + MXU_W] = w_in_ref[:, src:src + MXU_W].astype(_BF16)
        ones_scr[...] = _group_ones(MXU_W, CONV_W // CONV_HEADS)

    sh1 = mods_ref[mod_row:mod_row + 1, 0:D_MODEL]
    sc1 = mods_ref[mod_row:mod_row + 1, D_MODEL:2 * D_MODEL]
    tm = x_ref.shape[0]
    n_chunks = tm // CHUNK
    pos = lax.broadcasted_iota(jnp.int32, (tm, MXU_W), 0) % GRID_W
    h = _rms_rows(x_ref[...], g1_ref[...]) * (1.0 + sc1) + sh1
    z = _dot(h.astype(_BF16), w_scr[...])
    for q in range(N_SLABS):
        u_scr[q] = z[:, q * LANES:(q + 1) * LANES]
    for k in range(CONV_W // MXU_W):
        c0 = SSM_W + 3 * k * MXU_W
        lanes = slice(k * MXU_W, (k + 1) * MXU_W)
        b = z[:, c0:c0 + MXU_W]
        cv = z[:, c0 + MXU_W:c0 + 2 * MXU_W] * z[:, c0 + 2 * MXU_W:c0 + 3 * MXU_W]
        prev = jnp.where(pos == 0, 0.0, pltpu.roll(cv, 1, axis=0))
        nxt = jnp.where(pos == GRID_W - 1, 0.0, pltpu.roll(cv, tm - 1, axis=0))
        y = b * (convw_ref[0:1, lanes] * prev + convw_ref[1:2, lanes] * cv + convw_ref[2:3, lanes] * nxt)
        ssq = _dot((y * y).astype(_BF16), ones_scr[...])
        yn = y * lax.rsqrt(ssq * (1.0 / (CONV_W // CONV_HEADS)) + RMS_EPS) * gconv_ref[:, lanes]
        conv_ref[:, lanes] = yn.astype(_BF16)
    for q in range(N_SLABS):
        for half in range(CHUNK // BLOCKS):
            rows = [u_scr[q, pl.ds(half * BLOCKS + t, n_chunks, stride=CHUNK), :] for t in range(BLOCKS)]
            cols = _lane_block_transpose(rows)
            for g8 in range(BLOCKS):
                ug_ref[q * BLOCKS + g8, :, half * LANES:(half + 1) * LANES] = cols[g8].astype(_BF16)


def _in_proj(x2d, mods, mod_row, g1, w_in, conv_w, g_conv, tm):
    n = x2d.shape[0]
    const = lambda i: (0, 0)
    return pl.pallas_call(
        functools.partial(_in_kernel, mod_row),
        grid=(n // tm,),
        in_specs=[pl.BlockSpec((tm, D_MODEL), lambda i: (i, 0)),
                  pl.BlockSpec(mods.shape, const),
                  pl.BlockSpec((1, D_MODEL), const),
                  pl.BlockSpec((D_MODEL, IN_W), const, pipeline_mode=pl.Buffered(1)),
                  pl.BlockSpec((3, CONV_W), const),
                  pl.BlockSpec((1, CONV_W), const)],
        out_specs=[pl.BlockSpec((tm, CONV_W), lambda i: (i, 0)),
                   pl.BlockSpec((SSM_GROUPS, tm // CHUNK, CHUNK_W), lambda i: (0, i, 0))],
        out_shape=[jax.ShapeDtypeStruct((n, CONV_W), _BF16),
                   jax.ShapeDtypeStruct((SSM_GROUPS, n // CHUNK, CHUNK_W), _BF16)],
        scratch_shapes=[pltpu.VMEM((N_SLABS, tm, LANES), _F32),
                        pltpu.VMEM((D_MODEL, IN_W), _BF16),
                        pltpu.VMEM((MXU_W, MXU_W), _BF16)],
        compiler_params=pltpu.CompilerParams(dimension_semantics=("arbitrary",),
                                             vmem_limit_bytes=VMEM_LIMIT),
        name="in_proj",
    )(x2d, mods, g1, w_in, conv_w, g_conv)


def _scan_steps(n_rows):
    return max(1, math.ceil(math.log2(n_rows)))


def _cmul(ar, ai, br, bi):
    return ar * br - ai * bi, ar * bi + ai * br


def _eye(n):
    return (lax.broadcasted_iota(jnp.int32, (n, n), 0) == lax.broadcasted_iota(jnp.int32, (n, n), 1)).astype(_F32)


def _dot_nt_exact(a, b):
    return lax.dot_general(a, b, (((1,), (1,)), ((), ())), precision=lax.Precision.HIGHEST,
                           preferred_element_type=_F32)


def _ssm_prep(n_steps, par_ref, br_ref, bi_ref, cr_ref, ci_ref, wst_ref, wout_ref, cmat_ref):
    T = CHUNK
    left8 = lax.broadcasted_iota(jnp.int32, (SUBLANES, LANES), 1) < SSM_STATE
    row0 = lax.broadcasted_iota(jnp.int32, (SUBLANES, LANES), 0) == 0

    def member_rows(vf, vb):
        f8 = jnp.broadcast_to(vf, (SUBLANES, LANES))
        b8 = pltpu.roll(jnp.broadcast_to(vb, (SUBLANES, LANES)), SSM_STATE, axis=1)
        return jnp.where(row0, jnp.where(left8, f8, b8), jnp.where(left8, b8, f8))

    pf, pb = par_ref[0, 0], par_ref[1, 0]
    lr = jnp.minimum(member_rows(pf[0:1], pb[0:1]), -1e-4)
    li = member_rows(pf[1:2], pb[1:2])
    dt = jnp.exp(member_rows(pf[2:3], pb[2:3]))
    mag = jnp.exp(lr * dt)
    ab_re = mag * jnp.cos(li * dt)
    ab_im = mag * jnp.sin(li * dt)
    nr = ab_re - 1.0
    den = lr * lr + li * li
    f_re = (nr * lr + ab_im * li) / den
    f_im = (ab_im * lr - nr * li) / den
    pt_re, pt_im = [jnp.ones_like(ab_re)], [jnp.zeros_like(ab_im)]
    for _ in range(T):
        nxt = _cmul(pt_re[-1], pt_im[-1], ab_re, ab_im)
        pt_re.append(nxt[0])
        pt_im.append(nxt[1])

    cols = jnp.concatenate([ab_re, ab_im, jnp.zeros((LANES - 2 * SUBLANES, LANES), _F32)], axis=0).T
    left1 = lax.broadcasted_iota(jnp.int32, (1, LANES), 1) < SSM_STATE
    eye_i = _eye(SSM_GROUP)
    eye_p = _eye(SSM_STATE)
    rows = lax.broadcasted_iota(jnp.int32, (LANES, CHUNK_W), 0)
    tcol = lax.broadcasted_iota(jnp.int32, (LANES, CHUNK_W), 1) // SSM_GROUP
    zero_half = jnp.zeros((SSM_STATE, CHUNK_W), _BF16)
    for e in range(PAIR):
        fwd_first = e == 0
        d0, d1 = (0, 1) if fwd_first else (1, 0)
        b_re = _dot_nt_exact(eye_i, jnp.concatenate([br_ref[d0, e], br_ref[d1, e]], axis=0))
        b_im = _dot_nt_exact(eye_i, jnp.concatenate([bi_ref[d0, e], bi_ref[d1, e]], axis=0))
        bbar_re, bbar_im = _cmul(f_re[e:e + 1], f_im[e:e + 1], b_re, b_im)
        fwd_lane = left1 if fwd_first else jnp.logical_not(left1)
        for s in range(T):
            pr = jnp.where(fwd_lane, pt_re[T - 1 - s][e:e + 1], pt_re[s][e:e + 1])
            pi = jnp.where(fwd_lane, pt_im[T - 1 - s][e:e + 1], pt_im[s][e:e + 1])
            w_re, w_im = _cmul(pr, pi, bbar_re, bbar_im)
            wst_ref[e, s * SSM_GROUP:(s + 1) * SSM_GROUP, 0:LANES] = w_re.astype(_BF16)
            wst_ref[e, s * SSM_GROUP:(s + 1) * SSM_GROUP, LANES:] = w_im.astype(_BF16)
        fwd_row = (rows < SSM_STATE) if fwd_first else (rows >= SSM_STATE)
        expo = jnp.where(fwd_row, tcol + 1, T - tcol)
        q_re = jnp.ones((LANES, CHUNK_W), _F32)
        q_im = jnp.zeros((LANES, CHUNK_W), _F32)
        s_re, s_im = cols[:, e:e + 1], cols[:, SUBLANES + e:SUBLANES + e + 1]
        for bit in range(T.bit_length()):
            m_re, m_im = _cmul(q_re, q_im, jnp.broadcast_to(s_re, q_re.shape), jnp.broadcast_to(s_im, q_re.shape))
            take = ((expo >> bit) & 1) == 1
            q_re, q_im = jnp.where(take, m_re, q_re), jnp.where(take, m_im, q_im)
            s_re, s_im = _cmul(s_re, s_im, s_re, s_im)
        tiled = lambda c: _dot_nt_exact(eye_p, jnp.concatenate([c] * T, axis=0))
        c_re = jnp.concatenate([tiled(cr_ref[d0, e]), tiled(cr_ref[d1, e])], axis=0)
        c_im = jnp.concatenate([tiled(ci_ref[d0, e]), tiled(ci_ref[d1, e])], axis=0)
        o_re, o_im = _cmul(c_re, c_im, q_re, q_im)
        wout_ref[e, 0:LANES] = o_re.astype(_BF16)
        wout_ref[e, LANES:] = (-o_im).astype(_BF16)
        for ri, src in enumerate((c_re, -c_im)):
            for slot, d in enumerate((d0, d1)):
                r0 = ri * LANES + slot * SSM_STATE
                block = src[slot * SSM_STATE:(slot + 1) * SSM_STATE].astype(_BF16)
                cmat_ref[e, r0:r0 + SSM_STATE, d * CHUNK_W:(d + 1) * CHUNK_W] = block
                cmat_ref[e, r0:r0 + SSM_STATE, (1 - d) * CHUNK_W:(2 - d) * CHUNK_W] = zero_half

    m0_re, m1_re = pt_re[T][0:1], pt_re[T][1:2]
    m0_im, m1_im = pt_im[T][0:1], pt_im[T][1:2]
    cf = (jnp.where(left1, m0_re, m1_re), jnp.where(left1, m0_im, m1_im))
    cb = (jnp.where(left1, m1_re, m0_re), jnp.where(left1, m1_im, m0_im))
    coefs = []
    for _ in range(n_steps):
        coefs.append([cf[0], cf[1], cb[0], cb[1]])
        cf = _cmul(cf[0], cf[1], cf[0], cf[1])
        cb = _cmul(cb[0], cb[1], cb[0], cb[1])
    return coefs


def _ssm_kernel(n_lat, n_ctx, u_ref, uc_ref, par_ref, br_ref, bi_ref, cr_ref, ci_ref, d_ref,
                y_ref, buf_a, buf_b, wst_ref, wout_ref, cmat_ref, m_scr):
    n_rows = n_lat + n_ctx
    pad = SUBLANES
    coefs = _ssm_prep(_scan_steps(n_rows), par_ref, br_ref, bi_ref, cr_ref, ci_ref,
                      wst_ref, wout_ref, cmat_ref)

    lane = lax.broadcasted_iota(jnp.int32, (SSM_GROUP, CHUNK_W), 1)
    blk = lane // SSM_GROUP
    on_diag = lane % SSM_GROUP == lax.broadcasted_iota(jnp.int32, (SSM_GROUP, CHUNK_W), 0)
    for e in range(PAIR):
        d_skip = jnp.where(on_diag, jnp.broadcast_to(d_ref[e], (SSM_GROUP, CHUNK_W)), 0.0)
        taps = _dot(wst_ref[e], cmat_ref[e])
        taps_f = taps[:, 0:CHUNK_W]
        taps_b = taps[:, CHUNK_W:]
        for s in range(CHUNK):
            acc = jnp.zeros((SSM_GROUP, CHUNK_W), _F32)
            for t in range(CHUNK):
                val = None
                if s >= t:
                    r0 = (s - t) * SSM_GROUP
                    val = taps_b[r0:r0 + SSM_GROUP]
                if s <= t:
                    r0 = (CHUNK - 1 - t + s) * SSM_GROUP
                    v2 = taps_f[r0:r0 + SSM_GROUP]
                    val = v2 if val is None else val + v2
                if s == t:
                    val = val + d_skip
                acc = jnp.where(blk == t, val, acc)
            m_scr[e, s * SSM_GROUP:(s + 1) * SSM_GROUP, :] = acc.astype(_BF16)

    def paired(m0, m1):
        left = lax.broadcasted_iota(jnp.int32, (m0.shape[0], PAIR_STATE), 1) < SSM_STATE
        re0, im0 = m0[:, 0:PAIR_STATE], m0[:, PAIR_STATE:]
        re1, im1 = m1[:, 0:PAIR_STATE], m1[:, PAIR_STATE:]
        return [jnp.where(left, re0, re1), jnp.where(left, im0, im1),
                jnp.where(left, re1, re0), jnp.where(left, im1, im0)]

    s_lat = paired(_dot(u_ref[0], wst_ref[0]), _dot(u_ref[1], wst_ref[1]))
    s_ctx = paired(_dot(uc_ref[0], wst_ref[0]), _dot(uc_ref[1], wst_ref[1]))

    zeros_pad = jnp.zeros((pad, PAIR_STATE), _F32)
    for buf in (buf_a, buf_b):
        for comp in range(N_COMP):
            buf[comp, 0:pad] = zeros_pad
            buf[comp, pad + n_rows:pad + n_rows + pad] = zeros_pad
    for comp in range(2):
        buf_a[comp, pad:pad + n_ctx] = s_ctx[comp]
        buf_a[comp, pad + n_ctx:pad + n_rows] = s_lat[comp]
    for comp in range(2, 4):
        buf_a[comp, pad:pad + n_lat] = s_lat[comp]
        buf_a[comp, pad + n_lat:pad + n_rows] = s_ctx[comp]

    src, dst = buf_a, buf_b
    for k in range(_scan_steps(n_rows)):
        sh = 1 << k
        coef = coefs[k]
        if sh < SUBLANES:
            lo, hi = pad, pad + n_rows
        else:
            lo, hi = pad + sh, pad + n_rows
        xr, xi = src[0, lo - sh:hi - sh], src[1, lo - sh:hi - sh]
        dst[0, lo:hi] = src[0, lo:hi] + coef[0] * xr - coef[1] * xi
        dst[1, lo:hi] = src[1, lo:hi] + coef[0] * xi + coef[1] * xr
        if sh < SUBLANES:
            blo, bhi = pad, pad + n_rows
        else:
            blo, bhi = pad, pad + n_rows - sh
        xr, xi = src[2, blo + sh:bhi + sh], src[3, blo + sh:bhi + sh]
        dst[2, blo:bhi] = src[2, blo:bhi] + coef[2] * xr - coef[3] * xi
        dst[3, blo:bhi] = src[3, blo:bhi] + coef[2] * xi + coef[3] * xr
        if sh >= SUBLANES:
            for comp in range(2):
                dst[comp, pad:pad + sh] = src[comp, pad:pad + sh]
            for comp in range(2, 4):
                dst[comp, pad + n_rows - sh:pad + n_rows] = src[comp, pad + n_rows - sh:pad + n_rows]
        src, dst = dst, src

    f0 = pad + n_ctx - 1
    b0 = pad + 1
    f_re, f_im = src[0, f0:f0 + n_lat], src[1, f0:f0 + n_lat]
    b_re, b_im = src[2, b0:b0 + n_lat], src[3, b0:b0 + n_lat]
    left = lax.broadcasted_iota(jnp.int32, (n_lat, PAIR_STATE), 1) < SSM_STATE
    h0 = jnp.concatenate([jnp.where(left, f_re, b_re), jnp.where(left, f_im, b_im)], axis=1)
    h1 = jnp.concatenate([jnp.where(left, b_re, f_re), jnp.where(left, b_im, f_im)], axis=1)
    y_ref[0] = _dot(u_ref[0], m_scr[0]) + _dot(h0.astype(_BF16), wout_ref[0])
    y_ref[1] = _dot(u_ref[1], m_scr[1]) + _dot(h1.astype(_BF16), wout_ref[1])


def _ssm(u_g, uc_g, par, b_re, b_im, c_re, c_im, d_col):
    n_groups, n_lat, _ = u_g.shape
    n_ctx = uc_g.shape[1]
    n_rows = n_lat + n_ctx
    n_pairs = n_groups // PAIR
    buf = pltpu.VMEM((N_COMP, n_rows + 2 * SUBLANES, PAIR_STATE), _F32)
    mat = pltpu.VMEM((PAIR, CHUNK_W, CHUNK_W), _BF16)
    pair3 = lambda q: (q, 0, 0)
    by_dir = lambda q: (0, q, 0, 0)
    return pl.pallas_call(
        functools.partial(_ssm_kernel, n_lat, n_ctx),
        grid=(n_pairs,),
        in_specs=[pl.BlockSpec((PAIR, n_lat, CHUNK_W), pair3),
                  pl.BlockSpec((PAIR, n_ctx, CHUNK_W), pair3),
                  pl.BlockSpec((2, 1) + par.shape[2:], by_dir),
                  pl.BlockSpec((2, PAIR) + b_re.shape[2:], by_dir),
                  pl.BlockSpec((2, PAIR) + b_im.shape[2:], by_dir),
                  pl.BlockSpec((2, PAIR) + c_re.shape[2:], by_dir),
                  pl.BlockSpec((2, PAIR) + c_im.shape[2:], by_dir),
                  pl.BlockSpec((PAIR,) + d_col.shape[1:], pair3)],
        out_specs=pl.BlockSpec((PAIR, n_lat, CHUNK_W), pair3),
        out_shape=jax.ShapeDtypeStruct((n_groups, n_lat, CHUNK_W), _F32),
        scratch_shapes=[buf, buf, mat, mat, pltpu.VMEM((PAIR, CHUNK_W, 2 * CHUNK_W), _BF16), mat],
        compiler_params=pltpu.CompilerParams(dimension_semantics=("arbitrary",),
                                             vmem_limit_bytes=VMEM_LIMIT),
        name="ssm",
    )(u_g, uc_g, par, b_re, b_im, c_re, c_im, d_col)


def _out_kernel(x_ref, conv_ref, yg_ref, mods_ref, gssm_ref, g2_ref, gfin_ref,
                wglu32_ref, wout32_ref, w1_ref, w2_ref, o_ref, y_scr, mix_scr, wglu_ref, wout_ref, ones_ref):
    @pl.when(pl.program_id(0) == 0)
    def _():
        wglu_ref[...] = wglu32_ref[...].astype(_BF16)
        wout_ref[...] = wout32_ref[...].astype(_BF16)
        ones_ref[...] = _group_ones(MXU_W, SSM_GROUP)

    x = x_ref[...]
    n_chunks = x.shape[0] // CHUNK
    gt1 = mods_ref[0:1, 2 * D_MODEL:3 * D_MODEL]
    sh2 = mods_ref[0:1, 3 * D_MODEL:4 * D_MODEL]
    sc2 = mods_ref[0:1, 4 * D_MODEL:5 * D_MODEL]
    gt2 = mods_ref[0:1, 5 * D_MODEL:6 * D_MODEL]
    mix_scr[...] = _dot(conv_ref[...], wout_ref[0:CONV_W])
    for q in range(N_SLABS):
        for half in range(CHUNK // BLOCKS):
            cols = [yg_ref[q * BLOCKS + g8, :, half * LANES:(half + 1) * LANES] for g8 in range(BLOCKS)]
            rows = _lane_block_transpose(cols)
            for t in range(BLOCKS):
                y_scr[q, pl.ds(half * BLOCKS + t, n_chunks, stride=CHUNK), :] = rows[t]
    s = jax.nn.gelu(jnp.concatenate([y_scr[q] for q in range(N_SLABS)], axis=1))
    s = s * jax.nn.sigmoid(_dot(s.astype(_BF16), wglu_ref[...]))
    sq = (s * s).astype(_BF16)
    ssq = jnp.concatenate([_dot(sq[:, k * MXU_W:(k + 1) * MXU_W], ones_ref[...])
                           for k in range(SSM_W // MXU_W)], axis=1)
    sn = s * lax.rsqrt(ssq * (1.0 / SSM_GROUP) + RMS_EPS) * gssm_ref[...]
    mix = mix_scr[...] + _dot(sn.astype(_BF16), wout_ref[CONV_W:])
    x_mid = x + gt1 * mix
    h2 = (_rms_rows(x_mid, g2_ref[...]) * (1.0 + sc2) + sh2).astype(_BF16)
    n_split = 4
    ff = D_FF // n_split
    acc = None
    for j in range(n_split):
        a = jnp.maximum(_dot(h2, w1_ref[:, j * ff:(j + 1) * ff]), 0.0)
        part = _dot((a * a).astype(_BF16), w2_ref[j * ff:(j + 1) * ff])
        acc = part if acc is None else acc + part
    x_out = x_mid + gt2 * acc
    o_ref[...] = _rms_rows(x_out, gfin_ref[...])


def _out_proj(x2d, conv_n, y_g, mods, g_ssm, g2, g_fin, wglu, wout, w1, w2, tm):
    n = x2d.shape[0]
    const = lambda i: (0, 0)
    row = lambda i: (i, 0)
    resident = lambda shape: pl.BlockSpec(shape, const, pipeline_mode=pl.Buffered(1))
    return pl.pallas_call(
        _out_kernel,
        grid=(n // tm,),
        in_specs=[pl.BlockSpec((tm, D_MODEL), row),
                  pl.BlockSpec((tm, CONV_W), row),
                  pl.BlockSpec((SSM_GROUPS, tm // CHUNK, CHUNK_W), lambda i: (0, i, 0)),
                  pl.BlockSpec(mods.shape, const),
                  pl.BlockSpec((1, SSM_W), const),
                  pl.BlockSpec((1, D_MODEL), const),
                  pl.BlockSpec((1, D_MODEL), const),
                  resident((SSM_W, SSM_W)),
                  resident((D_MODEL, D_MODEL)),
                  resident((D_MODEL, D_FF)),
                  resident((D_FF, D_MODEL))],
        out_specs=pl.BlockSpec((tm, D_MODEL), row),
        out_shape=jax.ShapeDtypeStruct((n, D_MODEL), _F32),
        scratch_shapes=[pltpu.VMEM((N_SLABS, tm, LANES), _F32), pltpu.VMEM((tm, D_MODEL), _F32),
                        pltpu.VMEM((SSM_W, SSM_W), _BF16), pltpu.VMEM((D_MODEL, D_MODEL), _BF16),
                        pltpu.VMEM((MXU_W, MXU_W), _BF16)],
        compiler_params=pltpu.CompilerParams(dimension_semantics=("arbitrary",),
                                             vmem_limit_bytes=VMEM_LIMIT),
        name="out_proj",
    )(x2d, conv_n, y_g, mods, g_ssm, g2, g_fin, wglu, wout, w1, w2)


def _ssm_lane_params(a_re, a_im, log_dt):
    n_pairs = SSM_GROUPS // PAIR
    pair_lanes = lambda v: v.reshape(2, n_pairs, PAIR * SSM_STATE)
    dt_lanes = jnp.broadcast_to(log_dt[..., None], (2, SSM_GROUPS, SSM_STATE))
    return jnp.stack([pair_lanes(a_re), pair_lanes(a_im), pair_lanes(dt_lanes)], axis=2)


def kernel(x, c, ctx, c_ctx, w_mod, b_mod, g_norm1, w_in, conv_w, ssm_a_re, ssm_a_im, ssm_log_dt,
           ssm_b_re, ssm_b_im, ssm_c_re, ssm_c_im, ssm_d, w_glu, g_conv_out, g_ssm_out, w_out,
           g_norm2, w_mlp1, w_mlp2, g_final):
    bsz, n_lat, d_model = x.shape
    n_ctx = ctx.shape[1]
    assert bsz == 1 and d_model == D_MODEL and w_mod.shape[0] == 1
    assert n_lat % (CHUNK * SUBLANES) == 0 and n_ctx % (CHUNK * CHUNK) == 0 and n_lat % GRID_W == 0
    layer = 0
    x2d = x[0]
    ctx2d = ctx[0]

    mods = _modulation(c, c_ctx[None, :], w_mod[layer], b_mod[layer][None, :])

    g1 = g_norm1[layer][None, :]
    g_conv = g_conv_out[layer][None, :]
    conv_n, u_g = _in_proj(x2d, mods, 0, g1, w_in[layer], conv_w[layer], g_conv, tm=512)
    _, uc_g = _in_proj(ctx2d, mods, 1, g1, w_in[layer], conv_w[layer], g_conv, tm=n_ctx)

    y_g = _ssm(u_g, uc_g, _ssm_lane_params(ssm_a_re[layer], ssm_a_im[layer], ssm_log_dt[layer]),
               ssm_b_re[layer], ssm_b_im[layer], ssm_c_re[layer], ssm_c_im[layer],
               ssm_d[layer].reshape(SSM_GROUPS, SSM_GROUP, 1))

    out = _out_proj(x2d, conv_n, y_g, mods, g_ssm_out[layer][None, :], g_norm2[layer][None, :],
                    g_final[None, :], w_glu[layer], w_out[layer],
                    w_mlp1[layer].astype(_BF16), w_mlp2[layer].astype(_BF16), tm=512)
    return out[None]
```

```python
import functools
import math

import jax
import jax.numpy as jnp
from jax import lax
from jax.experimental import pallas as pl
from jax.experimental.pallas import tpu as pltpu

D_MODEL = 1024
GRID_W = 64
CONV_W = 512
CONV_HEADS = 8
SSM_W = 512
SSM_GROUP = 16
SSM_GROUPS = 32
SSM_STATE = 64
IN_W = 3 * CONV_W + SSM_W
D_FF = 4 * D_MODEL
N_MOD = 6
RMS_EPS = 1e-6

LANES = 128
MXU_W = 256
SUBLANES = 8
CHUNK = 16
CHUNK_W = CHUNK * SSM_GROUP
PAIR = 2
PAIR_STATE = PAIR * SSM_STATE
N_COMP = 4
BLOCKS = LANES // SSM_GROUP
N_SLABS = SSM_W // LANES
VMEM_LIMIT = 56 * 1024 * 1024

_BF16 = jnp.bfloat16
_F32 = jnp.float32


def _dot(a, b):
    return jnp.dot(a, b, preferred_element_type=_F32)


def _lane_block_transpose(vs):
    blk = lax.broadcasted_iota(jnp.int32, vs[0].shape, 1) // SSM_GROUP
    dist = BLOCKS // 2
    while dist:
        shift = dist * SSM_GROUP
        upper = (blk & dist) != 0
        out = [None] * BLOCKS
        for a in range(BLOCKS):
            if a & dist:
                continue
            lo, hi = vs[a], vs[a + dist]
            out[a] = jnp.where(upper, pltpu.roll(hi, shift, axis=1), lo)
            out[a + dist] = jnp.where(upper, hi, pltpu.roll(lo, LANES - shift, axis=1))
        vs = out
        dist //= 2
    return vs


def _mod_kernel(c_ref, cctx_ref, w_ref, b_ref, o_ref):
    first = lax.broadcasted_iota(jnp.int32, (SUBLANES, D_MODEL), 0) == 0
    s = jnp.where(first, jnp.broadcast_to(c_ref[...], first.shape), jnp.broadcast_to(cctx_ref[...], first.shape))
    act = s * jax.nn.sigmoid(s)
    o_ref[...] = jnp.dot(act, w_ref[...], preferred_element_type=_F32,
                         precision=lax.Precision.HIGHEST) + b_ref[...]


def _modulation(c_row, cctx_row, w_mod, b_mod):
    n_out = w_mod.shape[1]
    tn = 1024
    return pl.pallas_call(
        _mod_kernel,
        grid=(n_out // tn,),
        in_specs=[pl.BlockSpec((1, D_MODEL), lambda j: (0, 0)),
                  pl.BlockSpec((1, D_MODEL), lambda j: (0, 0)),
                  pl.BlockSpec((D_MODEL, tn), lambda j: (0, j)),
                  pl.BlockSpec((1, tn), lambda j: (0, j))],
        out_specs=pl.BlockSpec((SUBLANES, tn), lambda j: (0, j)),
        out_shape=jax.ShapeDtypeStruct((SUBLANES, n_out), _F32),
        compiler_params=pltpu.CompilerParams(dimension_semantics=("arbitrary",),
                                             vmem_limit_bytes=VMEM_LIMIT),
        name="mod",
    )(c_row, cctx_row, w_mod, b_mod)


def _rms_rows(x, g):
    ms = jnp.mean(x * x, axis=-1, keepdims=True)
    return x * lax.rsqrt(ms + RMS_EPS) * g


def _group_ones(width, group):
    rows = lax.broadcasted_iota(jnp.int32, (width, width), 0) // group
    cols = lax.broadcasted_iota(jnp.int32, (width, width), 1) // group
    return jnp.where(rows == cols, 1.0, 0.0).astype(_BF16)


def _in_kernel(mod_row, x_ref, mods_ref, g1_ref, w_in_ref, convw_ref, gconv_ref,
               conv_ref, ug_ref, u_scr, w_scr, ones_scr):
    @pl.when(pl.program_id(0) == 0)
    def _():
        w_scr[:, 0:SSM_W] = w_in_ref[:, 3 * CONV_W:].astype(_BF16)
        for k in range(CONV_W // MXU_W):
            for part in range(3):
                dst = SSM_W + (3 * k + part) * MXU_W
                src = part * CONV_W + k * MXU_W
                w_scr[:, dst:dst + MXU_W] = w_in_ref[:, src:src + MXU_W].astype(_BF16)
        ones_scr[...] = _group_ones(MXU_W, CONV_W // CONV_HEADS)

    sh1 = mods_ref[mod_row:mod_row + 1, 0:D_MODEL]
    sc1 = mods_ref[mod_row:mod_row + 1, D_MODEL:2 * D_MODEL]
    tm = x_ref.shape[0]
    n_chunks = tm // CHUNK
    pos = lax.broadcasted_iota(jnp.int32, (tm, MXU_W), 0) % GRID_W
    h = _rms_rows(x_ref[...], g1_ref[...]) * (1.0 + sc1) + sh1
    z = _dot(h.astype(_BF16), w_scr[...])
    for q in range(N_SLABS):
        u_scr[q] = z[:, q * LANES:(q + 1) * LANES]
    for k in range(CONV_W // MXU_W):
        c0 = SSM_W + 3 * k * MXU_W
        lanes = slice(k * MXU_W, (k + 1) * MXU_W)
        b = z[:, c0:c0 + MXU_W]
        cv = z[:, c0 + MXU_W:c0 + 2 * MXU_W] * z[:, c0 + 2 * MXU_W:c0 + 3 * MXU_W]
        prev = jnp.where(pos == 0, 0.0, pltpu.roll(cv, 1, axis=0))
        nxt = jnp.where(pos == GRID_W - 1, 0.0, pltpu.roll(cv, tm - 1, axis=0))
        y = b * (convw_ref[0:1, lanes] * prev + convw_ref[1:2, lanes] * cv + convw_ref[2:3, lanes] * nxt)
        ssq = _dot((y * y).astype(_BF16), ones_scr[...])
        yn = y * lax.rsqrt(ssq * (1.0 / (CONV_W // CONV_HEADS)) + RMS_EPS) * gconv_ref[:, lanes]
        conv_ref[:, lanes] = yn.astype(_BF16)
    for q in range(N_SLABS):
        for half in range(CHUNK // BLOCKS):
            rows = [u_scr[q, pl.ds(half * BLOCKS + t, n_chunks, stride=CHUNK), :] for t in range(BLOCKS)]
            cols = _lane_block_transpose(rows)
            for g8 in range(BLOCKS):
                ug_ref[q * BLOCKS + g8, :, half * LANES:(half + 1) * LANES] = cols[g8].astype(_BF16)


def _in_proj(x2d, mods, mod_row, g1, w_in, conv_w, g_conv, tm):
    n = x2d.shape[0]
    const = lambda i: (0, 0)
    return pl.pallas_call(
        functools.partial(_in_kernel, mod_row),
        grid=(n // tm,),
        in_specs=[pl.BlockSpec((tm, D_MODEL), lambda i: (i, 0)),
                  pl.BlockSpec(mods.shape, const),
                  pl.BlockSpec((1, D_MODEL), const),
                  pl.BlockSpec((D_MODEL, IN_W), const, pipeline_mode=pl.Buffered(1)),
                  pl.BlockSpec((3, CONV_W), const),
                  pl.BlockSpec((1, CONV_W), const)],
        out_specs=[pl.BlockSpec((tm, CONV_W), lambda i: (i, 0)),
                   pl.BlockSpec((SSM_GROUPS, tm // CHUNK, CHUNK_W), lambda i: (0, i, 0))],
        out_shape=[jax.ShapeDtypeStruct((n, CONV_W), _BF16),
                   jax.ShapeDtypeStruct((SSM_GROUPS, n // CHUNK, CHUNK_W), _BF16)],
        scratch_shapes=[pltpu.VMEM((N_SLABS, tm, LANES), _F32),
                        pltpu.VMEM((D_MODEL, IN_W), _BF16),
                        pltpu.VMEM((MXU_W, MXU_W), _BF16)],
        compiler_params=pltpu.CompilerParams(dimension_semantics=("arbitrary",),
                                             vmem_limit_bytes=VMEM_LIMIT),
        name="in_proj",
    )(x2d, mods, g1, w_in, conv_w, g_conv)


def _scan_steps(n_rows):
    return max(1, math.ceil(math.log2(n_rows)))


def _ends_pad(n_tiles):
    return max(SUBLANES, 1 << (_scan_steps(n_tiles) - 1))


def _cmul(ar, ai, br, bi):
    return ar * br - ai * bi, ar * bi + ai * br


def _eye(n):
    rows, cols = (lax.broadcasted_iota(jnp.int32, (n, n), axis) for axis in (0, 1))
    return jnp.where(rows == cols, 1.0, 0.0).astype(_BF16)


def _dot_nt_exact(eye, x):
    hi = x.astype(_BF16)
    rest = x - hi.astype(_F32)
    mid = rest.astype(_BF16)
    lo = (rest - mid.astype(_F32)).astype(_BF16)
    nt = lambda term: lax.dot_general(eye, term, (((1,), (1,)), ((), ())), preferred_element_type=_F32)
    return nt(hi) + nt(mid) + nt(lo)


def _ssm_prep(n_steps, par_ref, br_ref, bi_ref, cr_ref, ci_ref, wst_ref, wout_ref, cmat_ref):
    T = CHUNK
    left8 = lax.broadcasted_iota(jnp.int32, (SUBLANES, LANES), 1) < SSM_STATE
    row0 = lax.broadcasted_iota(jnp.int32, (SUBLANES, LANES), 0) == 0

    def member_rows(vf, vb):
        f8 = jnp.broadcast_to(vf, (SUBLANES, LANES))
        b8 = pltpu.roll(jnp.broadcast_to(vb, (SUBLANES, LANES)), SSM_STATE, axis=1)
        return jnp.where(row0, jnp.where(left8, f8, b8), jnp.where(left8, b8, f8))

    pf, pb = par_ref[0, 0], par_ref[1, 0]
    lr = jnp.minimum(member_rows(pf[0:1], pb[0:1]), -1e-4)
    li = member_rows(pf[1:2], pb[1:2])
    dt = jnp.exp(member_rows(pf[2:3], pb[2:3]))
    mag = jnp.exp(lr * dt)
    ab_re = mag * jnp.cos(li * dt)
    ab_im = mag * jnp.sin(li * dt)
    nr = ab_re - 1.0
    den = lr * lr + li * li
    f_re = (nr * lr + ab_im * li) / den
    f_im = (ab_im * lr - nr * li) / den
    pt_re, pt_im = [jnp.ones_like(ab_re)], [jnp.zeros_like(ab_im)]
    for _ in range(T):
        nxt = _cmul(pt_re[-1], pt_im[-1], ab_re, ab_im)
        pt_re.append(nxt[0])
        pt_im.append(nxt[1])

    cols = jnp.concatenate([ab_re, ab_im, jnp.zeros((LANES - 2 * SUBLANES, LANES), _F32)], axis=0).T
    left1 = lax.broadcasted_iota(jnp.int32, (1, LANES), 1) < SSM_STATE
    eye_i = _eye(SSM_GROUP)
    eye_p = _eye(SSM_STATE)
    rows = lax.broadcasted_iota(jnp.int32, (LANES, CHUNK_W), 0)
    tcol = lax.broadcasted_iota(jnp.int32, (LANES, CHUNK_W), 1) // SSM_GROUP
    zero_half = jnp.zeros((SSM_STATE, CHUNK_W), _BF16)
    for e in range(PAIR):
        fwd_first = e == 0
        d0, d1 = (0, 1) if fwd_first else (1, 0)
        b_re = _dot_nt_exact(eye_i, jnp.concatenate([br_ref[d0, e], br_ref[d1, e]], axis=0))
        b_im = _dot_nt_exact(eye_i, jnp.concatenate([bi_ref[d0, e], bi_ref[d1, e]], axis=0))
        bbar_re, bbar_im = _cmul(f_re[e:e + 1], f_im[e:e + 1], b_re, b_im)
        fwd_lane = left1 if fwd_first else jnp.logical_not(left1)
        for s in range(T):
            pr = jnp.where(fwd_lane, pt_re[T - 1 - s][e:e + 1], pt_re[s][e:e + 1])
            pi = jnp.where(fwd_lane, pt_im[T - 1 - s][e:e + 1], pt_im[s][e:e + 1])
            w_re, w_im = _cmul(pr, pi, bbar_re, bbar_im)
            wst_ref[e, s * SSM_GROUP:(s + 1) * SSM_GROUP, 0:LANES] = w_re.astype(_BF16)
            wst_ref[e, s * SSM_GROUP:(s + 1) * SSM_GROUP, LANES:] = w_im.astype(_BF16)
        fwd_row = (rows < SSM_STATE) if fwd_first else (rows >= SSM_STATE)
        expo = jnp.where(fwd_row, tcol + 1, T - tcol)
        q_re = jnp.ones((LANES, CHUNK_W), _F32)
        q_im = jnp.zeros((LANES, CHUNK_W), _F32)
        s_re, s_im = cols[:, e:e + 1], cols[:, SUBLANES + e:SUBLANES + e + 1]
        for bit in range(T.bit_length()):
            m_re, m_im = _cmul(q_re, q_im, jnp.broadcast_to(s_re, q_re.shape), jnp.broadcast_to(s_im, q_re.shape))
            take = ((expo >> bit) & 1) == 1
            q_re, q_im = jnp.where(take, m_re, q_re), jnp.where(take, m_im, q_im)
            s_re, s_im = _cmul(s_re, s_im, s_re, s_im)
        tiled = lambda c: _dot_nt_exact(eye_p, jnp.concatenate([c] * T, axis=0))
        c_re = jnp.concatenate([tiled(cr_ref[d0, e]), tiled(cr_ref[d1, e])], axis=0)
        c_im = jnp.concatenate([tiled(ci_ref[d0, e]), tiled(ci_ref[d1, e])], axis=0)
        o_re, o_im = _cmul(c_re, c_im, q_re, q_im)
        wout_ref[e, 0:LANES] = o_re.astype(_BF16)
        wout_ref[e, LANES:] = (-o_im).astype(_BF16)
        for ri, src in enumerate((c_re, -c_im)):
            for slot, d in enumerate((d0, d1)):
                r0 = ri * LANES + slot * SSM_STATE
                block = src[slot * SSM_STATE:(slot + 1) * SSM_STATE].astype(_BF16)
                cmat_ref[e, r0:r0 + SSM_STATE, d * CHUNK_W:(d + 1) * CHUNK_W] = block
                cmat_ref[e, r0:r0 + SSM_STATE, (1 - d) * CHUNK_W:(2 - d) * CHUNK_W] = zero_half

    m0_re, m1_re = pt_re[T][0:1], pt_re[T][1:2]
    m0_im, m1_im = pt_im[T][0:1], pt_im[T][1:2]
    cf = (jnp.where(left1, m0_re, m1_re), jnp.where(left1, m0_im, m1_im))
    cb = (jnp.where(left1, m1_re, m0_re), jnp.where(left1, m1_im, m0_im))
    coefs = []
    for _ in range(n_steps):
        coefs.append([cf[0], cf[1], cb[0], cb[1]])
        cf = _cmul(cf[0], cf[1], cf[0], cf[1])
        cb = _cmul(cb[0], cb[1], cb[0], cb[1])
    return coefs


def _ssm_kernel(n_lat, n_ctx, u_ref, uc_ref, par_ref, br_ref, bi_ref, cr_ref, ci_ref, d_ref,
                y_ref, buf_a, buf_b, ends_a, ends_b, wst_ref, wout_ref, cmat_ref, m_scr):
    n_rows = n_lat + n_ctx
    pad = SUBLANES
    coefs = _ssm_prep(_scan_steps(n_rows), par_ref, br_ref, bi_ref, cr_ref, ci_ref,
                      wst_ref, wout_ref, cmat_ref)

    lane = lax.broadcasted_iota(jnp.int32, (SSM_GROUP, CHUNK_W), 1)
    blk = lane // SSM_GROUP
    on_diag = lane % SSM_GROUP == lax.broadcasted_iota(jnp.int32, (SSM_GROUP, CHUNK_W), 0)
    for e in range(PAIR):
        d_skip = jnp.where(on_diag, jnp.broadcast_to(d_ref[e], (SSM_GROUP, CHUNK_W)), 0.0)
        taps = _dot(wst_ref[e], cmat_ref[e])
        taps_f = taps[:, 0:CHUNK_W]
        taps_b = taps[:, CHUNK_W:]
        for s in range(CHUNK):
            acc = jnp.zeros((SSM_GROUP, CHUNK_W), _F32)
            for t in range(CHUNK):
                val = None
                if s >= t:
                    r0 = (s - t) * SSM_GROUP
                    val = taps_b[r0:r0 + SSM_GROUP]
                if s <= t:
                    r0 = (CHUNK - 1 - t + s) * SSM_GROUP
                    v2 = taps_f[r0:r0 + SSM_GROUP]
                    val = v2 if val is None else val + v2
                if s == t:
                    val = val + d_skip
                acc = jnp.where(blk == t, val, acc)
            m_scr[e, s * SSM_GROUP:(s + 1) * SSM_GROUP, :] = acc.astype(_BF16)

    def paired(m0, m1):
        left = lax.broadcasted_iota(jnp.int32, (m0.shape[0], PAIR_STATE), 1) < SSM_STATE
        re0, im0 = m0[:, 0:PAIR_STATE], m0[:, PAIR_STATE:]
        re1, im1 = m1[:, 0:PAIR_STATE], m1[:, PAIR_STATE:]
        return [jnp.where(left, re0, re1), jnp.where(left, im0, im1),
                jnp.where(left, re1, re0), jnp.where(left, im1, im0)]

    s_lat = paired(_dot(u_ref[0], wst_ref[0]), _dot(u_ref[1], wst_ref[1]))
    s_ctx = paired(_dot(uc_ref[0], wst_ref[0]), _dot(uc_ref[1], wst_ref[1]))

    n_tiles = n_rows // SUBLANES
    tile_row = lax.broadcasted_iota(jnp.int32, (n_tiles, SUBLANES, PAIR_STATE), 1)

    def tile_scan(x_re, x_im, comp0, backward):
        for k in range(SUBLANES.bit_length() - 1):
            sh = 1 << k
            a_re, a_im = coefs[k][comp0], coefs[k][comp0 + 1]
            shift, keep = (SUBLANES - sh, tile_row < SUBLANES - sh) if backward else (sh, tile_row >= sh)
            p_re = jnp.where(keep, pltpu.roll(x_re, shift, axis=1), 0.0)
            p_im = jnp.where(keep, pltpu.roll(x_im, shift, axis=1), 0.0)
            x_re, x_im = x_re + a_re * p_re - a_im * p_im, x_im + a_re * p_im + a_im * p_re
        return x_re, x_im

    as_tiles = lambda parts: jnp.concatenate(parts, axis=0).reshape(n_tiles, SUBLANES, PAIR_STATE)
    local = list(tile_scan(as_tiles([s_ctx[0], s_lat[0]]), as_tiles([s_ctx[1], s_lat[1]]), 0, False))
    local += tile_scan(as_tiles([s_lat[2], s_ctx[2]]), as_tiles([s_lat[3], s_ctx[3]]), 2, True)
    for comp in range(N_COMP):
        buf_a[comp, pad:pad + n_rows] = local[comp].reshape(n_rows, PAIR_STATE)

    for buf in (ends_a, ends_b):
        buf[...] = jnp.zeros(buf.shape, _F32)
    epad = _ends_pad(n_tiles)
    for comp in range(N_COMP):
        end_row = pad + (SUBLANES - 1 if comp < 2 else 0)
        ends_a[comp, epad:epad + n_tiles] = buf_a[comp, pl.ds(end_row, n_tiles, stride=SUBLANES), :]
    src, dst = ends_a, ends_b
    for k in range(_scan_steps(n_tiles)):
        sh = 1 << k
        coef = coefs[SUBLANES.bit_length() - 1 + k]
        lo, hi = epad, epad + n_tiles
        xr, xi = src[0, lo - sh:hi - sh], src[1, lo - sh:hi - sh]
        dst[0, lo:hi] = src[0, lo:hi] + coef[0] * xr - coef[1] * xi
        dst[1, lo:hi] = src[1, lo:hi] + coef[0] * xi + coef[1] * xr
        xr, xi = src[2, lo + sh:hi + sh], src[3, lo + sh:hi + sh]
        dst[2, lo:hi] = src[2, lo:hi] + coef[2] * xr - coef[3] * xi
        dst[3, lo:hi] = src[3, lo:hi] + coef[2] * xi + coef[3] * xr
        src, dst = dst, src

    def carry_powers(comp0, backward):
        base = [(coefs[k][comp0], coefs[k][comp0 + 1]) for k in range(SUBLANES.bit_length())]
        pw = []
        for r in range(1, SUBLANES + 1):
            acc = None
            for k, term in enumerate(base):
                if (r >> k) & 1:
                    acc = term if acc is None else _cmul(acc[0], acc[1], term[0], term[1])
            pw.append(acc)
        if backward:
            pw = pw[::-1]
        return (jnp.concatenate([p[0] for p in pw], axis=0), jnp.concatenate([p[1] for p in pw], axis=0))

    for comp0, backward in ((0, False), (2, True)):
        p_re, p_im = carry_powers(comp0, backward)
        first = epad + 1 if backward else epad - 1
        for i in range(n_tiles):
            c_re = src[comp0, first + i:first + i + 1]
            c_im = src[comp0 + 1, first + i:first + i + 1]
            rows = slice(pad + i * SUBLANES, pad + (i + 1) * SUBLANES)
            buf_b[comp0, rows] = buf_a[comp0, rows] + p_re * c_re - p_im * c_im
            buf_b[comp0 + 1, rows] = buf_a[comp0 + 1, rows] + p_re * c_im + p_im * c_re

    f0 = pad + n_ctx - 1
    b0 = pad + 1
    f_re, f_im = buf_b[0, f0:f0 + n_lat], buf_b[1, f0:f0 + n_lat]
    b_re, b_im = buf_b[2, b0:b0 + n_lat], buf_b[3, b0:b0 + n_lat]
    left = lax.broadcasted_iota(jnp.int32, (n_lat, PAIR_STATE), 1) < SSM_STATE
    h0 = jnp.concatenate([jnp.where(left, f_re, b_re), jnp.where(left, f_im, b_im)], axis=1)
    h1 = jnp.concatenate([jnp.where(left, b_re, f_re), jnp.where(left, b_im, f_im)], axis=1)
    y_ref[0] = _dot(u_ref[0], m_scr[0]) + _dot(h0.astype(_BF16), wout_ref[0])
    y_ref[1] = _dot(u_ref[1], m_scr[1]) + _dot(h1.astype(_BF16), wout_ref[1])


def _ssm(u_g, uc_g, par, b_re, b_im, c_re, c_im, d_col):
    n_groups, n_lat, _ = u_g.shape
    n_ctx = uc_g.shape[1]
    n_rows = n_lat + n_ctx
    n_pairs = n_groups // PAIR
    buf = pltpu.VMEM((N_COMP, n_rows + 2 * SUBLANES, PAIR_STATE), _F32)
    n_tiles = n_rows // SUBLANES
    ends_rows = _ends_pad(n_tiles) + -(-(n_tiles + _ends_pad(n_tiles)) // SUBLANES) * SUBLANES
    ends = pltpu.VMEM((N_COMP, ends_rows, PAIR_STATE), _F32)
    mat = pltpu.VMEM((PAIR, CHUNK_W, CHUNK_W), _BF16)
    pair3 = lambda q: (q, 0, 0)
    by_dir = lambda q: (0, q, 0, 0)
    return pl.pallas_call(
        functools.partial(_ssm_kernel, n_lat, n_ctx),
        grid=(n_pairs,),
        in_specs=[pl.BlockSpec((PAIR, n_lat, CHUNK_W), pair3),
                  pl.BlockSpec((PAIR, n_ctx, CHUNK_W), pair3),
                  pl.BlockSpec((2, 1) + par.shape[2:], by_dir),
                  pl.BlockSpec((2, PAIR) + b_re.shape[2:], by_dir),
                  pl.BlockSpec((2, PAIR) + b_im.shape[2:], by_dir),
                  pl.BlockSpec((2, PAIR) + c_re.shape[2:], by_dir),
                  pl.BlockSpec((2, PAIR) + c_im.shape[2:], by_dir),
                  pl.BlockSpec((PAIR,) + d_col.shape[1:], pair3)],
        out_specs=pl.BlockSpec((PAIR, n_lat, CHUNK_W), pair3),
        out_shape=jax.ShapeDtypeStruct((n_groups, n_lat, CHUNK_W), _F32),
        scratch_shapes=[buf, buf, ends, ends, mat, mat, pltpu.VMEM((PAIR, CHUNK_W, 2 * CHUNK_W), _BF16), mat],
        compiler_params=pltpu.CompilerParams(dimension_semantics=("arbitrary",),
                                             vmem_limit_bytes=VMEM_LIMIT),
        name="ssm",
    )(u_g, uc_g, par, b_re, b_im, c_re, c_im, d_col)


def _out_kernel(x_ref, conv_ref, yg_ref, mods_ref, gssm_ref, g2_ref, gfin_ref,
                wglu32_ref, wout32_ref, w1_ref, w2_ref, o_ref, y_scr, mix_scr, wglu_ref, wout_ref, ones_ref):
    @pl.when(pl.program_id(0) == 0)
    def _():
        wglu_ref[...] = wglu32_ref[...].astype(_BF16)
        wout_ref[...] = wout32_ref[...].astype(_BF16)
        ones_ref[...] = _group_ones(MXU_W, SSM_GROUP)

    x = x_ref[...]
    n_chunks = x.shape[0] // CHUNK
    gt1 = mods_ref[0:1, 2 * D_MODEL:3 * D_MODEL]
    sh2 = mods_ref[0:1, 3 * D_MODEL:4 * D_MODEL]
    sc2 = mods_ref[0:1, 4 * D_MODEL:5 * D_MODEL]
    gt2 = mods_ref[0:1, 5 * D_MODEL:6 * D_MODEL]
    mix_scr[...] = _dot(conv_ref[...], wout_ref[0:CONV_W])
    for q in range(N_SLABS):
        for half in range(CHUNK // BLOCKS):
            cols = [yg_ref[q * BLOCKS + g8, :, half * LANES:(half + 1) * LANES] for g8 in range(BLOCKS)]
            rows = _lane_block_transpose(cols)
            for t in range(BLOCKS):
                y_scr[q, pl.ds(half * BLOCKS + t, n_chunks, stride=CHUNK), :] = rows[t]
    s = jax.nn.gelu(jnp.concatenate([y_scr[q] for q in range(N_SLABS)], axis=1))
    s = s * jax.nn.sigmoid(_dot(s.astype(_BF16), wglu_ref[...]))
    sq = (s * s).astype(_BF16)
    ssq = jnp.concatenate([_dot(sq[:, k * MXU_W:(k + 1) * MXU_W], ones_ref[...])
                           for k in range(SSM_W // MXU_W)], axis=1)
    sn = s * lax.rsqrt(ssq * (1.0 / SSM_GROUP) + RMS_EPS) * gssm_ref[...]
    mix = mix_scr[...] + _dot(sn.astype(_BF16), wout_ref[CONV_W:])
    x_mid = x + gt1 * mix
    h2 = (_rms_rows(x_mid, g2_ref[...]) * (1.0 + sc2) + sh2).astype(_BF16)
    n_split = 4
    ff = D_FF // n_split
    acc = None
    for j in range(n_split):
        a = jnp.maximum(_dot(h2, w1_ref[:, j * ff:(j + 1) * ff]), 0.0)
        part = _dot((a * a).astype(_BF16), w2_ref[j * ff:(j + 1) * ff])
        acc = part if acc is None else acc + part
    x_out = x_mid + gt2 * acc
    o_ref[...] = _rms_rows(x_out, gfin_ref[...])


def _out_proj(x2d, conv_n, y_g, mods, g_ssm, g2, g_fin, wglu, wout, w1, w2, tm):
    n = x2d.shape[0]
    const = lambda i: (0, 0)
    row = lambda i: (i, 0)
    resident = lambda shape: pl.BlockSpec(shape, const, pipeline_mode=pl.Buffered(1))
    return pl.pallas_call(
        _out_kernel,
        grid=(n // tm,),
        in_specs=[pl.BlockSpec((tm, D_MODEL), row),
                  pl.BlockSpec((tm, CONV_W), row),
                  pl.BlockSpec((SSM_GROUPS, tm // CHUNK, CHUNK_W), lambda i: (0, i, 0)),
                  pl.BlockSpec(mods.shape, const),
                  pl.BlockSpec((1, SSM_W), const),
                  pl.BlockSpec((1, D_MODEL), const),
                  pl.BlockSpec((1, D_MODEL), const),
                  resident((SSM_W, SSM_W)),
                  resident((D_MODEL, D_MODEL)),
                  resident((D_MODEL, D_FF)),
                  resident((D_FF, D_MODEL))],
        out_specs=pl.BlockSpec((tm, D_MODEL), row),
        out_shape=jax.ShapeDtypeStruct((n, D_MODEL), _F32),
        scratch_shapes=[pltpu.VMEM((N_SLABS, tm, LANES), _F32), pltpu.VMEM((tm, D_MODEL), _F32),
                        pltpu.VMEM((SSM_W, SSM_W), _BF16), pltpu.VMEM((D_MODEL, D_MODEL), _BF16),
                        pltpu.VMEM((MXU_W, MXU_W), _BF16)],
        compiler_params=pltpu.CompilerParams(dimension_semantics=("arbitrary",),
                                             vmem_limit_bytes=VMEM_LIMIT),
        name="out_proj",
    )(x2d, conv_n, y_g, mods, g_ssm, g2, g_fin, wglu, wout, w1, w2)


def _ssm_lane_params(a_re, a_im, log_dt):
    n_pairs = SSM_GROUPS // PAIR
    pair_lanes = lambda v: v.reshape(2, n_pairs, PAIR * SSM_STATE)
    dt_lanes = jnp.broadcast_to(log_dt[..., None], (2, SSM_GROUPS, SSM_STATE))
    return jnp.stack([pair_lanes(a_re), pair_lanes(a_im), pair_lanes(dt_lanes)], axis=2)


def kernel(x, c, ctx, c_ctx, w_mod, b_mod, g_norm1, w_in, conv_w, ssm_a_re, ssm_a_im, ssm_log_dt,
           ssm_b_re, ssm_b_im, ssm_c_re, ssm_c_im, ssm_d, w_glu, g_conv_out, g_ssm_out, w_out,
           g_norm2, w_mlp1, w_mlp2, g_final):
    bsz, n_lat, d_model = x.shape
    n_ctx = ctx.shape[1]
    assert bsz == 1 and d_model == D_MODEL and w_mod.shape[0] == 1
    assert n_lat % (CHUNK * SUBLANES) == 0 and n_ctx % (CHUNK * CHUNK) == 0 and n_lat % GRID_W == 0
    layer = 0
    x2d = x[0]
    ctx2d = ctx[0]

    mods = _modulation(c, c_ctx[None, :], w_mod[layer], b_mod[layer][None, :])

    g1 = g_norm1[layer][None, :]
    g_conv = g_conv_out[layer][None, :]
    conv_n, u_g = _in_proj(x2d, mods, 0, g1, w_in[layer], conv_w[layer], g_conv, tm=512)
    _, uc_g = _in_proj(ctx2d, mods, 1, g1, w_in[layer], conv_w[layer], g_conv, tm=n_ctx)

    y_g = _ssm(u_g, uc_g, _ssm_lane_params(ssm_a_re[layer], ssm_a_im[layer], ssm_log_dt[layer]),
               ssm_b_re[layer], ssm_b_im[layer], ssm_c_re[layer], ssm_c_im[layer],
               ssm_d[layer].reshape(SSM_GROUPS, SSM_GROUP, 1))

    out = _out_proj(x2d, conv_n, y_g, mods, g_ssm_out[layer][None, :], g_norm2[layer][None, :],
                    g_final[None, :], w_glu[layer], w_out[layer],
                    w_mlp1[layer].astype(_BF16), w_mlp2[layer].astype(_BF16), tm=512)
    return out[None]
```

```python
import functools
import math

import jax
import jax.numpy as jnp
from jax import lax
from jax.experimental import pallas as pl
from jax.experimental.pallas import tpu as pltpu

D_MODEL = 1024
GRID_W = 64
CONV_W = 512
CONV_HEADS = 8
SSM_W = 512
SSM_GROUP = 16
SSM_GROUPS = 32
SSM_STATE = 64
IN_W = 3 * CONV_W + SSM_W
D_FF = 4 * D_MODEL
N_MOD = 6
RMS_EPS = 1e-6

LANES = 128
MXU_W = 256
SUBLANES = 8
CHUNK = 16
CHUNK_W = CHUNK * SSM_GROUP
PAIR = 2
PAIR_STATE = PAIR * SSM_STATE
N_COMP = 4
BLOCKS = LANES // SSM_GROUP
N_SLABS = SSM_W // LANES
VMEM_LIMIT = 62 * 1024 * 1024

_BF16 = jnp.bfloat16
_F32 = jnp.float32


def _dot(a, b):
    return jnp.dot(a, b, preferred_element_type=_F32)


def _lane_block_transpose(vs):
    blk = lax.broadcasted_iota(jnp.int32, vs[0].shape, 1) // SSM_GROUP
    dist = BLOCKS // 2
    while dist:
        shift = dist * SSM_GROUP
        upper = (blk & dist) != 0
        out = [None] * BLOCKS
        for a in range(BLOCKS):
            if a & dist:
                continue
            lo, hi = vs[a], vs[a + dist]
            out[a] = jnp.where(upper, pltpu.roll(hi, shift, axis=1), lo)
            out[a + dist] = jnp.where(upper, hi, pltpu.roll(lo, LANES - shift, axis=1))
        vs = out
        dist //= 2
    return vs


def _mod_kernel(c_ref, cctx_ref, w_ref, b_ref, o_ref):
    first = lax.broadcasted_iota(jnp.int32, (SUBLANES, D_MODEL), 0) == 0
    s = jnp.where(first, jnp.broadcast_to(c_ref[...], first.shape), jnp.broadcast_to(cctx_ref[...], first.shape))
    act = s * jax.nn.sigmoid(s)
    o_ref[...] = jnp.dot(act, w_ref[...], preferred_element_type=_F32,
                         precision=lax.Precision.HIGHEST) + b_ref[...]


def _modulation(c_row, cctx_row, w_mod, b_mod):
    n_out = w_mod.shape[1]
    tn = 1024
    return pl.pallas_call(
        _mod_kernel,
        grid=(n_out // tn,),
        in_specs=[pl.BlockSpec((1, D_MODEL), lambda j: (0, 0)),
                  pl.BlockSpec((1, D_MODEL), lambda j: (0, 0)),
                  pl.BlockSpec((D_MODEL, tn), lambda j: (0, j)),
                  pl.BlockSpec((1, tn), lambda j: (0, j))],
        out_specs=pl.BlockSpec((SUBLANES, tn), lambda j: (0, j)),
        out_shape=jax.ShapeDtypeStruct((SUBLANES, n_out), _F32),
        compiler_params=pltpu.CompilerParams(dimension_semantics=("arbitrary",),
                                             vmem_limit_bytes=VMEM_LIMIT),
        name="mod",
    )(c_row, cctx_row, w_mod, b_mod)


def _rms_rows(x, g):
    ms = jnp.mean(x * x, axis=-1, keepdims=True)
    return x * lax.rsqrt(ms + RMS_EPS) * g


def _group_ones(width, group):
    rows = lax.broadcasted_iota(jnp.int32, (width, width), 0) // group
    cols = lax.broadcasted_iota(jnp.int32, (width, width), 1) // group
    return jnp.where(rows == cols, 1.0, 0.0).astype(_BF16)


def _in_kernel(mod_row, x_ref, mods_ref, g1_ref, w_in_ref, convw_ref, gconv_ref,
               conv_ref, ug_ref, u_scr, w_scr, ones_scr):
    @pl.when(pl.program_id(0) == 0)
    def _():
        w_scr[:, 0:SSM_W] = w_in_ref[:, 3 * CONV_W:].astype(_BF16)
        for k in range(CONV_W // MXU_W):
            for part in range(3):
                dst = SSM_W + (3 * k + part) * MXU_W
                src = part * CONV_W + k * MXU_W
                w_scr[:, dst:dst + MXU_W] = w_in_ref[:, src:src + MXU_W].astype(_BF16)
        ones_scr[...] = _group_ones(MXU_W, CONV_W // CONV_HEADS)

    sh1 = mods_ref[mod_row:mod_row + 1, 0:D_MODEL]
    sc1 = mods_ref[mod_row:mod_row + 1, D_MODEL:2 * D_MODEL]
    tm = x_ref.shape[0]
    n_chunks = tm // CHUNK
    pos = lax.broadcasted_iota(jnp.int32, (tm, MXU_W), 0) % GRID_W
    h = _rms_rows(x_ref[...], g1_ref[...]) * (1.0 + sc1) + sh1
    z = _dot(h.astype(_BF16), w_scr[...])
    for q in range(N_SLABS):
        u_scr[q] = z[:, q * LANES:(q + 1) * LANES]
    for k in range(CONV_W // MXU_W):
        c0 = SSM_W + 3 * k * MXU_W
        lanes = slice(k * MXU_W, (k + 1) * MXU_W)
        b = z[:, c0:c0 + MXU_W]
        cv = z[:, c0 + MXU_W:c0 + 2 * MXU_W] * z[:, c0 + 2 * MXU_W:c0 + 3 * MXU_W]
        prev = jnp.where(pos == 0, 0.0, pltpu.roll(cv, 1, axis=0))
        nxt = jnp.where(pos == GRID_W - 1, 0.0, pltpu.roll(cv, tm - 1, axis=0))
        y = b * (convw_ref[0:1, lanes] * prev + convw_ref[1:2, lanes] * cv + convw_ref[2:3, lanes] * nxt)
        ssq = _dot((y * y).astype(_BF16), ones_scr[...])
        yn = y * lax.rsqrt(ssq * (1.0 / (CONV_W // CONV_HEADS)) + RMS_EPS) * gconv_ref[:, lanes]
        conv_ref[:, lanes] = yn.astype(_BF16)
    for q in range(N_SLABS):
        for half in range(CHUNK // BLOCKS):
            rows = [u_scr[q, pl.ds(half * BLOCKS + t, n_chunks, stride=CHUNK), :] for t in range(BLOCKS)]
            cols = _lane_block_transpose(rows)
            for g8 in range(BLOCKS):
                ug_ref[q * BLOCKS + g8, :, half * LANES:(half + 1) * LANES] = cols[g8].astype(_BF16)


def _in_proj(x2d, mods, mod_row, g1, w_in, conv_w, g_conv, tm):
    n = x2d.shape[0]
    const = lambda i: (0, 0)
    return pl.pallas_call(
        functools.partial(_in_kernel, mod_row),
        grid=(n // tm,),
        in_specs=[pl.BlockSpec((tm, D_MODEL), lambda i: (i, 0)),
                  pl.BlockSpec(mods.shape, const),
                  pl.BlockSpec((1, D_MODEL), const),
                  pl.BlockSpec((D_MODEL, IN_W), const, pipeline_mode=pl.Buffered(1)),
                  pl.BlockSpec((3, CONV_W), const),
                  pl.BlockSpec((1, CONV_W), const)],
        out_specs=[pl.BlockSpec((tm, CONV_W), lambda i: (i, 0)),
                   pl.BlockSpec((SSM_GROUPS, tm // CHUNK, CHUNK_W), lambda i: (0, i, 0))],
        out_shape=[jax.ShapeDtypeStruct((n, CONV_W), _BF16),
                   jax.ShapeDtypeStruct((SSM_GROUPS, n // CHUNK, CHUNK_W), _BF16)],
        scratch_shapes=[pltpu.VMEM((N_SLABS, tm, LANES), _F32),
                        pltpu.VMEM((D_MODEL, IN_W), _BF16),
                        pltpu.VMEM((MXU_W, MXU_W), _BF16)],
        compiler_params=pltpu.CompilerParams(dimension_semantics=("arbitrary",),
                                             vmem_limit_bytes=VMEM_LIMIT),
        name="in_proj",
    )(x2d, mods, g1, w_in, conv_w, g_conv)


def _scan_steps(n_rows):
    return max(1, math.ceil(math.log2(n_rows)))


def _ends_pad(n_tiles):
    return max(SUBLANES, 1 << (_scan_steps(n_tiles) - 1))


def _cmul(ar, ai, br, bi):
    return ar * br - ai * bi, ar * bi + ai * br


def _eye(n):
    rows, cols = (lax.broadcasted_iota(jnp.int32, (n, n), axis) for axis in (0, 1))
    return jnp.where(rows == cols, 1.0, 0.0).astype(_BF16)


def _dot_nt_exact(eye, x):
    hi = x.astype(_BF16)
    rest = x - hi.astype(_F32)
    mid = rest.astype(_BF16)
    lo = (rest - mid.astype(_F32)).astype(_BF16)
    nt = lambda term: lax.dot_general(eye, term, (((1,), (1,)), ((), ())), preferred_element_type=_F32)
    return nt(hi) + nt(mid) + nt(lo)


def _ssm_prep(n_steps, par_ref, br_ref, bi_ref, cr_ref, ci_ref, wst_ref, wout_ref, cmat_ref):
    T = CHUNK
    left8 = lax.broadcasted_iota(jnp.int32, (SUBLANES, LANES), 1) < SSM_STATE
    row0 = lax.broadcasted_iota(jnp.int32, (SUBLANES, LANES), 0) == 0

    def member_rows(vf, vb):
        f8 = jnp.broadcast_to(vf, (SUBLANES, LANES))
        b8 = pltpu.roll(jnp.broadcast_to(vb, (SUBLANES, LANES)), SSM_STATE, axis=1)
        return jnp.where(row0, jnp.where(left8, f8, b8), jnp.where(left8, b8, f8))

    pf, pb = par_ref[0, 0], par_ref[1, 0]
    lr = jnp.minimum(member_rows(pf[0:1], pb[0:1]), -1e-4)
    li = member_rows(pf[1:2], pb[1:2])
    dt = jnp.exp(member_rows(pf[2:3], pb[2:3]))
    mag = jnp.exp(lr * dt)
    ab_re = mag * jnp.cos(li * dt)
    ab_im = mag * jnp.sin(li * dt)
    nr = ab_re - 1.0
    den = lr * lr + li * li
    f_re = (nr * lr + ab_im * li) / den
    f_im = (ab_im * lr - nr * li) / den
    pt_re, pt_im = [jnp.ones_like(ab_re)], [jnp.zeros_like(ab_im)]
    for _ in range(T):
        nxt = _cmul(pt_re[-1], pt_im[-1], ab_re, ab_im)
        pt_re.append(nxt[0])
        pt_im.append(nxt[1])

    cols = jnp.concatenate([ab_re, ab_im, jnp.zeros((LANES - 2 * SUBLANES, LANES), _F32)], axis=0).T
    left1 = lax.broadcasted_iota(jnp.int32, (1, LANES), 1) < SSM_STATE
    eye_i = _eye(SSM_GROUP)
    eye_p = _eye(SSM_STATE)
    rows = lax.broadcasted_iota(jnp.int32, (LANES, CHUNK_W), 0)
    tcol = lax.broadcasted_iota(jnp.int32, (LANES, CHUNK_W), 1) // SSM_GROUP
    zero_half = jnp.zeros((SSM_STATE, CHUNK_W), _BF16)
    for e in range(PAIR):
        fwd_first = e == 0
        d0, d1 = (0, 1) if fwd_first else (1, 0)
        b_re = _dot_nt_exact(eye_i, jnp.concatenate([br_ref[d0, e], br_ref[d1, e]], axis=0))
        b_im = _dot_nt_exact(eye_i, jnp.concatenate([bi_ref[d0, e], bi_ref[d1, e]], axis=0))
        bbar_re, bbar_im = _cmul(f_re[e:e + 1], f_im[e:e + 1], b_re, b_im)
        fwd_lane = left1 if fwd_first else jnp.logical_not(left1)
        for s in range(T):
            pr = jnp.where(fwd_lane, pt_re[T - 1 - s][e:e + 1], pt_re[s][e:e + 1])
            pi = jnp.where(fwd_lane, pt_im[T - 1 - s][e:e + 1], pt_im[s][e:e + 1])
            w_re, w_im = _cmul(pr, pi, bbar_re, bbar_im)
            wst_ref[e, s * SSM_GROUP:(s + 1) * SSM_GROUP, 0:LANES] = w_re.astype(_BF16)
            wst_ref[e, s * SSM_GROUP:(s + 1) * SSM_GROUP, LANES:] = w_im.astype(_BF16)
        fwd_row = (rows < SSM_STATE) if fwd_first else (rows >= SSM_STATE)
        expo = jnp.where(fwd_row, tcol + 1, T - tcol)
        q_re = jnp.ones((LANES, CHUNK_W), _F32)
        q_im = jnp.zeros((LANES, CHUNK_W), _F32)
        s_re, s_im = cols[:, e:e + 1], cols[:, SUBLANES + e:SUBLANES + e + 1]
        for bit in range(T.bit_length()):
            m_re, m_im = _cmul(q_re, q_im, jnp.broadcast_to(s_re, q_re.shape), jnp.broadcast_to(s_im, q_re.shape))
            take = ((expo >> bit) & 1) == 1
            q_re, q_im = jnp.where(take, m_re, q_re), jnp.where(take, m_im, q_im)
            s_re, s_im = _cmul(s_re, s_im, s_re, s_im)
        tiled = lambda c: _dot_nt_exact(eye_p, jnp.concatenate([c] * T, axis=0))
        c_re = jnp.concatenate([tiled(cr_ref[d0, e]), tiled(cr_ref[d1, e])], axis=0)
        c_im = jnp.concatenate([tiled(ci_ref[d0, e]), tiled(ci_ref[d1, e])], axis=0)
        o_re, o_im = _cmul(c_re, c_im, q_re, q_im)
        wout_ref[e, 0:LANES] = o_re.astype(_BF16)
        wout_ref[e, LANES:] = (-o_im).astype(_BF16)
        for ri, src in enumerate((c_re, -c_im)):
            for slot, d in enumerate((d0, d1)):
                r0 = ri * LANES + slot * SSM_STATE
                block = src[slot * SSM_STATE:(slot + 1) * SSM_STATE].astype(_BF16)
                cmat_ref[e, r0:r0 + SSM_STATE, d * CHUNK_W:(d + 1) * CHUNK_W] = block
                cmat_ref[e, r0:r0 + SSM_STATE, (1 - d) * CHUNK_W:(2 - d) * CHUNK_W] = zero_half

    m0_re, m1_re = pt_re[T][0:1], pt_re[T][1:2]
    m0_im, m1_im = pt_im[T][0:1], pt_im[T][1:2]
    cf = (jnp.where(left1, m0_re, m1_re), jnp.where(left1, m0_im, m1_im))
    cb = (jnp.where(left1, m1_re, m0_re), jnp.where(left1, m1_im, m0_im))
    coefs = []
    for _ in range(n_steps):
        coefs.append([cf[0], cf[1], cb[0], cb[1]])
        cf = _cmul(cf[0], cf[1], cf[0], cf[1])
        cb = _cmul(cb[0], cb[1], cb[0], cb[1])
    return coefs


def _ssm_kernel(n_lat, n_ctx, u_ref, uc_ref, par_ref, br_ref, bi_ref, cr_ref, ci_ref, d_ref,
                y_ref, buf_a, buf_b, ends_a, ends_b, wst_ref, wout_ref, cmat_ref, m_scr):
    n_rows = n_lat + n_ctx
    pad = SUBLANES
    coefs = _ssm_prep(_scan_steps(n_rows), par_ref, br_ref, bi_ref, cr_ref, ci_ref,
                      wst_ref, wout_ref, cmat_ref)

    lane = lax.broadcasted_iota(jnp.int32, (SSM_GROUP, CHUNK_W), 1)
    blk = lane // SSM_GROUP
    on_diag = lane % SSM_GROUP == lax.broadcasted_iota(jnp.int32, (SSM_GROUP, CHUNK_W), 0)
    for e in range(PAIR):
        d_skip = jnp.where(on_diag, jnp.broadcast_to(d_ref[e], (SSM_GROUP, CHUNK_W)), 0.0)
        taps = _dot(wst_ref[e], cmat_ref[e])
        taps_f = taps[:, 0:CHUNK_W]
        taps_b = taps[:, CHUNK_W:]
        for s in range(CHUNK):
            acc = jnp.zeros((SSM_GROUP, CHUNK_W), _F32)
            for t in range(CHUNK):
                val = None
                if s >= t:
                    r0 = (s - t) * SSM_GROUP
                    val = taps_b[r0:r0 + SSM_GROUP]
                if s <= t:
                    r0 = (CHUNK - 1 - t + s) * SSM_GROUP
                    v2 = taps_f[r0:r0 + SSM_GROUP]
                    val = v2 if val is None else val + v2
                if s == t:
                    val = val + d_skip
                acc = jnp.where(blk == t, val, acc)
            m_scr[e, s * SSM_GROUP:(s + 1) * SSM_GROUP, :] = acc.astype(_BF16)

    def paired(m0, m1):
        left = lax.broadcasted_iota(jnp.int32, (m0.shape[0], PAIR_STATE), 1) < SSM_STATE
        re0, im0 = m0[:, 0:PAIR_STATE], m0[:, PAIR_STATE:]
        re1, im1 = m1[:, 0:PAIR_STATE], m1[:, PAIR_STATE:]
        return [jnp.where(left, re0, re1), jnp.where(left, im0, im1),
                jnp.where(left, re1, re0), jnp.where(left, im1, im0)]

    s_lat = paired(_dot(u_ref[0], wst_ref[0]), _dot(u_ref[1], wst_ref[1]))
    s_ctx = paired(_dot(uc_ref[0], wst_ref[0]), _dot(uc_ref[1], wst_ref[1]))

    n_tiles = n_rows // SUBLANES
    tile_row = lax.broadcasted_iota(jnp.int32, (1, SUBLANES, PAIR_STATE), 1)

    def tile_scan(x_re, x_im, comp0, backward):
        for k in range(SUBLANES.bit_length() - 1):
            sh = 1 << k
            shift, keep = (SUBLANES - sh, tile_row < SUBLANES - sh) if backward else (sh, tile_row >= sh)
            a_re = jnp.where(keep, coefs[k][comp0], 0.0)
            a_im = jnp.where(keep, coefs[k][comp0 + 1], 0.0)
            p_re, p_im = pltpu.roll(x_re, shift, axis=1), pltpu.roll(x_im, shift, axis=1)
            x_re, x_im = x_re + a_re * p_re - a_im * p_im, x_im + a_re * p_im + a_im * p_re
        return x_re, x_im

    as_tiles = lambda parts: jnp.concatenate(parts, axis=0).reshape(n_tiles, SUBLANES, PAIR_STATE)
    local = list(tile_scan(as_tiles([s_ctx[0], s_lat[0]]), as_tiles([s_ctx[1], s_lat[1]]), 0, False))
    local += tile_scan(as_tiles([s_lat[2], s_ctx[2]]), as_tiles([s_lat[3], s_ctx[3]]), 2, True)
    for comp in range(N_COMP):
        buf_a[comp, pad:pad + n_rows] = local[comp].reshape(n_rows, PAIR_STATE)

    for buf in (ends_a, ends_b):
        buf[...] = jnp.zeros(buf.shape, _F32)
    epad = _ends_pad(n_tiles)
    for comp in range(N_COMP):
        end_row = pad + (SUBLANES - 1 if comp < 2 else 0)
        ends_a[comp, epad:epad + n_tiles] = buf_a[comp, pl.ds(end_row, n_tiles, stride=SUBLANES), :]
    src, dst = ends_a, ends_b
    for k in range(_scan_steps(n_tiles)):
        sh = 1 << k
        coef = coefs[SUBLANES.bit_length() - 1 + k]
        lo, hi = epad, epad + n_tiles
        xr, xi = src[0, lo - sh:hi - sh], src[1, lo - sh:hi - sh]
        dst[0, lo:hi] = src[0, lo:hi] + coef[0] * xr - coef[1] * xi
        dst[1, lo:hi] = src[1, lo:hi] + coef[0] * xi + coef[1] * xr
        xr, xi = src[2, lo + sh:hi + sh], src[3, lo + sh:hi + sh]
        dst[2, lo:hi] = src[2, lo:hi] + coef[2] * xr - coef[3] * xi
        dst[3, lo:hi] = src[3, lo:hi] + coef[2] * xi + coef[3] * xr
        src, dst = dst, src

    def carry_powers(comp0, backward):
        base = [(coefs[k][comp0], coefs[k][comp0 + 1]) for k in range(SUBLANES.bit_length())]
        pw = []
        for r in range(1, SUBLANES + 1):
            acc = None
            for k, term in enumerate(base):
                if (r >> k) & 1:
                    acc = term if acc is None else _cmul(acc[0], acc[1], term[0], term[1])
            pw.append(acc)
        if backward:
            pw = pw[::-1]
        return (jnp.concatenate([p[0] for p in pw], axis=0), jnp.concatenate([p[1] for p in pw], axis=0))

    for comp0, backward in ((0, False), (2, True)):
        p_re, p_im = carry_powers(comp0, backward)
        first = epad + 1 if backward else epad - 1
        for i in range(n_tiles):
            c_re = src[comp0, first + i:first + i + 1]
            c_im = src[comp0 + 1, first + i:first + i + 1]
            rows = slice(pad + i * SUBLANES, pad + (i + 1) * SUBLANES)
            buf_b[comp0, rows] = buf_a[comp0, rows] + p_re * c_re - p_im * c_im
            buf_b[comp0 + 1, rows] = buf_a[comp0 + 1, rows] + p_re * c_im + p_im * c_re

    f0 = pad + n_ctx - 1
    b0 = pad + 1
    f_re, f_im = buf_b[0, f0:f0 + n_lat], buf_b[1, f0:f0 + n_lat]
    b_re, b_im = buf_b[2, b0:b0 + n_lat], buf_b[3, b0:b0 + n_lat]
    left = lax.broadcasted_iota(jnp.int32, (n_lat, PAIR_STATE), 1) < SSM_STATE
    h0 = jnp.concatenate([jnp.where(left, f_re, b_re), jnp.where(left, f_im, b_im)], axis=1)
    h1 = jnp.concatenate([jnp.where(left, b_re, f_re), jnp.where(left, b_im, f_im)], axis=1)
    y_ref[0] = _dot(u_ref[0], m_scr[0]) + _dot(h0.astype(_BF16), wout_ref[0])
    y_ref[1] = _dot(u_ref[1], m_scr[1]) + _dot(h1.astype(_BF16), wout_ref[1])


def _ssm(u_g, uc_g, par, b_re, b_im, c_re, c_im, d_col):
    n_groups, n_lat, _ = u_g.shape
    n_ctx = uc_g.shape[1]
    n_rows = n_lat + n_ctx
    n_pairs = n_groups // PAIR
    buf = pltpu.VMEM((N_COMP, n_rows + 2 * SUBLANES, PAIR_STATE), _F32)
    n_tiles = n_rows // SUBLANES
    ends_rows = _ends_pad(n_tiles) + -(-(n_tiles + _ends_pad(n_tiles)) // SUBLANES) * SUBLANES
    ends = pltpu.VMEM((N_COMP, ends_rows, PAIR_STATE), _F32)
    mat = pltpu.VMEM((PAIR, CHUNK_W, CHUNK_W), _BF16)
    pair3 = lambda q: (q, 0, 0)
    by_dir = lambda q: (0, q, 0, 0)
    return pl.pallas_call(
        functools.partial(_ssm_kernel, n_lat, n_ctx),
        grid=(n_pairs,),
        in_specs=[pl.BlockSpec((PAIR, n_lat, CHUNK_W), pair3),
                  pl.BlockSpec((PAIR, n_ctx, CHUNK_W), pair3),
                  pl.BlockSpec((2, 1) + par.shape[2:], by_dir),
                  pl.BlockSpec((2, PAIR) + b_re.shape[2:], by_dir),
                  pl.BlockSpec((2, PAIR) + b_im.shape[2:], by_dir),
                  pl.BlockSpec((2, PAIR) + c_re.shape[2:], by_dir),
                  pl.BlockSpec((2, PAIR) + c_im.shape[2:], by_dir),
                  pl.BlockSpec((PAIR,) + d_col.shape[1:], pair3)],
        out_specs=pl.BlockSpec((PAIR, n_lat, CHUNK_W), pair3),
        out_shape=jax.ShapeDtypeStruct((n_groups, n_lat, CHUNK_W), _F32),
        scratch_shapes=[buf, buf, ends, ends, mat, mat, pltpu.VMEM((PAIR, CHUNK_W, 2 * CHUNK_W), _BF16), mat],
        compiler_params=pltpu.CompilerParams(dimension_semantics=("arbitrary",),
                                             vmem_limit_bytes=VMEM_LIMIT),
        name="ssm",
    )(u_g, uc_g, par, b_re, b_im, c_re, c_im, d_col)


def _out_kernel(x_ref, conv_ref, yg_ref, mods_ref, gssm_ref, g2_ref, gfin_ref,
                wglu32_ref, wout32_ref, w1_ref, w2_ref, o_ref, y_scr, mix_scr, wglu_ref, wout_ref, ones_ref):
    @pl.when(pl.program_id(0) == 0)
    def _():
        wglu_ref[...] = wglu32_ref[...].astype(_BF16)
        wout_ref[...] = wout32_ref[...].astype(_BF16)
        ones_ref[...] = _group_ones(MXU_W, SSM_GROUP)

    x = x_ref[...]
    n_chunks = x.shape[0] // CHUNK
    gt1 = mods_ref[0:1, 2 * D_MODEL:3 * D_MODEL]
    sh2 = mods_ref[0:1, 3 * D_MODEL:4 * D_MODEL]
    sc2 = mods_ref[0:1, 4 * D_MODEL:5 * D_MODEL]
    gt2 = mods_ref[0:1, 5 * D_MODEL:6 * D_MODEL]
    mix_scr[...] = _dot(conv_ref[...], wout_ref[0:CONV_W])
    for q in range(N_SLABS):
        for half in range(CHUNK // BLOCKS):
            cols = [yg_ref[q * BLOCKS + g8, :, half * LANES:(half + 1) * LANES] for g8 in range(BLOCKS)]
            rows = _lane_block_transpose(cols)
            for t in range(BLOCKS):
                y_scr[q, pl.ds(half * BLOCKS + t, n_chunks, stride=CHUNK), :] = rows[t]
    s = jax.nn.gelu(jnp.concatenate([y_scr[q] for q in range(N_SLABS)], axis=1))
    s = s * jax.nn.sigmoid(_dot(s.astype(_BF16), wglu_ref[...]))
    sq = (s * s).astype(_BF16)
    ssq = jnp.concatenate([_dot(sq[:, k * MXU_W:(k + 1) * MXU_W], ones_ref[...])
                           for k in range(SSM_W // MXU_W)], axis=1)
    sn = s * lax.rsqrt(ssq * (1.0 / SSM_GROUP) + RMS_EPS) * gssm_ref[...]
    mix = mix_scr[...] + _dot(sn.astype(_BF16), wout_ref[CONV_W:])
    x_mid = x + gt1 * mix
    h2 = (_rms_rows(x_mid, g2_ref[...]) * (1.0 + sc2) + sh2).astype(_BF16)
    n_split = 4
    ff = D_FF // n_split
    acc = None
    for j in range(n_split):
        a = jnp.maximum(_dot(h2, w1_ref[:, j * ff:(j + 1) * ff]), 0.0)
        part = _dot((a * a).astype(_BF16), w2_ref[j * ff:(j + 1) * ff])
        acc = part if acc is None else acc + part
    x_out = x_mid + gt2 * acc
    o_ref[...] = _rms_rows(x_out, gfin_ref[...])


def _out_proj(x2d, conv_n, y_g, mods, g_ssm, g2, g_fin, wglu, wout, w1, w2, tm):
    n = x2d.shape[0]
    const = lambda i: (0, 0)
    row = lambda i: (i, 0)
    resident = lambda shape: pl.BlockSpec(shape, const, pipeline_mode=pl.Buffered(1))
    return pl.pallas_call(
        _out_kernel,
        grid=(n // tm,),
        in_specs=[pl.BlockSpec((tm, D_MODEL), row),
                  pl.BlockSpec((tm, CONV_W), row),
                  pl.BlockSpec((SSM_GROUPS, tm // CHUNK, CHUNK_W), lambda i: (0, i, 0)),
                  pl.BlockSpec(mods.shape, const),
                  pl.BlockSpec((1, SSM_W), const),
                  pl.BlockSpec((1, D_MODEL), const),
                  pl.BlockSpec((1, D_MODEL), const),
                  resident((SSM_W, SSM_W)),
                  resident((D_MODEL, D_MODEL)),
                  resident((D_MODEL, D_FF)),
                  resident((D_FF, D_MODEL))],
        out_specs=pl.BlockSpec((tm, D_MODEL), row),
        out_shape=jax.ShapeDtypeStruct((n, D_MODEL), _F32),
        scratch_shapes=[pltpu.VMEM((N_SLABS, tm, LANES), _F32), pltpu.VMEM((tm, D_MODEL), _F32),
                        pltpu.VMEM((SSM_W, SSM_W), _BF16), pltpu.VMEM((D_MODEL, D_MODEL), _BF16),
                        pltpu.VMEM((MXU_W, MXU_W), _BF16)],
        compiler_params=pltpu.CompilerParams(dimension_semantics=("arbitrary",),
                                             vmem_limit_bytes=VMEM_LIMIT),
        name="out_proj",
    )(x2d, conv_n, y_g, mods, g_ssm, g2, g_fin, wglu, wout, w1, w2)


def _ssm_lane_params(a_re, a_im, log_dt):
    n_pairs = SSM_GROUPS // PAIR
    pair_lanes = lambda v: v.reshape(2, n_pairs, PAIR * SSM_STATE)
    dt_lanes = jnp.broadcast_to(log_dt[..., None], (2, SSM_GROUPS, SSM_STATE))
    return jnp.stack([pair_lanes(a_re), pair_lanes(a_im), pair_lanes(dt_lanes)], axis=2)


def kernel(x, c, ctx, c_ctx, w_mod, b_mod, g_norm1, w_in, conv_w, ssm_a_re, ssm_a_im, ssm_log_dt,
           ssm_b_re, ssm_b_im, ssm_c_re, ssm_c_im, ssm_d, w_glu, g_conv_out, g_ssm_out, w_out,
           g_norm2, w_mlp1, w_mlp2, g_final):
    bsz, n_lat, d_model = x.shape
    n_ctx = ctx.shape[1]
    assert bsz == 1 and d_model == D_MODEL and w_mod.shape[0] == 1
    assert n_lat % (CHUNK * SUBLANES) == 0 and n_ctx % (CHUNK * CHUNK) == 0 and n_lat % GRID_W == 0
    layer = 0
    x2d = x[0]
    ctx2d = ctx[0]

    mods = _modulation(c, c_ctx[None, :], w_mod[layer], b_mod[layer][None, :])

    g1 = g_norm1[layer][None, :]
    g_conv = g_conv_out[layer][None, :]
    conv_n, u_g = _in_proj(x2d, mods, 0, g1, w_in[layer], conv_w[layer], g_conv, tm=1024)
    _, uc_g = _in_proj(ctx2d, mods, 1, g1, w_in[layer], conv_w[layer], g_conv, tm=n_ctx)

    y_g = _ssm(u_g, uc_g, _ssm_lane_params(ssm_a_re[layer], ssm_a_im[layer], ssm_log_dt[layer]),
               ssm_b_re[layer], ssm_b_im[layer], ssm_c_re[layer], ssm_c_im[layer],
               ssm_d[layer].reshape(SSM_GROUPS, SSM_GROUP, 1))

    out = _out_proj(x2d, conv_n, y_g, mods, g_ssm_out[layer][None, :], g_norm2[layer][None, :],
                    g_final[None, :], w_glu[layer], w_out[layer],
                    w_mlp1[layer].astype(_BF16), w_mlp2[layer].astype(_BF16), tm=1024)
    return out[None]
```

```python
import functools
import math

import jax
import jax.numpy as jnp
from jax import lax
from jax.experimental import pallas as pl
from jax.experimental.pallas import tpu as pltpu

D_MODEL = 1024
GRID_W = 64
CONV_W = 512
CONV_HEADS = 8
SSM_W = 512
SSM_GROUP = 16
SSM_GROUPS = 32
SSM_STATE = 64
IN_W = 3 * CONV_W + SSM_W
D_FF = 4 * D_MODEL
N_MOD = 6
RMS_EPS = 1e-6

LANES = 128
MXU_W = 256
SUBLANES = 8
CHUNK = 16
CHUNK_W = CHUNK * SSM_GROUP
PAIR = 2
PAIR_STATE = PAIR * SSM_STATE
N_COMP = 4
BLOCKS = LANES // SSM_GROUP
N_SLABS = SSM_W // LANES
VMEM_LIMIT = 62 * 1024 * 1024

_BF16 = jnp.bfloat16
_F32 = jnp.float32


def _dot(a, b):
    return jnp.dot(a, b, preferred_element_type=_F32)


def _lane_block_transpose(vs):
    blk = lax.broadcasted_iota(jnp.int32, vs[0].shape, 1) // SSM_GROUP
    dist = BLOCKS // 2
    while dist:
        shift = dist * SSM_GROUP
        upper = (blk & dist) != 0
        out = [None] * BLOCKS
        for a in range(BLOCKS):
            if a & dist:
                continue
            lo, hi = vs[a], vs[a + dist]
            out[a] = jnp.where(upper, pltpu.roll(hi, shift, axis=1), lo)
            out[a + dist] = jnp.where(upper, hi, pltpu.roll(lo, LANES - shift, axis=1))
        vs = out
        dist //= 2
    return vs


def _mod_kernel(c_ref, cctx_ref, w_ref, b_ref, o_ref):
    first = lax.broadcasted_iota(jnp.int32, (SUBLANES, D_MODEL), 0) == 0
    s = jnp.where(first, jnp.broadcast_to(c_ref[...], first.shape), jnp.broadcast_to(cctx_ref[...], first.shape))
    act = s * jax.nn.sigmoid(s)
    w = w_ref[...]
    a_hi, w_hi = act.astype(_BF16), w.astype(_BF16)
    a_lo = (act - a_hi.astype(_F32)).astype(_BF16)
    w_lo = (w - w_hi.astype(_F32)).astype(_BF16)
    by_hi = _dot(jnp.concatenate([a_hi, a_lo], axis=0), w_hi)
    o_ref[...] = by_hi[0:SUBLANES] + by_hi[SUBLANES:] + _dot(a_hi, w_lo) + b_ref[...]


def _modulation(c_row, cctx_row, w_mod, b_mod):
    n_out = w_mod.shape[1]
    tn = 1024
    return pl.pallas_call(
        _mod_kernel,
        grid=(n_out // tn,),
        in_specs=[pl.BlockSpec((1, D_MODEL), lambda j: (0, 0)),
                  pl.BlockSpec((1, D_MODEL), lambda j: (0, 0)),
                  pl.BlockSpec((D_MODEL, tn), lambda j: (0, j)),
                  pl.BlockSpec((1, tn), lambda j: (0, j))],
        out_specs=pl.BlockSpec((SUBLANES, tn), lambda j: (0, j)),
        out_shape=jax.ShapeDtypeStruct((SUBLANES, n_out), _F32),
        compiler_params=pltpu.CompilerParams(dimension_semantics=("arbitrary",),
                                             vmem_limit_bytes=VMEM_LIMIT),
        name="mod",
    )(c_row, cctx_row, w_mod, b_mod)


def _rms_rows(x, g):
    ms = jnp.mean(x * x, axis=-1, keepdims=True)
    return x * lax.rsqrt(ms + RMS_EPS) * g


def _group_ones(width, group):
    rows = lax.broadcasted_iota(jnp.int32, (width, width), 0) // group
    cols = lax.broadcasted_iota(jnp.int32, (width, width), 1) // group
    return jnp.where(rows == cols, 1.0, 0.0).astype(_BF16)


def _in_kernel(mod_row, n_side, x_ref, mods_ref, g1_ref, w_in_ref, convw_ref, gconv_ref, *refs):
    side_in, (conv_ref, ug_ref), side_out = refs[:n_side], refs[n_side:n_side + 2], refs[n_side + 2:2 * n_side + 2]
    u_scr, w_scr, ones_scr = refs[2 * n_side + 2:]
    for src_ref, dst_ref in zip(side_in, side_out):
        dst_ref[...] = src_ref[...].astype(_BF16)

    @pl.when(pl.program_id(0) == 0)
    def _():
        w_scr[:, 0:SSM_W] = w_in_ref[:, 3 * CONV_W:].astype(_BF16)
        for k in range(CONV_W // MXU_W):
            for part in range(3):
                dst = SSM_W + (3 * k + part) * MXU_W
                src = part * CONV_W + k * MXU_W
                w_scr[:, dst:dst + MXU_W] = w_in_ref[:, src:src + MXU_W].astype(_BF16)
        ones_scr[...] = _group_ones(MXU_W, CONV_W // CONV_HEADS)

    sh1 = mods_ref[mod_row:mod_row + 1, 0:D_MODEL]
    sc1 = mods_ref[mod_row:mod_row + 1, D_MODEL:2 * D_MODEL]
    tm = x_ref.shape[0]
    n_chunks = tm // CHUNK
    pos = lax.broadcasted_iota(jnp.int32, (tm, MXU_W), 0) % GRID_W
    h = _rms_rows(x_ref[...], g1_ref[...]) * (1.0 + sc1) + sh1
    z = _dot(h.astype(_BF16), w_scr[...])
    for q in range(N_SLABS):
        u_scr[q] = z[:, q * LANES:(q + 1) * LANES]
    for k in range(CONV_W // MXU_W):
        c0 = SSM_W + 3 * k * MXU_W
        lanes = slice(k * MXU_W, (k + 1) * MXU_W)
        b = z[:, c0:c0 + MXU_W]
        cv = z[:, c0 + MXU_W:c0 + 2 * MXU_W] * z[:, c0 + 2 * MXU_W:c0 + 3 * MXU_W]
        prev = jnp.where(pos == 0, 0.0, pltpu.roll(cv, 1, axis=0))
        nxt = jnp.where(pos == GRID_W - 1, 0.0, pltpu.roll(cv, tm - 1, axis=0))
        y = b * (convw_ref[0:1, lanes] * prev + convw_ref[1:2, lanes] * cv + convw_ref[2:3, lanes] * nxt)
        ssq = _dot((y * y).astype(_BF16), ones_scr[...])
        yn = y * lax.rsqrt(ssq * (1.0 / (CONV_W // CONV_HEADS)) + RMS_EPS) * gconv_ref[:, lanes]
        conv_ref[:, lanes] = yn.astype(_BF16)
    for q in range(N_SLABS):
        for half in range(CHUNK // BLOCKS):
            rows = [u_scr[q, pl.ds(half * BLOCKS + t, n_chunks, stride=CHUNK), :] for t in range(BLOCKS)]
            cols = _lane_block_transpose(rows)
            for g8 in range(BLOCKS):
                ug_ref[q * BLOCKS + g8, :, half * LANES:(half + 1) * LANES] = cols[g8].astype(_BF16)


def _in_proj(x2d, mods, mod_row, g1, w_in, conv_w, g_conv, tm, side_weights=()):
    n = x2d.shape[0]
    steps = n // tm
    const = lambda i: (0, 0)
    row = lambda i: (i, 0)
    side_specs = [pl.BlockSpec((w.shape[0] // steps, w.shape[1]), row) for w in side_weights]
    return pl.pallas_call(
        functools.partial(_in_kernel, mod_row, len(side_weights)),
        grid=(steps,),
        in_specs=[pl.BlockSpec((tm, D_MODEL), row),
                  pl.BlockSpec(mods.shape, const),
                  pl.BlockSpec((1, D_MODEL), const),
                  pl.BlockSpec((D_MODEL, IN_W), const, pipeline_mode=pl.Buffered(1)),
                  pl.BlockSpec((3, CONV_W), const),
                  pl.BlockSpec((1, CONV_W), const)] + side_specs,
        out_specs=[pl.BlockSpec((tm, CONV_W), row),
                   pl.BlockSpec((SSM_GROUPS, tm // CHUNK, CHUNK_W), lambda i: (0, i, 0))] + side_specs,
        out_shape=[jax.ShapeDtypeStruct((n, CONV_W), _BF16),
                   jax.ShapeDtypeStruct((SSM_GROUPS, n // CHUNK, CHUNK_W), _BF16)]
                  + [jax.ShapeDtypeStruct(w.shape, _BF16) for w in side_weights],
        scratch_shapes=[pltpu.VMEM((N_SLABS, tm, LANES), _F32),
                        pltpu.VMEM((D_MODEL, IN_W), _BF16),
                        pltpu.VMEM((MXU_W, MXU_W), _BF16)],
        compiler_params=pltpu.CompilerParams(dimension_semantics=("arbitrary",),
                                             vmem_limit_bytes=VMEM_LIMIT),
        name="in_proj",
    )(x2d, mods, g1, w_in, conv_w, g_conv, *side_weights)


def _scan_steps(n_rows):
    return max(1, math.ceil(math.log2(n_rows)))


def _ends_pad(n_tiles):
    return max(SUBLANES, 1 << (_scan_steps(n_tiles) - 1))


def _cmul(ar, ai, br, bi):
    return ar * br - ai * bi, ar * bi + ai * br


def _eye(n):
    rows, cols = (lax.broadcasted_iota(jnp.int32, (n, n), axis) for axis in (0, 1))
    return jnp.where(rows == cols, 1.0, 0.0).astype(_BF16)


def _dot_nt_exact(eye, x):
    hi = x.astype(_BF16)
    rest = x - hi.astype(_F32)
    mid = rest.astype(_BF16)
    lo = (rest - mid.astype(_F32)).astype(_BF16)
    nt = lambda term: lax.dot_general(eye, term, (((1,), (1,)), ((), ())), preferred_element_type=_F32)
    return nt(hi) + nt(mid) + nt(lo)


def _ssm_prep(n_steps, par_ref, br_ref, bi_ref, cr_ref, ci_ref, wst_ref, wout_ref, cmat_ref):
    T = CHUNK
    left8 = lax.broadcasted_iota(jnp.int32, (SUBLANES, LANES), 1) < SSM_STATE
    row0 = lax.broadcasted_iota(jnp.int32, (SUBLANES, LANES), 0) == 0

    def member_rows(vf, vb):
        f8 = jnp.broadcast_to(vf, (SUBLANES, LANES))
        b8 = pltpu.roll(jnp.broadcast_to(vb, (SUBLANES, LANES)), SSM_STATE, axis=1)
        return jnp.where(row0, jnp.where(left8, f8, b8), jnp.where(left8, b8, f8))

    pf, pb = par_ref[0, 0], par_ref[1, 0]
    lr = jnp.minimum(member_rows(pf[0:1], pb[0:1]), -1e-4)
    li = member_rows(pf[1:2], pb[1:2])
    dt = jnp.exp(member_rows(pf[2:3], pb[2:3]))
    mag = jnp.exp(lr * dt)
    ab_re = mag * jnp.cos(li * dt)
    ab_im = mag * jnp.sin(li * dt)
    nr = ab_re - 1.0
    den = lr * lr + li * li
    f_re = (nr * lr + ab_im * li) / den
    f_im = (ab_im * lr - nr * li) / den
    pt_re, pt_im = [jnp.ones_like(ab_re)], [jnp.zeros_like(ab_im)]
    for _ in range(T):
        nxt = _cmul(pt_re[-1], pt_im[-1], ab_re, ab_im)
        pt_re.append(nxt[0])
        pt_im.append(nxt[1])

    cols = jnp.concatenate([ab_re, ab_im, jnp.zeros((LANES - 2 * SUBLANES, LANES), _F32)], axis=0).T
    left1 = lax.broadcasted_iota(jnp.int32, (1, LANES), 1) < SSM_STATE
    eye_i = _eye(SSM_GROUP)
    eye_p = _eye(SSM_STATE)
    rows = lax.broadcasted_iota(jnp.int32, (LANES, CHUNK_W), 0)
    tcol = lax.broadcasted_iota(jnp.int32, (LANES, CHUNK_W), 1) // SSM_GROUP
    zero_half = jnp.zeros((SSM_STATE, CHUNK_W), _BF16)
    for e in range(PAIR):
        fwd_first = e == 0
        d0, d1 = (0, 1) if fwd_first else (1, 0)
        b_re = _dot_nt_exact(eye_i, jnp.concatenate([br_ref[d0, e], br_ref[d1, e]], axis=0))
        b_im = _dot_nt_exact(eye_i, jnp.concatenate([bi_ref[d0, e], bi_ref[d1, e]], axis=0))
        bbar_re, bbar_im = _cmul(f_re[e:e + 1], f_im[e:e + 1], b_re, b_im)
        fwd_lane = left1 if fwd_first else jnp.logical_not(left1)
        for s in range(T):
            pr = jnp.where(fwd_lane, pt_re[T - 1 - s][e:e + 1], pt_re[s][e:e + 1])
            pi = jnp.where(fwd_lane, pt_im[T - 1 - s][e:e + 1], pt_im[s][e:e + 1])
            w_re, w_im = _cmul(pr, pi, bbar_re, bbar_im)
            wst_ref[e, s * SSM_GROUP:(s + 1) * SSM_GROUP, 0:LANES] = w_re.astype(_BF16)
            wst_ref[e, s * SSM_GROUP:(s + 1) * SSM_GROUP, LANES:] = w_im.astype(_BF16)
        fwd_row = (rows < SSM_STATE) if fwd_first else (rows >= SSM_STATE)
        expo = jnp.where(fwd_row, tcol + 1, T - tcol)
        q_re = jnp.ones((LANES, CHUNK_W), _F32)
        q_im = jnp.zeros((LANES, CHUNK_W), _F32)
        s_re, s_im = cols[:, e:e + 1], cols[:, SUBLANES + e:SUBLANES + e + 1]
        for bit in range(T.bit_length()):
            m_re, m_im = _cmul(q_re, q_im, jnp.broadcast_to(s_re, q_re.shape), jnp.broadcast_to(s_im, q_re.shape))
            take = ((expo >> bit) & 1) == 1
            q_re, q_im = jnp.where(take, m_re, q_re), jnp.where(take, m_im, q_im)
            s_re, s_im = _cmul(s_re, s_im, s_re, s_im)
        tiled = lambda c: _dot_nt_exact(eye_p, jnp.concatenate([c] * T, axis=0))
        c_re = jnp.concatenate([tiled(cr_ref[d0, e]), tiled(cr_ref[d1, e])], axis=0)
        c_im = jnp.concatenate([tiled(ci_ref[d0, e]), tiled(ci_ref[d1, e])], axis=0)
        o_re, o_im = _cmul(c_re, c_im, q_re, q_im)
        wout_ref[e, 0:LANES] = o_re.astype(_BF16)
        wout_ref[e, LANES:] = (-o_im).astype(_BF16)
        for ri, src in enumerate((c_re, -c_im)):
            for slot, d in enumerate((d0, d1)):
                r0 = ri * LANES + slot * SSM_STATE
                block = src[slot * SSM_STATE:(slot + 1) * SSM_STATE].astype(_BF16)
                cmat_ref[e, r0:r0 + SSM_STATE, d * CHUNK_W:(d + 1) * CHUNK_W] = block
                cmat_ref[e, r0:r0 + SSM_STATE, (1 - d) * CHUNK_W:(2 - d) * CHUNK_W] = zero_half

    m0_re, m1_re = pt_re[T][0:1], pt_re[T][1:2]
    m0_im, m1_im = pt_im[T][0:1], pt_im[T][1:2]
    cf = (jnp.where(left1, m0_re, m1_re), jnp.where(left1, m0_im, m1_im))
    cb = (jnp.where(left1, m1_re, m0_re), jnp.where(left1, m1_im, m0_im))
    coefs = []
    for _ in range(n_steps):
        coefs.append([cf[0], cf[1], cb[0], cb[1]])
        cf = _cmul(cf[0], cf[1], cf[0], cf[1])
        cb = _cmul(cb[0], cb[1], cb[0], cb[1])
    return coefs


def _ssm_kernel(n_lat, n_ctx, u_ref, uc_ref, par_ref, br_ref, bi_ref, cr_ref, ci_ref, d_ref,
                y_ref, buf_a, buf_b, ends_a, ends_b, wst_ref, wout_ref, cmat_ref, m_scr):
    n_rows = n_lat + n_ctx
    pad = SUBLANES
    coefs = _ssm_prep(_scan_steps(n_rows), par_ref, br_ref, bi_ref, cr_ref, ci_ref,
                      wst_ref, wout_ref, cmat_ref)

    lane = lax.broadcasted_iota(jnp.int32, (SSM_GROUP, CHUNK_W), 1)
    blk = lane // SSM_GROUP
    on_diag = lane % SSM_GROUP == lax.broadcasted_iota(jnp.int32, (SSM_GROUP, CHUNK_W), 0)
    for e in range(PAIR):
        d_skip = jnp.where(on_diag, jnp.broadcast_to(d_ref[e], (SSM_GROUP, CHUNK_W)), 0.0)
        taps = _dot(wst_ref[e], cmat_ref[e])
        taps_f = taps[:, 0:CHUNK_W]
        taps_b = taps[:, CHUNK_W:]
        for s in range(CHUNK):
            acc = jnp.zeros((SSM_GROUP, CHUNK_W), _F32)
            for t in range(CHUNK):
                val = None
                if s >= t:
                    r0 = (s - t) * SSM_GROUP
                    val = taps_b[r0:r0 + SSM_GROUP]
                if s <= t:
                    r0 = (CHUNK - 1 - t + s) * SSM_GROUP
                    v2 = taps_f[r0:r0 + SSM_GROUP]
                    val = v2 if val is None else val + v2
                if s == t:
                    val = val + d_skip
                acc = jnp.where(blk == t, val, acc)
            m_scr[e, s * SSM_GROUP:(s + 1) * SSM_GROUP, :] = acc.astype(_BF16)

    def paired(m0, m1):
        left = lax.broadcasted_iota(jnp.int32, (m0.shape[0], PAIR_STATE), 1) < SSM_STATE
        re0, im0 = m0[:, 0:PAIR_STATE], m0[:, PAIR_STATE:]
        re1, im1 = m1[:, 0:PAIR_STATE], m1[:, PAIR_STATE:]
        return [jnp.where(left, re0, re1), jnp.where(left, im0, im1),
                jnp.where(left, re1, re0), jnp.where(left, im1, im0)]

    s_lat = paired(_dot(u_ref[0], wst_ref[0]), _dot(u_ref[1], wst_ref[1]))
    s_ctx = paired(_dot(uc_ref[0], wst_ref[0]), _dot(uc_ref[1], wst_ref[1]))

    n_tiles = n_rows // SUBLANES
    tile_row = lax.broadcasted_iota(jnp.int32, (1, SUBLANES, PAIR_STATE), 1)

    def tile_scan(x_re, x_im, comp0, backward):
        for k in range(SUBLANES.bit_length() - 1):
            sh = 1 << k
            shift, keep = (SUBLANES - sh, tile_row < SUBLANES - sh) if backward else (sh, tile_row >= sh)
            a_re = jnp.where(keep, coefs[k][comp0], 0.0)
            a_im = jnp.where(keep, coefs[k][comp0 + 1], 0.0)
            p_re, p_im = pltpu.roll(x_re, shift, axis=1), pltpu.roll(x_im, shift, axis=1)
            x_re, x_im = x_re + a_re * p_re - a_im * p_im, x_im + a_re * p_im + a_im * p_re
        return x_re, x_im

    as_tiles = lambda parts: jnp.concatenate(parts, axis=0).reshape(n_tiles, SUBLANES, PAIR_STATE)
    local = list(tile_scan(as_tiles([s_ctx[0], s_lat[0]]), as_tiles([s_ctx[1], s_lat[1]]), 0, False))
    local += tile_scan(as_tiles([s_lat[2], s_ctx[2]]), as_tiles([s_lat[3], s_ctx[3]]), 2, True)
    for comp in range(N_COMP):
        buf_a[comp, pad:pad + n_rows] = local[comp].reshape(n_rows, PAIR_STATE)

    for buf in (ends_a, ends_b):
        buf[...] = jnp.zeros(buf.shape, _F32)
    epad = _ends_pad(n_tiles)
    for comp in range(N_COMP):
        end_row = pad + (SUBLANES - 1 if comp < 2 else 0)
        ends_a[comp, epad:epad + n_tiles] = buf_a[comp, pl.ds(end_row, n_tiles, stride=SUBLANES), :]
    src, dst = ends_a, ends_b
    for k in range(_scan_steps(n_tiles)):
        sh = 1 << k
        coef = coefs[SUBLANES.bit_length() - 1 + k]
        lo, hi = epad, epad + n_tiles
        xr, xi = src[0, lo - sh:hi - sh], src[1, lo - sh:hi - sh]
        dst[0, lo:hi] = src[0, lo:hi] + coef[0] * xr - coef[1] * xi
        dst[1, lo:hi] = src[1, lo:hi] + coef[0] * xi + coef[1] * xr
        xr, xi = src[2, lo + sh:hi + sh], src[3, lo + sh:hi + sh]
        dst[2, lo:hi] = src[2, lo:hi] + coef[2] * xr - coef[3] * xi
        dst[3, lo:hi] = src[3, lo:hi] + coef[2] * xi + coef[3] * xr
        src, dst = dst, src

    def carry_powers(comp0, backward):
        base = [(coefs[k][comp0], coefs[k][comp0 + 1]) for k in range(SUBLANES.bit_length())]
        pw = []
        for r in range(1, SUBLANES + 1):
            acc = None
            for k, term in enumerate(base):
                if (r >> k) & 1:
                    acc = term if acc is None else _cmul(acc[0], acc[1], term[0], term[1])
            pw.append(acc)
        if backward:
            pw = pw[::-1]
        return (jnp.concatenate([p[0] for p in pw], axis=0), jnp.concatenate([p[1] for p in pw], axis=0))

    for comp0, backward in ((0, False), (2, True)):
        p_re, p_im = carry_powers(comp0, backward)
        first = epad + 1 if backward else epad - 1
        for i in range(n_tiles):
            c_re = src[comp0, first + i:first + i + 1]
            c_im = src[comp0 + 1, first + i:first + i + 1]
            rows = slice(pad + i * SUBLANES, pad + (i + 1) * SUBLANES)
            buf_b[comp0, rows] = buf_a[comp0, rows] + p_re * c_re - p_im * c_im
            buf_b[comp0 + 1, rows] = buf_a[comp0 + 1, rows] + p_re * c_im + p_im * c_re

    f0 = pad + n_ctx - 1
    b0 = pad + 1
    f_re, f_im = buf_b[0, f0:f0 + n_lat], buf_b[1, f0:f0 + n_lat]
    b_re, b_im = buf_b[2, b0:b0 + n_lat], buf_b[3, b0:b0 + n_lat]
    left = lax.broadcasted_iota(jnp.int32, (n_lat, PAIR_STATE), 1) < SSM_STATE
    h0 = jnp.concatenate([jnp.where(left, f_re, b_re), jnp.where(left, f_im, b_im)], axis=1)
    h1 = jnp.concatenate([jnp.where(left, b_re, f_re), jnp.where(left, b_im, f_im)], axis=1)
    y_ref[0] = _dot(u_ref[0], m_scr[0]) + _dot(h0.astype(_BF16), wout_ref[0])
    y_ref[1] = _dot(u_ref[1], m_scr[1]) + _dot(h1.astype(_BF16), wout_ref[1])


def _ssm(u_g, uc_g, par, b_re, b_im, c_re, c_im, d_col):
    n_groups, n_lat, _ = u_g.shape
    n_ctx = uc_g.shape[1]
    n_rows = n_lat + n_ctx
    n_pairs = n_groups // PAIR
    buf = pltpu.VMEM((N_COMP, n_rows + 2 * SUBLANES, PAIR_STATE), _F32)
    n_tiles = n_rows // SUBLANES
    ends_rows = _ends_pad(n_tiles) + -(-(n_tiles + _ends_pad(n_tiles)) // SUBLANES) * SUBLANES
    ends = pltpu.VMEM((N_COMP, ends_rows, PAIR_STATE), _F32)
    mat = pltpu.VMEM((PAIR, CHUNK_W, CHUNK_W), _BF16)
    pair3 = lambda q: (q, 0, 0)
    by_dir = lambda q: (0, q, 0, 0)
    return pl.pallas_call(
        functools.partial(_ssm_kernel, n_lat, n_ctx),
        grid=(n_pairs,),
        in_specs=[pl.BlockSpec((PAIR, n_lat, CHUNK_W), pair3),
                  pl.BlockSpec((PAIR, n_ctx, CHUNK_W), pair3),
                  pl.BlockSpec((2, 1) + par.shape[2:], by_dir),
                  pl.BlockSpec((2, PAIR) + b_re.shape[2:], by_dir),
                  pl.BlockSpec((2, PAIR) + b_im.shape[2:], by_dir),
                  pl.BlockSpec((2, PAIR) + c_re.shape[2:], by_dir),
                  pl.BlockSpec((2, PAIR) + c_im.shape[2:], by_dir),
                  pl.BlockSpec((PAIR,) + d_col.shape[1:], pair3)],
        out_specs=pl.BlockSpec((PAIR, n_lat, CHUNK_W), pair3),
        out_shape=jax.ShapeDtypeStruct((n_groups, n_lat, CHUNK_W), _F32),
        scratch_shapes=[buf, buf, ends, ends, mat, mat, pltpu.VMEM((PAIR, CHUNK_W, 2 * CHUNK_W), _BF16), mat],
        compiler_params=pltpu.CompilerParams(dimension_semantics=("arbitrary",),
                                             vmem_limit_bytes=VMEM_LIMIT),
        name="ssm",
    )(u_g, uc_g, par, b_re, b_im, c_re, c_im, d_col)


def _out_kernel(x_ref, conv_ref, yg_ref, mods_ref, gssm_ref, g2_ref, gfin_ref,
                wglu32_ref, wout32_ref, w1_ref, w2_ref, o_ref, y_scr, mix_scr, wglu_ref, wout_ref, ones_ref):
    @pl.when(pl.program_id(0) == 0)
    def _():
        wglu_ref[...] = wglu32_ref[...].astype(_BF16)
        wout_ref[...] = wout32_ref[...].astype(_BF16)
        ones_ref[...] = _group_ones(MXU_W, SSM_GROUP)

    x = x_ref[...]
    n_chunks = x.shape[0] // CHUNK
    gt1 = mods_ref[0:1, 2 * D_MODEL:3 * D_MODEL]
    sh2 = mods_ref[0:1, 3 * D_MODEL:4 * D_MODEL]
    sc2 = mods_ref[0:1, 4 * D_MODEL:5 * D_MODEL]
    gt2 = mods_ref[0:1, 5 * D_MODEL:6 * D_MODEL]
    mix_scr[...] = _dot(conv_ref[...], wout_ref[0:CONV_W])
    for q in range(N_SLABS):
        for half in range(CHUNK // BLOCKS):
            cols = [yg_ref[q * BLOCKS + g8, :, half * LANES:(half + 1) * LANES] for g8 in range(BLOCKS)]
            rows = _lane_block_transpose(cols)
            for t in range(BLOCKS):
                y_scr[q, pl.ds(half * BLOCKS + t, n_chunks, stride=CHUNK), :] = rows[t]
    s = jax.nn.gelu(jnp.concatenate([y_scr[q] for q in range(N_SLABS)], axis=1))
    s = s * jax.nn.sigmoid(_dot(s.astype(_BF16), wglu_ref[...]))
    sq = (s * s).astype(_BF16)
    ssq = jnp.concatenate([_dot(sq[:, k * MXU_W:(k + 1) * MXU_W], ones_ref[...])
                           for k in range(SSM_W // MXU_W)], axis=1)
    sn = s * lax.rsqrt(ssq * (1.0 / SSM_GROUP) + RMS_EPS) * gssm_ref[...]
    mix = mix_scr[...] + _dot(sn.astype(_BF16), wout_ref[CONV_W:])
    x_mid = x + gt1 * mix
    h2 = (_rms_rows(x_mid, g2_ref[...]) * (1.0 + sc2) + sh2).astype(_BF16)
    n_split = 4
    ff = D_FF // n_split
    acc = None
    for j in range(n_split):
        a = jnp.maximum(_dot(h2, w1_ref[:, j * ff:(j + 1) * ff]), 0.0)
        part = _dot((a * a).astype(_BF16), w2_ref[j * ff:(j + 1) * ff])
        acc = part if acc is None else acc + part
    x_out = x_mid + gt2 * acc
    o_ref[...] = _rms_rows(x_out, gfin_ref[...])


def _out_proj(x2d, conv_n, y_g, mods, g_ssm, g2, g_fin, wglu, wout, w1, w2, tm):
    n = x2d.shape[0]
    const = lambda i: (0, 0)
    row = lambda i: (i, 0)
    resident = lambda shape: pl.BlockSpec(shape, const, pipeline_mode=pl.Buffered(1))
    return pl.pallas_call(
        _out_kernel,
        grid=(n // tm,),
        in_specs=[pl.BlockSpec((tm, D_MODEL), row),
                  pl.BlockSpec((tm, CONV_W), row),
                  pl.BlockSpec((SSM_GROUPS, tm // CHUNK, CHUNK_W), lambda i: (0, i, 0)),
                  pl.BlockSpec(mods.shape, const),
                  pl.BlockSpec((1, SSM_W), const),
                  pl.BlockSpec((1, D_MODEL), const),
                  pl.BlockSpec((1, D_MODEL), const),
                  resident((SSM_W, SSM_W)),
                  resident((D_MODEL, D_MODEL)),
                  resident((D_MODEL, D_FF)),
                  resident((D_FF, D_MODEL))],
        out_specs=pl.BlockSpec((tm, D_MODEL), row),
        out_shape=jax.ShapeDtypeStruct((n, D_MODEL), _F32),
        scratch_shapes=[pltpu.VMEM((N_SLABS, tm, LANES), _F32), pltpu.VMEM((tm, D_MODEL), _F32),
                        pltpu.VMEM((SSM_W, SSM_W), _BF16), pltpu.VMEM((D_MODEL, D_MODEL), _BF16),
                        pltpu.VMEM((MXU_W, MXU_W), _BF16)],
        compiler_params=pltpu.CompilerParams(dimension_semantics=("arbitrary",),
                                             vmem_limit_bytes=VMEM_LIMIT),
        name="out_proj",
    )(x2d, conv_n, y_g, mods, g_ssm, g2, g_fin, wglu, wout, w1, w2)


def _ssm_lane_params(a_re, a_im, log_dt):
    n_pairs = SSM_GROUPS // PAIR
    pair_lanes = lambda v: v.reshape(2, n_pairs, PAIR * SSM_STATE)
    dt_lanes = jnp.broadcast_to(log_dt[..., None], (2, SSM_GROUPS, SSM_STATE))
    return jnp.stack([pair_lanes(a_re), pair_lanes(a_im), pair_lanes(dt_lanes)], axis=2)


def kernel(x, c, ctx, c_ctx, w_mod, b_mod, g_norm1, w_in, conv_w, ssm_a_re, ssm_a_im, ssm_log_dt,
           ssm_b_re, ssm_b_im, ssm_c_re, ssm_c_im, ssm_d, w_glu, g_conv_out, g_ssm_out, w_out,
           g_norm2, w_mlp1, w_mlp2, g_final):
    bsz, n_lat, d_model = x.shape
    n_ctx = ctx.shape[1]
    assert bsz == 1 and d_model == D_MODEL and w_mod.shape[0] == 1
    assert n_lat % (CHUNK * SUBLANES) == 0 and n_ctx % (CHUNK * CHUNK) == 0 and n_lat % GRID_W == 0
    layer = 0
    x2d = x[0]
    ctx2d = ctx[0]

    mods = _modulation(c, c_ctx[None, :], w_mod[layer], b_mod[layer][None, :])

    g1 = g_norm1[layer][None, :]
    g_conv = g_conv_out[layer][None, :]
    conv_n, u_g, w1_bf, w2_bf = _in_proj(x2d, mods, 0, g1, w_in[layer], conv_w[layer], g_conv, tm=1024,
                                          side_weights=(w_mlp1[layer], w_mlp2[layer]))
    _, uc_g = _in_proj(ctx2d, mods, 1, g1, w_in[layer], conv_w[layer], g_conv, tm=n_ctx)

    y_g = _ssm(u_g, uc_g, _ssm_lane_params(ssm_a_re[layer], ssm_a_im[layer], ssm_log_dt[layer]),
               ssm_b_re[layer], ssm_b_im[layer], ssm_c_re[layer], ssm_c_im[layer],
               ssm_d[layer].reshape(SSM_GROUPS, SSM_GROUP, 1))

    out = _out_proj(x2d, conv_n, y_g, mods, g_ssm_out[layer][None, :], g_norm2[layer][None, :],
                    g_final[None, :], w_glu[layer], w_out[layer], w1_bf, w2_bf, tm=1024)
    return out[None]
```

```python
import functools
import math

import jax
import jax.numpy as jnp
from jax import lax
from jax.experimental import pallas as pl
from jax.experimental.pallas import tpu as pltpu

D_MODEL = 1024
GRID_W = 64
CONV_W = 512
CONV_HEADS = 8
SSM_W = 512
SSM_GROUP = 16
SSM_GROUPS = 32
SSM_STATE = 64
IN_W = 3 * CONV_W + SSM_W
D_FF = 4 * D_MODEL
N_MOD = 6
RMS_EPS = 1e-6

LANES = 128
MXU_W = 256
SUBLANES = 8
CHUNK = 16
CHUNK_W = CHUNK * SSM_GROUP
PAIR = 2
PAIR_STATE = PAIR * SSM_STATE
N_COMP = 4
BLOCKS = LANES // SSM_GROUP
N_SLABS = SSM_W // LANES
VMEM_LIMIT = 62 * 1024 * 1024

_BF16 = jnp.bfloat16
_F32 = jnp.float32


def _dot(a, b):
    return jnp.dot(a, b, preferred_element_type=_F32)


def _lane_block_transpose(vs):
    blk = lax.broadcasted_iota(jnp.int32, vs[0].shape, 1) // SSM_GROUP
    dist = BLOCKS // 2
    while dist:
        shift = dist * SSM_GROUP
        upper = (blk & dist) != 0
        out = [None] * BLOCKS
        for a in range(BLOCKS):
            if a & dist:
                continue
            lo, hi = vs[a], vs[a + dist]
            out[a] = jnp.where(upper, pltpu.roll(hi, shift, axis=1), lo)
            out[a + dist] = jnp.where(upper, hi, pltpu.roll(lo, LANES - shift, axis=1))
        vs = out
        dist //= 2
    return vs


def _mod_kernel(c_ref, cctx_ref, w_ref, b_ref, o_ref):
    first = lax.broadcasted_iota(jnp.int32, (SUBLANES, D_MODEL), 0) == 0
    s = jnp.where(first, jnp.broadcast_to(c_ref[...], first.shape), jnp.broadcast_to(cctx_ref[...], first.shape))
    act = s * jax.nn.sigmoid(s)
    w = w_ref[...]
    a_hi, w_hi = act.astype(_BF16), w.astype(_BF16)
    a_lo = (act - a_hi.astype(_F32)).astype(_BF16)
    w_lo = (w - w_hi.astype(_F32)).astype(_BF16)
    by_hi = _dot(jnp.concatenate([a_hi, a_lo], axis=0), w_hi)
    o_ref[...] = by_hi[0:SUBLANES] + by_hi[SUBLANES:] + _dot(a_hi, w_lo) + b_ref[...]


def _modulation(c_row, cctx_row, w_mod, b_mod):
    n_out = w_mod.shape[1]
    tn = 2048
    return pl.pallas_call(
        _mod_kernel,
        grid=(n_out // tn,),
        in_specs=[pl.BlockSpec((1, D_MODEL), lambda j: (0, 0)),
                  pl.BlockSpec((1, D_MODEL), lambda j: (0, 0)),
                  pl.BlockSpec((D_MODEL, tn), lambda j: (0, j)),
                  pl.BlockSpec((1, tn), lambda j: (0, j))],
        out_specs=pl.BlockSpec((SUBLANES, tn), lambda j: (0, j)),
        out_shape=jax.ShapeDtypeStruct((SUBLANES, n_out), _F32),
        compiler_params=pltpu.CompilerParams(dimension_semantics=("arbitrary",),
                                             vmem_limit_bytes=VMEM_LIMIT),
        name="mod",
    )(c_row, cctx_row, w_mod, b_mod)


def _rms_rows(x, g):
    ms = jnp.mean(x * x, axis=-1, keepdims=True)
    return x * lax.rsqrt(ms + RMS_EPS) * g


def _group_ones(width, group):
    rows = lax.broadcasted_iota(jnp.int32, (width, width), 0) // group
    cols = lax.broadcasted_iota(jnp.int32, (width, width), 1) // group
    return jnp.where(rows == cols, 1.0, 0.0).astype(_BF16)


def _in_kernel(n_side, x_ref, ctx_ref, mods_ref, g1_ref, w_in_ref, convw_ref, gconv_ref, *refs):
    side_in, (conv_ref, ug_ref, ucg_ref) = refs[:n_side], refs[n_side:n_side + 3]
    side_out = refs[n_side + 3:2 * n_side + 3]
    u_scr, w_scr, ones_scr = refs[2 * n_side + 3:]
    step = pl.program_id(0)
    last = pl.num_programs(0) - 1

    @pl.when(step == 0)
    def _():
        w_scr[:, 0:SSM_W] = w_in_ref[:, 3 * CONV_W:].astype(_BF16)
        for k in range(CONV_W // MXU_W):
            for part in range(3):
                dst = SSM_W + (3 * k + part) * MXU_W
                src = part * CONV_W + k * MXU_W
                w_scr[:, dst:dst + MXU_W] = w_in_ref[:, src:src + MXU_W].astype(_BF16)
        ones_scr[...] = _group_ones(MXU_W, CONV_W // CONV_HEADS)

    def normed(rows_ref, mod_row):
        sh1 = mods_ref[mod_row:mod_row + 1, 0:D_MODEL]
        sc1 = mods_ref[mod_row:mod_row + 1, D_MODEL:2 * D_MODEL]
        return (_rms_rows(rows_ref[...], g1_ref[...]) * (1.0 + sc1) + sh1).astype(_BF16)

    def to_group_major(z_u, out_ref):
        n_rows = z_u.shape[0]
        for q in range(N_SLABS):
            u_scr[q, 0:n_rows] = z_u[:, q * LANES:(q + 1) * LANES]
        for q in range(N_SLABS):
            for half in range(CHUNK // BLOCKS):
                rows = [u_scr[q, pl.ds(half * BLOCKS + t, n_rows // CHUNK, stride=CHUNK), :] for t in range(BLOCKS)]
                cols = _lane_block_transpose(rows)
                for g8 in range(BLOCKS):
                    out_ref[q * BLOCKS + g8, :, half * LANES:(half + 1) * LANES] = cols[g8].astype(_BF16)

    @pl.when(step < last)
    def _():
        for src_ref, dst_ref in zip(side_in, side_out):
            dst_ref[...] = src_ref[...].astype(_BF16)
        tm = x_ref.shape[0]
        pos = lax.broadcasted_iota(jnp.int32, (tm, MXU_W), 0) % GRID_W
        z = _dot(normed(x_ref, 0), w_scr[...])
        to_group_major(z[:, 0:SSM_W], ug_ref)
        for k in range(CONV_W // MXU_W):
            c0 = SSM_W + 3 * k * MXU_W
            lanes = slice(k * MXU_W, (k + 1) * MXU_W)
            b = z[:, c0:c0 + MXU_W]
            cv = z[:, c0 + MXU_W:c0 + 2 * MXU_W] * z[:, c0 + 2 * MXU_W:c0 + 3 * MXU_W]
            prev = jnp.where(pos == 0, 0.0, pltpu.roll(cv, 1, axis=0))
            nxt = jnp.where(pos == GRID_W - 1, 0.0, pltpu.roll(cv, tm - 1, axis=0))
            y = b * (convw_ref[0:1, lanes] * prev + convw_ref[1:2, lanes] * cv + convw_ref[2:3, lanes] * nxt)
            ssq = _dot((y * y).astype(_BF16), ones_scr[...])
            yn = y * lax.rsqrt(ssq * (1.0 / (CONV_W // CONV_HEADS)) + RMS_EPS) * gconv_ref[:, lanes]
            conv_ref[:, lanes] = yn.astype(_BF16)

    @pl.when(step == last)
    def _():
        to_group_major(_dot(normed(ctx_ref, 1), w_scr[:, 0:SSM_W]), ucg_ref)


def _in_proj(x2d, ctx2d, mods, g1, w_in, conv_w, g_conv, tm, side_weights):
    n, n_ctx = x2d.shape[0], ctx2d.shape[0]
    steps = n // tm
    const = lambda i: (0, 0)
    row = lambda i: (jnp.minimum(i, steps - 1), 0)
    side_specs = [pl.BlockSpec((w.shape[0] // steps, w.shape[1]), row) for w in side_weights]
    return pl.pallas_call(
        functools.partial(_in_kernel, len(side_weights)),
        grid=(steps + 1,),
        in_specs=[pl.BlockSpec((tm, D_MODEL), row),
                  pl.BlockSpec((n_ctx, D_MODEL), const),
                  pl.BlockSpec(mods.shape, const),
                  pl.BlockSpec((1, D_MODEL), const),
                  pl.BlockSpec((D_MODEL, IN_W), const, pipeline_mode=pl.Buffered(1)),
                  pl.BlockSpec((3, CONV_W), const),
                  pl.BlockSpec((1, CONV_W), const)] + side_specs,
        out_specs=[pl.BlockSpec((tm, CONV_W), row),
                   pl.BlockSpec((SSM_GROUPS, tm // CHUNK, CHUNK_W), lambda i: (0, jnp.minimum(i, steps - 1), 0)),
                   pl.BlockSpec((SSM_GROUPS, n_ctx // CHUNK, CHUNK_W), lambda i: (0, 0, 0))] + side_specs,
        out_shape=[jax.ShapeDtypeStruct((n, CONV_W), _BF16),
                   jax.ShapeDtypeStruct((SSM_GROUPS, n // CHUNK, CHUNK_W), _BF16),
                   jax.ShapeDtypeStruct((SSM_GROUPS, n_ctx // CHUNK, CHUNK_W), _BF16)]
                  + [jax.ShapeDtypeStruct(w.shape, _BF16) for w in side_weights],
        scratch_shapes=[pltpu.VMEM((N_SLABS, tm, LANES), _F32),
                        pltpu.VMEM((D_MODEL, IN_W), _BF16),
                        pltpu.VMEM((MXU_W, MXU_W), _BF16)],
        compiler_params=pltpu.CompilerParams(dimension_semantics=("arbitrary",),
                                             vmem_limit_bytes=VMEM_LIMIT),
        name="in_proj",
    )(x2d, ctx2d, mods, g1, w_in, conv_w, g_conv, *side_weights)


def _scan_steps(n_rows):
    return max(1, math.ceil(math.log2(n_rows)))


def _ends_pad(n_tiles):
    return max(SUBLANES, 1 << (_scan_steps(n_tiles) - 1))


def _cmul(ar, ai, br, bi):
    return ar * br - ai * bi, ar * bi + ai * br


def _eye(n):
    rows, cols = (lax.broadcasted_iota(jnp.int32, (n, n), axis) for axis in (0, 1))
    return jnp.where(rows == cols, 1.0, 0.0).astype(_BF16)


def _dot_nt_exact(eye, x):
    hi = x.astype(_BF16)
    rest = x - hi.astype(_F32)
    mid = rest.astype(_BF16)
    lo = (rest - mid.astype(_F32)).astype(_BF16)
    nt = lambda term: lax.dot_general(eye, term, (((1,), (1,)), ((), ())), preferred_element_type=_F32)
    return nt(hi) + nt(mid) + nt(lo)


def _ssm_prep(n_steps, par_ref, br_ref, bi_ref, cr_ref, ci_ref, wst_ref, wout_ref, cmat_ref):
    T = CHUNK
    left8 = lax.broadcasted_iota(jnp.int32, (SUBLANES, LANES), 1) < SSM_STATE
    row0 = lax.broadcasted_iota(jnp.int32, (SUBLANES, LANES), 0) == 0

    def member_rows(vf, vb):
        f8 = jnp.broadcast_to(vf, (SUBLANES, LANES))
        b8 = pltpu.roll(jnp.broadcast_to(vb, (SUBLANES, LANES)), SSM_STATE, axis=1)
        return jnp.where(row0, jnp.where(left8, f8, b8), jnp.where(left8, b8, f8))

    pf, pb = par_ref[0, 0], par_ref[1, 0]
    lr = jnp.minimum(member_rows(pf[0:1], pb[0:1]), -1e-4)
    li = member_rows(pf[1:2], pb[1:2])
    dt = jnp.exp(member_rows(pf[2:3], pb[2:3]))
    mag = jnp.exp(lr * dt)
    ab_re = mag * jnp.cos(li * dt)
    ab_im = mag * jnp.sin(li * dt)
    nr = ab_re - 1.0
    den = lr * lr + li * li
    f_re = (nr * lr + ab_im * li) / den
    f_im = (ab_im * lr - nr * li) / den
    pt_re, pt_im = [jnp.ones_like(ab_re)], [jnp.zeros_like(ab_im)]
    for _ in range(T):
        nxt = _cmul(pt_re[-1], pt_im[-1], ab_re, ab_im)
        pt_re.append(nxt[0])
        pt_im.append(nxt[1])

    cols = jnp.concatenate([ab_re, ab_im, jnp.zeros((LANES - 2 * SUBLANES, LANES), _F32)], axis=0).T
    left1 = lax.broadcasted_iota(jnp.int32, (1, LANES), 1) < SSM_STATE
    eye_i = _eye(SSM_GROUP)
    eye_p = _eye(SSM_STATE)
    rows = lax.broadcasted_iota(jnp.int32, (LANES, CHUNK_W), 0)
    tcol = lax.broadcasted_iota(jnp.int32, (LANES, CHUNK_W), 1) // SSM_GROUP
    zero_half = jnp.zeros((SSM_STATE, CHUNK_W), _BF16)
    for e in range(PAIR):
        fwd_first = e == 0
        d0, d1 = (0, 1) if fwd_first else (1, 0)
        b_re = _dot_nt_exact(eye_i, jnp.concatenate([br_ref[d0, e], br_ref[d1, e]], axis=0))
        b_im = _dot_nt_exact(eye_i, jnp.concatenate([bi_ref[d0, e], bi_ref[d1, e]], axis=0))
        bbar_re, bbar_im = _cmul(f_re[e:e + 1], f_im[e:e + 1], b_re, b_im)
        fwd_lane = left1 if fwd_first else jnp.logical_not(left1)
        for s in range(T):
            pr = jnp.where(fwd_lane, pt_re[T - 1 - s][e:e + 1], pt_re[s][e:e + 1])
            pi = jnp.where(fwd_lane, pt_im[T - 1 - s][e:e + 1], pt_im[s][e:e + 1])
            w_re, w_im = _cmul(pr, pi, bbar_re, bbar_im)
            wst_ref[e, s * SSM_GROUP:(s + 1) * SSM_GROUP, 0:LANES] = w_re.astype(_BF16)
            wst_ref[e, s * SSM_GROUP:(s + 1) * SSM_GROUP, LANES:] = w_im.astype(_BF16)
        fwd_row = (rows < SSM_STATE) if fwd_first else (rows >= SSM_STATE)
        expo = jnp.where(fwd_row, tcol + 1, T - tcol)
        q_re = jnp.ones((LANES, CHUNK_W), _F32)
        q_im = jnp.zeros((LANES, CHUNK_W), _F32)
        s_re, s_im = cols[:, e:e + 1], cols[:, SUBLANES + e:SUBLANES + e + 1]
        for bit in range(T.bit_length()):
            m_re, m_im = _cmul(q_re, q_im, jnp.broadcast_to(s_re, q_re.shape), jnp.broadcast_to(s_im, q_re.shape))
            take = ((expo >> bit) & 1) == 1
            q_re, q_im = jnp.where(take, m_re, q_re), jnp.where(take, m_im, q_im)
            s_re, s_im = _cmul(s_re, s_im, s_re, s_im)
        tiled = lambda c: _dot_nt_exact(eye_p, jnp.concatenate([c] * T, axis=0))
        c_re = jnp.concatenate([tiled(cr_ref[d0, e]), tiled(cr_ref[d1, e])], axis=0)
        c_im = jnp.concatenate([tiled(ci_ref[d0, e]), tiled(ci_ref[d1, e])], axis=0)
        o_re, o_im = _cmul(c_re, c_im, q_re, q_im)
        wout_ref[e, 0:LANES] = o_re.astype(_BF16)
        wout_ref[e, LANES:] = (-o_im).astype(_BF16)
        for ri, src in enumerate((c_re, -c_im)):
            for slot, d in enumerate((d0, d1)):
                r0 = ri * LANES + slot * SSM_STATE
                block = src[slot * SSM_STATE:(slot + 1) * SSM_STATE].astype(_BF16)
                cmat_ref[e, r0:r0 + SSM_STATE, d * CHUNK_W:(d + 1) * CHUNK_W] = block
                cmat_ref[e, r0:r0 + SSM_STATE, (1 - d) * CHUNK_W:(2 - d) * CHUNK_W] = zero_half

    m0_re, m1_re = pt_re[T][0:1], pt_re[T][1:2]
    m0_im, m1_im = pt_im[T][0:1], pt_im[T][1:2]
    cf = (jnp.where(left1, m0_re, m1_re), jnp.where(left1, m0_im, m1_im))
    cb = (jnp.where(left1, m1_re, m0_re), jnp.where(left1, m1_im, m0_im))
    coefs = []
    for _ in range(n_steps):
        coefs.append([cf[0], cf[1], cb[0], cb[1]])
        cf = _cmul(cf[0], cf[1], cf[0], cf[1])
        cb = _cmul(cb[0], cb[1], cb[0], cb[1])
    return coefs


def _ssm_kernel(n_lat, n_ctx, u_ref, uc_ref, par_ref, br_ref, bi_ref, cr_ref, ci_ref, d_ref,
                y_ref, buf_a, buf_b, ends_a, ends_b, wst_ref, wout_ref, cmat_ref, m_scr):
    n_rows = n_lat + n_ctx
    pad = SUBLANES
    coefs = _ssm_prep(_scan_steps(n_rows), par_ref, br_ref, bi_ref, cr_ref, ci_ref,
                      wst_ref, wout_ref, cmat_ref)

    lane = lax.broadcasted_iota(jnp.int32, (SSM_GROUP, CHUNK_W), 1)
    blk = lane // SSM_GROUP
    on_diag = lane % SSM_GROUP == lax.broadcasted_iota(jnp.int32, (SSM_GROUP, CHUNK_W), 0)
    for e in range(PAIR):
        d_skip = jnp.where(on_diag, jnp.broadcast_to(d_ref[e], (SSM_GROUP, CHUNK_W)), 0.0)
        taps = _dot(wst_ref[e], cmat_ref[e])
        taps_f = taps[:, 0:CHUNK_W]
        taps_b = taps[:, CHUNK_W:]
        for s in range(CHUNK):
            acc = jnp.zeros((SSM_GROUP, CHUNK_W), _F32)
            for t in range(CHUNK):
                val = None
                if s >= t:
                    r0 = (s - t) * SSM_GROUP
                    val = taps_b[r0:r0 + SSM_GROUP]
                if s <= t:
                    r0 = (CHUNK - 1 - t + s) * SSM_GROUP
                    v2 = taps_f[r0:r0 + SSM_GROUP]
                    val = v2 if val is None else val + v2
                if s == t:
                    val = val + d_skip
                acc = jnp.where(blk == t, val, acc)
            m_scr[e, s * SSM_GROUP:(s + 1) * SSM_GROUP, :] = acc.astype(_BF16)

    def paired(m0, m1):
        left = lax.broadcasted_iota(jnp.int32, (m0.shape[0], PAIR_STATE), 1) < SSM_STATE
        re0, im0 = m0[:, 0:PAIR_STATE], m0[:, PAIR_STATE:]
        re1, im1 = m1[:, 0:PAIR_STATE], m1[:, PAIR_STATE:]
        return [jnp.where(left, re0, re1), jnp.where(left, im0, im1),
                jnp.where(left, re1, re0), jnp.where(left, im1, im0)]

    s_lat = paired(_dot(u_ref[0], wst_ref[0]), _dot(u_ref[1], wst_ref[1]))
    s_ctx = paired(_dot(uc_ref[0], wst_ref[0]), _dot(uc_ref[1], wst_ref[1]))

    n_tiles = n_rows // SUBLANES
    tile_row = lax.broadcasted_iota(jnp.int32, (1, SUBLANES, PAIR_STATE), 1)

    def tile_scan(x_re, x_im, comp0, backward):
        for k in range(SUBLANES.bit_length() - 1):
            sh = 1 << k
            shift, keep = (SUBLANES - sh, tile_row < SUBLANES - sh) if backward else (sh, tile_row >= sh)
            a_re = jnp.where(keep, coefs[k][comp0], 0.0)
            a_im = jnp.where(keep, coefs[k][comp0 + 1], 0.0)
            p_re, p_im = pltpu.roll(x_re, shift, axis=1), pltpu.roll(x_im, shift, axis=1)
            x_re, x_im = x_re + a_re * p_re - a_im * p_im, x_im + a_re * p_im + a_im * p_re
        return x_re, x_im

    as_tiles = lambda parts: jnp.concatenate(parts, axis=0).reshape(n_tiles, SUBLANES, PAIR_STATE)
    local = list(tile_scan(as_tiles([s_ctx[0], s_lat[0]]), as_tiles([s_ctx[1], s_lat[1]]), 0, False))
    local += tile_scan(as_tiles([s_lat[2], s_ctx[2]]), as_tiles([s_lat[3], s_ctx[3]]), 2, True)
    for comp in range(N_COMP):
        buf_a[comp, pad:pad + n_rows] = local[comp].reshape(n_rows, PAIR_STATE)

    for buf in (ends_a, ends_b):
        buf[...] = jnp.zeros(buf.shape, _F32)
    epad = _ends_pad(n_tiles)
    for comp in range(N_COMP):
        end_row = pad + (SUBLANES - 1 if comp < 2 else 0)
        ends_a[comp, epad:epad + n_tiles] = buf_a[comp, pl.ds(end_row, n_tiles, stride=SUBLANES), :]
    src, dst = ends_a, ends_b
    for k in range(_scan_steps(n_tiles)):
        sh = 1 << k
        coef = coefs[SUBLANES.bit_length() - 1 + k]
        lo, hi = epad, epad + n_tiles
        xr, xi = src[0, lo - sh:hi - sh], src[1, lo - sh:hi - sh]
        dst[0, lo:hi] = src[0, lo:hi] + coef[0] * xr - coef[1] * xi
        dst[1, lo:hi] = src[1, lo:hi] + coef[0] * xi + coef[1] * xr
        xr, xi = src[2, lo + sh:hi + sh], src[3, lo + sh:hi + sh]
        dst[2, lo:hi] = src[2, lo:hi] + coef[2] * xr - coef[3] * xi
        dst[3, lo:hi] = src[3, lo:hi] + coef[2] * xi + coef[3] * xr
        src, dst = dst, src

    def carry_powers(comp0, backward):
        base = [(coefs[k][comp0], coefs[k][comp0 + 1]) for k in range(SUBLANES.bit_length())]
        pw = []
        for r in range(1, SUBLANES + 1):
            acc = None
            for k, term in enumerate(base):
                if (r >> k) & 1:
                    acc = term if acc is None else _cmul(acc[0], acc[1], term[0], term[1])
            pw.append(acc)
        if backward:
            pw = pw[::-1]
        return (jnp.concatenate([p[0] for p in pw], axis=0), jnp.concatenate([p[1] for p in pw], axis=0))

    for comp0, backward in ((0, False), (2, True)):
        p_re, p_im = carry_powers(comp0, backward)
        first = epad + 1 if backward else epad - 1
        for i in range(n_tiles):
            c_re = src[comp0, first + i:first + i + 1]
            c_im = src[comp0 + 1, first + i:first + i + 1]
            rows = slice(pad + i * SUBLANES, pad + (i + 1) * SUBLANES)
            buf_b[comp0, rows] = buf_a[comp0, rows] + p_re * c_re - p_im * c_im
            buf_b[comp0 + 1, rows] = buf_a[comp0 + 1, rows] + p_re * c_im + p_im * c_re

    f0 = pad + n_ctx - 1
    b0 = pad + 1
    f_re, f_im = buf_b[0, f0:f0 + n_lat], buf_b[1, f0:f0 + n_lat]
    b_re, b_im = buf_b[2, b0:b0 + n_lat], buf_b[3, b0:b0 + n_lat]
    left = lax.broadcasted_iota(jnp.int32, (n_lat, PAIR_STATE), 1) < SSM_STATE
    h0 = jnp.concatenate([jnp.where(left, f_re, b_re), jnp.where(left, f_im, b_im)], axis=1)
    h1 = jnp.concatenate([jnp.where(left, b_re, f_re), jnp.where(left, b_im, f_im)], axis=1)
    y_ref[0] = _dot(u_ref[0], m_scr[0]) + _dot(h0.astype(_BF16), wout_ref[0])
    y_ref[1] = _dot(u_ref[1], m_scr[1]) + _dot(h1.astype(_BF16), wout_ref[1])


def _ssm(u_g, uc_g, par, b_re, b_im, c_re, c_im, d_col):
    n_groups, n_lat, _ = u_g.shape
    n_ctx = uc_g.shape[1]
    n_rows = n_lat + n_ctx
    n_pairs = n_groups // PAIR
    buf = pltpu.VMEM((N_COMP, n_rows + 2 * SUBLANES, PAIR_STATE), _F32)
    n_tiles = n_rows // SUBLANES
    ends_rows = _ends_pad(n_tiles) + -(-(n_tiles + _ends_pad(n_tiles)) // SUBLANES) * SUBLANES
    ends = pltpu.VMEM((N_COMP, ends_rows, PAIR_STATE), _F32)
    mat = pltpu.VMEM((PAIR, CHUNK_W, CHUNK_W), _BF16)
    pair3 = lambda q: (q, 0, 0)
    by_dir = lambda q: (0, q, 0, 0)
    return pl.pallas_call(
        functools.partial(_ssm_kernel, n_lat, n_ctx),
        grid=(n_pairs,),
        in_specs=[pl.BlockSpec((PAIR, n_lat, CHUNK_W), pair3),
                  pl.BlockSpec((PAIR, n_ctx, CHUNK_W), pair3),
                  pl.BlockSpec((2, 1) + par.shape[2:], by_dir),
                  pl.BlockSpec((2, PAIR) + b_re.shape[2:], by_dir),
                  pl.BlockSpec((2, PAIR) + b_im.shape[2:], by_dir),
                  pl.BlockSpec((2, PAIR) + c_re.shape[2:], by_dir),
                  pl.BlockSpec((2, PAIR) + c_im.shape[2:], by_dir),
                  pl.BlockSpec((PAIR,) + d_col.shape[1:], pair3)],
        out_specs=pl.BlockSpec((PAIR, n_lat, CHUNK_W), pair3),
        out_shape=jax.ShapeDtypeStruct((n_groups, n_lat, CHUNK_W), _F32),
        scratch_shapes=[buf, buf, ends, ends, mat, mat, pltpu.VMEM((PAIR, CHUNK_W, 2 * CHUNK_W), _BF16), mat],
        compiler_params=pltpu.CompilerParams(dimension_semantics=("arbitrary",),
                                             vmem_limit_bytes=VMEM_LIMIT),
        name="ssm",
    )(u_g, uc_g, par, b_re, b_im, c_re, c_im, d_col)


def _out_kernel(x_ref, conv_ref, yg_ref, mods_ref, gssm_ref, g2_ref, gfin_ref,
                wglu32_ref, wout32_ref, w1_ref, w2_ref, o_ref, y_scr, mix_scr, wglu_ref, wout_ref, ones_ref):
    @pl.when(pl.program_id(0) == 0)
    def _():
        wglu_ref[...] = wglu32_ref[...].astype(_BF16)
        wout_ref[...] = wout32_ref[...].astype(_BF16)
        ones_ref[...] = _group_ones(MXU_W, SSM_GROUP)

    x = x_ref[...]
    n_chunks = x.shape[0] // CHUNK
    gt1 = mods_ref[0:1, 2 * D_MODEL:3 * D_MODEL]
    sh2 = mods_ref[0:1, 3 * D_MODEL:4 * D_MODEL]
    sc2 = mods_ref[0:1, 4 * D_MODEL:5 * D_MODEL]
    gt2 = mods_ref[0:1, 5 * D_MODEL:6 * D_MODEL]
    mix_scr[...] = _dot(conv_ref[...], wout_ref[0:CONV_W])
    for q in range(N_SLABS):
        for half in range(CHUNK // BLOCKS):
            cols = [yg_ref[q * BLOCKS + g8, :, half * LANES:(half + 1) * LANES] for g8 in range(BLOCKS)]
            rows = _lane_block_transpose(cols)
            for t in range(BLOCKS):
                y_scr[q, pl.ds(half * BLOCKS + t, n_chunks, stride=CHUNK), :] = rows[t]
    s = jax.nn.gelu(jnp.concatenate([y_scr[q] for q in range(N_SLABS)], axis=1))
    s = s * jax.nn.sigmoid(_dot(s.astype(_BF16), wglu_ref[...]))
    sq = (s * s).astype(_BF16)
    ssq = jnp.concatenate([_dot(sq[:, k * MXU_W:(k + 1) * MXU_W], ones_ref[...])
                           for k in range(SSM_W // MXU_W)], axis=1)
    sn = s * lax.rsqrt(ssq * (1.0 / SSM_GROUP) + RMS_EPS) * gssm_ref[...]
    mix = mix_scr[...] + _dot(sn.astype(_BF16), wout_ref[CONV_W:])
    x_mid = x + gt1 * mix
    h2 = (_rms_rows(x_mid, g2_ref[...]) * (1.0 + sc2) + sh2).astype(_BF16)
    n_split = 4
    ff = D_FF // n_split
    acc = None
    for j in range(n_split):
        a = jnp.maximum(_dot(h2, w1_ref[:, j * ff:(j + 1) * ff]), 0.0)
        part = _dot((a * a).astype(_BF16), w2_ref[j * ff:(j + 1) * ff])
        acc = part if acc is None else acc + part
    x_out = x_mid + gt2 * acc
    o_ref[...] = _rms_rows(x_out, gfin_ref[...])


def _out_proj(x2d, conv_n, y_g, mods, g_ssm, g2, g_fin, wglu, wout, w1, w2, tm):
    n = x2d.shape[0]
    const = lambda i: (0, 0)
    row = lambda i: (i, 0)
    resident = lambda shape: pl.BlockSpec(shape, const, pipeline_mode=pl.Buffered(1))
    return pl.pallas_call(
        _out_kernel,
        grid=(n // tm,),
        in_specs=[pl.BlockSpec((tm, D_MODEL), row),
                  pl.BlockSpec((tm, CONV_W), row),
                  pl.BlockSpec((SSM_GROUPS, tm // CHUNK, CHUNK_W), lambda i: (0, i, 0)),
                  pl.BlockSpec(mods.shape, const),
                  pl.BlockSpec((1, SSM_W), const),
                  pl.BlockSpec((1, D_MODEL), const),
                  pl.BlockSpec((1, D_MODEL), const),
                  resident((SSM_W, SSM_W)),
                  resident((D_MODEL, D_MODEL)),
                  resident((D_MODEL, D_FF)),
                  resident((D_FF, D_MODEL))],
        out_specs=pl.BlockSpec((tm, D_MODEL), row),
        out_shape=jax.ShapeDtypeStruct((n, D_MODEL), _F32),
        scratch_shapes=[pltpu.VMEM((N_SLABS, tm, LANES), _F32), pltpu.VMEM((tm, D_MODEL), _F32),
                        pltpu.VMEM((SSM_W, SSM_W), _BF16), pltpu.VMEM((D_MODEL, D_MODEL), _BF16),
                        pltpu.VMEM((MXU_W, MXU_W), _BF16)],
        compiler_params=pltpu.CompilerParams(dimension_semantics=("arbitrary",),
                                             vmem_limit_bytes=VMEM_LIMIT),
        name="out_proj",
    )(x2d, conv_n, y_g, mods, g_ssm, g2, g_fin, wglu, wout, w1, w2)


def _ssm_lane_params(a_re, a_im, log_dt):
    n_pairs = SSM_GROUPS // PAIR
    pair_lanes = lambda v: v.reshape(2, n_pairs, PAIR * SSM_STATE)
    dt_lanes = jnp.broadcast_to(log_dt[..., None], (2, SSM_GROUPS, SSM_STATE))
    return jnp.stack([pair_lanes(a_re), pair_lanes(a_im), pair_lanes(dt_lanes)], axis=2)


def kernel(x, c, ctx, c_ctx, w_mod, b_mod, g_norm1, w_in, conv_w, ssm_a_re, ssm_a_im, ssm_log_dt,
           ssm_b_re, ssm_b_im, ssm_c_re, ssm_c_im, ssm_d, w_glu, g_conv_out, g_ssm_out, w_out,
           g_norm2, w_mlp1, w_mlp2, g_final):
    bsz, n_lat, d_model = x.shape
    n_ctx = ctx.shape[1]
    assert bsz == 1 and d_model == D_MODEL and w_mod.shape[0] == 1
    assert n_lat % (CHUNK * SUBLANES) == 0 and n_ctx % (CHUNK * CHUNK) == 0 and n_lat % GRID_W == 0
    layer = 0
    x2d = x[0]
    ctx2d = ctx[0]

    mods = _modulation(c, c_ctx[None, :], w_mod[layer], b_mod[layer][None, :])

    g1 = g_norm1[layer][None, :]
    g_conv = g_conv_out[layer][None, :]
    conv_n, u_g, uc_g, w1_bf, w2_bf = _in_proj(x2d, ctx2d, mods, g1, w_in[layer], conv_w[layer], g_conv, tm=1024,
                                                side_weights=(w_mlp1[layer], w_mlp2[layer]))

    y_g = _ssm(u_g, uc_g, _ssm_lane_params(ssm_a_re[layer], ssm_a_im[layer], ssm_log_dt[layer]),
               ssm_b_re[layer], ssm_b_im[layer], ssm_c_re[layer], ssm_c_im[layer],
               ssm_d.reshape(SSM_GROUPS, SSM_GROUP, 1))

    out = _out_proj(x2d, conv_n, y_g, mods, g_ssm_out[layer][None, :], g_norm2[layer][None, :],
                    g_final[None, :], w_glu[layer], w_out[layer], w1_bf, w2_bf, tm=1024)
    return out[None]
```

```python
import functools
import math

import jax
import jax.numpy as jnp
from jax import lax
from jax.experimental import pallas as pl
from jax.experimental.pallas import tpu as pltpu

D_MODEL = 1024
GRID_W = 64
CONV_W = 512
CONV_HEADS = 8
SSM_W = 512
SSM_GROUP = 16
SSM_GROUPS = 32
SSM_STATE = 64
IN_W = 3 * CONV_W + SSM_W
D_FF = 4 * D_MODEL
N_MOD = 6
RMS_EPS = 1e-6

LANES = 128
MXU_W = 256
SUBLANES = 8
CHUNK = 16
CHUNK_W = CHUNK * SSM_GROUP
PAIR = 2
PAIR_STATE = PAIR * SSM_STATE
N_COMP = 4
BLOCKS = LANES // SSM_GROUP
N_SLABS = SSM_W // LANES
VMEM_LIMIT = 62 * 1024 * 1024

_BF16 = jnp.bfloat16
_F32 = jnp.float32


def _dot(a, b):
    return jnp.dot(a, b, preferred_element_type=_F32)


def _lane_block_transpose(vs):
    blk = lax.broadcasted_iota(jnp.int32, vs[0].shape, 1) // SSM_GROUP
    dist = BLOCKS // 2
    while dist:
        shift = dist * SSM_GROUP
        upper = (blk & dist) != 0
        out = [None] * BLOCKS
        for a in range(BLOCKS):
            if a & dist:
                continue
            lo, hi = vs[a], vs[a + dist]
            out[a] = jnp.where(upper, pltpu.roll(hi, shift, axis=1), lo)
            out[a + dist] = jnp.where(upper, hi, pltpu.roll(lo, LANES - shift, axis=1))
        vs = out
        dist //= 2
    return vs


def _mod_kernel(c_ref, cctx_ref, w_ref, b_ref, o_ref):
    tk = w_ref.shape[0]
    first = lax.broadcasted_iota(jnp.int32, (SUBLANES, tk), 0) == 0
    s = jnp.where(first, jnp.broadcast_to(c_ref[...], first.shape), jnp.broadcast_to(cctx_ref[...], first.shape))
    act = s * jax.nn.sigmoid(s)
    w = w_ref[...]
    a_hi, w_hi = act.astype(_BF16), w.astype(_BF16)
    a_lo = (act - a_hi.astype(_F32)).astype(_BF16)
    w_lo = (w - w_hi.astype(_F32)).astype(_BF16)
    by_hi = _dot(jnp.concatenate([a_hi, a_lo], axis=0), w_hi)
    part = by_hi[0:SUBLANES] + by_hi[SUBLANES:] + _dot(a_hi, w_lo)

    @pl.when(pl.program_id(0) == 0)
    def _():
        o_ref[...] = part + b_ref[...]

    @pl.when(pl.program_id(0) > 0)
    def _():
        o_ref[...] += part


def _modulation(c_row, cctx_row, w_mod, b_mod):
    d_in, n_out = w_mod.shape
    tk = 256
    return pl.pallas_call(
        _mod_kernel,
        grid=(d_in // tk,),
        in_specs=[pl.BlockSpec((1, tk), lambda k: (0, k)),
                  pl.BlockSpec((1, tk), lambda k: (0, k)),
                  pl.BlockSpec((tk, n_out), lambda k: (k, 0)),
                  pl.BlockSpec((1, n_out), lambda k: (0, 0))],
        out_specs=pl.BlockSpec((SUBLANES, n_out), lambda k: (0, 0)),
        out_shape=jax.ShapeDtypeStruct((SUBLANES, n_out), _F32),
        compiler_params=pltpu.CompilerParams(dimension_semantics=("arbitrary",),
                                             vmem_limit_bytes=VMEM_LIMIT),
        name="mod",
    )(c_row, cctx_row, w_mod, b_mod)


def _rms_rows(x, g):
    ms = jnp.mean(x * x, axis=-1, keepdims=True)
    return x * lax.rsqrt(ms + RMS_EPS) * g


def _group_ones(width, group):
    rows = lax.broadcasted_iota(jnp.int32, (width, width), 0) // group
    cols = lax.broadcasted_iota(jnp.int32, (width, width), 1) // group
    return jnp.where(rows == cols, 1.0, 0.0).astype(_BF16)


def _in_kernel(n_side, x_ref, ctx_ref, mods_ref, g1_ref, w_in_ref, convw_ref, gconv_ref, *refs):
    side_in, (conv_ref, ug_ref, ucg_ref) = refs[:n_side], refs[n_side:n_side + 3]
    side_out = refs[n_side + 3:2 * n_side + 3]
    u_scr, w_scr, ones_scr = refs[2 * n_side + 3:]
    step = pl.program_id(0)
    last = pl.num_programs(0) - 1

    @pl.when(step == 0)
    def _():
        w_scr[:, 0:SSM_W] = w_in_ref[:, 3 * CONV_W:].astype(_BF16)
        for k in range(CONV_W // MXU_W):
            for part in range(3):
                dst = SSM_W + (3 * k + part) * MXU_W
                src = part * CONV_W + k * MXU_W
                w_scr[:, dst:dst + MXU_W] = w_in_ref[:, src:src + MXU_W].astype(_BF16)
        ones_scr[...] = _group_ones(MXU_W, CONV_W // CONV_HEADS)

    def normed(rows_ref, mod_row):
        sh1 = mods_ref[mod_row:mod_row + 1, 0:D_MODEL]
        sc1 = mods_ref[mod_row:mod_row + 1, D_MODEL:2 * D_MODEL]
        return (_rms_rows(rows_ref[...], g1_ref[...]) * (1.0 + sc1) + sh1).astype(_BF16)

    def to_group_major(z_u, out_ref, r0):
        n_rows = z_u.shape[0]
        chunks = slice(r0 // CHUNK, (r0 + n_rows) // CHUNK)
        for q in range(N_SLABS):
            u_scr[q, r0:r0 + n_rows] = z_u[:, q * LANES:(q + 1) * LANES]
        for q in range(N_SLABS):
            for half in range(CHUNK // BLOCKS):
                rows = [u_scr[q, pl.ds(r0 + half * BLOCKS + t, n_rows // CHUNK, stride=CHUNK), :]
                        for t in range(BLOCKS)]
                cols = _lane_block_transpose(rows)
                for g8 in range(BLOCKS):
                    out_ref[q * BLOCKS + g8, chunks, half * LANES:(half + 1) * LANES] = cols[g8].astype(_BF16)

    def conv_branch(z, r0):
        n_rows = z.shape[0]
        pos = lax.broadcasted_iota(jnp.int32, (n_rows, MXU_W), 0) % GRID_W
        for k in range(CONV_W // MXU_W):
            c0 = SSM_W + 3 * k * MXU_W
            lanes = slice(k * MXU_W, (k + 1) * MXU_W)
            b = z[:, c0:c0 + MXU_W]
            cv = z[:, c0 + MXU_W:c0 + 2 * MXU_W] * z[:, c0 + 2 * MXU_W:c0 + 3 * MXU_W]
            prev = jnp.where(pos == 0, 0.0, pltpu.roll(cv, 1, axis=0))
            nxt = jnp.where(pos == GRID_W - 1, 0.0, pltpu.roll(cv, n_rows - 1, axis=0))
            y = b * (convw_ref[0:1, lanes] * prev + convw_ref[1:2, lanes] * cv + convw_ref[2:3, lanes] * nxt)
            ssq = _dot((y * y).astype(_BF16), ones_scr[...])
            yn = y * lax.rsqrt(ssq * (1.0 / (CONV_W // CONV_HEADS)) + RMS_EPS) * gconv_ref[:, lanes]
            conv_ref[r0:r0 + n_rows, lanes] = yn.astype(_BF16)

    @pl.when(step < last)
    def _():
        for src_ref, dst_ref in zip(side_in, side_out):
            dst_ref[...] = src_ref[...].astype(_BF16)
        th = x_ref.shape[0] // 2
        z_a = _dot(normed(x_ref.at[0:th], 0), w_scr[...])
        z_b = _dot(normed(x_ref.at[th:2 * th], 0), w_scr[...])
        to_group_major(z_a[:, 0:SSM_W], ug_ref, 0)
        conv_branch(z_a, 0)
        to_group_major(z_b[:, 0:SSM_W], ug_ref, th)
        conv_branch(z_b, th)

    @pl.when(step == last)
    def _():
        to_group_major(_dot(normed(ctx_ref, 1), w_scr[:, 0:SSM_W]), ucg_ref, 0)


def _in_proj(x2d, ctx2d, mods, g1, w_in, conv_w, g_conv, tm, side_weights):
    n, n_ctx = x2d.shape[0], ctx2d.shape[0]
    steps = n // tm
    const = lambda i: (0, 0)
    row = lambda i: (jnp.minimum(i, steps - 1), 0)
    side_specs = [pl.BlockSpec((w.shape[0] // steps, w.shape[1]), row) for w in side_weights]
    return pl.pallas_call(
        functools.partial(_in_kernel, len(side_weights)),
        grid=(steps + 1,),
        in_specs=[pl.BlockSpec((tm, D_MODEL), row),
                  pl.BlockSpec((n_ctx, D_MODEL), const),
                  pl.BlockSpec(mods.shape, const),
                  pl.BlockSpec((1, D_MODEL), const),
                  pl.BlockSpec((D_MODEL, IN_W), const, pipeline_mode=pl.Buffered(1)),
                  pl.BlockSpec((3, CONV_W), const),
                  pl.BlockSpec((1, CONV_W), const)] + side_specs,
        out_specs=[pl.BlockSpec((tm, CONV_W), row),
                   pl.BlockSpec((SSM_GROUPS, tm // CHUNK, CHUNK_W), lambda i: (0, jnp.minimum(i, steps - 1), 0)),
                   pl.BlockSpec((SSM_GROUPS, n_ctx // CHUNK, CHUNK_W), lambda i: (0, 0, 0))] + side_specs,
        out_shape=[jax.ShapeDtypeStruct((n, CONV_W), _BF16),
                   jax.ShapeDtypeStruct((SSM_GROUPS, n // CHUNK, CHUNK_W), _BF16),
                   jax.ShapeDtypeStruct((SSM_GROUPS, n_ctx // CHUNK, CHUNK_W), _BF16)]
                  + [jax.ShapeDtypeStruct(w.shape, _BF16) for w in side_weights],
        scratch_shapes=[pltpu.VMEM((N_SLABS, tm, LANES), _F32),
                        pltpu.VMEM((D_MODEL, IN_W), _BF16),
                        pltpu.VMEM((MXU_W, MXU_W), _BF16)],
        compiler_params=pltpu.CompilerParams(dimension_semantics=("arbitrary",),
                                             vmem_limit_bytes=VMEM_LIMIT),
        name="in_proj",
    )(x2d, ctx2d, mods, g1, w_in, conv_w, g_conv, *side_weights)


def _scan_steps(n_rows):
    return max(1, math.ceil(math.log2(n_rows)))


def _ends_pad(n_tiles):
    return max(SUBLANES, 1 << (_scan_steps(n_tiles) - 1))


def _cmul(ar, ai, br, bi):
    return ar * br - ai * bi, ar * bi + ai * br


def _eye(n):
    rows, cols = (lax.broadcasted_iota(jnp.int32, (n, n), axis) for axis in (0, 1))
    return jnp.where(rows == cols, 1.0, 0.0).astype(_BF16)


def _dot_nt_exact(eye, x, repeat=1):
    hi = x.astype(_BF16)
    rest = x - hi.astype(_F32)
    mid = rest.astype(_BF16)
    lo = (rest - mid.astype(_F32)).astype(_BF16)
    nt = lambda term: lax.dot_general(eye, jnp.concatenate([term] * repeat, axis=0),
                                      (((1,), (1,)), ((), ())), preferred_element_type=_F32)
    return nt(hi) + nt(mid) + nt(lo)


def _ssm_prep(n_steps, par_ref, br_ref, bi_ref, cr_ref, ci_ref, wst_ref, wout_ref, cmat_ref):
    T = CHUNK
    left8 = lax.broadcasted_iota(jnp.int32, (SUBLANES, LANES), 1) < SSM_STATE
    row0 = lax.broadcasted_iota(jnp.int32, (SUBLANES, LANES), 0) == 0

    def member_rows(vf, vb):
        f8 = jnp.broadcast_to(vf, (SUBLANES, LANES))
        b8 = pltpu.roll(jnp.broadcast_to(vb, (SUBLANES, LANES)), SSM_STATE, axis=1)
        return jnp.where(row0, jnp.where(left8, f8, b8), jnp.where(left8, b8, f8))

    pf, pb = par_ref[0, 0], par_ref[1, 0]
    lr = jnp.minimum(member_rows(pf[0:1], pb[0:1]), -1e-4)
    li = member_rows(pf[1:2], pb[1:2])
    dt = jnp.exp(member_rows(pf[2:3], pb[2:3]))
    mag = jnp.exp(lr * dt)
    ab_re = mag * jnp.cos(li * dt)
    ab_im = mag * jnp.sin(li * dt)
    nr = ab_re - 1.0
    den = lr * lr + li * li
    f_re = (nr * lr + ab_im * li) / den
    f_im = (ab_im * lr - nr * li) / den
    pt_re, pt_im = [jnp.ones_like(ab_re)], [jnp.zeros_like(ab_im)]
    for _ in range(T):
        nxt = _cmul(pt_re[-1], pt_im[-1], ab_re, ab_im)
        pt_re.append(nxt[0])
        pt_im.append(nxt[1])

    cols = jnp.concatenate([ab_re, ab_im, jnp.zeros((LANES - 2 * SUBLANES, LANES), _F32)], axis=0).T
    left1 = lax.broadcasted_iota(jnp.int32, (1, LANES), 1) < SSM_STATE
    eye_i = _eye(SSM_GROUP)
    eye_p = _eye(SSM_STATE)
    rows = lax.broadcasted_iota(jnp.int32, (LANES, CHUNK_W), 0)
    tcol = lax.broadcasted_iota(jnp.int32, (LANES, CHUNK_W), 1) // SSM_GROUP
    zero_half = jnp.zeros((SSM_STATE, CHUNK_W), _BF16)
    for e in range(PAIR):
        fwd_first = e == 0
        d0, d1 = (0, 1) if fwd_first else (1, 0)
        b_re = _dot_nt_exact(eye_i, jnp.concatenate([br_ref[d0, e], br_ref[d1, e]], axis=0))
        b_im = _dot_nt_exact(eye_i, jnp.concatenate([bi_ref[d0, e], bi_ref[d1, e]], axis=0))
        bbar_re, bbar_im = _cmul(f_re[e:e + 1], f_im[e:e + 1], b_re, b_im)
        fwd_lane = left1 if fwd_first else jnp.logical_not(left1)
        for s in range(T):
            pr = jnp.where(fwd_lane, pt_re[T - 1 - s][e:e + 1], pt_re[s][e:e + 1])
            pi = jnp.where(fwd_lane, pt_im[T - 1 - s][e:e + 1], pt_im[s][e:e + 1])
            w_re, w_im = _cmul(pr, pi, bbar_re, bbar_im)
            wst_ref[e, s * SSM_GROUP:(s + 1) * SSM_GROUP, 0:LANES] = w_re.astype(_BF16)
            wst_ref[e, s * SSM_GROUP:(s + 1) * SSM_GROUP, LANES:] = w_im.astype(_BF16)
        fwd_row = (rows < SSM_STATE) if fwd_first else (rows >= SSM_STATE)
        expo = jnp.where(fwd_row, tcol + 1, T - tcol)
        q_re = jnp.ones((LANES, CHUNK_W), _F32)
        q_im = jnp.zeros((LANES, CHUNK_W), _F32)
        s_re, s_im = cols[:, e:e + 1], cols[:, SUBLANES + e:SUBLANES + e + 1]
        for bit in range(T.bit_length()):
            m_re, m_im = _cmul(q_re, q_im, jnp.broadcast_to(s_re, q_re.shape), jnp.broadcast_to(s_im, q_re.shape))
            take = ((expo >> bit) & 1) == 1
            q_re, q_im = jnp.where(take, m_re, q_re), jnp.where(take, m_im, q_im)
            s_re, s_im = _cmul(s_re, s_im, s_re, s_im)
        tiled = lambda c: _dot_nt_exact(eye_p, c, repeat=T)
        c_re = jnp.concatenate([tiled(cr_ref[d0, e]), tiled(cr_ref[d1, e])], axis=0)
        c_im = jnp.concatenate([tiled(ci_ref[d0, e]), tiled(ci_ref[d1, e])], axis=0)
        o_re, o_im = _cmul(c_re, c_im, q_re, q_im)
        wout_ref[e, 0:LANES] = o_re.astype(_BF16)
        wout_ref[e, LANES:] = (-o_im).astype(_BF16)
        for ri, src in enumerate((c_re, -c_im)):
            for slot, d in enumerate((d0, d1)):
                r0 = ri * LANES + slot * SSM_STATE
                block = src[slot * SSM_STATE:(slot + 1) * SSM_STATE].astype(_BF16)
                cmat_ref[e, r0:r0 + SSM_STATE, d * CHUNK_W:(d + 1) * CHUNK_W] = block
                cmat_ref[e, r0:r0 + SSM_STATE, (1 - d) * CHUNK_W:(2 - d) * CHUNK_W] = zero_half

    m0_re, m1_re = pt_re[T][0:1], pt_re[T][1:2]
    m0_im, m1_im = pt_im[T][0:1], pt_im[T][1:2]
    cf = (jnp.where(left1, m0_re, m1_re), jnp.where(left1, m0_im, m1_im))
    cb = (jnp.where(left1, m1_re, m0_re), jnp.where(left1, m1_im, m0_im))
    coefs = []
    for _ in range(n_steps):
        coefs.append([cf[0], cf[1], cb[0], cb[1]])
        cf = _cmul(cf[0], cf[1], cf[0], cf[1])
        cb = _cmul(cb[0], cb[1], cb[0], cb[1])
    return coefs


def _ssm_kernel(n_lat, n_ctx, u_ref, uc_ref, par_ref, br_ref, bi_ref, cr_ref, ci_ref, d_ref,
                y_ref, buf_a, buf_b, ends_a, ends_b, wst_ref, wout_ref, cmat_ref, m_scr):
    n_rows = n_lat + n_ctx
    pad = SUBLANES
    coefs = _ssm_prep(_scan_steps(n_rows), par_ref, br_ref, bi_ref, cr_ref, ci_ref,
                      wst_ref, wout_ref, cmat_ref)

    lane = lax.broadcasted_iota(jnp.int32, (SSM_GROUP, CHUNK_W), 1)
    blk = lane // SSM_GROUP
    on_diag = lane % SSM_GROUP == lax.broadcasted_iota(jnp.int32, (SSM_GROUP, CHUNK_W), 0)
    for e in range(PAIR):
        d_skip = jnp.where(on_diag, jnp.broadcast_to(d_ref[e], (SSM_GROUP, CHUNK_W)), 0.0)
        taps = _dot(wst_ref[e], cmat_ref[e])
        taps_f = taps[:, 0:CHUNK_W]
        taps_b = taps[:, CHUNK_W:]
        for s in range(CHUNK):
            acc = jnp.zeros((SSM_GROUP, CHUNK_W), _F32)
            for t in range(CHUNK):
                val = None
                if s >= t:
                    r0 = (s - t) * SSM_GROUP
                    val = taps_b[r0:r0 + SSM_GROUP]
                if s <= t:
                    r0 = (CHUNK - 1 - t + s) * SSM_GROUP
                    v2 = taps_f[r0:r0 + SSM_GROUP]
                    val = v2 if val is None else val + v2
                if s == t:
                    val = val + d_skip
                acc = jnp.where(blk == t, val, acc)
            m_scr[e, s * SSM_GROUP:(s + 1) * SSM_GROUP, :] = acc.astype(_BF16)

    def paired(m0, m1):
        left = lax.broadcasted_iota(jnp.int32, (m0.shape[0], PAIR_STATE), 1) < SSM_STATE
        re0, im0 = m0[:, 0:PAIR_STATE], m0[:, PAIR_STATE:]
        re1, im1 = m1[:, 0:PAIR_STATE], m1[:, PAIR_STATE:]
        return [jnp.where(left, re0, re1), jnp.where(left, im0, im1),
                jnp.where(left, re1, re0), jnp.where(left, im1, im0)]

    s_lat = paired(_dot(u_ref[0], wst_ref[0]), _dot(u_ref[1], wst_ref[1]))
    s_ctx = paired(_dot(uc_ref[0], wst_ref[0]), _dot(uc_ref[1], wst_ref[1]))

    n_tiles = n_rows // SUBLANES
    tile_row = lax.broadcasted_iota(jnp.int32, (1, SUBLANES, PAIR_STATE), 1)

    def tile_scan(x_re, x_im, comp0, backward):
        for k in range(SUBLANES.bit_length() - 1):
            sh = 1 << k
            shift, keep = (SUBLANES - sh, tile_row < SUBLANES - sh) if backward else (sh, tile_row >= sh)
            a_re = jnp.where(keep, coefs[k][comp0], 0.0)
            a_im = jnp.where(keep, coefs[k][comp0 + 1], 0.0)
            p_re, p_im = pltpu.roll(x_re, shift, axis=1), pltpu.roll(x_im, shift, axis=1)
            x_re, x_im = x_re + a_re * p_re - a_im * p_im, x_im + a_re * p_im + a_im * p_re
        return x_re, x_im

    as_tiles = lambda parts: jnp.concatenate(parts, axis=0).reshape(n_tiles, SUBLANES, PAIR_STATE)
    local = list(tile_scan(as_tiles([s_ctx[0], s_lat[0]]), as_tiles([s_ctx[1], s_lat[1]]), 0, False))
    local += tile_scan(as_tiles([s_lat[2], s_ctx[2]]), as_tiles([s_lat[3], s_ctx[3]]), 2, True)
    for comp in range(N_COMP):
        buf_a[comp, pad:pad + n_rows] = local[comp].reshape(n_rows, PAIR_STATE)

    for buf in (ends_a, ends_b):
        buf[...] = jnp.zeros(buf.shape, _F32)
    epad = _ends_pad(n_tiles)
    for comp in range(N_COMP):
        end_row = pad + (SUBLANES - 1 if comp < 2 else 0)
        ends_a[comp, epad:epad + n_tiles] = buf_a[comp, pl.ds(end_row, n_tiles, stride=SUBLANES), :]
    src, dst = ends_a, ends_b
    for k in range(_scan_steps(n_tiles)):
        sh = 1 << k
        coef = coefs[SUBLANES.bit_length() - 1 + k]
        lo, hi = epad, epad + n_tiles
        xr, xi = src[0, lo - sh:hi - sh], src[1, lo - sh:hi - sh]
        dst[0, lo:hi] = src[0, lo:hi] + coef[0] * xr - coef[1] * xi
        dst[1, lo:hi] = src[1, lo:hi] + coef[0] * xi + coef[1] * xr
        xr, xi = src[2, lo + sh:hi + sh], src[3, lo + sh:hi + sh]
        dst[2, lo:hi] = src[2, lo:hi] + coef[2] * xr - coef[3] * xi
        dst[3, lo:hi] = src[3, lo:hi] + coef[2] * xi + coef[3] * xr
        src, dst = dst, src

    def carry_powers(comp0, backward):
        base = [(coefs[k][comp0], coefs[k][comp0 + 1]) for k in range(SUBLANES.bit_length())]
        pw = []
        for r in range(1, SUBLANES + 1):
            acc = None
            for k, term in enumerate(base):
                if (r >> k) & 1:
                    acc = term if acc is None else _cmul(acc[0], acc[1], term[0], term[1])
            pw.append(acc)
        if backward:
            pw = pw[::-1]
        return (jnp.concatenate([p[0] for p in pw], axis=0), jnp.concatenate([p[1] for p in pw], axis=0))

    for comp0, backward in ((0, False), (2, True)):
        p_re, p_im = carry_powers(comp0, backward)
        first = epad + 1 if backward else epad - 1
        for i in range(n_tiles):
            c_re = src[comp0, first + i:first + i + 1]
            c_im = src[comp0 + 1, first + i:first + i + 1]
            rows = slice(pad + i * SUBLANES, pad + (i + 1) * SUBLANES)
            buf_b[comp0, rows] = buf_a[comp0, rows] + p_re * c_re - p_im * c_im
            buf_b[comp0 + 1, rows] = buf_a[comp0 + 1, rows] + p_re * c_im + p_im * c_re

    f0 = pad + n_ctx - 1
    b0 = pad + 1
    f_re, f_im = buf_b[0, f0:f0 + n_lat], buf_b[1, f0:f0 + n_lat]
    b_re, b_im = buf_b[2, b0:b0 + n_lat], buf_b[3, b0:b0 + n_lat]
    left = lax.broadcasted_iota(jnp.int32, (n_lat, PAIR_STATE), 1) < SSM_STATE
    h0 = jnp.concatenate([jnp.where(left, f_re, b_re), jnp.where(left, f_im, b_im)], axis=1)
    h1 = jnp.concatenate([jnp.where(left, b_re, f_re), jnp.where(left, b_im, f_im)], axis=1)
    y_ref[0] = _dot(u_ref[0], m_scr[0]) + _dot(h0.astype(_BF16), wout_ref[0])
    y_ref[1] = _dot(u_ref[1], m_scr[1]) + _dot(h1.astype(_BF16), wout_ref[1])


def _ssm(u_g, uc_g, par, b_re, b_im, c_re, c_im, d_col):
    n_groups, n_lat, _ = u_g.shape
    n_ctx = uc_g.shape[1]
    n_rows = n_lat + n_ctx
    n_pairs = n_groups // PAIR
    buf = pltpu.VMEM((N_COMP, n_rows + 2 * SUBLANES, PAIR_STATE), _F32)
    n_tiles = n_rows // SUBLANES
    ends_rows = _ends_pad(n_tiles) + -(-(n_tiles + _ends_pad(n_tiles)) // SUBLANES) * SUBLANES
    ends = pltpu.VMEM((N_COMP, ends_rows, PAIR_STATE), _F32)
    mat = pltpu.VMEM((PAIR, CHUNK_W, CHUNK_W), _BF16)
    pair3 = lambda q: (q, 0, 0)
    by_dir = lambda q: (0, q, 0, 0)
    return pl.pallas_call(
        functools.partial(_ssm_kernel, n_lat, n_ctx),
        grid=(n_pairs,),
        in_specs=[pl.BlockSpec((PAIR, n_lat, CHUNK_W), pair3),
                  pl.BlockSpec((PAIR, n_ctx, CHUNK_W), pair3),
                  pl.BlockSpec((2, 1) + par.shape[2:], by_dir),
                  pl.BlockSpec((2, PAIR) + b_re.shape[2:], by_dir),
                  pl.BlockSpec((2, PAIR) + b_im.shape[2:], by_dir),
                  pl.BlockSpec((2, PAIR) + c_re.shape[2:], by_dir),
                  pl.BlockSpec((2, PAIR) + c_im.shape[2:], by_dir),
                  pl.BlockSpec((PAIR,) + d_col.shape[1:], pair3)],
        out_specs=pl.BlockSpec((PAIR, n_lat, CHUNK_W), pair3),
        out_shape=jax.ShapeDtypeStruct((n_groups, n_lat, CHUNK_W), _F32),
        scratch_shapes=[buf, buf, ends, ends, mat, mat, pltpu.VMEM((PAIR, CHUNK_W, 2 * CHUNK_W), _BF16), mat],
        compiler_params=pltpu.CompilerParams(dimension_semantics=("arbitrary",),
                                             vmem_limit_bytes=VMEM_LIMIT),
        name="ssm",
    )(u_g, uc_g, par, b_re, b_im, c_re, c_im, d_col)


def _out_kernel(x_ref, conv_ref, yg_ref, mods_ref, gssm_ref, g2_ref, gfin_ref,
                wglu32_ref, wout32_ref, w1_ref, w2_ref, o_ref, y_scr, mix_scr, wglu_ref, wout_ref, ones_ref):
    @pl.when(pl.program_id(0) == 0)
    def _():
        wglu_ref[...] = wglu32_ref[...].astype(_BF16)
        wout_ref[...] = wout32_ref[...].astype(_BF16)
        ones_ref[...] = _group_ones(MXU_W, SSM_GROUP)

    x = x_ref[...]
    n_chunks = x.shape[0] // CHUNK
    gt1 = mods_ref[0:1, 2 * D_MODEL:3 * D_MODEL]
    sh2 = mods_ref[0:1, 3 * D_MODEL:4 * D_MODEL]
    sc2 = mods_ref[0:1, 4 * D_MODEL:5 * D_MODEL]
    gt2 = mods_ref[0:1, 5 * D_MODEL:6 * D_MODEL]
    mix_scr[...] = _dot(conv_ref[...], wout_ref[0:CONV_W])
    for q in range(N_SLABS):
        for half in range(CHUNK // BLOCKS):
            cols = [yg_ref[q * BLOCKS + g8, :, half * LANES:(half + 1) * LANES] for g8 in range(BLOCKS)]
            rows = _lane_block_transpose(cols)
            for t in range(BLOCKS):
                y_scr[q, pl.ds(half * BLOCKS + t, n_chunks, stride=CHUNK), :] = rows[t]
    s = jax.nn.gelu(jnp.concatenate([y_scr[q] for q in range(N_SLABS)], axis=1))
    s = s * jax.nn.sigmoid(_dot(s.astype(_BF16), wglu_ref[...]))
    sq = (s * s).astype(_BF16)
    ssq = jnp.concatenate([_dot(sq[:, k * MXU_W:(k + 1) * MXU_W], ones_ref[...])
                           for k in range(SSM_W // MXU_W)], axis=1)
    sn = s * lax.rsqrt(ssq * (1.0 / SSM_GROUP) + RMS_EPS) * gssm_ref[...]
    mix = mix_scr[...] + _dot(sn.astype(_BF16), wout_ref[CONV_W:])
    x_mid = x + gt1 * mix
    h2 = (_rms_rows(x_mid, g2_ref[...]) * (1.0 + sc2) + sh2).astype(_BF16)
    n_split = 4
    ff = D_FF // n_split
    acc = None
    for j in range(n_split):
        a = jnp.maximum(_dot(h2, w1_ref[:, j * ff:(j + 1) * ff]), 0.0)
        part = _dot((a * a).astype(_BF16), w2_ref[j * ff:(j + 1) * ff])
        acc = part if acc is None else acc + part
    x_out = x_mid + gt2 * acc
    o_ref[...] = _rms_rows(x_out, gfin_ref[...])


def _out_proj(x2d, conv_n, y_g, mods, g_ssm, g2, g_fin, wglu, wout, w1, w2, tm):
    n = x2d.shape[0]
    const = lambda i: (0, 0)
    row = lambda i: (i, 0)
    resident = lambda shape: pl.BlockSpec(shape, const, pipeline_mode=pl.Buffered(1))
    return pl.pallas_call(
        _out_kernel,
        grid=(n // tm,),
        in_specs=[pl.BlockSpec((tm, D_MODEL), row),
                  pl.BlockSpec((tm, CONV_W), row),
                  pl.BlockSpec((SSM_GROUPS, tm // CHUNK, CHUNK_W), lambda i: (0, i, 0)),
                  pl.BlockSpec(mods.shape, const),
                  pl.BlockSpec((1, SSM_W), const),
                  pl.BlockSpec((1, D_MODEL), const),
                  pl.BlockSpec((1, D_MODEL), const),
                  resident((SSM_W, SSM_W)),
                  resident((D_MODEL, D_MODEL)),
                  resident((D_MODEL, D_FF)),
                  resident((D_FF, D_MODEL))],
        out_specs=pl.BlockSpec((tm, D_MODEL), row),
        out_shape=jax.ShapeDtypeStruct((n, D_MODEL), _F32),
        scratch_shapes=[pltpu.VMEM((N_SLABS, tm, LANES), _F32), pltpu.VMEM((tm, D_MODEL), _F32),
                        pltpu.VMEM((SSM_W, SSM_W), _BF16), pltpu.VMEM((D_MODEL, D_MODEL), _BF16),
                        pltpu.VMEM((MXU_W, MXU_W), _BF16)],
        compiler_params=pltpu.CompilerParams(dimension_semantics=("arbitrary",),
                                             vmem_limit_bytes=VMEM_LIMIT),
        name="out_proj",
    )(x2d, conv_n, y_g, mods, g_ssm, g2, g_fin, wglu, wout, w1, w2)


def _ssm_lane_params(a_re, a_im, log_dt):
    n_pairs = SSM_GROUPS // PAIR
    pair_lanes = lambda v: v.reshape(2, n_pairs, PAIR * SSM_STATE)
    dt_lanes = jnp.broadcast_to(log_dt[..., None], (2, SSM_GROUPS, SSM_STATE))
    return jnp.stack([pair_lanes(a_re), pair_lanes(a_im), pair_lanes(dt_lanes)], axis=2)


def kernel(x, c, ctx, c_ctx, w_mod, b_mod, g_norm1, w_in, conv_w, ssm_a_re, ssm_a_im, ssm_log_dt,
           ssm_b_re, ssm_b_im, ssm_c_re, ssm_c_im, ssm_d, w_glu, g_conv_out, g_ssm_out, w_out,
           g_norm2, w_mlp1, w_mlp2, g_final):
    bsz, n_lat, d_model = x.shape
    n_ctx = ctx.shape[1]
    assert bsz == 1 and d_model == D_MODEL and w_mod.shape[0] == 1
    assert n_lat % (CHUNK * SUBLANES) == 0 and n_ctx % (CHUNK * CHUNK) == 0 and n_lat % GRID_W == 0
    layer = 0
    x2d = x[0]
    ctx2d = ctx[0]

    mods = _modulation(c, c_ctx[None, :], w_mod[layer], b_mod[layer][None, :])

    g1 = g_norm1[layer][None, :]
    g_conv = g_conv_out[layer][None, :]
    conv_n, u_g, uc_g, w1_bf, w2_bf = _in_proj(x2d, ctx2d, mods, g1, w_in[layer], conv_w[layer], g_conv, tm=1024,
                                                side_weights=(w_mlp1[layer], w_mlp2[layer]))

    y_g = _ssm(u_g, uc_g, _ssm_lane_params(ssm_a_re[layer], ssm_a_im[layer], ssm_log_dt[layer]),
               ssm_b_re[layer], ssm_b_im[layer], ssm_c_re[layer], ssm_c_im[layer],
               ssm_d.reshape(SSM_GROUPS, SSM_GROUP, 1))

    out = _out_proj(x2d, conv_n, y_g, mods, g_ssm_out[layer][None, :], g_norm2[layer][None, :],
                    g_final[None, :], w_glu[layer], w_out[layer], w1_bf, w2_bf, tm=1024)
    return out[None]
```

```python
import functools
import math

import jax
import jax.numpy as jnp
from jax import lax
from jax.experimental import pallas as pl
from jax.experimental.pallas import tpu as pltpu

D_MODEL = 1024
GRID_W = 64
CONV_W = 512
CONV_HEADS = 8
SSM_W = 512
SSM_GROUP = 16
SSM_GROUPS = 32
SSM_STATE = 64
IN_W = 3 * CONV_W + SSM_W
D_FF = 4 * D_MODEL
N_MOD = 6
RMS_EPS = 1e-6

LANES = 128
MXU_W = 256
SUBLANES = 8
CHUNK = 16
CHUNK_W = CHUNK * SSM_GROUP
PAIR = 2
PAIR_STATE = PAIR * SSM_STATE
N_COMP = 4
BLOCKS = LANES // SSM_GROUP
N_SLABS = SSM_W // LANES
VMEM_LIMIT = 62 * 1024 * 1024

_BF16 = jnp.bfloat16
_F32 = jnp.float32


def _dot(a, b):
    return jnp.dot(a, b, preferred_element_type=_F32)


def _lane_block_transpose(vs):
    blk = lax.broadcasted_iota(jnp.int32, vs[0].shape, 1) // SSM_GROUP
    dist = BLOCKS // 2
    while dist:
        shift = dist * SSM_GROUP
        upper = (blk & dist) != 0
        out = [None] * BLOCKS
        for a in range(BLOCKS):
            if a & dist:
                continue
            lo, hi = vs[a], vs[a + dist]
            out[a] = jnp.where(upper, pltpu.roll(hi, shift, axis=1), lo)
            out[a + dist] = jnp.where(upper, hi, pltpu.roll(lo, LANES - shift, axis=1))
        vs = out
        dist //= 2
    return vs


def _mod_kernel(c_ref, cctx_ref, w_ref, b_ref, o_ref):
    tk = w_ref.shape[0]
    first = lax.broadcasted_iota(jnp.int32, (SUBLANES, tk), 0) == 0
    s = jnp.where(first, jnp.broadcast_to(c_ref[...], first.shape), jnp.broadcast_to(cctx_ref[...], first.shape))
    act = s * jax.nn.sigmoid(s)
    w = w_ref[...]
    a_hi, w_hi = act.astype(_BF16), w.astype(_BF16)
    a_lo = (act - a_hi.astype(_F32)).astype(_BF16)
    w_lo = (w - w_hi.astype(_F32)).astype(_BF16)
    by_hi = _dot(jnp.concatenate([a_hi, a_lo], axis=0), w_hi)
    part = by_hi[0:SUBLANES] + by_hi[SUBLANES:] + _dot(a_hi, w_lo)

    @pl.when(pl.program_id(0) == 0)
    def _():
        o_ref[...] = part + b_ref[...]

    @pl.when(pl.program_id(0) > 0)
    def _():
        o_ref[...] += part


def _modulation(c_row, cctx_row, w_mod, b_mod):
    d_in, n_out = w_mod.shape
    tk = 256
    return pl.pallas_call(
        _mod_kernel,
        grid=(d_in // tk,),
        in_specs=[pl.BlockSpec((1, tk), lambda k: (0, k)),
                  pl.BlockSpec((1, tk), lambda k: (0, k)),
                  pl.BlockSpec((tk, n_out), lambda k: (k, 0)),
                  pl.BlockSpec((1, n_out), lambda k: (0, 0))],
        out_specs=pl.BlockSpec((SUBLANES, n_out), lambda k: (0, 0)),
        out_shape=jax.ShapeDtypeStruct((SUBLANES, n_out), _F32),
        compiler_params=pltpu.CompilerParams(dimension_semantics=("arbitrary",),
                                             vmem_limit_bytes=VMEM_LIMIT),
        name="mod",
    )(c_row, cctx_row, w_mod, b_mod)


def _rms_rows(x, g):
    ms = jnp.mean(x * x, axis=-1, keepdims=True)
    return x * lax.rsqrt(ms + RMS_EPS) * g


def _group_ones(width, group):
    rows = lax.broadcasted_iota(jnp.int32, (width, width), 0) // group
    cols = lax.broadcasted_iota(jnp.int32, (width, width), 1) // group
    return jnp.where(rows == cols, 1.0, 0.0).astype(_BF16)


def _in_kernel(n_side, x_ref, ctx_ref, mods_ref, g1_ref, w_in_ref, convw_ref, gconv_ref, *refs):
    side_in, (conv_ref, ug_ref, ucg_ref) = refs[:n_side], refs[n_side:n_side + 3]
    side_out = refs[n_side + 3:2 * n_side + 3]
    u_scr, w_scr, ones_scr = refs[2 * n_side + 3:]
    step = pl.program_id(0)
    last = pl.num_programs(0) - 1

    @pl.when(step == 0)
    def _():
        w_scr[:, 0:SSM_W] = w_in_ref[:, 3 * CONV_W:].astype(_BF16)
        for k in range(CONV_W // MXU_W):
            for part in range(3):
                dst = SSM_W + (3 * k + part) * MXU_W
                src = part * CONV_W + k * MXU_W
                w_scr[:, dst:dst + MXU_W] = w_in_ref[:, src:src + MXU_W].astype(_BF16)
        ones_scr[...] = _group_ones(MXU_W, CONV_W // CONV_HEADS)

    def normed(rows_ref, mod_row):
        sh1 = mods_ref[mod_row:mod_row + 1, 0:D_MODEL]
        sc1 = mods_ref[mod_row:mod_row + 1, D_MODEL:2 * D_MODEL]
        return (_rms_rows(rows_ref[...], g1_ref[...]) * (1.0 + sc1) + sh1).astype(_BF16)

    def to_group_major(z_u, out_ref, r0):
        n_rows = z_u.shape[0]
        chunks = slice(r0 // CHUNK, (r0 + n_rows) // CHUNK)
        for q in range(N_SLABS):
            u_scr[q, r0:r0 + n_rows] = z_u[:, q * LANES:(q + 1) * LANES]
        for q in range(N_SLABS):
            for half in range(CHUNK // BLOCKS):
                rows = [u_scr[q, pl.ds(r0 + half * BLOCKS + t, n_rows // CHUNK, stride=CHUNK), :]
                        for t in range(BLOCKS)]
                cols = _lane_block_transpose(rows)
                for g8 in range(BLOCKS):
                    out_ref[q * BLOCKS + g8, chunks, half * LANES:(half + 1) * LANES] = cols[g8].astype(_BF16)

    def conv_branch(z, r0):
        n_rows = z.shape[0]
        pos = lax.broadcasted_iota(jnp.int32, (n_rows, MXU_W), 0) % GRID_W
        for k in range(CONV_W // MXU_W):
            c0 = SSM_W + 3 * k * MXU_W
            lanes = slice(k * MXU_W, (k + 1) * MXU_W)
            b = z[:, c0:c0 + MXU_W]
            cv = z[:, c0 + MXU_W:c0 + 2 * MXU_W] * z[:, c0 + 2 * MXU_W:c0 + 3 * MXU_W]
            prev = jnp.where(pos == 0, 0.0, pltpu.roll(cv, 1, axis=0))
            nxt = jnp.where(pos == GRID_W - 1, 0.0, pltpu.roll(cv, n_rows - 1, axis=0))
            y = b * (convw_ref[0:1, lanes] * prev + convw_ref[1:2, lanes] * cv + convw_ref[2:3, lanes] * nxt)
            ssq = _dot((y * y).astype(_BF16), ones_scr[...])
            yn = y * lax.rsqrt(ssq * (1.0 / (CONV_W // CONV_HEADS)) + RMS_EPS) * gconv_ref[:, lanes]
            conv_ref[r0:r0 + n_rows, lanes] = yn.astype(_BF16)

    @pl.when(step < last)
    def _():
        for src_ref, dst_ref in zip(side_in, side_out):
            dst_ref[...] = src_ref[...].astype(_BF16)
        th = x_ref.shape[0] // 2
        z_a = _dot(normed(x_ref.at[0:th], 0), w_scr[...])
        z_b = _dot(normed(x_ref.at[th:2 * th], 0), w_scr[...])
        to_group_major(z_a[:, 0:SSM_W], ug_ref, 0)
        conv_branch(z_a, 0)
        to_group_major(z_b[:, 0:SSM_W], ug_ref, th)
        conv_branch(z_b, th)

    @pl.when(step == last)
    def _():
        to_group_major(_dot(normed(ctx_ref, 1), w_scr[:, 0:SSM_W]), ucg_ref, 0)


def _in_proj(x2d, ctx2d, mods, g1, w_in, conv_w, g_conv, tm, side_weights):
    n, n_ctx = x2d.shape[0], ctx2d.shape[0]
    steps = n // tm
    const = lambda i: (0, 0)
    row = lambda i: (jnp.minimum(i, steps - 1), 0)
    side_specs = [pl.BlockSpec((w.shape[0] // steps, w.shape[1]), row) for w in side_weights]
    return pl.pallas_call(
        functools.partial(_in_kernel, len(side_weights)),
        grid=(steps + 1,),
        in_specs=[pl.BlockSpec((tm, D_MODEL), row),
                  pl.BlockSpec((n_ctx, D_MODEL), const),
                  pl.BlockSpec(mods.shape, const),
                  pl.BlockSpec((1, D_MODEL), const),
                  pl.BlockSpec((D_MODEL, IN_W), const, pipeline_mode=pl.Buffered(1)),
                  pl.BlockSpec((3, CONV_W), const),
                  pl.BlockSpec((1, CONV_W), const)] + side_specs,
        out_specs=[pl.BlockSpec((tm, CONV_W), row),
                   pl.BlockSpec((SSM_GROUPS, tm // CHUNK, CHUNK_W), lambda i: (0, jnp.minimum(i, steps - 1), 0)),
                   pl.BlockSpec((SSM_GROUPS, n_ctx // CHUNK, CHUNK_W), lambda i: (0, 0, 0))] + side_specs,
        out_shape=[jax.ShapeDtypeStruct((n, CONV_W), _BF16),
                   jax.ShapeDtypeStruct((SSM_GROUPS, n // CHUNK, CHUNK_W), _BF16),
                   jax.ShapeDtypeStruct((SSM_GROUPS, n_ctx // CHUNK, CHUNK_W), _BF16)]
                  + [jax.ShapeDtypeStruct(w.shape, _BF16) for w in side_weights],
        scratch_shapes=[pltpu.VMEM((N_SLABS, tm, LANES), _F32),
                        pltpu.VMEM((D_MODEL, IN_W), _BF16),
                        pltpu.VMEM((MXU_W, MXU_W), _BF16)],
        compiler_params=pltpu.CompilerParams(dimension_semantics=("arbitrary",),
                                             vmem_limit_bytes=VMEM_LIMIT),
        name="in_proj",
    )(x2d, ctx2d, mods, g1, w_in, conv_w, g_conv, *side_weights)


def _scan_steps(n_rows):
    return max(1, math.ceil(math.log2(n_rows)))


def _ends_pad(n_tiles):
    return max(SUBLANES, 1 << (_scan_steps(n_tiles) - 1))


def _cmul(ar, ai, br, bi):
    return ar * br - ai * bi, ar * bi + ai * br


def _eye(n):
    rows, cols = (lax.broadcasted_iota(jnp.int32, (n, n), axis) for axis in (0, 1))
    return jnp.where(rows == cols, 1.0, 0.0).astype(_BF16)


def _dot_nt_exact(eye, x, repeat=1):
    hi = x.astype(_BF16)
    rest = x - hi.astype(_F32)
    mid = rest.astype(_BF16)
    lo = (rest - mid.astype(_F32)).astype(_BF16)
    nt = lambda term: lax.dot_general(eye, jnp.concatenate([term] * repeat, axis=0),
                                      (((1,), (1,)), ((), ())), preferred_element_type=_F32)
    return nt(hi) + nt(mid) + nt(lo)


def _ssm_prep(n_steps, par_ref, br_ref, bi_ref, cr_ref, ci_ref, wst_ref, wout_ref, cmat_ref):
    T = CHUNK
    left8 = lax.broadcasted_iota(jnp.int32, (SUBLANES, LANES), 1) < SSM_STATE
    row0 = lax.broadcasted_iota(jnp.int32, (SUBLANES, LANES), 0) == 0

    def member_rows(vf, vb):
        f8 = jnp.broadcast_to(vf, (SUBLANES, LANES))
        b8 = pltpu.roll(jnp.broadcast_to(vb, (SUBLANES, LANES)), SSM_STATE, axis=1)
        return jnp.where(row0, jnp.where(left8, f8, b8), jnp.where(left8, b8, f8))

    pf, pb = par_ref[0, 0], par_ref[1, 0]
    lr = jnp.minimum(member_rows(pf[0:1], pb[0:1]), -1e-4)
    li = member_rows(pf[1:2], pb[1:2])
    dt = jnp.exp(member_rows(pf[2:3], pb[2:3]))
    mag = jnp.exp(lr * dt)
    ab_re = mag * jnp.cos(li * dt)
    ab_im = mag * jnp.sin(li * dt)
    nr = ab_re - 1.0
    den = lr * lr + li * li
    f_re = (nr * lr + ab_im * li) / den
    f_im = (ab_im * lr - nr * li) / den
    pt_re, pt_im = [jnp.ones_like(ab_re)], [jnp.zeros_like(ab_im)]
    for _ in range(T):
        nxt = _cmul(pt_re[-1], pt_im[-1], ab_re, ab_im)
        pt_re.append(nxt[0])
        pt_im.append(nxt[1])

    cols = jnp.concatenate([ab_re, ab_im, jnp.zeros((LANES - 2 * SUBLANES, LANES), _F32)], axis=0).T
    left1 = lax.broadcasted_iota(jnp.int32, (1, LANES), 1) < SSM_STATE
    eye_i = _eye(SSM_GROUP)
    eye_p = _eye(SSM_STATE)
    rows = lax.broadcasted_iota(jnp.int32, (LANES, CHUNK_W), 0)
    tcol = lax.broadcasted_iota(jnp.int32, (LANES, CHUNK_W), 1) // SSM_GROUP
    zero_half = jnp.zeros((SSM_STATE, CHUNK_W), _BF16)
    for e in range(PAIR):
        fwd_first = e == 0
        d0, d1 = (0, 1) if fwd_first else (1, 0)
        b_re = _dot_nt_exact(eye_i, jnp.concatenate([br_ref[d0, e], br_ref[d1, e]], axis=0))
        b_im = _dot_nt_exact(eye_i, jnp.concatenate([bi_ref[d0, e], bi_ref[d1, e]], axis=0))
        bbar_re, bbar_im = _cmul(f_re[e:e + 1], f_im[e:e + 1], b_re, b_im)
        fwd_lane = left1 if fwd_first else jnp.logical_not(left1)
        for s in range(T):
            pr = jnp.where(fwd_lane, pt_re[T - 1 - s][e:e + 1], pt_re[s][e:e + 1])
            pi = jnp.where(fwd_lane, pt_im[T - 1 - s][e:e + 1], pt_im[s][e:e + 1])
            w_re, w_im = _cmul(pr, pi, bbar_re, bbar_im)
            wst_ref[e, s * SSM_GROUP:(s + 1) * SSM_GROUP, 0:LANES] = w_re.astype(_BF16)
            wst_ref[e, s * SSM_GROUP:(s + 1) * SSM_GROUP, LANES:] = w_im.astype(_BF16)
        fwd_row = (rows < SSM_STATE) if fwd_first else (rows >= SSM_STATE)
        expo = jnp.where(fwd_row, tcol + 1, T - tcol)
        q_re = jnp.ones((LANES, CHUNK_W), _F32)
        q_im = jnp.zeros((LANES, CHUNK_W), _F32)
        s_re, s_im = cols[:, e:e + 1], cols[:, SUBLANES + e:SUBLANES + e + 1]
        for bit in range(T.bit_length()):
            m_re, m_im = _cmul(q_re, q_im, jnp.broadcast_to(s_re, q_re.shape), jnp.broadcast_to(s_im, q_re.shape))
            take = ((expo >> bit) & 1) == 1
            q_re, q_im = jnp.where(take, m_re, q_re), jnp.where(take, m_im, q_im)
            s_re, s_im = _cmul(s_re, s_im, s_re, s_im)
        tiled = lambda c: _dot_nt_exact(eye_p, c, repeat=T)
        c_re = jnp.concatenate([tiled(cr_ref[d0, e]), tiled(cr_ref[d1, e])], axis=0)
        c_im = jnp.concatenate([tiled(ci_ref[d0, e]), tiled(ci_ref[d1, e])], axis=0)
        o_re, o_im = _cmul(c_re, c_im, q_re, q_im)
        wout_ref[e, 0:LANES] = o_re.astype(_BF16)
        wout_ref[e, LANES:] = (-o_im).astype(_BF16)
        for ri, src in enumerate((c_re, -c_im)):
            for slot, d in enumerate((d0, d1)):
                r0 = ri * LANES + slot * SSM_STATE
                block = src[slot * SSM_STATE:(slot + 1) * SSM_STATE].astype(_BF16)
                cmat_ref[e, r0:r0 + SSM_STATE, d * CHUNK_W:(d + 1) * CHUNK_W] = block
                cmat_ref[e, r0:r0 + SSM_STATE, (1 - d) * CHUNK_W:(2 - d) * CHUNK_W] = zero_half

    m0_re, m1_re = pt_re[T][0:1], pt_re[T][1:2]
    m0_im, m1_im = pt_im[T][0:1], pt_im[T][1:2]
    cf = (jnp.where(left1, m0_re, m1_re), jnp.where(left1, m0_im, m1_im))
    cb = (jnp.where(left1, m1_re, m0_re), jnp.where(left1, m1_im, m0_im))
    coefs = []
    for _ in range(n_steps):
        coefs.append([cf[0], cf[1], cb[0], cb[1]])
        cf = _cmul(cf[0], cf[1], cf[0], cf[1])
        cb = _cmul(cb[0], cb[1], cb[0], cb[1])
    return coefs


def _ssm_kernel(n_lat, n_ctx, u_ref, uc_ref, par_ref, br_ref, bi_ref, cr_ref, ci_ref, d_ref,
                y_ref, buf_a, buf_b, ends_a, ends_b, wst_ref, wout_ref, cmat_ref, m_scr):
    n_rows = n_lat + n_ctx
    pad = SUBLANES
    coefs = _ssm_prep(_scan_steps(n_rows), par_ref, br_ref, bi_ref, cr_ref, ci_ref,
                      wst_ref, wout_ref, cmat_ref)

    lane = lax.broadcasted_iota(jnp.int32, (SSM_GROUP, CHUNK_W), 1)
    blk = lane // SSM_GROUP
    on_diag = lane % SSM_GROUP == lax.broadcasted_iota(jnp.int32, (SSM_GROUP, CHUNK_W), 0)
    for e in range(PAIR):
        d_skip = jnp.where(on_diag, jnp.broadcast_to(d_ref[e], (SSM_GROUP, CHUNK_W)), 0.0)
        taps = _dot(wst_ref[e], cmat_ref[e])
        taps_f = taps[:, 0:CHUNK_W]
        taps_b = taps[:, CHUNK_W:]
        for s in range(CHUNK):
            acc = jnp.zeros((SSM_GROUP, CHUNK_W), _F32)
            for t in range(CHUNK):
                val = None
                if s >= t:
                    r0 = (s - t) * SSM_GROUP
                    val = taps_b[r0:r0 + SSM_GROUP]
                if s <= t:
                    r0 = (CHUNK - 1 - t + s) * SSM_GROUP
                    v2 = taps_f[r0:r0 + SSM_GROUP]
                    val = v2 if val is None else val + v2
                if s == t:
                    val = val + d_skip
                acc = jnp.where(blk == t, val, acc)
            m_scr[e, s * SSM_GROUP:(s + 1) * SSM_GROUP, :] = acc.astype(_BF16)

    def paired(m0, m1):
        left = lax.broadcasted_iota(jnp.int32, (m0.shape[0], PAIR_STATE), 1) < SSM_STATE
        re0, im0 = m0[:, 0:PAIR_STATE], m0[:, PAIR_STATE:]
        re1, im1 = m1[:, 0:PAIR_STATE], m1[:, PAIR_STATE:]
        return [jnp.where(left, re0, re1), jnp.where(left, im0, im1),
                jnp.where(left, re1, re0), jnp.where(left, im1, im0)]

    s_lat = paired(_dot(u_ref[0], wst_ref[0]), _dot(u_ref[1], wst_ref[1]))
    s_ctx = paired(_dot(uc_ref[0], wst_ref[0]), _dot(uc_ref[1], wst_ref[1]))

    n_tiles = n_rows // SUBLANES
    tile_row = lax.broadcasted_iota(jnp.int32, (1, SUBLANES, PAIR_STATE), 1)

    def tile_scan(x_re, x_im, comp0, backward):
        for k in range(SUBLANES.bit_length() - 1):
            sh = 1 << k
            shift, keep = (SUBLANES - sh, tile_row < SUBLANES - sh) if backward else (sh, tile_row >= sh)
            a_re = jnp.where(keep, coefs[k][comp0], 0.0)
            a_im = jnp.where(keep, coefs[k][comp0 + 1], 0.0)
            p_re, p_im = pltpu.roll(x_re, shift, axis=1), pltpu.roll(x_im, shift, axis=1)
            x_re, x_im = x_re + a_re * p_re - a_im * p_im, x_im + a_re * p_im + a_im * p_re
        return x_re, x_im

    as_tiles = lambda parts: jnp.concatenate(parts, axis=0).reshape(n_tiles, SUBLANES, PAIR_STATE)
    local = list(tile_scan(as_tiles([s_ctx[0], s_lat[0]]), as_tiles([s_ctx[1], s_lat[1]]), 0, False))
    local += tile_scan(as_tiles([s_lat[2], s_ctx[2]]), as_tiles([s_lat[3], s_ctx[3]]), 2, True)
    for comp in range(N_COMP):
        buf_a[comp, pad:pad + n_rows] = local[comp].reshape(n_rows, PAIR_STATE)

    for buf in (ends_a, ends_b):
        buf[...] = jnp.zeros(buf.shape, _F32)
    epad = _ends_pad(n_tiles)
    for comp in range(N_COMP):
        end_row = pad + (SUBLANES - 1 if comp < 2 else 0)
        ends_a[comp, epad:epad + n_tiles] = buf_a[comp, pl.ds(end_row, n_tiles, stride=SUBLANES), :]
    src, dst = ends_a, ends_b
    for k in range(_scan_steps(n_tiles)):
        sh = 1 << k
        coef = coefs[SUBLANES.bit_length() - 1 + k]
        lo, hi = epad, epad + n_tiles
        xr, xi = src[0, lo - sh:hi - sh], src[1, lo - sh:hi - sh]
        dst[0, lo:hi] = src[0, lo:hi] + coef[0] * xr - coef[1] * xi
        dst[1, lo:hi] = src[1, lo:hi] + coef[0] * xi + coef[1] * xr
        xr, xi = src[2, lo + sh:hi + sh], src[3, lo + sh:hi + sh]
        dst[2, lo:hi] = src[2, lo:hi] + coef[2] * xr - coef[3] * xi
        dst[3, lo:hi] = src[3, lo:hi] + coef[2] * xi + coef[3] * xr
        src, dst = dst, src

    def carry_powers(comp0, backward):
        base = [(coefs[k][comp0], coefs[k][comp0 + 1]) for k in range(SUBLANES.bit_length())]
        pw = []
        for r in range(1, SUBLANES + 1):
            acc = None
            for k, term in enumerate(base):
                if (r >> k) & 1:
                    acc = term if acc is None else _cmul(acc[0], acc[1], term[0], term[1])
            pw.append(acc)
        if backward:
            pw = pw[::-1]
        return (jnp.concatenate([p[0] for p in pw], axis=0), jnp.concatenate([p[1] for p in pw], axis=0))

    for comp0, backward in ((0, False), (2, True)):
        p_re, p_im = carry_powers(comp0, backward)
        first = epad + 1 if backward else epad - 1
        for i in range(n_tiles):
            c_re = src[comp0, first + i:first + i + 1]
            c_im = src[comp0 + 1, first + i:first + i + 1]
            rows = slice(pad + i * SUBLANES, pad + (i + 1) * SUBLANES)
            buf_b[comp0, rows] = buf_a[comp0, rows] + p_re * c_re - p_im * c_im
            buf_b[comp0 + 1, rows] = buf_a[comp0 + 1, rows] + p_re * c_im + p_im * c_re

    f0 = pad + n_ctx - 1
    b0 = pad + 1
    f_re, f_im = buf_b[0, f0:f0 + n_lat], buf_b[1, f0:f0 + n_lat]
    b_re, b_im = buf_b[2, b0:b0 + n_lat], buf_b[3, b0:b0 + n_lat]
    left = lax.broadcasted_iota(jnp.int32, (n_lat, PAIR_STATE), 1) < SSM_STATE
    h0 = jnp.concatenate([jnp.where(left, f_re, b_re), jnp.where(left, f_im, b_im)], axis=1)
    h1 = jnp.concatenate([jnp.where(left, b_re, f_re), jnp.where(left, b_im, f_im)], axis=1)
    y_ref[0] = _dot(u_ref[0], m_scr[0]) + _dot(h0.astype(_BF16), wout_ref[0])
    y_ref[1] = _dot(u_ref[1], m_scr[1]) + _dot(h1.astype(_BF16), wout_ref[1])


def _ssm(u_g, uc_g, par, b_re, b_im, c_re, c_im, d_col):
    n_groups, n_lat, _ = u_g.shape
    n_ctx = uc_g.shape[1]
    n_rows = n_lat + n_ctx
    n_pairs = n_groups // PAIR
    buf = pltpu.VMEM((N_COMP, n_rows + 2 * SUBLANES, PAIR_STATE), _F32)
    n_tiles = n_rows // SUBLANES
    ends_rows = _ends_pad(n_tiles) + -(-(n_tiles + _ends_pad(n_tiles)) // SUBLANES) * SUBLANES
    ends = pltpu.VMEM((N_COMP, ends_rows, PAIR_STATE), _F32)
    mat = pltpu.VMEM((PAIR, CHUNK_W, CHUNK_W), _BF16)
    pair3 = lambda q: (q, 0, 0)
    by_dir = lambda q: (0, q, 0, 0)
    return pl.pallas_call(
        functools.partial(_ssm_kernel, n_lat, n_ctx),
        grid=(n_pairs,),
        in_specs=[pl.BlockSpec((PAIR, n_lat, CHUNK_W), pair3),
                  pl.BlockSpec((PAIR, n_ctx, CHUNK_W), pair3),
                  pl.BlockSpec((2, 1) + par.shape[2:], by_dir),
                  pl.BlockSpec((2, PAIR) + b_re.shape[2:], by_dir),
                  pl.BlockSpec((2, PAIR) + b_im.shape[2:], by_dir),
                  pl.BlockSpec((2, PAIR) + c_re.shape[2:], by_dir),
                  pl.BlockSpec((2, PAIR) + c_im.shape[2:], by_dir),
                  pl.BlockSpec((PAIR,) + d_col.shape[1:], pair3)],
        out_specs=pl.BlockSpec((PAIR, n_lat, CHUNK_W), pair3),
        out_shape=jax.ShapeDtypeStruct((n_groups, n_lat, CHUNK_W), _F32),
        scratch_shapes=[buf, buf, ends, ends, mat, mat, pltpu.VMEM((PAIR, CHUNK_W, 2 * CHUNK_W), _BF16), mat],
        compiler_params=pltpu.CompilerParams(dimension_semantics=("arbitrary",),
                                             vmem_limit_bytes=VMEM_LIMIT),
        name="ssm",
    )(u_g, uc_g, par, b_re, b_im, c_re, c_im, d_col)


def _out_kernel(x_ref, conv_ref, yg_ref, mods_ref, gssm_ref, g2_ref, gfin_ref,
                wglu32_ref, wout32_ref, w1_ref, w2_ref, o_ref, y_scr, mix_scr, wglu_ref, wout_ref, ones_ref):
    @pl.when(pl.program_id(0) == 0)
    def _():
        wglu_ref[...] = wglu32_ref[...].astype(_BF16)
        wout_ref[...] = wout32_ref[...].astype(_BF16)
        ones_ref[...] = _group_ones(MXU_W, SSM_GROUP)

    gt1 = mods_ref[0:1, 2 * D_MODEL:3 * D_MODEL]
    sh2 = mods_ref[0:1, 3 * D_MODEL:4 * D_MODEL]
    sc2 = mods_ref[0:1, 4 * D_MODEL:5 * D_MODEL]
    gt2 = mods_ref[0:1, 5 * D_MODEL:6 * D_MODEL]
    th = x_ref.shape[0] // 2
    n_split = 4
    ff = D_FF // n_split

    def mixer(r0):
        rows_ = slice(r0, r0 + th)
        chunks = slice(r0 // CHUNK, (r0 + th) // CHUNK)
        for q in range(N_SLABS):
            for half in range(CHUNK // BLOCKS):
                cols = [yg_ref[q * BLOCKS + g8, chunks, half * LANES:(half + 1) * LANES] for g8 in range(BLOCKS)]
                rows = _lane_block_transpose(cols)
                for t in range(BLOCKS):
                    y_scr[q, pl.ds(r0 + half * BLOCKS + t, th // CHUNK, stride=CHUNK), :] = rows[t]
        s = jax.nn.gelu(jnp.concatenate([y_scr[q, rows_] for q in range(N_SLABS)], axis=1))
        s = s * jax.nn.sigmoid(_dot(s.astype(_BF16), wglu_ref[...]))
        sq = (s * s).astype(_BF16)
        ssq = jnp.concatenate([_dot(sq[:, k * MXU_W:(k + 1) * MXU_W], ones_ref[...])
                               for k in range(SSM_W // MXU_W)], axis=1)
        sn = s * lax.rsqrt(ssq * (1.0 / SSM_GROUP) + RMS_EPS) * gssm_ref[...]
        mix = mix_scr[rows_] + _dot(sn.astype(_BF16), wout_ref[CONV_W:])
        x_mid = x_ref[rows_, :] + gt1 * mix
        return x_mid, (_rms_rows(x_mid, g2_ref[...]) * (1.0 + sc2) + sh2).astype(_BF16)

    def mlp_chunk(h2, j):
        a = jnp.maximum(_dot(h2, w1_ref[:, j * ff:(j + 1) * ff]), 0.0)
        return _dot((a * a).astype(_BF16), w2_ref[j * ff:(j + 1) * ff])

    def finish(r0, x_mid, acc):
        o_ref[r0:r0 + th, :] = _rms_rows(x_mid + gt2 * acc, gfin_ref[...])

    mix_scr[...] = _dot(conv_ref[...], wout_ref[0:CONV_W])
    x_a, h_a = mixer(0)
    acc_a = mlp_chunk(h_a, 0)
    x_b, h_b = mixer(th)
    for j in range(1, n_split):
        acc_a = acc_a + mlp_chunk(h_a, j)
    acc_b = mlp_chunk(h_b, 0)
    finish(0, x_a, acc_a)
    for j in range(1, n_split):
        acc_b = acc_b + mlp_chunk(h_b, j)
    finish(th, x_b, acc_b)


def _out_proj(x2d, conv_n, y_g, mods, g_ssm, g2, g_fin, wglu, wout, w1, w2, tm):
    n = x2d.shape[0]
    const = lambda i: (0, 0)
    row = lambda i: (i, 0)
    resident = lambda shape: pl.BlockSpec(shape, const, pipeline_mode=pl.Buffered(1))
    return pl.pallas_call(
        _out_kernel,
        grid=(n // tm,),
        in_specs=[pl.BlockSpec((tm, D_MODEL), row),
                  pl.BlockSpec((tm, CONV_W), row),
                  pl.BlockSpec((SSM_GROUPS, tm // CHUNK, CHUNK_W), lambda i: (0, i, 0)),
                  pl.BlockSpec(mods.shape, const),
                  pl.BlockSpec((1, SSM_W), const),
                  pl.BlockSpec((1, D_MODEL), const),
                  pl.BlockSpec((1, D_MODEL), const),
                  resident((SSM_W, SSM_W)),
                  resident((D_MODEL, D_MODEL)),
                  resident((D_MODEL, D_FF)),
                  resident((D_FF, D_MODEL))],
        out_specs=pl.BlockSpec((tm, D_MODEL), row),
        out_shape=jax.ShapeDtypeStruct((n, D_MODEL), _F32),
        scratch_shapes=[pltpu.VMEM((N_SLABS, tm, LANES), _F32), pltpu.VMEM((tm, D_MODEL), _F32),
                        pltpu.VMEM((SSM_W, SSM_W), _BF16), pltpu.VMEM((D_MODEL, D_MODEL), _BF16),
                        pltpu.VMEM((MXU_W, MXU_W), _BF16)],
        compiler_params=pltpu.CompilerParams(dimension_semantics=("arbitrary",),
                                             vmem_limit_bytes=VMEM_LIMIT),
        name="out_proj",
    )(x2d, conv_n, y_g, mods, g_ssm, g2, g_fin, wglu, wout, w1, w2)


def _ssm_lane_params(a_re, a_im, log_dt):
    n_pairs = SSM_GROUPS // PAIR
    pair_lanes = lambda v: v.reshape(2, n_pairs, PAIR * SSM_STATE)
    dt_lanes = jnp.broadcast_to(log_dt[..., None], (2, SSM_GROUPS, SSM_STATE))
    return jnp.stack([pair_lanes(a_re), pair_lanes(a_im), pair_lanes(dt_lanes)], axis=2)


def kernel(x, c, ctx, c_ctx, w_mod, b_mod, g_norm1, w_in, conv_w, ssm_a_re, ssm_a_im, ssm_log_dt,
           ssm_b_re, ssm_b_im, ssm_c_re, ssm_c_im, ssm_d, w_glu, g_conv_out, g_ssm_out, w_out,
           g_norm2, w_mlp1, w_mlp2, g_final):
    bsz, n_lat, d_model = x.shape
    n_ctx = ctx.shape[1]
    assert bsz == 1 and d_model == D_MODEL and w_mod.shape[0] == 1
    assert n_lat % (CHUNK * SUBLANES) == 0 and n_ctx % (CHUNK * CHUNK) == 0 and n_lat % GRID_W == 0
    layer = 0
    x2d = x[0]
    ctx2d = ctx[0]

    mods = _modulation(c, c_ctx[None, :], w_mod[layer], b_mod[layer][None, :])

    g1 = g_norm1[layer][None, :]
    g_conv = g_conv_out[layer][None, :]
    conv_n, u_g, uc_g, w1_bf, w2_bf = _in_proj(x2d, ctx2d, mods, g1, w_in[layer], conv_w[layer], g_conv, tm=1024,
                                                side_weights=(w_mlp1[layer], w_mlp2[layer]))

    y_g = _ssm(u_g, uc_g, _ssm_lane_params(ssm_a_re[layer], ssm_a_im[layer], ssm_log_dt[layer]),
               ssm_b_re[layer], ssm_b_im[layer], ssm_c_re[layer], ssm_c_im[layer],
               ssm_d.reshape(SSM_GROUPS, SSM_GROUP, 1))

    out = _out_proj(x2d, conv_n, y_g, mods, g_ssm_out[layer][None, :], g_norm2[layer][None, :],
                    g_final[None, :], w_glu[layer], w_out[layer], w1_bf, w2_bf, tm=1024)
    return out[None]
```

```python
import functools
import math

import jax
import jax.numpy as jnp
from jax import lax
from jax.experimental import pallas as pl
from jax.experimental.pallas import tpu as pltpu

D_MODEL = 1024
GRID_W = 64
CONV_W = 512
CONV_HEADS = 8
SSM_W = 512
SSM_GROUP = 16
SSM_GROUPS = 32
SSM_STATE = 64
IN_W = 3 * CONV_W + SSM_W
D_FF = 4 * D_MODEL
N_MOD = 6
RMS_EPS = 1e-6

LANES = 128
MXU_W = 256
SUBLANES = 8
CHUNK = 16
CHUNK_W = CHUNK * SSM_GROUP
PAIR = 2
PAIR_STATE = PAIR * SSM_STATE
N_COMP = 4
BLOCKS = LANES // SSM_GROUP
N_SLABS = SSM_W // LANES
VMEM_LIMIT = 62 * 1024 * 1024

_BF16 = jnp.bfloat16
_F32 = jnp.float32


def _dot(a, b):
    return jnp.dot(a, b, preferred_element_type=_F32)


def _lane_block_transpose(vs):
    blk = lax.broadcasted_iota(jnp.int32, vs[0].shape, 1) // SSM_GROUP
    dist = BLOCKS // 2
    while dist:
        shift = dist * SSM_GROUP
        upper = (blk & dist) != 0
        out = [None] * BLOCKS
        for a in range(BLOCKS):
            if a & dist:
                continue
            lo, hi = vs[a], vs[a + dist]
            out[a] = jnp.where(upper, pltpu.roll(hi, shift, axis=1), lo)
            out[a + dist] = jnp.where(upper, hi, pltpu.roll(lo, LANES - shift, axis=1))
        vs = out
        dist //= 2
    return vs


def _mod_kernel(c_ref, cctx_ref, w_ref, b_ref, o_ref):
    tk = w_ref.shape[0]
    first = lax.broadcasted_iota(jnp.int32, (SUBLANES, tk), 0) == 0
    s = jnp.where(first, jnp.broadcast_to(c_ref[...], first.shape), jnp.broadcast_to(cctx_ref[...], first.shape))
    act = s * jax.nn.sigmoid(s)
    w = w_ref[...]
    a_hi, w_hi = act.astype(_BF16), w.astype(_BF16)
    a_lo = (act - a_hi.astype(_F32)).astype(_BF16)
    w_lo = (w - w_hi.astype(_F32)).astype(_BF16)
    by_hi = _dot(jnp.concatenate([a_hi, a_lo], axis=0), w_hi)
    part = by_hi[0:SUBLANES] + by_hi[SUBLANES:] + _dot(a_hi, w_lo)

    @pl.when(pl.program_id(0) == 0)
    def _():
        o_ref[...] = part + b_ref[...]

    @pl.when(pl.program_id(0) > 0)
    def _():
        o_ref[...] += part


def _modulation(c_row, cctx_row, w_mod, b_mod):
    d_in, n_out = w_mod.shape
    tk = 256
    return pl.pallas_call(
        _mod_kernel,
        grid=(d_in // tk,),
        in_specs=[pl.BlockSpec((1, tk), lambda k: (0, k)),
                  pl.BlockSpec((1, tk), lambda k: (0, k)),
                  pl.BlockSpec((tk, n_out), lambda k: (k, 0)),
                  pl.BlockSpec((1, n_out), lambda k: (0, 0))],
        out_specs=pl.BlockSpec((SUBLANES, n_out), lambda k: (0, 0)),
        out_shape=jax.ShapeDtypeStruct((SUBLANES, n_out), _F32),
        compiler_params=pltpu.CompilerParams(dimension_semantics=("arbitrary",),
                                             vmem_limit_bytes=VMEM_LIMIT),
        name="mod",
    )(c_row, cctx_row, w_mod, b_mod)


def _rms_rows(x, g):
    ms = jnp.mean(x * x, axis=-1, keepdims=True)
    return x * lax.rsqrt(ms + RMS_EPS) * g


def _group_ones(width, group):
    rows = lax.broadcasted_iota(jnp.int32, (width, width), 0) // group
    cols = lax.broadcasted_iota(jnp.int32, (width, width), 1) // group
    return jnp.where(rows == cols, 1.0, 0.0).astype(_BF16)


def _in_kernel(n_side, x_ref, ctx_ref, mods_ref, g1_ref, w_in_ref, convw_ref, gconv_ref, *refs):
    side_in, (conv_ref, ug_ref, ucg_ref) = refs[:n_side], refs[n_side:n_side + 3]
    side_out = refs[n_side + 3:2 * n_side + 3]
    u_scr, w_scr, ones_scr = refs[2 * n_side + 3:]
    step = pl.program_id(0)
    last = pl.num_programs(0) - 1

    @pl.when(step == 0)
    def _():
        w_scr[:, 0:SSM_W] = w_in_ref[:, 3 * CONV_W:].astype(_BF16)
        for k in range(CONV_W // MXU_W):
            for part in range(3):
                dst = SSM_W + (3 * k + part) * MXU_W
                src = part * CONV_W + k * MXU_W
                w_scr[:, dst:dst + MXU_W] = w_in_ref[:, src:src + MXU_W].astype(_BF16)
        ones_scr[...] = _group_ones(MXU_W, CONV_W // CONV_HEADS)

    def normed(rows_ref, mod_row):
        sh1 = mods_ref[mod_row:mod_row + 1, 0:D_MODEL]
        sc1 = mods_ref[mod_row:mod_row + 1, D_MODEL:2 * D_MODEL]
        return (_rms_rows(rows_ref[...], g1_ref[...]) * (1.0 + sc1) + sh1).astype(_BF16)

    def to_group_major(z_u, out_ref, r0):
        n_rows = z_u.shape[0]
        chunks = slice(r0 // CHUNK, (r0 + n_rows) // CHUNK)
        for q in range(N_SLABS):
            u_scr[q, r0:r0 + n_rows] = z_u[:, q * LANES:(q + 1) * LANES]
        for q in range(N_SLABS):
            for half in range(CHUNK // BLOCKS):
                rows = [u_scr[q, pl.ds(r0 + half * BLOCKS + t, n_rows // CHUNK, stride=CHUNK), :]
                        for t in range(BLOCKS)]
                cols = _lane_block_transpose(rows)
                for g8 in range(BLOCKS):
                    out_ref[q * BLOCKS + g8, chunks, half * LANES:(half + 1) * LANES] = cols[g8].astype(_BF16)

    def conv_branch(z, r0):
        n_rows = z.shape[0]
        pos = lax.broadcasted_iota(jnp.int32, (n_rows, MXU_W), 0) % GRID_W
        for k in range(CONV_W // MXU_W):
            c0 = SSM_W + 3 * k * MXU_W
            lanes = slice(k * MXU_W, (k + 1) * MXU_W)
            b = z[:, c0:c0 + MXU_W]
            cv = z[:, c0 + MXU_W:c0 + 2 * MXU_W] * z[:, c0 + 2 * MXU_W:c0 + 3 * MXU_W]
            prev = jnp.where(pos == 0, 0.0, pltpu.roll(cv, 1, axis=0))
            nxt = jnp.where(pos == GRID_W - 1, 0.0, pltpu.roll(cv, n_rows - 1, axis=0))
            y = b * (convw_ref[0:1, lanes] * prev + convw_ref[1:2, lanes] * cv + convw_ref[2:3, lanes] * nxt)
            ssq = _dot((y * y).astype(_BF16), ones_scr[...])
            yn = y * lax.rsqrt(ssq * (1.0 / (CONV_W // CONV_HEADS)) + RMS_EPS) * gconv_ref[:, lanes]
            conv_ref[r0:r0 + n_rows, lanes] = yn.astype(_BF16)

    @pl.when(step < last)
    def _():
        for src_ref, dst_ref in zip(side_in, side_out):
            dst_ref[...] = src_ref[...].astype(_BF16)
        n_parts = 2
        th = x_ref.shape[0] // n_parts
        z = [None] * n_parts

        def tail(p):
            to_group_major(z[p][:, 0:SSM_W], ug_ref, p * th)
            conv_branch(z[p], p * th)

        for p in range(n_parts):
            z[p] = _dot(normed(x_ref.at[p * th:(p + 1) * th], 0), w_scr[...])
            if p >= 2:
                tail(p - 2)
        tail(n_parts - 2)
        tail(n_parts - 1)

    @pl.when(step == last)
    def _():
        to_group_major(_dot(normed(ctx_ref, 1), w_scr[:, 0:SSM_W]), ucg_ref, 0)


def _in_proj(x2d, ctx2d, mods, g1, w_in, conv_w, g_conv, tm, side_weights):
    n, n_ctx = x2d.shape[0], ctx2d.shape[0]
    steps = n // tm
    const = lambda i: (0, 0)
    row = lambda i: (jnp.minimum(i, steps - 1), 0)
    side_specs = [pl.BlockSpec((w.shape[0] // steps, w.shape[1]), row) for w in side_weights]
    return pl.pallas_call(
        functools.partial(_in_kernel, len(side_weights)),
        grid=(steps + 1,),
        in_specs=[pl.BlockSpec((tm, D_MODEL), row),
                  pl.BlockSpec((n_ctx, D_MODEL), const),
                  pl.BlockSpec(mods.shape, const),
                  pl.BlockSpec((1, D_MODEL), const),
                  pl.BlockSpec((D_MODEL, IN_W), const, pipeline_mode=pl.Buffered(1)),
                  pl.BlockSpec((3, CONV_W), const),
                  pl.BlockSpec((1, CONV_W), const)] + side_specs,
        out_specs=[pl.BlockSpec((tm, CONV_W), row),
                   pl.BlockSpec((SSM_GROUPS, tm // CHUNK, CHUNK_W), lambda i: (0, jnp.minimum(i, steps - 1), 0)),
                   pl.BlockSpec((SSM_GROUPS, n_ctx // CHUNK, CHUNK_W), lambda i: (0, 0, 0))] + side_specs,
        out_shape=[jax.ShapeDtypeStruct((n, CONV_W), _BF16),
                   jax.ShapeDtypeStruct((SSM_GROUPS, n // CHUNK, CHUNK_W), _BF16),
                   jax.ShapeDtypeStruct((SSM_GROUPS, n_ctx // CHUNK, CHUNK_W), _BF16)]
                  + [jax.ShapeDtypeStruct(w.shape, _BF16) for w in side_weights],
        scratch_shapes=[pltpu.VMEM((N_SLABS, tm, LANES), _F32),
                        pltpu.VMEM((D_MODEL, IN_W), _BF16),
                        pltpu.VMEM((MXU_W, MXU_W), _BF16)],
        compiler_params=pltpu.CompilerParams(dimension_semantics=("arbitrary",),
                                             vmem_limit_bytes=VMEM_LIMIT),
        name="in_proj",
    )(x2d, ctx2d, mods, g1, w_in, conv_w, g_conv, *side_weights)


def _scan_steps(n_rows):
    return max(1, math.ceil(math.log2(n_rows)))


def _ends_pad(n_tiles):
    return max(SUBLANES, 1 << (_scan_steps(n_tiles) - 1))


def _cmul(ar, ai, br, bi):
    return ar * br - ai * bi, ar * bi + ai * br


def _eye(n):
    rows, cols = (lax.broadcasted_iota(jnp.int32, (n, n), axis) for axis in (0, 1))
    return jnp.where(rows == cols, 1.0, 0.0).astype(_BF16)


def _dot_nt_exact(eye, x, repeat=1):
    hi = x.astype(_BF16)
    rest = x - hi.astype(_F32)
    mid = rest.astype(_BF16)
    lo = (rest - mid.astype(_F32)).astype(_BF16)
    nt = lambda term: lax.dot_general(eye, jnp.concatenate([term] * repeat, axis=0),
                                      (((1,), (1,)), ((), ())), preferred_element_type=_F32)
    return nt(hi) + nt(mid) + nt(lo)


def _ssm_prep(n_steps, par_ref, br_ref, bi_ref, cr_ref, ci_ref, wst_ref, wout_ref, cmat_ref):
    T = CHUNK
    left8 = lax.broadcasted_iota(jnp.int32, (SUBLANES, LANES), 1) < SSM_STATE
    row0 = lax.broadcasted_iota(jnp.int32, (SUBLANES, LANES), 0) == 0

    def member_rows(vf, vb):
        f8 = jnp.broadcast_to(vf, (SUBLANES, LANES))
        b8 = pltpu.roll(jnp.broadcast_to(vb, (SUBLANES, LANES)), SSM_STATE, axis=1)
        return jnp.where(row0, jnp.where(left8, f8, b8), jnp.where(left8, b8, f8))

    pf, pb = par_ref[0, 0], par_ref[1, 0]
    lr = jnp.minimum(member_rows(pf[0:1], pb[0:1]), -1e-4)
    li = member_rows(pf[1:2], pb[1:2])
    dt = jnp.exp(member_rows(pf[2:3], pb[2:3]))
    mag = jnp.exp(lr * dt)
    ab_re = mag * jnp.cos(li * dt)
    ab_im = mag * jnp.sin(li * dt)
    nr = ab_re - 1.0
    den = lr * lr + li * li
    f_re = (nr * lr + ab_im * li) / den
    f_im = (ab_im * lr - nr * li) / den
    pt_re, pt_im = [jnp.ones_like(ab_re)], [jnp.zeros_like(ab_im)]
    for _ in range(T):
        nxt = _cmul(pt_re[-1], pt_im[-1], ab_re, ab_im)
        pt_re.append(nxt[0])
        pt_im.append(nxt[1])

    cols = jnp.concatenate([ab_re, ab_im, jnp.zeros((LANES - 2 * SUBLANES, LANES), _F32)], axis=0).T
    left1 = lax.broadcasted_iota(jnp.int32, (1, LANES), 1) < SSM_STATE
    eye_i = _eye(SSM_GROUP)
    eye_p = _eye(SSM_STATE)
    rows = lax.broadcasted_iota(jnp.int32, (LANES, CHUNK_W), 0)
    tcol = lax.broadcasted_iota(jnp.int32, (LANES, CHUNK_W), 1) // SSM_GROUP
    zero_half = jnp.zeros((SSM_STATE, CHUNK_W), _BF16)
    for e in range(PAIR):
        fwd_first = e == 0
        d0, d1 = (0, 1) if fwd_first else (1, 0)
        b_re = _dot_nt_exact(eye_i, jnp.concatenate([br_ref[d0, e], br_ref[d1, e]], axis=0))
        b_im = _dot_nt_exact(eye_i, jnp.concatenate([bi_ref[d0, e], bi_ref[d1, e]], axis=0))
        bbar_re, bbar_im = _cmul(f_re[e:e + 1], f_im[e:e + 1], b_re, b_im)
        fwd_lane = left1 if fwd_first else jnp.logical_not(left1)
        for s in range(T):
            pr = jnp.where(fwd_lane, pt_re[T - 1 - s][e:e + 1], pt_re[s][e:e + 1])
            pi = jnp.where(fwd_lane, pt_im[T - 1 - s][e:e + 1], pt_im[s][e:e + 1])
            w_re, w_im = _cmul(pr, pi, bbar_re, bbar_im)
            wst_ref[e, s * SSM_GROUP:(s + 1) * SSM_GROUP, 0:LANES] = w_re.astype(_BF16)
            wst_ref[e, s * SSM_GROUP:(s + 1) * SSM_GROUP, LANES:] = w_im.astype(_BF16)
        fwd_row = (rows < SSM_STATE) if fwd_first else (rows >= SSM_STATE)
        expo = jnp.where(fwd_row, tcol + 1, T - tcol)
        q_re = jnp.ones((LANES, CHUNK_W), _F32)
        q_im = jnp.zeros((LANES, CHUNK_W), _F32)
        s_re, s_im = cols[:, e:e + 1], cols[:, SUBLANES + e:SUBLANES + e + 1]
        for bit in range(T.bit_length()):
            m_re, m_im = _cmul(q_re, q_im, jnp.broadcast_to(s_re, q_re.shape), jnp.broadcast_to(s_im, q_re.shape))
            take = ((expo >> bit) & 1) == 1
            q_re, q_im = jnp.where(take, m_re, q_re), jnp.where(take, m_im, q_im)
            s_re, s_im = _cmul(s_re, s_im, s_re, s_im)
        tiled = lambda c: _dot_nt_exact(eye_p, c, repeat=T)
        c_re = jnp.concatenate([tiled(cr_ref[d0, e]), tiled(cr_ref[d1, e])], axis=0)
        c_im = jnp.concatenate([tiled(ci_ref[d0, e]), tiled(ci_ref[d1, e])], axis=0)
        o_re, o_im = _cmul(c_re, c_im, q_re, q_im)
        wout_ref[e, 0:LANES] = o_re.astype(_BF16)
        wout_ref[e, LANES:] = (-o_im).astype(_BF16)
        for ri, src in enumerate((c_re, -c_im)):
            for slot, d in enumerate((d0, d1)):
                r0 = ri * LANES + slot * SSM_STATE
                block = src[slot * SSM_STATE:(slot + 1) * SSM_STATE].astype(_BF16)
                cmat_ref[e, r0:r0 + SSM_STATE, d * CHUNK_W:(d + 1) * CHUNK_W] = block
                cmat_ref[e, r0:r0 + SSM_STATE, (1 - d) * CHUNK_W:(2 - d) * CHUNK_W] = zero_half

    m0_re, m1_re = pt_re[T][0:1], pt_re[T][1:2]
    m0_im, m1_im = pt_im[T][0:1], pt_im[T][1:2]
    cf = (jnp.where(left1, m0_re, m1_re), jnp.where(left1, m0_im, m1_im))
    cb = (jnp.where(left1, m1_re, m0_re), jnp.where(left1, m1_im, m0_im))
    coefs = []
    for _ in range(n_steps):
        coefs.append([cf[0], cf[1], cb[0], cb[1]])
        cf = _cmul(cf[0], cf[1], cf[0], cf[1])
        cb = _cmul(cb[0], cb[1], cb[0], cb[1])
    return coefs


def _ssm_kernel(n_lat, n_ctx, pairs_per_step, u_ref, uc_ref, par_ref, br_ref, bi_ref, cr_ref, ci_ref, d_ref,
                y_ref, *scratch):
    programs = []
    for p in range(pairs_per_step):
        members = slice(p * PAIR, (p + 1) * PAIR)
        programs.append(_ssm_pair(
            n_lat, n_ctx, u_ref.at[members], uc_ref.at[members], par_ref.at[:, p:p + 1],
            br_ref.at[:, members], bi_ref.at[:, members], cr_ref.at[:, members], ci_ref.at[:, members],
            d_ref.at[members], y_ref.at[members], *[s.at[p] for s in scratch]))
    live = []
    while programs or live:
        if programs:
            live.append(programs.pop(0))
        for prog in list(live):
            if next(prog, "done") == "done":
                live.remove(prog)


def _ssm_pair(n_lat, n_ctx, u_ref, uc_ref, par_ref, br_ref, bi_ref, cr_ref, ci_ref, d_ref,
              y_ref, buf_a, buf_b, ends_a, ends_b, wst_ref, wout_ref, cmat_ref, m_scr):
    n_rows = n_lat + n_ctx
    pad = SUBLANES
    coefs = _ssm_prep(_scan_steps(n_rows), par_ref, br_ref, bi_ref, cr_ref, ci_ref,
                      wst_ref, wout_ref, cmat_ref)
    yield

    lane = lax.broadcasted_iota(jnp.int32, (SSM_GROUP, CHUNK_W), 1)
    blk = lane // SSM_GROUP
    on_diag = lane % SSM_GROUP == lax.broadcasted_iota(jnp.int32, (SSM_GROUP, CHUNK_W), 0)
    for e in range(PAIR):
        d_skip = jnp.where(on_diag, jnp.broadcast_to(d_ref[e], (SSM_GROUP, CHUNK_W)), 0.0)
        taps = _dot(wst_ref[e], cmat_ref[e])
        taps_f = taps[:, 0:CHUNK_W]
        taps_b = taps[:, CHUNK_W:]
        for s in range(CHUNK):
            acc = jnp.zeros((SSM_GROUP, CHUNK_W), _F32)
            for t in range(CHUNK):
                val = None
                if s >= t:
                    r0 = (s - t) * SSM_GROUP
                    val = taps_b[r0:r0 + SSM_GROUP]
                if s <= t:
                    r0 = (CHUNK - 1 - t + s) * SSM_GROUP
                    v2 = taps_f[r0:r0 + SSM_GROUP]
                    val = v2 if val is None else val + v2
                if s == t:
                    val = val + d_skip
                acc = jnp.where(blk == t, val, acc)
            m_scr[e, s * SSM_GROUP:(s + 1) * SSM_GROUP, :] = acc.astype(_BF16)
    yield

    def paired(m0, m1):
        left = lax.broadcasted_iota(jnp.int32, (m0.shape[0], PAIR_STATE), 1) < SSM_STATE
        re0, im0 = m0[:, 0:PAIR_STATE], m0[:, PAIR_STATE:]
        re1, im1 = m1[:, 0:PAIR_STATE], m1[:, PAIR_STATE:]
        return [jnp.where(left, re0, re1), jnp.where(left, im0, im1),
                jnp.where(left, re1, re0), jnp.where(left, im1, im0)]

    s_lat = paired(_dot(u_ref[0], wst_ref[0]), _dot(u_ref[1], wst_ref[1]))
    s_ctx = paired(_dot(uc_ref[0], wst_ref[0]), _dot(uc_ref[1], wst_ref[1]))
    yield

    n_tiles = n_rows // SUBLANES
    tile_row = lax.broadcasted_iota(jnp.int32, (1, SUBLANES, PAIR_STATE), 1)

    def tile_scan(x_re, x_im, comp0, backward):
        for k in range(SUBLANES.bit_length() - 1):
            sh = 1 << k
            shift, keep = (SUBLANES - sh, tile_row < SUBLANES - sh) if backward else (sh, tile_row >= sh)
            a_re = jnp.where(keep, coefs[k][comp0], 0.0)
            a_im = jnp.where(keep, coefs[k][comp0 + 1], 0.0)
            p_re, p_im = pltpu.roll(x_re, shift, axis=1), pltpu.roll(x_im, shift, axis=1)
            x_re, x_im = x_re + a_re * p_re - a_im * p_im, x_im + a_re * p_im + a_im * p_re
        return x_re, x_im

    as_tiles = lambda parts: jnp.concatenate(parts, axis=0).reshape(n_tiles, SUBLANES, PAIR_STATE)
    local = list(tile_scan(as_tiles([s_ctx[0], s_lat[0]]), as_tiles([s_ctx[1], s_lat[1]]), 0, False))
    local += tile_scan(as_tiles([s_lat[2], s_ctx[2]]), as_tiles([s_lat[3], s_ctx[3]]), 2, True)
    for comp in range(N_COMP):
        buf_a[comp, pad:pad + n_rows] = local[comp].reshape(n_rows, PAIR_STATE)
    yield

    for buf in (ends_a, ends_b):
        buf[...] = jnp.zeros(buf.shape, _F32)
    epad = _ends_pad(n_tiles)
    for comp in range(N_COMP):
        end_row = pad + (SUBLANES - 1 if comp < 2 else 0)
        ends_a[comp, epad:epad + n_tiles] = buf_a[comp, pl.ds(end_row, n_tiles, stride=SUBLANES), :]
    src, dst = ends_a, ends_b
    for k in range(_scan_steps(n_tiles)):
        sh = 1 << k
        coef = coefs[SUBLANES.bit_length() - 1 + k]
        lo, hi = epad, epad + n_tiles
        xr, xi = src[0, lo - sh:hi - sh], src[1, lo - sh:hi - sh]
        dst[0, lo:hi] = src[0, lo:hi] + coef[0] * xr - coef[1] * xi
        dst[1, lo:hi] = src[1, lo:hi] + coef[0] * xi + coef[1] * xr
        xr, xi = src[2, lo + sh:hi + sh], src[3, lo + sh:hi + sh]
        dst[2, lo:hi] = src[2, lo:hi] + coef[2] * xr - coef[3] * xi
        dst[3, lo:hi] = src[3, lo:hi] + coef[2] * xi + coef[3] * xr
        src, dst = dst, src

    def carry_powers(comp0, backward):
        base = [(coefs[k][comp0], coefs[k][comp0 + 1]) for k in range(SUBLANES.bit_length())]
        pw = []
        for r in range(1, SUBLANES + 1):
            acc = None
            for k, term in enumerate(base):
                if (r >> k) & 1:
                    acc = term if acc is None else _cmul(acc[0], acc[1], term[0], term[1])
            pw.append(acc)
        if backward:
            pw = pw[::-1]
        return (jnp.concatenate([p[0] for p in pw], axis=0), jnp.concatenate([p[1] for p in pw], axis=0))

    for comp0, backward in ((0, False), (2, True)):
        p_re, p_im = carry_powers(comp0, backward)
        first = epad + 1 if backward else epad - 1
        for i in range(n_tiles):
            c_re = src[comp0, first + i:first + i + 1]
            c_im = src[comp0 + 1, first + i:first + i + 1]
            rows = slice(pad + i * SUBLANES, pad + (i + 1) * SUBLANES)
            buf_b[comp0, rows] = buf_a[comp0, rows] + p_re * c_re - p_im * c_im
            buf_b[comp0 + 1, rows] = buf_a[comp0 + 1, rows] + p_re * c_im + p_im * c_re
    yield

    f0 = pad + n_ctx - 1
    b0 = pad + 1
    f_re, f_im = buf_b[0, f0:f0 + n_lat], buf_b[1, f0:f0 + n_lat]
    b_re, b_im = buf_b[2, b0:b0 + n_lat], buf_b[3, b0:b0 + n_lat]
    left = lax.broadcasted_iota(jnp.int32, (n_lat, PAIR_STATE), 1) < SSM_STATE
    h0 = jnp.concatenate([jnp.where(left, f_re, b_re), jnp.where(left, f_im, b_im)], axis=1)
    h1 = jnp.concatenate([jnp.where(left, b_re, f_re), jnp.where(left, b_im, f_im)], axis=1)
    y_ref[0] = _dot(u_ref[0], m_scr[0]) + _dot(h0.astype(_BF16), wout_ref[0])
    y_ref[1] = _dot(u_ref[1], m_scr[1]) + _dot(h1.astype(_BF16), wout_ref[1])


def _ssm(u_g, uc_g, par, b_re, b_im, c_re, c_im, d_col):
    n_groups, n_lat, _ = u_g.shape
    n_ctx = uc_g.shape[1]
    n_rows = n_lat + n_ctx
    n_pairs = n_groups // PAIR
    pps = 2
    gps = pps * PAIR
    buf = pltpu.VMEM((pps, N_COMP, n_rows + 2 * SUBLANES, PAIR_STATE), _F32)
    n_tiles = n_rows // SUBLANES
    ends_rows = _ends_pad(n_tiles) + -(-(n_tiles + _ends_pad(n_tiles)) // SUBLANES) * SUBLANES
    ends = pltpu.VMEM((pps, N_COMP, ends_rows, PAIR_STATE), _F32)
    mat = pltpu.VMEM((pps, PAIR, CHUNK_W, CHUNK_W), _BF16)
    pair3 = lambda q: (q, 0, 0)
    by_dir = lambda q: (0, q, 0, 0)
    return pl.pallas_call(
        functools.partial(_ssm_kernel, n_lat, n_ctx, pps),
        grid=(n_pairs // pps,),
        in_specs=[pl.BlockSpec((gps, n_lat, CHUNK_W), pair3),
                  pl.BlockSpec((gps, n_ctx, CHUNK_W), pair3),
                  pl.BlockSpec((2, pps) + par.shape[2:], by_dir),
                  pl.BlockSpec((2, gps) + b_re.shape[2:], by_dir),
                  pl.BlockSpec((2, gps) + b_im.shape[2:], by_dir),
                  pl.BlockSpec((2, gps) + c_re.shape[2:], by_dir),
                  pl.BlockSpec((2, gps) + c_im.shape[2:], by_dir),
                  pl.BlockSpec((gps,) + d_col.shape[1:], pair3)],
        out_specs=pl.BlockSpec((gps, n_lat, CHUNK_W), pair3),
        out_shape=jax.ShapeDtypeStruct((n_groups, n_lat, CHUNK_W), _F32),
        scratch_shapes=[buf, buf, ends, ends, mat, mat,
                        pltpu.VMEM((pps, PAIR, CHUNK_W, 2 * CHUNK_W), _BF16), mat],
        compiler_params=pltpu.CompilerParams(dimension_semantics=("arbitrary",),
                                             vmem_limit_bytes=VMEM_LIMIT),
        name="ssm",
    )(u_g, uc_g, par, b_re, b_im, c_re, c_im, d_col)


def _out_kernel(x_ref, conv_ref, yg_ref, mods_ref, gssm_ref, g2_ref, gfin_ref,
                wglu32_ref, wout32_ref, w1_ref, w2_ref, o_ref, y_scr, mix_scr, wglu_ref, wout_ref, ones_ref):
    @pl.when(pl.program_id(0) == 0)
    def _():
        wglu_ref[...] = wglu32_ref[...].astype(_BF16)
        wout_ref[...] = wout32_ref[...].astype(_BF16)
        ones_ref[...] = _group_ones(MXU_W, SSM_GROUP)

    gt1 = mods_ref[0:1, 2 * D_MODEL:3 * D_MODEL]
    sh2 = mods_ref[0:1, 3 * D_MODEL:4 * D_MODEL]
    sc2 = mods_ref[0:1, 4 * D_MODEL:5 * D_MODEL]
    gt2 = mods_ref[0:1, 5 * D_MODEL:6 * D_MODEL]
    th = x_ref.shape[0] // 2
    n_split = 4
    ff = D_FF // n_split

    def mixer(r0):
        rows_ = slice(r0, r0 + th)
        chunks = slice(r0 // CHUNK, (r0 + th) // CHUNK)
        for q in range(N_SLABS):
            for half in range(CHUNK // BLOCKS):
                cols = [yg_ref[q * BLOCKS + g8, chunks, half * LANES:(half + 1) * LANES] for g8 in range(BLOCKS)]
                rows = _lane_block_transpose(cols)
                for t in range(BLOCKS):
                    y_scr[q, pl.ds(r0 + half * BLOCKS + t, th // CHUNK, stride=CHUNK), :] = rows[t]
        s = jax.nn.gelu(jnp.concatenate([y_scr[q, rows_] for q in range(N_SLABS)], axis=1))
        s = s * jax.nn.sigmoid(_dot(s.astype(_BF16), wglu_ref[...]))
        sq = (s * s).astype(_BF16)
        ssq = jnp.concatenate([_dot(sq[:, k * MXU_W:(k + 1) * MXU_W], ones_ref[...])
                               for k in range(SSM_W // MXU_W)], axis=1)
        sn = s * lax.rsqrt(ssq * (1.0 / SSM_GROUP) + RMS_EPS) * gssm_ref[...]
        mix = mix_scr[rows_] + _dot(sn.astype(_BF16), wout_ref[CONV_W:])
        x_mid = x_ref[rows_, :] + gt1 * mix
        return x_mid, (_rms_rows(x_mid, g2_ref[...]) * (1.0 + sc2) + sh2).astype(_BF16)

    def mlp_chunk(h2, j):
        a = jnp.maximum(_dot(h2, w1_ref[:, j * ff:(j + 1) * ff]), 0.0)
        return _dot((a * a).astype(_BF16), w2_ref[j * ff:(j + 1) * ff])

    def finish(r0, x_mid, acc):
        o_ref[r0:r0 + th, :] = _rms_rows(x_mid + gt2 * acc, gfin_ref[...])

    mix_scr[...] = _dot(conv_ref[...], wout_ref[0:CONV_W])
    x_a, h_a = mixer(0)
    acc_a = mlp_chunk(h_a, 0)
    x_b, h_b = mixer(th)
    for j in range(1, n_split):
        acc_a = acc_a + mlp_chunk(h_a, j)
    acc_b = mlp_chunk(h_b, 0)
    finish(0, x_a, acc_a)
    for j in range(1, n_split):
        acc_b = acc_b + mlp_chunk(h_b, j)
    finish(th, x_b, acc_b)


def _out_proj(x2d, conv_n, y_g, mods, g_ssm, g2, g_fin, wglu, wout, w1, w2, tm):
    n = x2d.shape[0]
    const = lambda i: (0, 0)
    row = lambda i: (i, 0)
    resident = lambda shape: pl.BlockSpec(shape, const, pipeline_mode=pl.Buffered(1))
    return pl.pallas_call(
        _out_kernel,
        grid=(n // tm,),
        in_specs=[pl.BlockSpec((tm, D_MODEL), row),
                  pl.BlockSpec((tm, CONV_W), row),
                  pl.BlockSpec((SSM_GROUPS, tm // CHUNK, CHUNK_W), lambda i: (0, i, 0)),
                  pl.BlockSpec(mods.shape, const),
                  pl.BlockSpec((1, SSM_W), const),
                  pl.BlockSpec((1, D_MODEL), const),
                  pl.BlockSpec((1, D_MODEL), const),
                  resident((SSM_W, SSM_W)),
                  resident((D_MODEL, D_MODEL)),
                  resident((D_MODEL, D_FF)),
                  resident((D_FF, D_MODEL))],
        out_specs=pl.BlockSpec((tm, D_MODEL), row),
        out_shape=jax.ShapeDtypeStruct((n, D_MODEL), _F32),
        scratch_shapes=[pltpu.VMEM((N_SLABS, tm, LANES), _F32), pltpu.VMEM((tm, D_MODEL), _F32),
                        pltpu.VMEM((SSM_W, SSM_W), _BF16), pltpu.VMEM((D_MODEL, D_MODEL), _BF16),
                        pltpu.VMEM((MXU_W, MXU_W), _BF16)],
        compiler_params=pltpu.CompilerParams(dimension_semantics=("arbitrary",),
                                             vmem_limit_bytes=VMEM_LIMIT),
        name="out_proj",
    )(x2d, conv_n, y_g, mods, g_ssm, g2, g_fin, wglu, wout, w1, w2)


def _ssm_lane_params(a_re, a_im, log_dt):
    n_pairs = SSM_GROUPS // PAIR
    pair_lanes = lambda v: v.reshape(2, n_pairs, PAIR * SSM_STATE)
    dt_lanes = jnp.broadcast_to(log_dt[..., None], (2, SSM_GROUPS, SSM_STATE))
    return jnp.stack([pair_lanes(a_re), pair_lanes(a_im), pair_lanes(dt_lanes)], axis=2)


def kernel(x, c, ctx, c_ctx, w_mod, b_mod, g_norm1, w_in, conv_w, ssm_a_re, ssm_a_im, ssm_log_dt,
           ssm_b_re, ssm_b_im, ssm_c_re, ssm_c_im, ssm_d, w_glu, g_conv_out, g_ssm_out, w_out,
           g_norm2, w_mlp1, w_mlp2, g_final):
    bsz, n_lat, d_model = x.shape
    n_ctx = ctx.shape[1]
    assert bsz == 1 and d_model == D_MODEL and w_mod.shape[0] == 1
    assert n_lat % (CHUNK * SUBLANES) == 0 and n_ctx % (CHUNK * CHUNK) == 0 and n_lat % GRID_W == 0
    layer = 0
    x2d = x[0]
    ctx2d = ctx[0]

    mods = _modulation(c, c_ctx[None, :], w_mod[layer], b_mod[layer][None, :])

    g1 = g_norm1[layer][None, :]
    g_conv = g_conv_out[layer][None, :]
    conv_n, u_g, uc_g, w1_bf, w2_bf = _in_proj(x2d, ctx2d, mods, g1, w_in[layer], conv_w[layer], g_conv, tm=1024,
                                                side_weights=(w_mlp1[layer], w_mlp2[layer]))

    y_g = _ssm(u_g, uc_g, _ssm_lane_params(ssm_a_re[layer], ssm_a_im[layer], ssm_log_dt[layer]),
               ssm_b_re[layer], ssm_b_im[layer], ssm_c_re[layer], ssm_c_im[layer],
               ssm_d.reshape(SSM_GROUPS, SSM_GROUP, 1))

    out = _out_proj(x2d, conv_n, y_g, mods, g_ssm_out[layer][None, :], g_norm2[layer][None, :],
                    g_final[None, :], w_glu[layer], w_out[layer], w1_bf, w2_bf, tm=1024)
    return out[None]
```

```python
import functools
import math

import jax
import jax.numpy as jnp
from jax import lax
from jax.experimental import pallas as pl
from jax.experimental.pallas import tpu as pltpu

D_MODEL = 1024
GRID_W = 64
CONV_W = 512
CONV_HEADS = 8
SSM_W = 512
SSM_GROUP = 16
SSM_GROUPS = 32
SSM_STATE = 64
IN_W = 3 * CONV_W + SSM_W
D_FF = 4 * D_MODEL
N_MOD = 6
RMS_EPS = 1e-6

LANES = 128
MXU_W = 256
SUBLANES = 8
CHUNK = 16
CHUNK_W = CHUNK * SSM_GROUP
PAIR = 2
PAIR_STATE = PAIR * SSM_STATE
N_COMP = 4
BLOCKS = LANES // SSM_GROUP
N_SLABS = SSM_W // LANES
VMEM_LIMIT = 62 * 1024 * 1024

_BF16 = jnp.bfloat16
_F32 = jnp.float32


def _dot(a, b):
    return jnp.dot(a, b, preferred_element_type=_F32)


def _lane_block_transpose(vs):
    blk = lax.broadcasted_iota(jnp.int32, vs[0].shape, 1) // SSM_GROUP
    dist = BLOCKS // 2
    while dist:
        shift = dist * SSM_GROUP
        upper = (blk & dist) != 0
        out = [None] * BLOCKS
        for a in range(BLOCKS):
            if a & dist:
                continue
            lo, hi = vs[a], vs[a + dist]
            out[a] = jnp.where(upper, pltpu.roll(hi, shift, axis=1), lo)
            out[a + dist] = jnp.where(upper, hi, pltpu.roll(lo, LANES - shift, axis=1))
        vs = out
        dist //= 2
    return vs


def _mod_kernel(c_ref, cctx_ref, w_ref, b_ref, o_ref):
    tk = w_ref.shape[0]
    first = lax.broadcasted_iota(jnp.int32, (SUBLANES, tk), 0) == 0
    s = jnp.where(first, jnp.broadcast_to(c_ref[...], first.shape), jnp.broadcast_to(cctx_ref[...], first.shape))
    act = s * jax.nn.sigmoid(s)
    w = w_ref[...]
    a_hi, w_hi = act.astype(_BF16), w.astype(_BF16)
    a_lo = (act - a_hi.astype(_F32)).astype(_BF16)
    w_lo = (w - w_hi.astype(_F32)).astype(_BF16)
    by_hi = _dot(jnp.concatenate([a_hi, a_lo], axis=0), w_hi)
    part = by_hi[0:SUBLANES] + by_hi[SUBLANES:] + _dot(a_hi, w_lo)

    @pl.when(pl.program_id(0) == 0)
    def _():
        o_ref[...] = part + b_ref[...]

    @pl.when(pl.program_id(0) > 0)
    def _():
        o_ref[...] += part


def _modulation(c_row, cctx_row, w_mod, b_mod):
    d_in, n_out = w_mod.shape
    tk = 256
    return pl.pallas_call(
        _mod_kernel,
        grid=(d_in // tk,),
        in_specs=[pl.BlockSpec((1, tk), lambda k: (0, k)),
                  pl.BlockSpec((1, tk), lambda k: (0, k)),
                  pl.BlockSpec((tk, n_out), lambda k: (k, 0)),
                  pl.BlockSpec((1, n_out), lambda k: (0, 0))],
        out_specs=pl.BlockSpec((SUBLANES, n_out), lambda k: (0, 0)),
        out_shape=jax.ShapeDtypeStruct((SUBLANES, n_out), _F32),
        compiler_params=pltpu.CompilerParams(dimension_semantics=("arbitrary",),
                                             vmem_limit_bytes=VMEM_LIMIT),
        name="mod",
    )(c_row, cctx_row, w_mod, b_mod)


def _rms_rows(x, g):
    ms = jnp.mean(x * x, axis=-1, keepdims=True)
    return x * lax.rsqrt(ms + RMS_EPS) * g


def _group_ones(width, group):
    rows = lax.broadcasted_iota(jnp.int32, (width, width), 0) // group
    cols = lax.broadcasted_iota(jnp.int32, (width, width), 1) // group
    return jnp.where(rows == cols, 1.0, 0.0).astype(_BF16)


def _in_kernel(n_side, n_scan_steps, x_ref, ctx_ref, mods_ref, g1_ref, w_in_ref, convw_ref, gconv_ref,
               par_ref, br_ref, bi_ref, cr_ref, ci_ref, d_ref, *refs):
    side_in = refs[:n_side]
    conv_ref, ug_ref, ucg_ref, wst_ref, wout_ref, m_ref, coef_ref = refs[n_side:n_side + 7]
    side_out = refs[n_side + 7:2 * n_side + 7]
    u_scr, w_scr, ones_scr, cmat_scr = refs[2 * n_side + 7:]
    step = pl.program_id(0)
    last = pl.num_programs(0) - 1

    @pl.when(step == 0)
    def _():
        w_scr[:, 0:SSM_W] = w_in_ref[:, 3 * CONV_W:].astype(_BF16)
        for k in range(CONV_W // MXU_W):
            for part in range(3):
                dst = SSM_W + (3 * k + part) * MXU_W
                src = part * CONV_W + k * MXU_W
                w_scr[:, dst:dst + MXU_W] = w_in_ref[:, src:src + MXU_W].astype(_BF16)
        ones_scr[...] = _group_ones(MXU_W, CONV_W // CONV_HEADS)

    def normed(rows_ref, mod_row):
        sh1 = mods_ref[mod_row:mod_row + 1, 0:D_MODEL]
        sc1 = mods_ref[mod_row:mod_row + 1, D_MODEL:2 * D_MODEL]
        return (_rms_rows(rows_ref[...], g1_ref[...]) * (1.0 + sc1) + sh1).astype(_BF16)

    def to_group_major(z_u, out_ref, r0):
        n_rows = z_u.shape[0]
        chunks = slice(r0 // CHUNK, (r0 + n_rows) // CHUNK)
        for q in range(N_SLABS):
            u_scr[q, r0:r0 + n_rows] = z_u[:, q * LANES:(q + 1) * LANES]
        for q in range(N_SLABS):
            for half in range(CHUNK // BLOCKS):
                rows = [u_scr[q, pl.ds(r0 + half * BLOCKS + t, n_rows // CHUNK, stride=CHUNK), :]
                        for t in range(BLOCKS)]
                cols = _lane_block_transpose(rows)
                for g8 in range(BLOCKS):
                    out_ref[q * BLOCKS + g8, chunks, half * LANES:(half + 1) * LANES] = cols[g8].astype(_BF16)

    def conv_slab(z_slab, k, r0):
        n_rows = z_slab.shape[0]
        pos = lax.broadcasted_iota(jnp.int32, (n_rows, MXU_W), 0) % GRID_W
        lanes = slice(k * MXU_W, (k + 1) * MXU_W)
        b = z_slab[:, 0:MXU_W]
        cv = z_slab[:, MXU_W:2 * MXU_W] * z_slab[:, 2 * MXU_W:]
        prev = jnp.where(pos == 0, 0.0, pltpu.roll(cv, 1, axis=0))
        nxt = jnp.where(pos == GRID_W - 1, 0.0, pltpu.roll(cv, n_rows - 1, axis=0))
        y = b * (convw_ref[0:1, lanes] * prev + convw_ref[1:2, lanes] * cv + convw_ref[2:3, lanes] * nxt)
        ssq = _dot((y * y).astype(_BF16), ones_scr[...])
        yn = y * lax.rsqrt(ssq * (1.0 / (CONV_W // CONV_HEADS)) + RMS_EPS) * gconv_ref[:, lanes]
        conv_ref[r0:r0 + n_rows, lanes] = yn.astype(_BF16)

    @pl.when(step < last)
    def _():
        for src_ref, dst_ref in zip(side_in, side_out):
            dst_ref[...] = src_ref[...].astype(_BF16)
        th = x_ref.shape[0] // 2
        mats = _ssm_pair_matrices(n_scan_steps, par_ref, br_ref, bi_ref, cr_ref, ci_ref, d_ref,
                                  wst_ref, wout_ref, m_ref, coef_ref, cmat_scr)
        z = []
        for r0 in (0, th):
            z.append(_dot(normed(x_ref.at[r0:r0 + th], 0), w_scr[...]))
            next(mats)
        for half, r0 in enumerate((0, th)):
            next(mats, None)
            to_group_major(z[half][:, 0:SSM_W], ug_ref, r0)
            for k in range(CONV_W // MXU_W):
                c0 = SSM_W + 3 * k * MXU_W
                conv_slab(z[half][:, c0:c0 + 3 * MXU_W], k, r0)

    @pl.when(step == last)
    def _():
        to_group_major(_dot(normed(ctx_ref, 1), w_scr[:, 0:SSM_W]), ucg_ref, 0)


def _in_proj(x2d, ctx2d, mods, g1, w_in, conv_w, g_conv, ssm_params, tm, side_weights):
    n, n_ctx = x2d.shape[0], ctx2d.shape[0]
    steps = n // tm
    n_pairs = SSM_GROUPS // PAIR
    assert steps == n_pairs
    scan_steps = _scan_steps((n + n_ctx) // CHUNK)
    coef_rows = -(-scan_steps // SUBLANES) * SUBLANES
    par, b_re, b_im, c_re, c_im, d_col = ssm_params
    const = lambda i: (0, 0)
    row = lambda i: (jnp.minimum(i, steps - 1), 0)
    pair3 = lambda i: (jnp.minimum(i, steps - 1), 0, 0)
    by_dir = lambda i: (0, jnp.minimum(i, steps - 1), 0, 0)
    side_specs = [pl.BlockSpec((w.shape[0] // steps, w.shape[1]), row) for w in side_weights]
    mat_spec = pl.BlockSpec((PAIR, CHUNK_W, CHUNK_W), pair3)
    mat_shape = jax.ShapeDtypeStruct((SSM_GROUPS, CHUNK_W, CHUNK_W), _BF16)
    return pl.pallas_call(
        functools.partial(_in_kernel, len(side_weights), scan_steps),
        grid=(steps + 1,),
        in_specs=[pl.BlockSpec((tm, D_MODEL), row),
                  pl.BlockSpec((n_ctx, D_MODEL), const),
                  pl.BlockSpec(mods.shape, const),
                  pl.BlockSpec((1, D_MODEL), const),
                  pl.BlockSpec((D_MODEL, IN_W), const, pipeline_mode=pl.Buffered(1)),
                  pl.BlockSpec((3, CONV_W), const),
                  pl.BlockSpec((1, CONV_W), const),
                  pl.BlockSpec((2, 1) + par.shape[2:], by_dir),
                  pl.BlockSpec((2, PAIR) + b_re.shape[2:], by_dir),
                  pl.BlockSpec((2, PAIR) + b_im.shape[2:], by_dir),
                  pl.BlockSpec((2, PAIR) + c_re.shape[2:], by_dir),
                  pl.BlockSpec((2, PAIR) + c_im.shape[2:], by_dir),
                  pl.BlockSpec((PAIR,) + d_col.shape[1:], pair3)] + side_specs,
        out_specs=[pl.BlockSpec((tm, CONV_W), row),
                   pl.BlockSpec((SSM_GROUPS, tm // CHUNK, CHUNK_W), lambda i: (0, jnp.minimum(i, steps - 1), 0)),
                   pl.BlockSpec((SSM_GROUPS, n_ctx // CHUNK, CHUNK_W), lambda i: (0, 0, 0)),
                   mat_spec, mat_spec, mat_spec,
                   pl.BlockSpec((1, coef_rows, N_COMP * PAIR_STATE), pair3)] + side_specs,
        out_shape=[jax.ShapeDtypeStruct((n, CONV_W), _BF16),
                   jax.ShapeDtypeStruct((SSM_GROUPS, n // CHUNK, CHUNK_W), _BF16),
                   jax.ShapeDtypeStruct((SSM_GROUPS, n_ctx // CHUNK, CHUNK_W), _BF16),
                   mat_shape, mat_shape, mat_shape,
                   jax.ShapeDtypeStruct((n_pairs, coef_rows, N_COMP * PAIR_STATE), _F32)]
                  + [jax.ShapeDtypeStruct(w.shape, _BF16) for w in side_weights],
        scratch_shapes=[pltpu.VMEM((N_SLABS, tm, LANES), _F32),
                        pltpu.VMEM((D_MODEL, IN_W), _BF16),
                        pltpu.VMEM((MXU_W, MXU_W), _BF16),
                        pltpu.VMEM((PAIR, CHUNK_W, 2 * CHUNK_W), _BF16)],
        compiler_params=pltpu.CompilerParams(dimension_semantics=("arbitrary",),
                                             vmem_limit_bytes=VMEM_LIMIT),
        name="in_proj",
    )(x2d, ctx2d, mods, g1, w_in, conv_w, g_conv, par, b_re, b_im, c_re, c_im, d_col, *side_weights)


def _scan_steps(n_rows):
    return max(1, math.ceil(math.log2(n_rows)))


def _ends_pad(n_tiles):
    return max(SUBLANES, 1 << (_scan_steps(n_tiles) - 1))


def _cmul(ar, ai, br, bi):
    return ar * br - ai * bi, ar * bi + ai * br


def _eye(n):
    rows, cols = (lax.broadcasted_iota(jnp.int32, (n, n), axis) for axis in (0, 1))
    return jnp.where(rows == cols, 1.0, 0.0).astype(_BF16)


def _dot_nt_exact(eye, x, repeat=1):
    hi = x.astype(_BF16)
    rest = x - hi.astype(_F32)
    mid = rest.astype(_BF16)
    lo = (rest - mid.astype(_F32)).astype(_BF16)
    nt = lambda term: lax.dot_general(eye, jnp.concatenate([term] * repeat, axis=0),
                                      (((1,), (1,)), ((), ())), preferred_element_type=_F32)
    return nt(hi) + nt(mid) + nt(lo)


def _ssm_prep(n_steps, par_ref, br_ref, bi_ref, cr_ref, ci_ref, wst_ref, wout_ref, cmat_ref):
    T = CHUNK
    left8 = lax.broadcasted_iota(jnp.int32, (SUBLANES, LANES), 1) < SSM_STATE
    row0 = lax.broadcasted_iota(jnp.int32, (SUBLANES, LANES), 0) == 0

    def member_rows(vf, vb):
        f8 = jnp.broadcast_to(vf, (SUBLANES, LANES))
        b8 = pltpu.roll(jnp.broadcast_to(vb, (SUBLANES, LANES)), SSM_STATE, axis=1)
        return jnp.where(row0, jnp.where(left8, f8, b8), jnp.where(left8, b8, f8))

    eye_i = _eye(SSM_GROUP)
    eye_p = _eye(SSM_STATE)
    slot_dirs = [(0, 1), (1, 0)]
    b_t, c_t = [], []
    for e, (d0, d1) in enumerate(slot_dirs):
        b_t.append([_dot_nt_exact(eye_i, jnp.concatenate([ref[d0, e], ref[d1, e]], axis=0))
                    for ref in (br_ref, bi_ref)])
        c_t.append([jnp.concatenate([_dot_nt_exact(eye_p, ref[d0, e], repeat=T),
                                     _dot_nt_exact(eye_p, ref[d1, e], repeat=T)], axis=0)
                    for ref in (cr_ref, ci_ref)])
    yield

    pf, pb = par_ref[0, 0], par_ref[1, 0]
    lr = jnp.minimum(member_rows(pf[0:1], pb[0:1]), -1e-4)
    li = member_rows(pf[1:2], pb[1:2])
    dt = jnp.exp(member_rows(pf[2:3], pb[2:3]))
    mag = jnp.exp(lr * dt)
    ab_re = mag * jnp.cos(li * dt)
    ab_im = mag * jnp.sin(li * dt)
    nr = ab_re - 1.0
    den = lr * lr + li * li
    f_re = (nr * lr + ab_im * li) / den
    f_im = (ab_im * lr - nr * li) / den
    pt_re, pt_im = [jnp.ones_like(ab_re)], [jnp.zeros_like(ab_im)]
    for _ in range(T):
        nxt = _cmul(pt_re[-1], pt_im[-1], ab_re, ab_im)
        pt_re.append(nxt[0])
        pt_im.append(nxt[1])

    cols = jnp.concatenate([ab_re, ab_im, jnp.zeros((LANES - 2 * SUBLANES, LANES), _F32)], axis=0).T
    left1 = lax.broadcasted_iota(jnp.int32, (1, LANES), 1) < SSM_STATE
    rows = lax.broadcasted_iota(jnp.int32, (LANES, CHUNK_W), 0)
    tcol = lax.broadcasted_iota(jnp.int32, (LANES, CHUNK_W), 1) // SSM_GROUP
    zero_half = jnp.zeros((SSM_STATE, CHUNK_W), _BF16)
    for e, (d0, d1) in enumerate(slot_dirs):
        fwd_first = e == 0
        b_re, b_im = b_t[e]
        bbar_re, bbar_im = _cmul(f_re[e:e + 1], f_im[e:e + 1], b_re, b_im)
        fwd_lane = left1 if fwd_first else jnp.logical_not(left1)
        for s in range(T):
            pr = jnp.where(fwd_lane, pt_re[T - 1 - s][e:e + 1], pt_re[s][e:e + 1])
            pi = jnp.where(fwd_lane, pt_im[T - 1 - s][e:e + 1], pt_im[s][e:e + 1])
            w_re, w_im = _cmul(pr, pi, bbar_re, bbar_im)
            wst_ref[e, s * SSM_GROUP:(s + 1) * SSM_GROUP, 0:LANES] = w_re.astype(_BF16)
            wst_ref[e, s * SSM_GROUP:(s + 1) * SSM_GROUP, LANES:] = w_im.astype(_BF16)
        fwd_row = (rows < SSM_STATE) if fwd_first else (rows >= SSM_STATE)
        expo = jnp.where(fwd_row, tcol + 1, T - tcol)
        q_re = jnp.ones((LANES, CHUNK_W), _F32)
        q_im = jnp.zeros((LANES, CHUNK_W), _F32)
        s_re, s_im = cols[:, e:e + 1], cols[:, SUBLANES + e:SUBLANES + e + 1]
        for bit in range(T.bit_length()):
            m_re, m_im = _cmul(q_re, q_im, jnp.broadcast_to(s_re, q_re.shape), jnp.broadcast_to(s_im, q_re.shape))
            take = ((expo >> bit) & 1) == 1
            q_re, q_im = jnp.where(take, m_re, q_re), jnp.where(take, m_im, q_im)
            s_re, s_im = _cmul(s_re, s_im, s_re, s_im)
        c_re, c_im = c_t[e]
        o_re, o_im = _cmul(c_re, c_im, q_re, q_im)
        wout_ref[e, 0:LANES] = o_re.astype(_BF16)
        wout_ref[e, LANES:] = (-o_im).astype(_BF16)
        for ri, src in enumerate((c_re, -c_im)):
            for slot, d in enumerate((d0, d1)):
                r0 = ri * LANES + slot * SSM_STATE
                block = src[slot * SSM_STATE:(slot + 1) * SSM_STATE].astype(_BF16)
                cmat_ref[e, r0:r0 + SSM_STATE, d * CHUNK_W:(d + 1) * CHUNK_W] = block
                cmat_ref[e, r0:r0 + SSM_STATE, (1 - d) * CHUNK_W:(2 - d) * CHUNK_W] = zero_half

    m0_re, m1_re = pt_re[T][0:1], pt_re[T][1:2]
    m0_im, m1_im = pt_im[T][0:1], pt_im[T][1:2]
    cf = (jnp.where(left1, m0_re, m1_re), jnp.where(left1, m0_im, m1_im))
    cb = (jnp.where(left1, m1_re, m0_re), jnp.where(left1, m1_im, m0_im))
    coefs = []
    for _ in range(n_steps):
        coefs.append([cf[0], cf[1], cb[0], cb[1]])
        cf = _cmul(cf[0], cf[1], cf[0], cf[1])
        cb = _cmul(cb[0], cb[1], cb[0], cb[1])
    return coefs


def _ssm_pair_matrices(n_steps, par_ref, br_ref, bi_ref, cr_ref, ci_ref, d_ref,
                       wst_ref, wout_ref, m_ref, coef_ref, cmat_ref):
    coefs = yield from _ssm_prep(n_steps, par_ref, br_ref, bi_ref, cr_ref, ci_ref, wst_ref, wout_ref, cmat_ref)
    for k, row in enumerate(coefs):
        for comp in range(N_COMP):
            coef_ref[0, k:k + 1, comp * PAIR_STATE:(comp + 1) * PAIR_STATE] = row[comp]
    unused = coef_ref.shape[1] - len(coefs)
    if unused:
        coef_ref[0, len(coefs):, :] = jnp.zeros((unused, coef_ref.shape[2]), _F32)
    yield

    all_taps = [_dot(wst_ref[e], cmat_ref[e]) for e in range(PAIR)]
    yield

    lane = lax.broadcasted_iota(jnp.int32, (SSM_GROUP, CHUNK_W), 1)
    blk = lane // SSM_GROUP
    on_diag = lane % SSM_GROUP == lax.broadcasted_iota(jnp.int32, (SSM_GROUP, CHUNK_W), 0)
    for e in range(PAIR):
        d_skip = jnp.where(on_diag, jnp.broadcast_to(d_ref[e], (SSM_GROUP, CHUNK_W)), 0.0)
        taps_f = all_taps[e][:, 0:CHUNK_W]
        taps_b = all_taps[e][:, CHUNK_W:]
        for s in range(CHUNK):
            acc = jnp.zeros((SSM_GROUP, CHUNK_W), _F32)
            for t in range(CHUNK):
                val = None
                if s >= t:
                    r0 = (s - t) * SSM_GROUP
                    val = taps_b[r0:r0 + SSM_GROUP]
                if s <= t:
                    r0 = (CHUNK - 1 - t + s) * SSM_GROUP
                    v2 = taps_f[r0:r0 + SSM_GROUP]
                    val = v2 if val is None else val + v2
                if s == t:
                    val = val + d_skip
                acc = jnp.where(blk == t, val, acc)
            m_ref[e, s * SSM_GROUP:(s + 1) * SSM_GROUP, :] = acc.astype(_BF16)


def _ssm_kernel(n_lat, n_ctx, pairs_per_step, u_ref, uc_ref, wst_ref, wout_ref, m_ref, coef_ref,
                y_ref, *scratch):
    programs = []
    for p in range(pairs_per_step):
        members = slice(p * PAIR, (p + 1) * PAIR)
        programs.append(_ssm_pair(
            n_lat, n_ctx, u_ref.at[members], uc_ref.at[members], wst_ref.at[members], wout_ref.at[members],
            m_ref.at[members], coef_ref.at[p], y_ref.at[members], *[s.at[p] for s in scratch]))
    live = []
    while programs or live:
        if programs:
            live.append(programs.pop(0))
        for prog in list(live):
            if next(prog, "done") == "done":
                live.remove(prog)


def _ssm_pair(n_lat, n_ctx, u_ref, uc_ref, wst_ref, wout_ref, m_scr, coef_ref,
              y_ref, buf_a, buf_b, ends_a, ends_b):
    n_rows = n_lat + n_ctx
    pad = SUBLANES
    coefs = [[coef_ref[k:k + 1, comp * PAIR_STATE:(comp + 1) * PAIR_STATE] for comp in range(N_COMP)]
             for k in range(_scan_steps(n_rows))]

    def paired(m0, m1):
        left = lax.broadcasted_iota(jnp.int32, (m0.shape[0], PAIR_STATE), 1) < SSM_STATE
        re0, im0 = m0[:, 0:PAIR_STATE], m0[:, PAIR_STATE:]
        re1, im1 = m1[:, 0:PAIR_STATE], m1[:, PAIR_STATE:]
        return [jnp.where(left, re0, re1), jnp.where(left, im0, im1),
                jnp.where(left, re1, re0), jnp.where(left, im1, im0)]

    s_lat = paired(_dot(u_ref[0], wst_ref[0]), _dot(u_ref[1], wst_ref[1]))
    s_ctx = paired(_dot(uc_ref[0], wst_ref[0]), _dot(uc_ref[1], wst_ref[1]))
    yield

    n_tiles = n_rows // SUBLANES
    tile_row = lax.broadcasted_iota(jnp.int32, (1, SUBLANES, PAIR_STATE), 1)

    def tile_scan(x_re, x_im, comp0, backward):
        for k in range(SUBLANES.bit_length() - 1):
            sh = 1 << k
            shift, keep = (SUBLANES - sh, tile_row < SUBLANES - sh) if backward else (sh, tile_row >= sh)
            a_re = jnp.where(keep, coefs[k][comp0], 0.0)
            a_im = jnp.where(keep, coefs[k][comp0 + 1], 0.0)
            p_re, p_im = pltpu.roll(x_re, shift, axis=1), pltpu.roll(x_im, shift, axis=1)
            x_re, x_im = x_re + a_re * p_re - a_im * p_im, x_im + a_re * p_im + a_im * p_re
        return x_re, x_im

    as_tiles = lambda parts: jnp.concatenate(parts, axis=0).reshape(n_tiles, SUBLANES, PAIR_STATE)
    local = list(tile_scan(as_tiles([s_ctx[0], s_lat[0]]), as_tiles([s_ctx[1], s_lat[1]]), 0, False))
    local += tile_scan(as_tiles([s_lat[2], s_ctx[2]]), as_tiles([s_lat[3], s_ctx[3]]), 2, True)
    for comp in range(N_COMP):
        buf_a[comp, pad:pad + n_rows] = local[comp].reshape(n_rows, PAIR_STATE)
    yield

    for buf in (ends_a, ends_b):
        buf[...] = jnp.zeros(buf.shape, _F32)
    epad = _ends_pad(n_tiles)
    for comp in range(N_COMP):
        end_row = pad + (SUBLANES - 1 if comp < 2 else 0)
        ends_a[comp, epad:epad + n_tiles] = buf_a[comp, pl.ds(end_row, n_tiles, stride=SUBLANES), :]
    src, dst = ends_a, ends_b
    for k in range(_scan_steps(n_tiles)):
        sh = 1 << k
        coef = coefs[SUBLANES.bit_length() - 1 + k]
        lo, hi = epad, epad + n_tiles
        xr, xi = src[0, lo - sh:hi - sh], src[1, lo - sh:hi - sh]
        dst[0, lo:hi] = src[0, lo:hi] + coef[0] * xr - coef[1] * xi
        dst[1, lo:hi] = src[1, lo:hi] + coef[0] * xi + coef[1] * xr
        xr, xi = src[2, lo + sh:hi + sh], src[3, lo + sh:hi + sh]
        dst[2, lo:hi] = src[2, lo:hi] + coef[2] * xr - coef[3] * xi
        dst[3, lo:hi] = src[3, lo:hi] + coef[2] * xi + coef[3] * xr
        src, dst = dst, src

    def carry_powers(comp0, backward):
        base = [(coefs[k][comp0], coefs[k][comp0 + 1]) for k in range(SUBLANES.bit_length())]
        pw = []
        for r in range(1, SUBLANES + 1):
            acc = None
            for k, term in enumerate(base):
                if (r >> k) & 1:
                    acc = term if acc is None else _cmul(acc[0], acc[1], term[0], term[1])
            pw.append(acc)
        if backward:
            pw = pw[::-1]
        return (jnp.concatenate([p[0] for p in pw], axis=0), jnp.concatenate([p[1] for p in pw], axis=0))

    for comp0, backward in ((0, False), (2, True)):
        p_re, p_im = carry_powers(comp0, backward)
        first = epad + 1 if backward else epad - 1
        for i in range(n_tiles):
            c_re = src[comp0, first + i:first + i + 1]
            c_im = src[comp0 + 1, first + i:first + i + 1]
            rows = slice(pad + i * SUBLANES, pad + (i + 1) * SUBLANES)
            buf_b[comp0, rows] = buf_a[comp0, rows] + p_re * c_re - p_im * c_im
            buf_b[comp0 + 1, rows] = buf_a[comp0 + 1, rows] + p_re * c_im + p_im * c_re
    yield

    f0 = pad + n_ctx - 1
    b0 = pad + 1
    f_re, f_im = buf_b[0, f0:f0 + n_lat], buf_b[1, f0:f0 + n_lat]
    b_re, b_im = buf_b[2, b0:b0 + n_lat], buf_b[3, b0:b0 + n_lat]
    left = lax.broadcasted_iota(jnp.int32, (n_lat, PAIR_STATE), 1) < SSM_STATE
    h0 = jnp.concatenate([jnp.where(left, f_re, b_re), jnp.where(left, f_im, b_im)], axis=1)
    h1 = jnp.concatenate([jnp.where(left, b_re, f_re), jnp.where(left, b_im, f_im)], axis=1)
    y_ref[0] = _dot(u_ref[0], m_scr[0]) + _dot(h0.astype(_BF16), wout_ref[0])
    y_ref[1] = _dot(u_ref[1], m_scr[1]) + _dot(h1.astype(_BF16), wout_ref[1])


def _ssm(u_g, uc_g, wst, wout, m_toep, coef):
    n_groups, n_lat, _ = u_g.shape
    n_ctx = uc_g.shape[1]
    n_rows = n_lat + n_ctx
    n_pairs = n_groups // PAIR
    pps = 2
    gps = pps * PAIR
    buf = pltpu.VMEM((pps, N_COMP, n_rows + 2 * SUBLANES, PAIR_STATE), _F32)
    n_tiles = n_rows // SUBLANES
    ends_rows = _ends_pad(n_tiles) + -(-(n_tiles + _ends_pad(n_tiles)) // SUBLANES) * SUBLANES
    ends = pltpu.VMEM((pps, N_COMP, ends_rows, PAIR_STATE), _F32)
    pair3 = lambda q: (q, 0, 0)
    mat_spec = pl.BlockSpec((gps, CHUNK_W, CHUNK_W), pair3)
    return pl.pallas_call(
        functools.partial(_ssm_kernel, n_lat, n_ctx, pps),
        grid=(n_pairs // pps,),
        in_specs=[pl.BlockSpec((gps, n_lat, CHUNK_W), pair3),
                  pl.BlockSpec((gps, n_ctx, CHUNK_W), pair3),
                  mat_spec, mat_spec, mat_spec,
                  pl.BlockSpec((pps,) + coef.shape[1:], pair3)],
        out_specs=pl.BlockSpec((gps, n_lat, CHUNK_W), pair3),
        out_shape=jax.ShapeDtypeStruct((n_groups, n_lat, CHUNK_W), _F32),
        scratch_shapes=[buf, buf, ends, ends],
        compiler_params=pltpu.CompilerParams(dimension_semantics=("arbitrary",),
                                             vmem_limit_bytes=VMEM_LIMIT),
        name="ssm",
    )(u_g, uc_g, wst, wout, m_toep, coef)


def _out_kernel(x_ref, conv_ref, yg_ref, mods_ref, gssm_ref, g2_ref, gfin_ref,
                wglu32_ref, wout32_ref, w1_ref, w2_ref, o_ref, y_scr, mix_scr, wglu_ref, wout_ref, ones_ref):
    @pl.when(pl.program_id(0) == 0)
    def _():
        wglu_ref[...] = wglu32_ref[...].astype(_BF16)
        wout_ref[...] = wout32_ref[...].astype(_BF16)
        ones_ref[...] = _group_ones(MXU_W, SSM_GROUP)

    gt1 = mods_ref[0:1, 2 * D_MODEL:3 * D_MODEL]
    sh2 = mods_ref[0:1, 3 * D_MODEL:4 * D_MODEL]
    sc2 = mods_ref[0:1, 4 * D_MODEL:5 * D_MODEL]
    gt2 = mods_ref[0:1, 5 * D_MODEL:6 * D_MODEL]
    th = x_ref.shape[0] // 2
    n_split = 4
    ff = D_FF // n_split

    def mixer(r0):
        rows_ = slice(r0, r0 + th)
        chunks = slice(r0 // CHUNK, (r0 + th) // CHUNK)
        for q in range(N_SLABS):
            for half in range(CHUNK // BLOCKS):
                cols = [yg_ref[q * BLOCKS + g8, chunks, half * LANES:(half + 1) * LANES] for g8 in range(BLOCKS)]
                rows = _lane_block_transpose(cols)
                for t in range(BLOCKS):
                    y_scr[q, pl.ds(r0 + half * BLOCKS + t, th // CHUNK, stride=CHUNK), :] = rows[t]
        s = jax.nn.gelu(jnp.concatenate([y_scr[q, rows_] for q in range(N_SLABS)], axis=1))
        s = s * jax.nn.sigmoid(_dot(s.astype(_BF16), wglu_ref[...]))
        sq = (s * s).astype(_BF16)
        ssq = jnp.concatenate([_dot(sq[:, k * MXU_W:(k + 1) * MXU_W], ones_ref[...])
                               for k in range(SSM_W // MXU_W)], axis=1)
        sn = s * lax.rsqrt(ssq * (1.0 / SSM_GROUP) + RMS_EPS) * gssm_ref[...]
        mix = mix_scr[rows_] + _dot(sn.astype(_BF16), wout_ref[CONV_W:])
        x_mid = x_ref[rows_, :] + gt1 * mix
        return x_mid, (_rms_rows(x_mid, g2_ref[...]) * (1.0 + sc2) + sh2).astype(_BF16)

    def mlp_chunk(h2, j):
        a = jnp.maximum(_dot(h2, w1_ref[:, j * ff:(j + 1) * ff]), 0.0)
        return _dot((a * a).astype(_BF16), w2_ref[j * ff:(j + 1) * ff])

    def finish(r0, x_mid, acc):
        o_ref[r0:r0 + th, :] = _rms_rows(x_mid + gt2 * acc, gfin_ref[...])

    mix_scr[...] = _dot(conv_ref[...], wout_ref[0:CONV_W])
    x_a, h_a = mixer(0)
    acc_a = mlp_chunk(h_a, 0)
    x_b, h_b = mixer(th)
    for j in range(1, n_split):
        acc_a = acc_a + mlp_chunk(h_a, j)
    acc_b = mlp_chunk(h_b, 0)
    finish(0, x_a, acc_a)
    for j in range(1, n_split):
        acc_b = acc_b + mlp_chunk(h_b, j)
    finish(th, x_b, acc_b)


def _out_proj(x2d, conv_n, y_g, mods, g_ssm, g2, g_fin, wglu, wout, w1, w2, tm):
    n = x2d.shape[0]
    const = lambda i: (0, 0)
    row = lambda i: (i, 0)
    resident = lambda shape: pl.BlockSpec(shape, const, pipeline_mode=pl.Buffered(1))
    return pl.pallas_call(
        _out_kernel,
        grid=(n // tm,),
        in_specs=[pl.BlockSpec((tm, D_MODEL), row),
                  pl.BlockSpec((tm, CONV_W), row),
                  pl.BlockSpec((SSM_GROUPS, tm // CHUNK, CHUNK_W), lambda i: (0, i, 0)),
                  pl.BlockSpec(mods.shape, const),
                  pl.BlockSpec((1, SSM_W), const),
                  pl.BlockSpec((1, D_MODEL), const),
                  pl.BlockSpec((1, D_MODEL), const),
                  resident((SSM_W, SSM_W)),
                  resident((D_MODEL, D_MODEL)),
                  resident((D_MODEL, D_FF)),
                  resident((D_FF, D_MODEL))],
        out_specs=pl.BlockSpec((tm, D_MODEL), row),
        out_shape=jax.ShapeDtypeStruct((n, D_MODEL), _F32),
        scratch_shapes=[pltpu.VMEM((N_SLABS, tm, LANES), _F32), pltpu.VMEM((tm, D_MODEL), _F32),
                        pltpu.VMEM((SSM_W, SSM_W), _BF16), pltpu.VMEM((D_MODEL, D_MODEL), _BF16),
                        pltpu.VMEM((MXU_W, MXU_W), _BF16)],
        compiler_params=pltpu.CompilerParams(dimension_semantics=("arbitrary",),
                                             vmem_limit_bytes=VMEM_LIMIT),
        name="out_proj",
    )(x2d, conv_n, y_g, mods, g_ssm, g2, g_fin, wglu, wout, w1, w2)


def _ssm_lane_params(a_re, a_im, log_dt):
    n_pairs = SSM_GROUPS // PAIR
    pair_lanes = lambda v: v.reshape(2, n_pairs, PAIR * SSM_STATE)
    dt_lanes = jnp.broadcast_to(log_dt[..., None], (2, SSM_GROUPS, SSM_STATE))
    return jnp.stack([pair_lanes(a_re), pair_lanes(a_im), pair_lanes(dt_lanes)], axis=2)


def kernel(x, c, ctx, c_ctx, w_mod, b_mod, g_norm1, w_in, conv_w, ssm_a_re, ssm_a_im, ssm_log_dt,
           ssm_b_re, ssm_b_im, ssm_c_re, ssm_c_im, ssm_d, w_glu, g_conv_out, g_ssm_out, w_out,
           g_norm2, w_mlp1, w_mlp2, g_final):
    bsz, n_lat, d_model = x.shape
    n_ctx = ctx.shape[1]
    assert bsz == 1 and d_model == D_MODEL and w_mod.shape[0] == 1
    assert n_lat % (CHUNK * SUBLANES) == 0 and n_ctx % (CHUNK * CHUNK) == 0 and n_lat % GRID_W == 0
    layer = 0
    x2d = x[0]
    ctx2d = ctx[0]

    mods = _modulation(c, c_ctx[None, :], w_mod[layer], b_mod[layer][None, :])

    g1 = g_norm1[layer][None, :]
    g_conv = g_conv_out[layer][None, :]
    ssm_params = (_ssm_lane_params(ssm_a_re[layer], ssm_a_im[layer], ssm_log_dt[layer]),
                  ssm_b_re[layer], ssm_b_im[layer], ssm_c_re[layer], ssm_c_im[layer],
                  ssm_d.reshape(SSM_GROUPS, SSM_GROUP, 1))
    conv_n, u_g, uc_g, wst, wout_s, m_toep, coef, w1_bf, w2_bf = _in_proj(
        x2d, ctx2d, mods, g1, w_in[layer], conv_w[layer], g_conv, ssm_params, tm=1024,
        side_weights=(w_mlp1[layer], w_mlp2[layer]))

    y_g = _ssm(u_g, uc_g, wst, wout_s, m_toep, coef)

    out = _out_proj(x2d, conv_n, y_g, mods, g_ssm_out[layer][None, :], g_norm2[layer][None, :],
                    g_final[None, :], w_glu[layer], w_out[layer], w1_bf, w2_bf, tm=1024)
    return out[None]
```

```python
import functools
import math

import jax
import jax.numpy as jnp
from jax import lax
from jax.experimental import pallas as pl
from jax.experimental.pallas import tpu as pltpu

D_MODEL = 1024
GRID_W = 64
CONV_W = 512
CONV_HEADS = 8
SSM_W = 512
SSM_GROUP = 16
SSM_GROUPS = 32
SSM_STATE = 64
IN_W = 3 * CONV_W + SSM_W
D_FF = 4 * D_MODEL
RMS_EPS = 1e-6

LANES = 128
MXU_W = 256
SUBLANES = 8
CHUNK = 16
CHUNK_W = CHUNK * SSM_GROUP
PAIR = 2
PAIR_STATE = PAIR * SSM_STATE
N_COMP = 4
BLOCKS = LANES // SSM_GROUP
N_SLABS = SSM_W // LANES
VMEM_LIMIT = 62 * 1024 * 1024

_BF16 = jnp.bfloat16
_F32 = jnp.float32


def _dot(a, b):
    return jnp.dot(a, b, preferred_element_type=_F32)


def _lane_block_transpose(vs):
    blk = lax.broadcasted_iota(jnp.int32, vs[0].shape, 1) // SSM_GROUP
    dist = BLOCKS // 2
    while dist:
        shift = dist * SSM_GROUP
        upper = (blk & dist) != 0
        out = [None] * BLOCKS
        for a in range(BLOCKS):
            if a & dist:
                continue
            lo, hi = vs[a], vs[a + dist]
            out[a] = jnp.where(upper, pltpu.roll(hi, shift, axis=1), lo)
            out[a + dist] = jnp.where(upper, hi, pltpu.roll(lo, LANES - shift, axis=1))
        vs = out
        dist //= 2
    return vs


def _mod_kernel(c_ref, cctx_ref, w_ref, b_ref, o_ref):
    tk = w_ref.shape[0]
    first = lax.broadcasted_iota(jnp.int32, (SUBLANES, tk), 0) == 0
    s = jnp.where(first, jnp.broadcast_to(c_ref[...], first.shape), jnp.broadcast_to(cctx_ref[...], first.shape))
    act = s * jax.nn.sigmoid(s)
    w = w_ref[...]
    a_hi, w_hi = act.astype(_BF16), w.astype(_BF16)
    a_lo = (act - a_hi.astype(_F32)).astype(_BF16)
    w_lo = (w - w_hi.astype(_F32)).astype(_BF16)
    by_hi = _dot(jnp.concatenate([a_hi, a_lo], axis=0), w_hi)
    part = by_hi[0:SUBLANES] + by_hi[SUBLANES:] + _dot(a_hi, w_lo)

    @pl.when(pl.program_id(0) == 0)
    def _():
        o_ref[...] = part + b_ref[...]

    @pl.when(pl.program_id(0) > 0)
    def _():
        o_ref[...] += part


def _modulation(c_row, cctx_row, w_mod, b_mod):
    d_in, n_out = w_mod.shape
    tk = 256
    return pl.pallas_call(
        _mod_kernel,
        grid=(d_in // tk,),
        in_specs=[pl.BlockSpec((1, tk), lambda k: (0, k)),
                  pl.BlockSpec((1, tk), lambda k: (0, k)),
                  pl.BlockSpec((tk, n_out), lambda k: (k, 0)),
                  pl.BlockSpec((1, n_out), lambda k: (0, 0))],
        out_specs=pl.BlockSpec((SUBLANES, n_out), lambda k: (0, 0)),
        out_shape=jax.ShapeDtypeStruct((SUBLANES, n_out), _F32),
        compiler_params=pltpu.CompilerParams(dimension_semantics=("arbitrary",),
                                             vmem_limit_bytes=VMEM_LIMIT),
        name="mod",
    )(c_row, cctx_row, w_mod, b_mod)


def _rms_rows(x, g):
    ms = jnp.mean(x * x, axis=-1, keepdims=True)
    return x * lax.rsqrt(ms + RMS_EPS) * g


def _group_ones(width, group):
    rows = lax.broadcasted_iota(jnp.int32, (width, width), 0) // group
    cols = lax.broadcasted_iota(jnp.int32, (width, width), 1) // group
    return jnp.where(rows == cols, 1.0, 0.0).astype(_BF16)


def _in_kernel(n_side, x_ref, ctx_ref, mods_ref, g1_ref, w_in_ref, convw_ref, gconv_ref, *refs):
    side_in, (conv_ref, ug_ref, ucg_ref) = refs[:n_side], refs[n_side:n_side + 3]
    side_out = refs[n_side + 3:2 * n_side + 3]
    u_scr, w_scr, ones_scr = refs[2 * n_side + 3:]
    step = pl.program_id(0)
    last = pl.num_programs(0) - 1

    @pl.when(step == 0)
    def _():
        w_scr[:, 0:SSM_W] = w_in_ref[:, 3 * CONV_W:].astype(_BF16)
        for k in range(CONV_W // MXU_W):
            for part in range(3):
                dst = SSM_W + (3 * k + part) * MXU_W
                src = part * CONV_W + k * MXU_W
                w_scr[:, dst:dst + MXU_W] = w_in_ref[:, src:src + MXU_W].astype(_BF16)
        ones_scr[...] = _group_ones(MXU_W, CONV_W // CONV_HEADS)

    def normed(rows_ref, mod_row):
        sh1 = mods_ref[mod_row:mod_row + 1, 0:D_MODEL]
        sc1 = mods_ref[mod_row:mod_row + 1, D_MODEL:2 * D_MODEL]
        return (_rms_rows(rows_ref[...], g1_ref[...]) * (1.0 + sc1) + sh1).astype(_BF16)

    def to_group_major(z_u, out_ref, r0):
        n_rows = z_u.shape[0]
        chunks = slice(r0 // CHUNK, (r0 + n_rows) // CHUNK)
        for q in range(N_SLABS):
            u_scr[q, r0:r0 + n_rows] = z_u[:, q * LANES:(q + 1) * LANES]
        for q in range(N_SLABS):
            for half in range(CHUNK // BLOCKS):
                rows = [u_scr[q, pl.ds(r0 + half * BLOCKS + t, n_rows // CHUNK, stride=CHUNK), :]
                        for t in range(BLOCKS)]
                cols = _lane_block_transpose(rows)
                for g8 in range(BLOCKS):
                    out_ref[q * BLOCKS + g8, chunks, half * LANES:(half + 1) * LANES] = cols[g8].astype(_BF16)

    def conv_branch(z, r0):
        n_rows = z.shape[0]
        pos = lax.broadcasted_iota(jnp.int32, (n_rows, MXU_W), 0) % GRID_W
        for k in range(CONV_W // MXU_W):
            c0 = SSM_W + 3 * k * MXU_W
            lanes = slice(k * MXU_W, (k + 1) * MXU_W)
            b = z[:, c0:c0 + MXU_W]
            cv = z[:, c0 + MXU_W:c0 + 2 * MXU_W] * z[:, c0 + 2 * MXU_W:c0 + 3 * MXU_W]
            prev = jnp.where(pos == 0, 0.0, pltpu.roll(cv, 1, axis=0))
            nxt = jnp.where(pos == GRID_W - 1, 0.0, pltpu.roll(cv, n_rows - 1, axis=0))
            y = b * (convw_ref[0:1, lanes] * prev + convw_ref[1:2, lanes] * cv + convw_ref[2:3, lanes] * nxt)
            ssq = _dot((y * y).astype(_BF16), ones_scr[...])
            yn = y * lax.rsqrt(ssq * (1.0 / (CONV_W // CONV_HEADS)) + RMS_EPS) * gconv_ref[:, lanes]
            conv_ref[r0:r0 + n_rows, lanes] = yn.astype(_BF16)

    @pl.when(step < last)
    def _():
        for src_ref, dst_ref in zip(side_in, side_out):
            dst_ref[...] = src_ref[...].astype(_BF16)
        th = x_ref.shape[0] // 2
        z = [_dot(normed(x_ref.at[r0:r0 + th], 0), w_scr[...]) for r0 in (0, th)]
        for half, r0 in enumerate((0, th)):
            to_group_major(z[half][:, 0:SSM_W], ug_ref, r0)
            conv_branch(z[half], r0)

    @pl.when(step == last)
    def _():
        to_group_major(_dot(normed(ctx_ref, 1), w_scr[:, 0:SSM_W]), ucg_ref, 0)


def _in_proj(x2d, ctx2d, mods, g1, w_in, conv_w, g_conv, tm, side_weights):
    n, n_ctx = x2d.shape[0], ctx2d.shape[0]
    steps = n // tm
    const = lambda i: (0, 0)
    row = lambda i: (jnp.minimum(i, steps - 1), 0)
    side_specs = [pl.BlockSpec((w.shape[0] // steps, w.shape[1]), row) for w in side_weights]
    return pl.pallas_call(
        functools.partial(_in_kernel, len(side_weights)),
        grid=(steps + 1,),
        in_specs=[pl.BlockSpec((tm, D_MODEL), row),
                  pl.BlockSpec((n_ctx, D_MODEL), const),
                  pl.BlockSpec(mods.shape, const),
                  pl.BlockSpec((1, D_MODEL), const),
                  pl.BlockSpec((D_MODEL, IN_W), const, pipeline_mode=pl.Buffered(1)),
                  pl.BlockSpec((3, CONV_W), const),
                  pl.BlockSpec((1, CONV_W), const)] + side_specs,
        out_specs=[pl.BlockSpec((tm, CONV_W), row),
                   pl.BlockSpec((SSM_GROUPS, tm // CHUNK, CHUNK_W), lambda i: (0, jnp.minimum(i, steps - 1), 0)),
                   pl.BlockSpec((SSM_GROUPS, n_ctx // CHUNK, CHUNK_W), lambda i: (0, 0, 0))] + side_specs,
        out_shape=[jax.ShapeDtypeStruct((n, CONV_W), _BF16),
                   jax.ShapeDtypeStruct((SSM_GROUPS, n // CHUNK, CHUNK_W), _BF16),
                   jax.ShapeDtypeStruct((SSM_GROUPS, n_ctx // CHUNK, CHUNK_W), _BF16)]
                  + [jax.ShapeDtypeStruct(w.shape, _BF16) for w in side_weights],
        scratch_shapes=[pltpu.VMEM((N_SLABS, tm, LANES), _F32),
                        pltpu.VMEM((D_MODEL, IN_W), _BF16),
                        pltpu.VMEM((MXU_W, MXU_W), _BF16)],
        compiler_params=pltpu.CompilerParams(dimension_semantics=("arbitrary",),
                                             vmem_limit_bytes=VMEM_LIMIT),
        name="in_proj",
    )(x2d, ctx2d, mods, g1, w_in, conv_w, g_conv, *side_weights)


def _scan_steps(n_rows):
    return max(1, math.ceil(math.log2(n_rows)))


def _ends_pad(n_tiles):
    return max(SUBLANES, 1 << (_scan_steps(n_tiles) - 1))


def _cmul(ar, ai, br, bi):
    return ar * br - ai * bi, ar * bi + ai * br


def _eye(n):
    rows, cols = (lax.broadcasted_iota(jnp.int32, (n, n), axis) for axis in (0, 1))
    return jnp.where(rows == cols, 1.0, 0.0).astype(_BF16)


def _dot_nt_exact(eye, x, repeat=1):
    hi = x.astype(_BF16)
    rest = x - hi.astype(_F32)
    mid = rest.astype(_BF16)
    lo = (rest - mid.astype(_F32)).astype(_BF16)
    nt = lambda term: lax.dot_general(eye, jnp.concatenate([term] * repeat, axis=0),
                                      (((1,), (1,)), ((), ())), preferred_element_type=_F32)
    return nt(hi) + nt(mid) + nt(lo)


def _ssm_prep(n_steps, par_ref, br_ref, bi_ref, cr_ref, ci_ref, wst_ref, wout_ref, cmat_ref):
    T = CHUNK
    left8 = lax.broadcasted_iota(jnp.int32, (SUBLANES, LANES), 1) < SSM_STATE
    row0 = lax.broadcasted_iota(jnp.int32, (SUBLANES, LANES), 0) == 0

    def member_rows(vf, vb):
        f8 = jnp.broadcast_to(vf, (SUBLANES, LANES))
        b8 = pltpu.roll(jnp.broadcast_to(vb, (SUBLANES, LANES)), SSM_STATE, axis=1)
        return jnp.where(row0, jnp.where(left8, f8, b8), jnp.where(left8, b8, f8))

    pf, pb = par_ref[0, 0], par_ref[1, 0]
    lr = jnp.minimum(member_rows(pf[0:1], pb[0:1]), -1e-4)
    li = member_rows(pf[1:2], pb[1:2])
    dt = jnp.exp(member_rows(pf[2:3], pb[2:3]))
    mag = jnp.exp(lr * dt)
    ab_re = mag * jnp.cos(li * dt)
    ab_im = mag * jnp.sin(li * dt)
    nr = ab_re - 1.0
    den = lr * lr + li * li
    f_re = (nr * lr + ab_im * li) / den
    f_im = (ab_im * lr - nr * li) / den
    pt_re, pt_im = [jnp.ones_like(ab_re)], [jnp.zeros_like(ab_im)]
    for _ in range(T):
        nxt = _cmul(pt_re[-1], pt_im[-1], ab_re, ab_im)
        pt_re.append(nxt[0])
        pt_im.append(nxt[1])

    cols = jnp.concatenate([ab_re, ab_im, jnp.zeros((LANES - 2 * SUBLANES, LANES), _F32)], axis=0).T
    left1 = lax.broadcasted_iota(jnp.int32, (1, LANES), 1) < SSM_STATE
    eye_i = _eye(SSM_GROUP)
    eye_p = _eye(SSM_STATE)
    rows = lax.broadcasted_iota(jnp.int32, (LANES, CHUNK_W), 0)
    tcol = lax.broadcasted_iota(jnp.int32, (LANES, CHUNK_W), 1) // SSM_GROUP
    zero_half = jnp.zeros((SSM_STATE, CHUNK_W), _BF16)
    for e in range(PAIR):
        fwd_first = e == 0
        d0, d1 = (0, 1) if fwd_first else (1, 0)
        b_re = _dot_nt_exact(eye_i, jnp.concatenate([br_ref[d0, e], br_ref[d1, e]], axis=0))
        b_im = _dot_nt_exact(eye_i, jnp.concatenate([bi_ref[d0, e], bi_ref[d1, e]], axis=0))
        bbar_re, bbar_im = _cmul(f_re[e:e + 1], f_im[e:e + 1], b_re, b_im)
        fwd_lane = left1 if fwd_first else jnp.logical_not(left1)
        for s in range(T):
            pr = jnp.where(fwd_lane, pt_re[T - 1 - s][e:e + 1], pt_re[s][e:e + 1])
            pi = jnp.where(fwd_lane, pt_im[T - 1 - s][e:e + 1], pt_im[s][e:e + 1])
            w_re, w_im = _cmul(pr, pi, bbar_re, bbar_im)
            wst_ref[e, s * SSM_GROUP:(s + 1) * SSM_GROUP, 0:LANES] = w_re.astype(_BF16)
            wst_ref[e, s * SSM_GROUP:(s + 1) * SSM_GROUP, LANES:] = w_im.astype(_BF16)
        fwd_row = (rows < SSM_STATE) if fwd_first else (rows >= SSM_STATE)
        expo = jnp.where(fwd_row, tcol + 1, T - tcol)
        q_re = jnp.ones((LANES, CHUNK_W), _F32)
        q_im = jnp.zeros((LANES, CHUNK_W), _F32)
        s_re, s_im = cols[:, e:e + 1], cols[:, SUBLANES + e:SUBLANES + e + 1]
        for bit in range(T.bit_length()):
            m_re, m_im = _cmul(q_re, q_im, jnp.broadcast_to(s_re, q_re.shape), jnp.broadcast_to(s_im, q_re.shape))
            take = ((expo >> bit) & 1) == 1
            q_re, q_im = jnp.where(take, m_re, q_re), jnp.where(take, m_im, q_im)
            s_re, s_im = _cmul(s_re, s_im, s_re, s_im)
        tiled = lambda c: _dot_nt_exact(eye_p, c, repeat=T)
        c_re = jnp.concatenate([tiled(cr_ref[d0, e]), tiled(cr_ref[d1, e])], axis=0)
        c_im = jnp.concatenate([tiled(ci_ref[d0, e]), tiled(ci_ref[d1, e])], axis=0)
        o_re, o_im = _cmul(c_re, c_im, q_re, q_im)
        wout_ref[e, 0:LANES] = o_re.astype(_BF16)
        wout_ref[e, LANES:] = (-o_im).astype(_BF16)
        for ri, src in enumerate((c_re, -c_im)):
            for slot, d in enumerate((d0, d1)):
                r0 = ri * LANES + slot * SSM_STATE
                block = src[slot * SSM_STATE:(slot + 1) * SSM_STATE].astype(_BF16)
                cmat_ref[e, r0:r0 + SSM_STATE, d * CHUNK_W:(d + 1) * CHUNK_W] = block
                cmat_ref[e, r0:r0 + SSM_STATE, (1 - d) * CHUNK_W:(2 - d) * CHUNK_W] = zero_half

    m0_re, m1_re = pt_re[T][0:1], pt_re[T][1:2]
    m0_im, m1_im = pt_im[T][0:1], pt_im[T][1:2]
    cf = (jnp.where(left1, m0_re, m1_re), jnp.where(left1, m0_im, m1_im))
    cb = (jnp.where(left1, m1_re, m0_re), jnp.where(left1, m1_im, m0_im))
    coefs = []
    for _ in range(n_steps):
        coefs.append([cf[0], cf[1], cb[0], cb[1]])
        cf = _cmul(cf[0], cf[1], cf[0], cf[1])
        cb = _cmul(cb[0], cb[1], cb[0], cb[1])
    return coefs


def _ssm_kernel(n_lat, n_ctx, pairs_per_step, u_ref, uc_ref, par_ref, br_ref, bi_ref, cr_ref, ci_ref, d_ref,
                y_ref, *scratch):
    programs = []
    for p in range(pairs_per_step):
        members = slice(p * PAIR, (p + 1) * PAIR)
        programs.append(_ssm_pair(
            n_lat, n_ctx, u_ref.at[members], uc_ref.at[members], par_ref.at[:, p:p + 1],
            br_ref.at[:, members], bi_ref.at[:, members], cr_ref.at[:, members], ci_ref.at[:, members],
            d_ref.at[members], y_ref.at[members], *[s.at[p] for s in scratch]))
    live = []
    while programs or live:
        if programs:
            live.append(programs.pop(0))
        for prog in list(live):
            if next(prog, "done") == "done":
                live.remove(prog)


def _ssm_pair(n_lat, n_ctx, u_ref, uc_ref, par_ref, br_ref, bi_ref, cr_ref, ci_ref, d_ref,
              y_ref, buf_a, buf_b, ends_a, ends_b, wst_ref, wout_ref, cmat_ref, m_scr):
    n_rows = n_lat + n_ctx
    pad = SUBLANES
    coefs = _ssm_prep(_scan_steps(n_rows), par_ref, br_ref, bi_ref, cr_ref, ci_ref,
                      wst_ref, wout_ref, cmat_ref)
    yield

    lane = lax.broadcasted_iota(jnp.int32, (SSM_GROUP, CHUNK_W), 1)
    blk = lane // SSM_GROUP
    on_diag = lane % SSM_GROUP == lax.broadcasted_iota(jnp.int32, (SSM_GROUP, CHUNK_W), 0)
    for e in range(PAIR):
        d_skip = jnp.where(on_diag, jnp.broadcast_to(d_ref[e], (SSM_GROUP, CHUNK_W)), 0.0)
        taps = _dot(wst_ref[e], cmat_ref[e])
        taps_f = taps[:, 0:CHUNK_W]
        taps_b = taps[:, CHUNK_W:]
        for s in range(CHUNK):
            acc = jnp.zeros((SSM_GROUP, CHUNK_W), _F32)
            for t in range(CHUNK):
                val = None
                if s >= t:
                    r0 = (s - t) * SSM_GROUP
                    val = taps_b[r0:r0 + SSM_GROUP]
                if s <= t:
                    r0 = (CHUNK - 1 - t + s) * SSM_GROUP
                    v2 = taps_f[r0:r0 + SSM_GROUP]
                    val = v2 if val is None else val + v2
                if s == t:
                    val = val + d_skip
                acc = jnp.where(blk == t, val, acc)
            m_scr[e, s * SSM_GROUP:(s + 1) * SSM_GROUP, :] = acc.astype(_BF16)
    yield

    def paired(m0, m1):
        left = lax.broadcasted_iota(jnp.int32, (m0.shape[0], PAIR_STATE), 1) < SSM_STATE
        re0, im0 = m0[:, 0:PAIR_STATE], m0[:, PAIR_STATE:]
        re1, im1 = m1[:, 0:PAIR_STATE], m1[:, PAIR_STATE:]
        return [jnp.where(left, re0, re1), jnp.where(left, im0, im1),
                jnp.where(left, re1, re0), jnp.where(left, im1, im0)]

    s_lat = paired(_dot(u_ref[0], wst_ref[0]), _dot(u_ref[1], wst_ref[1]))
    s_ctx = paired(_dot(uc_ref[0], wst_ref[0]), _dot(uc_ref[1], wst_ref[1]))
    yield

    n_tiles = n_rows // SUBLANES
    tile_row = lax.broadcasted_iota(jnp.int32, (1, SUBLANES, PAIR_STATE), 1)

    def tile_scan(x_re, x_im, comp0, backward):
        for k in range(SUBLANES.bit_length() - 1):
            sh = 1 << k
            shift, keep = (SUBLANES - sh, tile_row < SUBLANES - sh) if backward else (sh, tile_row >= sh)
            a_re = jnp.where(keep, coefs[k][comp0], 0.0)
            a_im = jnp.where(keep, coefs[k][comp0 + 1], 0.0)
            p_re, p_im = pltpu.roll(x_re, shift, axis=1), pltpu.roll(x_im, shift, axis=1)
            x_re, x_im = x_re + a_re * p_re - a_im * p_im, x_im + a_re * p_im + a_im * p_re
        return x_re, x_im

    as_tiles = lambda parts: jnp.concatenate(parts, axis=0).reshape(n_tiles, SUBLANES, PAIR_STATE)
    local = list(tile_scan(as_tiles([s_ctx[0], s_lat[0]]), as_tiles([s_ctx[1], s_lat[1]]), 0, False))
    local += tile_scan(as_tiles([s_lat[2], s_ctx[2]]), as_tiles([s_lat[3], s_ctx[3]]), 2, True)
    for comp in range(N_COMP):
        buf_a[comp, pad:pad + n_rows] = local[comp].reshape(n_rows, PAIR_STATE)
    yield

    for buf in (ends_a, ends_b):
        buf[...] = jnp.zeros(buf.shape, _F32)
    epad = _ends_pad(n_tiles)
    for comp in range(N_COMP):
        end_row = pad + (SUBLANES - 1 if comp < 2 else 0)
        ends_a[comp, epad:epad + n_tiles] = buf_a[comp, pl.ds(end_row, n_tiles, stride=SUBLANES), :]
    src, dst = ends_a, ends_b
    for k in range(_scan_steps(n_tiles)):
        sh = 1 << k
        coef = coefs[SUBLANES.bit_length() - 1 + k]
        lo, hi = epad, epad + n_tiles
        xr, xi = src[0, lo - sh:hi - sh], src[1, lo - sh:hi - sh]
        dst[0, lo:hi] = src[0, lo:hi] + coef[0] * xr - coef[1] * xi
        dst[1, lo:hi] = src[1, lo:hi] + coef[0] * xi + coef[1] * xr
        xr, xi = src[2, lo + sh:hi + sh], src[3, lo + sh:hi + sh]
        dst[2, lo:hi] = src[2, lo:hi] + coef[2] * xr - coef[3] * xi
        dst[3, lo:hi] = src[3, lo:hi] + coef[2] * xi + coef[3] * xr
        src, dst = dst, src

    def carry_powers(comp0, backward):
        base = [(coefs[k][comp0], coefs[k][comp0 + 1]) for k in range(SUBLANES.bit_length())]
        pw = []
        for r in range(1, SUBLANES + 1):
            acc = None
            for k, term in enumerate(base):
                if (r >> k) & 1:
                    acc = term if acc is None else _cmul(acc[0], acc[1], term[0], term[1])
            pw.append(acc)
        if backward:
            pw = pw[::-1]
        return (jnp.concatenate([p[0] for p in pw], axis=0), jnp.concatenate([p[1] for p in pw], axis=0))

    for comp0, backward in ((0, False), (2, True)):
        p_re, p_im = carry_powers(comp0, backward)
        first = epad + 1 if backward else epad - 1
        for i in range(n_tiles):
            c_re = src[comp0, first + i:first + i + 1]
            c_im = src[comp0 + 1, first + i:first + i + 1]
            rows = slice(pad + i * SUBLANES, pad + (i + 1) * SUBLANES)
            buf_b[comp0, rows] = buf_a[comp0, rows] + p_re * c_re - p_im * c_im
            buf_b[comp0 + 1, rows] = buf_a[comp0 + 1, rows] + p_re * c_im + p_im * c_re
    yield

    f0 = pad + n_ctx - 1
    b0 = pad + 1
    f_re, f_im = buf_b[0, f0:f0 + n_lat], buf_b[1, f0:f0 + n_lat]
    b_re, b_im = buf_b[2, b0:b0 + n_lat], buf_b[3, b0:b0 + n_lat]
    left = lax.broadcasted_iota(jnp.int32, (n_lat, PAIR_STATE), 1) < SSM_STATE
    h0 = jnp.concatenate([jnp.where(left, f_re, b_re), jnp.where(left, f_im, b_im)], axis=1)
    h1 = jnp.concatenate([jnp.where(left, b_re, f_re), jnp.where(left, b_im, f_im)], axis=1)
    y_ref[0] = _dot(u_ref[0], m_scr[0]) + _dot(h0.astype(_BF16), wout_ref[0])
    y_ref[1] = _dot(u_ref[1], m_scr[1]) + _dot(h1.astype(_BF16), wout_ref[1])


def _ssm(u_g, uc_g, par, b_re, b_im, c_re, c_im, d_col):
    n_groups, n_lat, _ = u_g.shape
    n_ctx = uc_g.shape[1]
    n_rows = n_lat + n_ctx
    n_pairs = n_groups // PAIR
    pps = 2
    gps = pps * PAIR
    buf = pltpu.VMEM((pps, N_COMP, n_rows + 2 * SUBLANES, PAIR_STATE), _F32)
    n_tiles = n_rows // SUBLANES
    ends_rows = _ends_pad(n_tiles) + -(-(n_tiles + _ends_pad(n_tiles)) // SUBLANES) * SUBLANES
    ends = pltpu.VMEM((pps, N_COMP, ends_rows, PAIR_STATE), _F32)
    mat = pltpu.VMEM((pps, PAIR, CHUNK_W, CHUNK_W), _BF16)
    pair3 = lambda q: (q, 0, 0)
    by_dir = lambda q: (0, q, 0, 0)
    return pl.pallas_call(
        functools.partial(_ssm_kernel, n_lat, n_ctx, pps),
        grid=(n_pairs // pps,),
        in_specs=[pl.BlockSpec((gps, n_lat, CHUNK_W), pair3),
                  pl.BlockSpec((gps, n_ctx, CHUNK_W), pair3),
                  pl.BlockSpec((2, pps) + par.shape[2:], by_dir),
                  pl.BlockSpec((2, gps) + b_re.shape[2:], by_dir),
                  pl.BlockSpec((2, gps) + b_im.shape[2:], by_dir),
                  pl.BlockSpec((2, gps) + c_re.shape[2:], by_dir),
                  pl.BlockSpec((2, gps) + c_im.shape[2:], by_dir),
                  pl.BlockSpec((gps,) + d_col.shape[1:], pair3)],
        out_specs=pl.BlockSpec((gps, n_lat, CHUNK_W), pair3),
        out_shape=jax.ShapeDtypeStruct((n_groups, n_lat, CHUNK_W), _F32),
        scratch_shapes=[buf, buf, ends, ends, mat, mat,
                        pltpu.VMEM((pps, PAIR, CHUNK_W, 2 * CHUNK_W), _BF16), mat],
        compiler_params=pltpu.CompilerParams(dimension_semantics=("arbitrary",),
                                             vmem_limit_bytes=VMEM_LIMIT),
        name="ssm",
    )(u_g, uc_g, par, b_re, b_im, c_re, c_im, d_col)


def _out_kernel(x_ref, conv_ref, yg_ref, mods_ref, gssm_ref, g2_ref, gfin_ref,
                wglu32_ref, wout32_ref, w1_ref, w2_ref, o_ref, y_scr, mix_scr, wglu_ref, wout_ref, ones_ref):
    @pl.when(pl.program_id(0) == 0)
    def _():
        wglu_ref[...] = wglu32_ref[...].astype(_BF16)
        wout_ref[...] = wout32_ref[...].astype(_BF16)
        ones_ref[...] = _group_ones(MXU_W, SSM_GROUP)

    gt1 = mods_ref[0:1, 2 * D_MODEL:3 * D_MODEL]
    sh2 = mods_ref[0:1, 3 * D_MODEL:4 * D_MODEL]
    sc2 = mods_ref[0:1, 4 * D_MODEL:5 * D_MODEL]
    gt2 = mods_ref[0:1, 5 * D_MODEL:6 * D_MODEL]
    th = x_ref.shape[0] // 2
    n_split = 4
    ff = D_FF // n_split

    def mixer(r0):
        rows_ = slice(r0, r0 + th)
        chunks = slice(r0 // CHUNK, (r0 + th) // CHUNK)
        for q in range(N_SLABS):
            for half in range(CHUNK // BLOCKS):
                cols = [yg_ref[q * BLOCKS + g8, chunks, half * LANES:(half + 1) * LANES] for g8 in range(BLOCKS)]
                rows = _lane_block_transpose(cols)
                for t in range(BLOCKS):
                    y_scr[q, pl.ds(r0 + half * BLOCKS + t, th // CHUNK, stride=CHUNK), :] = rows[t]
        s = jax.nn.gelu(jnp.concatenate([y_scr[q, rows_] for q in range(N_SLABS)], axis=1))
        s = s * jax.nn.sigmoid(_dot(s.astype(_BF16), wglu_ref[...]))
        sq = (s * s).astype(_BF16)
        ssq = jnp.concatenate([_dot(sq[:, k * MXU_W:(k + 1) * MXU_W], ones_ref[...])
                               for k in range(SSM_W // MXU_W)], axis=1)
        sn = s * lax.rsqrt(ssq * (1.0 / SSM_GROUP) + RMS_EPS) * gssm_ref[...]
        mix = mix_scr[rows_] + _dot(sn.astype(_BF16), wout_ref[CONV_W:])
        x_mid = x_ref[rows_, :] + gt1 * mix
        return x_mid, (_rms_rows(x_mid, g2_ref[...]) * (1.0 + sc2) + sh2).astype(_BF16)

    def mlp_chunk(h2, j):
        a = jnp.maximum(_dot(h2, w1_ref[:, j * ff:(j + 1) * ff]), 0.0)
        return _dot((a * a).astype(_BF16), w2_ref[j * ff:(j + 1) * ff])

    def finish(r0, x_mid, acc):
        o_ref[r0:r0 + th, :] = _rms_rows(x_mid + gt2 * acc, gfin_ref[...])

    mix_scr[...] = _dot(conv_ref[...], wout_ref[0:CONV_W])
    x_a, h_a = mixer(0)
    acc_a = mlp_chunk(h_a, 0)
    x_b, h_b = mixer(th)
    for j in range(1, n_split):
        acc_a = acc_a + mlp_chunk(h_a, j)
    acc_b = mlp_chunk(h_b, 0)
    finish(0, x_a, acc_a)
    for j in range(1, n_split):
        acc_b = acc_b + mlp_chunk(h_b, j)
    finish(th, x_b, acc_b)


def _out_proj(x2d, conv_n, y_g, mods, g_ssm, g2, g_fin, wglu, wout, w1, w2, tm):
    n = x2d.shape[0]
    const = lambda i: (0, 0)
    row = lambda i: (i, 0)
    resident = lambda shape: pl.BlockSpec(shape, const, pipeline_mode=pl.Buffered(1))
    return pl.pallas_call(
        _out_kernel,
        grid=(n // tm,),
        in_specs=[pl.BlockSpec((tm, D_MODEL), row),
                  pl.BlockSpec((tm, CONV_W), row),
                  pl.BlockSpec((SSM_GROUPS, tm // CHUNK, CHUNK_W), lambda i: (0, i, 0)),
                  pl.BlockSpec(mods.shape, const),
                  pl.BlockSpec((1, SSM_W), const),
                  pl.BlockSpec((1, D_MODEL), const),
                  pl.BlockSpec((1, D_MODEL), const),
                  resident((SSM_W, SSM_W)),
                  resident((D_MODEL, D_MODEL)),
                  resident((D_MODEL, D_FF)),
                  resident((D_FF, D_MODEL))],
        out_specs=pl.BlockSpec((tm, D_MODEL), row),
        out_shape=jax.ShapeDtypeStruct((n, D_MODEL), _F32),
        scratch_shapes=[pltpu.VMEM((N_SLABS, tm, LANES), _F32), pltpu.VMEM((tm, D_MODEL), _F32),
                        pltpu.VMEM((SSM_W, SSM_W), _BF16), pltpu.VMEM((D_MODEL, D_MODEL), _BF16),
                        pltpu.VMEM((MXU_W, MXU_W), _BF16)],
        compiler_params=pltpu.CompilerParams(dimension_semantics=("arbitrary",),
                                             vmem_limit_bytes=VMEM_LIMIT),
        name="out_proj",
    )(x2d, conv_n, y_g, mods, g_ssm, g2, g_fin, wglu, wout, w1, w2)


def _ssm_lane_params(a_re, a_im, log_dt):
    n_pairs = SSM_GROUPS // PAIR
    pair_lanes = lambda v: v.reshape(2, n_pairs, PAIR * SSM_STATE)
    dt_lanes = jnp.broadcast_to(log_dt[..., None], (2, SSM_GROUPS, SSM_STATE))
    return jnp.stack([pair_lanes(a_re), pair_lanes(a_im), pair_lanes(dt_lanes)], axis=2)


def kernel(x, c, ctx, c_ctx, w_mod, b_mod, g_norm1, w_in, conv_w, ssm_a_re, ssm_a_im, ssm_log_dt,
           ssm_b_re, ssm_b_im, ssm_c_re, ssm_c_im, ssm_d, w_glu, g_conv_out, g_ssm_out, w_out,
           g_norm2, w_mlp1, w_mlp2, g_final):
    bsz, n_lat, d_model = x.shape
    n_ctx = ctx.shape[1]
    assert bsz == 1 and d_model == D_MODEL and w_mod.shape[0] == 1
    assert n_lat % (CHUNK * SUBLANES) == 0 and n_ctx % (CHUNK * CHUNK) == 0 and n_lat % GRID_W == 0
    layer = 0
    x2d = x[0]
    ctx2d = ctx[0]

    mods = _modulation(c, c_ctx[None, :], w_mod[layer], b_mod[layer][None, :])

    g1 = g_norm1[layer][None, :]
    g_conv = g_conv_out[layer][None, :]
    conv_n, u_g, uc_g, w1_bf, w2_bf = _in_proj(x2d, ctx2d, mods, g1, w_in[layer], conv_w[layer], g_conv, tm=1024,
                                                side_weights=(w_mlp1[layer], w_mlp2[layer]))

    y_g = _ssm(u_g, uc_g, _ssm_lane_params(ssm_a_re[layer], ssm_a_im[layer], ssm_log_dt[layer]),
               ssm_b_re[layer], ssm_b_im[layer], ssm_c_re[layer], ssm_c_im[layer],
               ssm_d.reshape(SSM_GROUPS, SSM_GROUP, 1))

    out = _out_proj(x2d, conv_n, y_g, mods, g_ssm_out[layer][None, :], g_norm2[layer][None, :],
                    g_final[None, :], w_glu[layer], w_out[layer], w1_bf, w2_bf, tm=1024)
    return out[None]
```

```python
import functools
import math

import jax
import jax.numpy as jnp
from jax import lax
from jax.experimental import pallas as pl
from jax.experimental.pallas import tpu as pltpu

D_MODEL = 1024
GRID_W = 64
CONV_W = 512
CONV_HEADS = 8
SSM_W = 512
SSM_GROUP = 16
SSM_GROUPS = 32
SSM_STATE = 64
IN_W = 3 * CONV_W + SSM_W
D_FF = 4 * D_MODEL
RMS_EPS = 1e-6

LANES = 128
MXU_W = 256
SUBLANES = 8
CHUNK = 16
CHUNK_W = CHUNK * SSM_GROUP
PAIR = 2
PAIR_STATE = PAIR * SSM_STATE
N_COMP = 4
BLOCKS = LANES // SSM_GROUP
N_SLABS = SSM_W // LANES
VMEM_LIMIT = 62 * 1024 * 1024

_BF16 = jnp.bfloat16
_F32 = jnp.float32


def _dot(a, b):
    return jnp.dot(a, b, preferred_element_type=_F32)


def _lane_block_transpose(vs):
    blk = lax.broadcasted_iota(jnp.int32, vs[0].shape, 1) // SSM_GROUP
    dist = BLOCKS // 2
    while dist:
        shift = dist * SSM_GROUP
        upper = (blk & dist) != 0
        out = [None] * BLOCKS
        for a in range(BLOCKS):
            if a & dist:
                continue
            lo, hi = vs[a], vs[a + dist]
            out[a] = jnp.where(upper, pltpu.roll(hi, shift, axis=1), lo)
            out[a + dist] = jnp.where(upper, hi, pltpu.roll(lo, LANES - shift, axis=1))
        vs = out
        dist //= 2
    return vs


def _mod_kernel(c_ref, cctx_ref, w_ref, b_ref, o_ref):
    tk = w_ref.shape[0]
    first = lax.broadcasted_iota(jnp.int32, (SUBLANES, tk), 0) == 0
    s = jnp.where(first, jnp.broadcast_to(c_ref[...], first.shape), jnp.broadcast_to(cctx_ref[...], first.shape))
    act = s * jax.nn.sigmoid(s)
    w = w_ref[...]
    a_hi, w_hi = act.astype(_BF16), w.astype(_BF16)
    a_lo = (act - a_hi.astype(_F32)).astype(_BF16)
    w_lo = (w - w_hi.astype(_F32)).astype(_BF16)
    by_hi = _dot(jnp.concatenate([a_hi, a_lo], axis=0), w_hi)
    part = by_hi[0:SUBLANES] + by_hi[SUBLANES:] + _dot(a_hi, w_lo)

    @pl.when(pl.program_id(0) == 0)
    def _():
        o_ref[...] = part + b_ref[...]

    @pl.when(pl.program_id(0) > 0)
    def _():
        o_ref[...] += part


def _modulation(c_row, cctx_row, w_mod, b_mod):
    d_in, n_out = w_mod.shape
    tk = 256
    return pl.pallas_call(
        _mod_kernel,
        grid=(d_in // tk,),
        in_specs=[pl.BlockSpec((1, tk), lambda k: (0, k)),
                  pl.BlockSpec((1, tk), lambda k: (0, k)),
                  pl.BlockSpec((tk, n_out), lambda k: (k, 0)),
                  pl.BlockSpec((1, n_out), lambda k: (0, 0))],
        out_specs=pl.BlockSpec((SUBLANES, n_out), lambda k: (0, 0)),
        out_shape=jax.ShapeDtypeStruct((SUBLANES, n_out), _F32),
        compiler_params=pltpu.CompilerParams(dimension_semantics=("arbitrary",),
                                             vmem_limit_bytes=VMEM_LIMIT),
        name="mod",
    )(c_row, cctx_row, w_mod, b_mod)


def _rms_rows(x, g):
    ms = jnp.mean(x * x, axis=-1, keepdims=True)
    return x * lax.rsqrt(ms + RMS_EPS) * g


def _group_ones(width, group):
    rows = lax.broadcasted_iota(jnp.int32, (width, width), 0) // group
    cols = lax.broadcasted_iota(jnp.int32, (width, width), 1) // group
    return jnp.where(rows == cols, 1.0, 0.0).astype(_BF16)


def _in_kernel(n_side, x_ref, ctx_ref, mods_ref, g1_ref, w_in_ref, convw_ref, gconv_ref, *refs):
    side_in, (conv_ref, ug_ref, ucg_ref) = refs[:n_side], refs[n_side:n_side + 3]
    side_out = refs[n_side + 3:2 * n_side + 3]
    u_scr, w_scr, ones_scr = refs[2 * n_side + 3:]
    step = pl.program_id(0)
    last = pl.num_programs(0) - 1

    @pl.when(step == 0)
    def _():
        w_scr[:, 0:SSM_W] = w_in_ref[:, 3 * CONV_W:].astype(_BF16)
        for k in range(CONV_W // MXU_W):
            for part in range(3):
                dst = SSM_W + (3 * k + part) * MXU_W
                src = part * CONV_W + k * MXU_W
                w_scr[:, dst:dst + MXU_W] = w_in_ref[:, src:src + MXU_W].astype(_BF16)
        ones_scr[...] = _group_ones(MXU_W, CONV_W // CONV_HEADS)

    def normed(rows_ref, mod_row):
        sh1 = mods_ref[mod_row:mod_row + 1, 0:D_MODEL]
        sc1 = mods_ref[mod_row:mod_row + 1, D_MODEL:2 * D_MODEL]
        return (_rms_rows(rows_ref[...], g1_ref[...]) * (1.0 + sc1) + sh1).astype(_BF16)

    def to_group_major(z_u, out_ref, r0):
        n_rows = z_u.shape[0]
        chunks = slice(r0 // CHUNK, (r0 + n_rows) // CHUNK)
        for q in range(N_SLABS):
            u_scr[q, r0:r0 + n_rows] = z_u[:, q * LANES:(q + 1) * LANES]
        for q in range(N_SLABS):
            for half in range(CHUNK // BLOCKS):
                rows = [u_scr[q, pl.ds(r0 + half * BLOCKS + t, n_rows // CHUNK, stride=CHUNK), :]
                        for t in range(BLOCKS)]
                cols = _lane_block_transpose(rows)
                for g8 in range(BLOCKS):
                    out_ref[q * BLOCKS + g8, chunks, half * LANES:(half + 1) * LANES] = cols[g8].astype(_BF16)

    def conv_branch(z, r0):
        n_rows = z.shape[0]
        pos = lax.broadcasted_iota(jnp.int32, (n_rows, MXU_W), 0) % GRID_W
        for k in range(CONV_W // MXU_W):
            c0 = SSM_W + 3 * k * MXU_W
            lanes = slice(k * MXU_W, (k + 1) * MXU_W)
            b = z[:, c0:c0 + MXU_W]
            cv = z[:, c0 + MXU_W:c0 + 2 * MXU_W] * z[:, c0 + 2 * MXU_W:c0 + 3 * MXU_W]
            prev = jnp.where(pos == 0, 0.0, pltpu.roll(cv, 1, axis=0))
            nxt = jnp.where(pos == GRID_W - 1, 0.0, pltpu.roll(cv, n_rows - 1, axis=0))
            y = b * (convw_ref[0:1, lanes] * prev + convw_ref[1:2, lanes] * cv + convw_ref[2:3, lanes] * nxt)
            ssq = _dot((y * y).astype(_BF16), ones_scr[...])
            yn = y * lax.rsqrt(ssq * (1.0 / (CONV_W // CONV_HEADS)) + RMS_EPS) * gconv_ref[:, lanes]
            conv_ref[r0:r0 + n_rows, lanes] = yn.astype(_BF16)

    @pl.when(step < last)
    def _():
        for src_ref, dst_ref in zip(side_in, side_out):
            dst_ref[...] = src_ref[...].astype(_BF16)
        th = x_ref.shape[0] // 2
        z = [_dot(normed(x_ref.at[r0:r0 + th], 0), w_scr[...]) for r0 in (0, th)]
        for half, r0 in enumerate((0, th)):
            to_group_major(z[half][:, 0:SSM_W], ug_ref, r0)
            conv_branch(z[half], r0)

    @pl.when(step == last)
    def _():
        to_group_major(_dot(normed(ctx_ref, 1), w_scr[:, 0:SSM_W]), ucg_ref, 0)


def _in_proj(x2d, ctx2d, mods, g1, w_in, conv_w, g_conv, tm, side_weights):
    n, n_ctx = x2d.shape[0], ctx2d.shape[0]
    steps = n // tm
    const = lambda i: (0, 0)
    row = lambda i: (jnp.minimum(i, steps - 1), 0)
    side_specs = [pl.BlockSpec((w.shape[0] // steps, w.shape[1]), row) for w in side_weights]
    return pl.pallas_call(
        functools.partial(_in_kernel, len(side_weights)),
        grid=(steps + 1,),
        in_specs=[pl.BlockSpec((tm, D_MODEL), row),
                  pl.BlockSpec((n_ctx, D_MODEL), const),
                  pl.BlockSpec(mods.shape, const),
                  pl.BlockSpec((1, D_MODEL), const),
                  pl.BlockSpec((D_MODEL, IN_W), const, pipeline_mode=pl.Buffered(1)),
                  pl.BlockSpec((3, CONV_W), const),
                  pl.BlockSpec((1, CONV_W), const)] + side_specs,
        out_specs=[pl.BlockSpec((tm, CONV_W), row),
                   pl.BlockSpec((SSM_GROUPS, tm // CHUNK, CHUNK_W), lambda i: (0, jnp.minimum(i, steps - 1), 0)),
                   pl.BlockSpec((SSM_GROUPS, n_ctx // CHUNK, CHUNK_W), lambda i: (0, 0, 0))] + side_specs,
        out_shape=[jax.ShapeDtypeStruct((n, CONV_W), _BF16),
                   jax.ShapeDtypeStruct((SSM_GROUPS, n // CHUNK, CHUNK_W), _BF16),
                   jax.ShapeDtypeStruct((SSM_GROUPS, n_ctx // CHUNK, CHUNK_W), _BF16)]
                  + [jax.ShapeDtypeStruct(w.shape, _BF16) for w in side_weights],
        scratch_shapes=[pltpu.VMEM((N_SLABS, tm, LANES), _F32),
                        pltpu.VMEM((D_MODEL, IN_W), _BF16),
                        pltpu.VMEM((MXU_W, MXU_W), _BF16)],
        compiler_params=pltpu.CompilerParams(dimension_semantics=("arbitrary",),
                                             vmem_limit_bytes=VMEM_LIMIT),
        name="in_proj",
    )(x2d, ctx2d, mods, g1, w_in, conv_w, g_conv, *side_weights)


def _scan_steps(n_rows):
    return max(1, math.ceil(math.log2(n_rows)))


def _ends_pad(n_tiles):
    return max(SUBLANES, 1 << (_scan_steps(n_tiles) - 1))


def _cmul(ar, ai, br, bi):
    return ar * br - ai * bi, ar * bi + ai * br


def _eye(n):
    rows, cols = (lax.broadcasted_iota(jnp.int32, (n, n), axis) for axis in (0, 1))
    return jnp.where(rows == cols, 1.0, 0.0).astype(_BF16)


def _dot_nt_exact(eye, x, repeat=1):
    hi = x.astype(_BF16)
    rest = x - hi.astype(_F32)
    mid = rest.astype(_BF16)
    lo = (rest - mid.astype(_F32)).astype(_BF16)
    nt = lambda term: lax.dot_general(eye, jnp.concatenate([term] * repeat, axis=0),
                                      (((1,), (1,)), ((), ())), preferred_element_type=_F32)
    return nt(hi) + nt(mid) + nt(lo)


def _ssm_prep(n_steps, par_refs, br_ref, bi_ref, cr_ref, ci_ref, wst_ref, wout_ref, cmat_ref):
    T = CHUNK
    left8 = lax.broadcasted_iota(jnp.int32, (SUBLANES, LANES), 1) < SSM_STATE
    row0 = lax.broadcasted_iota(jnp.int32, (SUBLANES, LANES), 0) == 0

    def member_rows(vf, vb):
        f8 = jnp.broadcast_to(vf, (SUBLANES, LANES))
        b8 = pltpu.roll(jnp.broadcast_to(vb, (SUBLANES, LANES)), SSM_STATE, axis=1)
        return jnp.where(row0, jnp.where(left8, f8, b8), jnp.where(left8, b8, f8))

    are_ref, aim_ref, ldt_ref = par_refs
    in_first = lax.broadcasted_iota(jnp.int32, (1, LANES), 1) < SSM_STATE
    dt_lanes = lambda d: jnp.where(in_first, ldt_ref[d, 0][:, 0:1], ldt_ref[d, 0][:, 1:2])
    lr = jnp.minimum(member_rows(are_ref[0, 0], are_ref[1, 0]), -1e-4)
    li = member_rows(aim_ref[0, 0], aim_ref[1, 0])
    dt = jnp.exp(member_rows(dt_lanes(0), dt_lanes(1)))
    mag = jnp.exp(lr * dt)
    ab_re = mag * jnp.cos(li * dt)
    ab_im = mag * jnp.sin(li * dt)
    nr = ab_re - 1.0
    den = lr * lr + li * li
    f_re = (nr * lr + ab_im * li) / den
    f_im = (ab_im * lr - nr * li) / den
    pt_re, pt_im = [jnp.ones_like(ab_re)], [jnp.zeros_like(ab_im)]
    for _ in range(T):
        nxt = _cmul(pt_re[-1], pt_im[-1], ab_re, ab_im)
        pt_re.append(nxt[0])
        pt_im.append(nxt[1])

    cols = jnp.concatenate([ab_re, ab_im, jnp.zeros((LANES - 2 * SUBLANES, LANES), _F32)], axis=0).T
    left1 = lax.broadcasted_iota(jnp.int32, (1, LANES), 1) < SSM_STATE
    eye_i = _eye(SSM_GROUP)
    eye_p = _eye(SSM_STATE)
    rows = lax.broadcasted_iota(jnp.int32, (LANES, LANES), 0)
    tcol = lax.broadcasted_iota(jnp.int32, (LANES, LANES), 1) // SSM_GROUP
    zero_half = jnp.zeros((SSM_STATE, CHUNK_W), _BF16)
    for e in range(PAIR):
        fwd_first = e == 0
        d0, d1 = (0, 1) if fwd_first else (1, 0)
        b_re = _dot_nt_exact(eye_i, jnp.concatenate([br_ref[d0, e], br_ref[d1, e]], axis=0))
        b_im = _dot_nt_exact(eye_i, jnp.concatenate([bi_ref[d0, e], bi_ref[d1, e]], axis=0))
        bbar_re, bbar_im = _cmul(f_re[e:e + 1], f_im[e:e + 1], b_re, b_im)
        fwd_lane = left1 if fwd_first else jnp.logical_not(left1)
        for s in range(T):
            pr = jnp.where(fwd_lane, pt_re[T - 1 - s][e:e + 1], pt_re[s][e:e + 1])
            pi = jnp.where(fwd_lane, pt_im[T - 1 - s][e:e + 1], pt_im[s][e:e + 1])
            w_re, w_im = _cmul(pr, pi, bbar_re, bbar_im)
            wst_ref[e, s * SSM_GROUP:(s + 1) * SSM_GROUP, 0:LANES] = w_re.astype(_BF16)
            wst_ref[e, s * SSM_GROUP:(s + 1) * SSM_GROUP, LANES:] = w_im.astype(_BF16)
        fwd_row = (rows < SSM_STATE) if fwd_first else (rows >= SSM_STATE)
        expo = jnp.where(fwd_row, tcol + 1, T // 2 - tcol)
        lo_re = jnp.ones((LANES, LANES), _F32)
        lo_im = jnp.zeros((LANES, LANES), _F32)
        s_re, s_im = cols[:, e:e + 1], cols[:, SUBLANES + e:SUBLANES + e + 1]
        for bit in range((T // 2).bit_length()):
            m_re, m_im = _cmul(lo_re, lo_im, jnp.broadcast_to(s_re, lo_re.shape), jnp.broadcast_to(s_im, lo_re.shape))
            take = ((expo >> bit) & 1) == 1
            lo_re, lo_im = jnp.where(take, m_re, lo_re), jnp.where(take, m_im, lo_im)
            if (1 << bit) < T // 2:
                s_re, s_im = _cmul(s_re, s_im, s_re, s_im)
        hi_re, hi_im = _cmul(lo_re, lo_im, jnp.broadcast_to(s_re, lo_re.shape), jnp.broadcast_to(s_im, lo_re.shape))
        q_re = jnp.concatenate([jnp.where(fwd_row, lo_re, hi_re), jnp.where(fwd_row, hi_re, lo_re)], axis=1)
        q_im = jnp.concatenate([jnp.where(fwd_row, lo_im, hi_im), jnp.where(fwd_row, hi_im, lo_im)], axis=1)
        tiled = lambda c: _dot_nt_exact(eye_p, c, repeat=T)
        c_re = jnp.concatenate([tiled(cr_ref[d0, e]), tiled(cr_ref[d1, e])], axis=0)
        c_im = jnp.concatenate([tiled(ci_ref[d0, e]), tiled(ci_ref[d1, e])], axis=0)
        o_re, o_im = _cmul(c_re, c_im, q_re, q_im)
        wout_ref[e, 0:LANES] = o_re.astype(_BF16)
        wout_ref[e, LANES:] = (-o_im).astype(_BF16)
        for ri, src in enumerate((c_re, -c_im)):
            for slot, d in enumerate((d0, d1)):
                r0 = ri * LANES + slot * SSM_STATE
                block = src[slot * SSM_STATE:(slot + 1) * SSM_STATE].astype(_BF16)
                cmat_ref[e, r0:r0 + SSM_STATE, d * CHUNK_W:(d + 1) * CHUNK_W] = block
                cmat_ref[e, r0:r0 + SSM_STATE, (1 - d) * CHUNK_W:(2 - d) * CHUNK_W] = zero_half

    m0_re, m1_re = pt_re[T][0:1], pt_re[T][1:2]
    m0_im, m1_im = pt_im[T][0:1], pt_im[T][1:2]
    cf = (jnp.where(left1, m0_re, m1_re), jnp.where(left1, m0_im, m1_im))
    cb = (jnp.where(left1, m1_re, m0_re), jnp.where(left1, m1_im, m0_im))
    coefs = []
    for _ in range(n_steps):
        coefs.append([cf[0], cf[1], cb[0], cb[1]])
        cf = _cmul(cf[0], cf[1], cf[0], cf[1])
        cb = _cmul(cb[0], cb[1], cb[0], cb[1])
    return coefs


def _ssm_kernel(n_lat, n_ctx, pairs_per_step, u_ref, uc_ref, are_ref, aim_ref, ldt_ref,
                br_ref, bi_ref, cr_ref, ci_ref, d_ref, y_ref, *scratch):
    programs = []
    for p in range(pairs_per_step):
        members = slice(p * PAIR, (p + 1) * PAIR)
        programs.append(_ssm_pair(
            n_lat, n_ctx, u_ref.at[members], uc_ref.at[members],
            tuple(r.at[:, p:p + 1] for r in (are_ref, aim_ref, ldt_ref)), br_ref.at[:, members], bi_ref.at[:, members], cr_ref.at[:, members], ci_ref.at[:, members],
            d_ref.at[members], y_ref.at[members], *[s.at[p] for s in scratch]))
    live = []
    while programs or live:
        if programs:
            live.append(programs.pop(0))
        for prog in list(live):
            if next(prog, "done") == "done":
                live.remove(prog)


def _ssm_pair(n_lat, n_ctx, u_ref, uc_ref, par_refs, br_ref, bi_ref, cr_ref, ci_ref, d_ref,
              y_ref, buf_a, buf_b, ends_a, ends_b, wst_ref, wout_ref, cmat_ref, m_scr):
    n_rows = n_lat + n_ctx
    pad = SUBLANES
    coefs = _ssm_prep(_scan_steps(n_rows), par_refs, br_ref, bi_ref, cr_ref, ci_ref,
                      wst_ref, wout_ref, cmat_ref)
    yield

    lane = lax.broadcasted_iota(jnp.int32, (SSM_GROUP, CHUNK_W), 1)
    blk = lane // SSM_GROUP
    on_diag = lane % SSM_GROUP == lax.broadcasted_iota(jnp.int32, (SSM_GROUP, CHUNK_W), 0)
    for e in range(PAIR):
        d_skip = jnp.where(on_diag, jnp.broadcast_to(d_ref[e], (SSM_GROUP, CHUNK_W)), 0.0)
        taps = _dot(wst_ref[e], cmat_ref[e])
        taps_f = taps[:, 0:CHUNK_W]
        taps_b = taps[:, CHUNK_W:]
        for s in range(CHUNK):
            acc = jnp.zeros((SSM_GROUP, CHUNK_W), _F32)
            for t in range(CHUNK):
                val = None
                if s >= t:
                    r0 = (s - t) * SSM_GROUP
                    val = taps_b[r0:r0 + SSM_GROUP]
                if s <= t:
                    r0 = (CHUNK - 1 - t + s) * SSM_GROUP
                    v2 = taps_f[r0:r0 + SSM_GROUP]
                    val = v2 if val is None else val + v2
                if s == t:
                    val = val + d_skip
                acc = jnp.where(blk == t, val, acc)
            m_scr[e, s * SSM_GROUP:(s + 1) * SSM_GROUP, :] = acc.astype(_BF16)
    yield

    def paired(m0, m1):
        left = lax.broadcasted_iota(jnp.int32, (m0.shape[0], PAIR_STATE), 1) < SSM_STATE
        re0, im0 = m0[:, 0:PAIR_STATE], m0[:, PAIR_STATE:]
        re1, im1 = m1[:, 0:PAIR_STATE], m1[:, PAIR_STATE:]
        return [jnp.where(left, re0, re1), jnp.where(left, im0, im1),
                jnp.where(left, re1, re0), jnp.where(left, im1, im0)]

    s_lat = paired(_dot(u_ref[0], wst_ref[0]), _dot(u_ref[1], wst_ref[1]))
    s_ctx = paired(_dot(uc_ref[0], wst_ref[0]), _dot(uc_ref[1], wst_ref[1]))
    yield

    n_tiles = n_rows // SUBLANES
    tile_row = lax.broadcasted_iota(jnp.int32, (1, SUBLANES, PAIR_STATE), 1)

    def tile_scan(x_re, x_im, comp0, backward):
        for k in range(SUBLANES.bit_length() - 1):
            sh = 1 << k
            shift, keep = (SUBLANES - sh, tile_row < SUBLANES - sh) if backward else (sh, tile_row >= sh)
            a_re = jnp.where(keep, coefs[k][comp0], 0.0)
            a_im = jnp.where(keep, coefs[k][comp0 + 1], 0.0)
            p_re, p_im = pltpu.roll(x_re, shift, axis=1), pltpu.roll(x_im, shift, axis=1)
            x_re, x_im = x_re + a_re * p_re - a_im * p_im, x_im + a_re * p_im + a_im * p_re
        return x_re, x_im

    as_tiles = lambda parts: jnp.concatenate(parts, axis=0).reshape(n_tiles, SUBLANES, PAIR_STATE)
    local = list(tile_scan(as_tiles([s_ctx[0], s_lat[0]]), as_tiles([s_ctx[1], s_lat[1]]), 0, False))
    local += tile_scan(as_tiles([s_lat[2], s_ctx[2]]), as_tiles([s_lat[3], s_ctx[3]]), 2, True)
    for comp in range(N_COMP):
        buf_a[comp, pad:pad + n_rows] = local[comp].reshape(n_rows, PAIR_STATE)
    yield

    for buf in (ends_a, ends_b):
        buf[...] = jnp.zeros(buf.shape, _F32)
    epad = _ends_pad(n_tiles)
    for comp in range(N_COMP):
        end_row = pad + (SUBLANES - 1 if comp < 2 else 0)
        ends_a[comp, epad:epad + n_tiles] = buf_a[comp, pl.ds(end_row, n_tiles, stride=SUBLANES), :]
    src, dst = ends_a, ends_b
    for k in range(_scan_steps(n_tiles)):
        sh = 1 << k
        coef = coefs[SUBLANES.bit_length() - 1 + k]
        lo, hi = epad, epad + n_tiles
        xr, xi = src[0, lo - sh:hi - sh], src[1, lo - sh:hi - sh]
        dst[0, lo:hi] = src[0, lo:hi] + coef[0] * xr - coef[1] * xi
        dst[1, lo:hi] = src[1, lo:hi] + coef[0] * xi + coef[1] * xr
        xr, xi = src[2, lo + sh:hi + sh], src[3, lo + sh:hi + sh]
        dst[2, lo:hi] = src[2, lo:hi] + coef[2] * xr - coef[3] * xi
        dst[3, lo:hi] = src[3, lo:hi] + coef[2] * xi + coef[3] * xr
        src, dst = dst, src

    def carry_powers(comp0, backward):
        base = [(coefs[k][comp0], coefs[k][comp0 + 1]) for k in range(SUBLANES.bit_length())]
        pw = []
        for r in range(1, SUBLANES + 1):
            acc = None
            for k, term in enumerate(base):
                if (r >> k) & 1:
                    acc = term if acc is None else _cmul(acc[0], acc[1], term[0], term[1])
            pw.append(acc)
        if backward:
            pw = pw[::-1]
        return (jnp.concatenate([p[0] for p in pw], axis=0), jnp.concatenate([p[1] for p in pw], axis=0))

    for comp0, backward in ((0, False), (2, True)):
        p_re, p_im = carry_powers(comp0, backward)
        first = epad + 1 if backward else epad - 1
        for i in range(n_tiles):
            c_re = src[comp0, first + i:first + i + 1]
            c_im = src[comp0 + 1, first + i:first + i + 1]
            rows = slice(pad + i * SUBLANES, pad + (i + 1) * SUBLANES)
            buf_b[comp0, rows] = buf_a[comp0, rows] + p_re * c_re - p_im * c_im
            buf_b[comp0 + 1, rows] = buf_a[comp0 + 1, rows] + p_re * c_im + p_im * c_re
    yield

    f0 = pad + n_ctx - 1
    b0 = pad + 1
    f_re, f_im = buf_b[0, f0:f0 + n_lat], buf_b[1, f0:f0 + n_lat]
    b_re, b_im = buf_b[2, b0:b0 + n_lat], buf_b[3, b0:b0 + n_lat]
    left = lax.broadcasted_iota(jnp.int32, (n_lat, PAIR_STATE), 1) < SSM_STATE
    h0 = jnp.concatenate([jnp.where(left, f_re, b_re), jnp.where(left, f_im, b_im)], axis=1)
    h1 = jnp.concatenate([jnp.where(left, b_re, f_re), jnp.where(left, b_im, f_im)], axis=1)
    y_ref[0] = _dot(u_ref[0], m_scr[0]) + _dot(h0.astype(_BF16), wout_ref[0])
    y_ref[1] = _dot(u_ref[1], m_scr[1]) + _dot(h1.astype(_BF16), wout_ref[1])


def _ssm(u_g, uc_g, a_re, a_im, log_dt, b_re, b_im, c_re, c_im, d_col):
    n_groups, n_lat, _ = u_g.shape
    n_ctx = uc_g.shape[1]
    n_rows = n_lat + n_ctx
    n_pairs = n_groups // PAIR
    pps = 2
    gps = pps * PAIR
    buf = pltpu.VMEM((pps, N_COMP, n_rows + 2 * SUBLANES, PAIR_STATE), _F32)
    n_tiles = n_rows // SUBLANES
    ends_rows = _ends_pad(n_tiles) + -(-(n_tiles + _ends_pad(n_tiles)) // SUBLANES) * SUBLANES
    ends = pltpu.VMEM((pps, N_COMP, ends_rows, PAIR_STATE), _F32)
    mat = pltpu.VMEM((pps, PAIR, CHUNK_W, CHUNK_W), _BF16)
    pair3 = lambda q: (q, 0, 0)
    by_dir = lambda q: (0, q, 0, 0)
    return pl.pallas_call(
        functools.partial(_ssm_kernel, n_lat, n_ctx, pps),
        grid=(n_pairs // pps,),
        in_specs=[pl.BlockSpec((gps, n_lat, CHUNK_W), pair3),
                  pl.BlockSpec((gps, n_ctx, CHUNK_W), pair3),
                  pl.BlockSpec((2, pps) + a_re.shape[2:], by_dir),
                  pl.BlockSpec((2, pps) + a_im.shape[2:], by_dir),
                  pl.BlockSpec((2, pps) + log_dt.shape[2:], by_dir),
                  pl.BlockSpec((2, gps) + b_re.shape[2:], by_dir),
                  pl.BlockSpec((2, gps) + b_im.shape[2:], by_dir),
                  pl.BlockSpec((2, gps) + c_re.shape[2:], by_dir),
                  pl.BlockSpec((2, gps) + c_im.shape[2:], by_dir),
                  pl.BlockSpec((gps,) + d_col.shape[1:], pair3)],
        out_specs=pl.BlockSpec((gps, n_lat, CHUNK_W), pair3),
        out_shape=jax.ShapeDtypeStruct((n_groups, n_lat, CHUNK_W), _F32),
        scratch_shapes=[buf, buf, ends, ends, mat, mat,
                        pltpu.VMEM((pps, PAIR, CHUNK_W, 2 * CHUNK_W), _BF16), mat],
        compiler_params=pltpu.CompilerParams(dimension_semantics=("arbitrary",),
                                             vmem_limit_bytes=VMEM_LIMIT),
        name="ssm",
    )(u_g, uc_g, a_re, a_im, log_dt, b_re, b_im, c_re, c_im, d_col)


def _out_kernel(x_ref, conv_ref, yg_ref, mods_ref, gssm_ref, g2_ref, gfin_ref,
                wglu32_ref, wout32_ref, w1_ref, w2_ref, o_ref, y_scr, mix_scr, wglu_ref, wout_ref, ones_ref):
    @pl.when(pl.program_id(0) == 0)
    def _():
        wglu_ref[...] = wglu32_ref[...].astype(_BF16)
        wout_ref[...] = wout32_ref[...].astype(_BF16)
        ones_ref[...] = _group_ones(MXU_W, SSM_GROUP)

    gt1 = mods_ref[0:1, 2 * D_MODEL:3 * D_MODEL]
    sh2 = mods_ref[0:1, 3 * D_MODEL:4 * D_MODEL]
    sc2 = mods_ref[0:1, 4 * D_MODEL:5 * D_MODEL]
    gt2 = mods_ref[0:1, 5 * D_MODEL:6 * D_MODEL]
    th = x_ref.shape[0] // 2
    n_split = 4
    ff = D_FF // n_split

    def mixer(r0):
        rows_ = slice(r0, r0 + th)
        chunks = slice(r0 // CHUNK, (r0 + th) // CHUNK)
        for q in range(N_SLABS):
            for half in range(CHUNK // BLOCKS):
                cols = [yg_ref[q * BLOCKS + g8, chunks, half * LANES:(half + 1) * LANES] for g8 in range(BLOCKS)]
                rows = _lane_block_transpose(cols)
                for t in range(BLOCKS):
                    y_scr[q, pl.ds(r0 + half * BLOCKS + t, th // CHUNK, stride=CHUNK), :] = rows[t]
        s = jax.nn.gelu(jnp.concatenate([y_scr[q, rows_] for q in range(N_SLABS)], axis=1))
        s = s * jax.nn.sigmoid(_dot(s.astype(_BF16), wglu_ref[...]))
        sq = (s * s).astype(_BF16)
        ssq = jnp.concatenate([_dot(sq[:, k * MXU_W:(k + 1) * MXU_W], ones_ref[...])
                               for k in range(SSM_W // MXU_W)], axis=1)
        sn = s * lax.rsqrt(ssq * (1.0 / SSM_GROUP) + RMS_EPS) * gssm_ref[...]
        mix = mix_scr[rows_] + _dot(sn.astype(_BF16), wout_ref[CONV_W:])
        x_mid = x_ref[rows_, :] + gt1 * mix
        return x_mid, (_rms_rows(x_mid, g2_ref[...]) * (1.0 + sc2) + sh2).astype(_BF16)

    def mlp_chunk(h2, j):
        a = jnp.maximum(_dot(h2, w1_ref[:, j * ff:(j + 1) * ff]), 0.0)
        return _dot((a * a).astype(_BF16), w2_ref[j * ff:(j + 1) * ff])

    def finish(r0, x_mid, acc):
        o_ref[r0:r0 + th, :] = _rms_rows(x_mid + gt2 * acc, gfin_ref[...])

    mix_scr[...] = _dot(conv_ref[...], wout_ref[0:CONV_W])
    x_a, h_a = mixer(0)
    acc_a = mlp_chunk(h_a, 0)
    x_b, h_b = mixer(th)
    for j in range(1, n_split):
        acc_a = acc_a + mlp_chunk(h_a, j)
    acc_b = mlp_chunk(h_b, 0)
    finish(0, x_a, acc_a)
    for j in range(1, n_split):
        acc_b = acc_b + mlp_chunk(h_b, j)
    finish(th, x_b, acc_b)


def _out_proj(x2d, conv_n, y_g, mods, g_ssm, g2, g_fin, wglu, wout, w1, w2, tm):
    n = x2d.shape[0]
    const = lambda i: (0, 0)
    row = lambda i: (i, 0)
    resident = lambda shape: pl.BlockSpec(shape, const, pipeline_mode=pl.Buffered(1))
    return pl.pallas_call(
        _out_kernel,
        grid=(n // tm,),
        in_specs=[pl.BlockSpec((tm, D_MODEL), row),
                  pl.BlockSpec((tm, CONV_W), row),
                  pl.BlockSpec((SSM_GROUPS, tm // CHUNK, CHUNK_W), lambda i: (0, i, 0)),
                  pl.BlockSpec(mods.shape, const),
                  pl.BlockSpec((1, SSM_W), const),
                  pl.BlockSpec((1, D_MODEL), const),
                  pl.BlockSpec((1, D_MODEL), const),
                  resident((SSM_W, SSM_W)),
                  resident((D_MODEL, D_MODEL)),
                  resident((D_MODEL, D_FF)),
                  resident((D_FF, D_MODEL))],
        out_specs=pl.BlockSpec((tm, D_MODEL), row),
        out_shape=jax.ShapeDtypeStruct((n, D_MODEL), _F32),
        scratch_shapes=[pltpu.VMEM((N_SLABS, tm, LANES), _F32), pltpu.VMEM((tm, D_MODEL), _F32),
                        pltpu.VMEM((SSM_W, SSM_W), _BF16), pltpu.VMEM((D_MODEL, D_MODEL), _BF16),
                        pltpu.VMEM((MXU_W, MXU_W), _BF16)],
        compiler_params=pltpu.CompilerParams(dimension_semantics=("arbitrary",),
                                             vmem_limit_bytes=VMEM_LIMIT),
        name="out_proj",
    )(x2d, conv_n, y_g, mods, g_ssm, g2, g_fin, wglu, wout, w1, w2)


def kernel(x, c, ctx, c_ctx, w_mod, b_mod, g_norm1, w_in, conv_w, ssm_a_re, ssm_a_im, ssm_log_dt,
           ssm_b_re, ssm_b_im, ssm_c_re, ssm_c_im, ssm_d, w_glu, g_conv_out, g_ssm_out, w_out,
           g_norm2, w_mlp1, w_mlp2, g_final):
    bsz, n_lat, d_model = x.shape
    n_ctx = ctx.shape[1]
    assert bsz == 1 and d_model == D_MODEL and w_mod.shape[0] == 1
    assert n_lat % (CHUNK * SUBLANES) == 0 and n_ctx % (CHUNK * CHUNK) == 0 and n_lat % GRID_W == 0
    layer = 0
    x2d = x[0]
    ctx2d = ctx[0]

    mods = _modulation(c, c_ctx[None, :], w_mod[layer], b_mod[layer][None, :])

    g1 = g_norm1[layer][None, :]
    g_conv = g_conv_out[layer][None, :]
    conv_n, u_g, uc_g, w1_bf, w2_bf = _in_proj(x2d, ctx2d, mods, g1, w_in[layer], conv_w[layer], g_conv, tm=1024,
                                                side_weights=(w_mlp1[layer], w_mlp2[layer]))

    n_pairs = SSM_GROUPS // PAIR
    y_g = _ssm(u_g, uc_g,
               ssm_a_re.reshape(2, n_pairs, 1, PAIR_STATE), ssm_a_im.reshape(2, n_pairs, 1, PAIR_STATE),
               ssm_log_dt.reshape(2, n_pairs, 1, PAIR),
               ssm_b_re[layer], ssm_b_im[layer], ssm_c_re[layer], ssm_c_im[layer],
               ssm_d.reshape(SSM_GROUPS, SSM_GROUP, 1))

    out = _out_proj(x2d, conv_n, y_g, mods, g_ssm_out[layer][None, :], g_norm2[layer][None, :],
                    g_final[None, :], w_glu[layer], w_out[layer], w1_bf, w2_bf, tm=1024)
    return out[None]
```

```python
import functools
import math

import jax
import jax.numpy as jnp
from jax import lax
from jax.experimental import pallas as pl
from jax.experimental.pallas import tpu as pltpu

D_MODEL = 1024
GRID_W = 64
CONV_W = 512
CONV_HEADS = 8
SSM_W = 512
SSM_GROUP = 16
SSM_GROUPS = 32
SSM_STATE = 64
IN_W = 3 * CONV_W + SSM_W
D_FF = 4 * D_MODEL
RMS_EPS = 1e-6

LANES = 128
MXU_W = 256
SUBLANES = 8
CHUNK = 16
CHUNK_W = CHUNK * SSM_GROUP
PAIR = 2
PAIR_STATE = PAIR * SSM_STATE
N_COMP = 4
BLOCKS = LANES // SSM_GROUP
N_SLABS = SSM_W // LANES
VMEM_LIMIT = 62 * 1024 * 1024

_BF16 = jnp.bfloat16
_F32 = jnp.float32


def _dot(a, b):
    return jnp.dot(a, b, preferred_element_type=_F32)


def _lane_block_transpose(vs):
    blk = lax.broadcasted_iota(jnp.int32, vs[0].shape, 1) // SSM_GROUP
    dist = BLOCKS // 2
    while dist:
        shift = dist * SSM_GROUP
        upper = (blk & dist) != 0
        out = [None] * BLOCKS
        for a in range(BLOCKS):
            if a & dist:
                continue
            lo, hi = vs[a], vs[a + dist]
            out[a] = jnp.where(upper, pltpu.roll(hi, shift, axis=1), lo)
            out[a + dist] = jnp.where(upper, hi, pltpu.roll(lo, LANES - shift, axis=1))
        vs = out
        dist //= 2
    return vs


def _mod_kernel(c_ref, cctx_ref, w_ref, b_ref, o_ref):
    tk = w_ref.shape[0]
    first = lax.broadcasted_iota(jnp.int32, (SUBLANES, tk), 0) == 0
    s = jnp.where(first, jnp.broadcast_to(c_ref[...], first.shape), jnp.broadcast_to(cctx_ref[...], first.shape))
    act = s * jax.nn.sigmoid(s)
    w = w_ref[...]
    a_hi, w_hi = act.astype(_BF16), w.astype(_BF16)
    a_lo = (act - a_hi.astype(_F32)).astype(_BF16)
    w_lo = (w - w_hi.astype(_F32)).astype(_BF16)
    by_hi = _dot(jnp.concatenate([a_hi, a_lo], axis=0), w_hi)
    part = by_hi[0:SUBLANES] + by_hi[SUBLANES:] + _dot(a_hi, w_lo)

    @pl.when(pl.program_id(0) == 0)
    def _():
        o_ref[...] = part + b_ref[...]

    @pl.when(pl.program_id(0) > 0)
    def _():
        o_ref[...] += part


def _modulation(c_row, cctx_row, w_mod, b_mod):
    d_in, n_out = w_mod.shape
    tk = 256
    return pl.pallas_call(
        _mod_kernel,
        grid=(d_in // tk,),
        in_specs=[pl.BlockSpec((1, tk), lambda k: (0, k)),
                  pl.BlockSpec((1, tk), lambda k: (0, k)),
                  pl.BlockSpec((tk, n_out), lambda k: (k, 0)),
                  pl.BlockSpec((1, n_out), lambda k: (0, 0))],
        out_specs=pl.BlockSpec((SUBLANES, n_out), lambda k: (0, 0)),
        out_shape=jax.ShapeDtypeStruct((SUBLANES, n_out), _F32),
        compiler_params=pltpu.CompilerParams(dimension_semantics=("arbitrary",),
                                             vmem_limit_bytes=VMEM_LIMIT),
        name="mod",
    )(c_row, cctx_row, w_mod, b_mod)


def _rms_rows(x, g):
    ms = jnp.mean(x * x, axis=-1, keepdims=True)
    return x * lax.rsqrt(ms + RMS_EPS) * g


def _group_ones(width, group):
    rows = lax.broadcasted_iota(jnp.int32, (width, width), 0) // group
    cols = lax.broadcasted_iota(jnp.int32, (width, width), 1) // group
    return jnp.where(rows == cols, 1.0, 0.0).astype(_BF16)


def _in_kernel(n_side, x_ref, ctx_ref, mods_ref, g1_ref, w_in_ref, convw_ref, gconv_ref, *refs):
    side_in, (conv_ref, ug_ref, ucg_ref) = refs[:n_side], refs[n_side:n_side + 3]
    side_out = refs[n_side + 3:2 * n_side + 3]
    u_scr, w_scr, ones_scr = refs[2 * n_side + 3:]
    step = pl.program_id(0)
    last = pl.num_programs(0) - 1

    @pl.when(step == 0)
    def _():
        w_scr[:, 0:SSM_W] = w_in_ref[:, 3 * CONV_W:].astype(_BF16)
        for k in range(CONV_W // MXU_W):
            for part in range(3):
                dst = SSM_W + (3 * k + part) * MXU_W
                src = part * CONV_W + k * MXU_W
                w_scr[:, dst:dst + MXU_W] = w_in_ref[:, src:src + MXU_W].astype(_BF16)
        ones_scr[...] = _group_ones(MXU_W, CONV_W // CONV_HEADS)

    def normed(rows_ref, mod_row):
        sh1 = mods_ref[mod_row:mod_row + 1, 0:D_MODEL]
        sc1 = mods_ref[mod_row:mod_row + 1, D_MODEL:2 * D_MODEL]
        return (_rms_rows(rows_ref[...], g1_ref[...] * (1.0 + sc1)) + sh1).astype(_BF16)

    def to_group_major(z_u, out_ref, r0):
        n_rows = z_u.shape[0]
        chunks = slice(r0 // CHUNK, (r0 + n_rows) // CHUNK)
        for q in range(N_SLABS):
            u_scr[q, r0:r0 + n_rows] = z_u[:, q * LANES:(q + 1) * LANES]
        for q in range(N_SLABS):
            for half in range(CHUNK // BLOCKS):
                rows = [u_scr[q, pl.ds(r0 + half * BLOCKS + t, n_rows // CHUNK, stride=CHUNK), :]
                        for t in range(BLOCKS)]
                cols = _lane_block_transpose(rows)
                for g8 in range(BLOCKS):
                    out_ref[q * BLOCKS + g8, chunks, half * LANES:(half + 1) * LANES] = cols[g8].astype(_BF16)

    def conv_branch(z, r0):
        n_rows = z.shape[0]
        pos = lax.broadcasted_iota(jnp.int32, (n_rows, MXU_W), 0) % GRID_W
        for k in range(CONV_W // MXU_W):
            c0 = SSM_W + 3 * k * MXU_W
            lanes = slice(k * MXU_W, (k + 1) * MXU_W)
            b = z[:, c0:c0 + MXU_W]
            cv = z[:, c0 + MXU_W:c0 + 2 * MXU_W] * z[:, c0 + 2 * MXU_W:c0 + 3 * MXU_W]
            prev = jnp.where(pos == 0, 0.0, pltpu.roll(cv, 1, axis=0))
            nxt = jnp.where(pos == GRID_W - 1, 0.0, pltpu.roll(cv, n_rows - 1, axis=0))
            y = b * (convw_ref[0:1, lanes] * prev + convw_ref[1:2, lanes] * cv + convw_ref[2:3, lanes] * nxt)
            ssq = _dot((y * y).astype(_BF16), ones_scr[...])
            yn = y * lax.rsqrt(ssq * (1.0 / (CONV_W // CONV_HEADS)) + RMS_EPS) * gconv_ref[:, lanes]
            conv_ref[r0:r0 + n_rows, lanes] = yn.astype(_BF16)

    @pl.when(step < last)
    def _():
        for src_ref, dst_ref in zip(side_in, side_out):
            dst_ref[...] = src_ref[...].astype(_BF16)
        th = x_ref.shape[0] // 2
        z = [_dot(normed(x_ref.at[r0:r0 + th], 0), w_scr[...]) for r0 in (0, th)]
        for half, r0 in enumerate((0, th)):
            to_group_major(z[half][:, 0:SSM_W], ug_ref, r0)
            conv_branch(z[half], r0)

    @pl.when(step == last)
    def _():
        to_group_major(_dot(normed(ctx_ref, 1), w_scr[:, 0:SSM_W]), ucg_ref, 0)


def _in_proj(x2d, ctx2d, mods, g1, w_in, conv_w, g_conv, tm, side_weights):
    n, n_ctx = x2d.shape[0], ctx2d.shape[0]
    steps = n // tm
    const = lambda i: (0, 0)
    row = lambda i: (jnp.minimum(i, steps - 1), 0)
    side_specs = [pl.BlockSpec((w.shape[0] // steps, w.shape[1]), row) for w in side_weights]
    return pl.pallas_call(
        functools.partial(_in_kernel, len(side_weights)),
        grid=(steps + 1,),
        in_specs=[pl.BlockSpec((tm, D_MODEL), row),
                  pl.BlockSpec((n_ctx, D_MODEL), const),
                  pl.BlockSpec(mods.shape, const),
                  pl.BlockSpec((1, D_MODEL), const),
                  pl.BlockSpec((D_MODEL, IN_W), const, pipeline_mode=pl.Buffered(1)),
                  pl.BlockSpec((3, CONV_W), const),
                  pl.BlockSpec((1, CONV_W), const)] + side_specs,
        out_specs=[pl.BlockSpec((tm, CONV_W), row),
                   pl.BlockSpec((SSM_GROUPS, tm // CHUNK, CHUNK_W), lambda i: (0, jnp.minimum(i, steps - 1), 0)),
                   pl.BlockSpec((SSM_GROUPS, n_ctx // CHUNK, CHUNK_W), lambda i: (0, 0, 0))] + side_specs,
        out_shape=[jax.ShapeDtypeStruct((n, CONV_W), _BF16),
                   jax.ShapeDtypeStruct((SSM_GROUPS, n // CHUNK, CHUNK_W), _BF16),
                   jax.ShapeDtypeStruct((SSM_GROUPS, n_ctx // CHUNK, CHUNK_W), _BF16)]
                  + [jax.ShapeDtypeStruct(w.shape, _BF16) for w in side_weights],
        scratch_shapes=[pltpu.VMEM((N_SLABS, tm, LANES), _F32),
                        pltpu.VMEM((D_MODEL, IN_W), _BF16),
                        pltpu.VMEM((MXU_W, MXU_W), _BF16)],
        compiler_params=pltpu.CompilerParams(dimension_semantics=("arbitrary",),
                                             vmem_limit_bytes=VMEM_LIMIT),
        name="in_proj",
    )(x2d, ctx2d, mods, g1, w_in, conv_w, g_conv, *side_weights)


def _scan_steps(n_rows):
    return max(1, math.ceil(math.log2(n_rows)))


def _ends_pad(n_tiles):
    return max(SUBLANES, 1 << (_scan_steps(n_tiles) - 1))


def _cmul(ar, ai, br, bi):
    return ar * br - ai * bi, ar * bi + ai * br


def _eye(n):
    rows, cols = (lax.broadcasted_iota(jnp.int32, (n, n), axis) for axis in (0, 1))
    return jnp.where(rows == cols, 1.0, 0.0).astype(_BF16)


def _dot_nt_exact(eye, x, repeat=1):
    hi = x.astype(_BF16)
    rest = x - hi.astype(_F32)
    mid = rest.astype(_BF16)
    lo = (rest - mid.astype(_F32)).astype(_BF16)
    nt = lambda term: lax.dot_general(eye, jnp.concatenate([term] * repeat, axis=0),
                                      (((1,), (1,)), ((), ())), preferred_element_type=_F32)
    return nt(hi) + nt(mid) + nt(lo)


def _ssm_prep(n_steps, par_refs, br_ref, bi_ref, cr_ref, ci_ref, wst_ref, wout_ref, cmat_ref):
    T = CHUNK
    left8 = lax.broadcasted_iota(jnp.int32, (SUBLANES, LANES), 1) < SSM_STATE
    row0 = lax.broadcasted_iota(jnp.int32, (SUBLANES, LANES), 0) == 0

    def member_rows(vf, vb):
        f8 = jnp.broadcast_to(vf, (SUBLANES, LANES))
        b8 = pltpu.roll(jnp.broadcast_to(vb, (SUBLANES, LANES)), SSM_STATE, axis=1)
        return jnp.where(row0, jnp.where(left8, f8, b8), jnp.where(left8, b8, f8))

    are_ref, aim_ref, ldt_ref = par_refs
    in_first = lax.broadcasted_iota(jnp.int32, (1, LANES), 1) < SSM_STATE
    dt_lanes = lambda d: jnp.where(in_first, ldt_ref[d, 0][:, 0:1], ldt_ref[d, 0][:, 1:2])
    lr = jnp.minimum(member_rows(are_ref[0, 0], are_ref[1, 0]), -1e-4)
    li = member_rows(aim_ref[0, 0], aim_ref[1, 0])
    dt = jnp.exp(member_rows(dt_lanes(0), dt_lanes(1)))
    mag = jnp.exp(lr * dt)
    ab_re = mag * jnp.cos(li * dt)
    ab_im = mag * jnp.sin(li * dt)
    nr = ab_re - 1.0
    den = lr * lr + li * li
    f_re = (nr * lr + ab_im * li) / den
    f_im = (ab_im * lr - nr * li) / den
    pt_re, pt_im = [jnp.ones_like(ab_re)], [jnp.zeros_like(ab_im)]
    for _ in range(T):
        nxt = _cmul(pt_re[-1], pt_im[-1], ab_re, ab_im)
        pt_re.append(nxt[0])
        pt_im.append(nxt[1])

    cols = jnp.concatenate([ab_re, ab_im, jnp.zeros((LANES - 2 * SUBLANES, LANES), _F32)], axis=0).T
    left1 = lax.broadcasted_iota(jnp.int32, (1, LANES), 1) < SSM_STATE
    eye_i = _eye(SSM_GROUP)
    eye_p = _eye(SSM_STATE)
    rows = lax.broadcasted_iota(jnp.int32, (LANES, LANES), 0)
    tcol = lax.broadcasted_iota(jnp.int32, (LANES, LANES), 1) // SSM_GROUP
    zero_half = jnp.zeros((SSM_STATE, CHUNK_W), _BF16)
    for e in range(PAIR):
        fwd_first = e == 0
        d0, d1 = (0, 1) if fwd_first else (1, 0)
        b_re = _dot_nt_exact(eye_i, jnp.concatenate([br_ref[d0, e], br_ref[d1, e]], axis=0))
        b_im = _dot_nt_exact(eye_i, jnp.concatenate([bi_ref[d0, e], bi_ref[d1, e]], axis=0))
        bbar_re, bbar_im = _cmul(f_re[e:e + 1], f_im[e:e + 1], b_re, b_im)
        fwd_lane = left1 if fwd_first else jnp.logical_not(left1)
        for s in range(T):
            pr = jnp.where(fwd_lane, pt_re[T - 1 - s][e:e + 1], pt_re[s][e:e + 1])
            pi = jnp.where(fwd_lane, pt_im[T - 1 - s][e:e + 1], pt_im[s][e:e + 1])
            w_re, w_im = _cmul(pr, pi, bbar_re, bbar_im)
            wst_ref[e, s * SSM_GROUP:(s + 1) * SSM_GROUP, 0:LANES] = w_re.astype(_BF16)
            wst_ref[e, s * SSM_GROUP:(s + 1) * SSM_GROUP, LANES:] = w_im.astype(_BF16)
        fwd_row = (rows < SSM_STATE) if fwd_first else (rows >= SSM_STATE)
        expo = jnp.where(fwd_row, tcol + 1, T // 2 - tcol)
        lo_re = jnp.ones((LANES, LANES), _F32)
        lo_im = jnp.zeros((LANES, LANES), _F32)
        s_re, s_im = cols[:, e:e + 1], cols[:, SUBLANES + e:SUBLANES + e + 1]
        for bit in range((T // 2).bit_length()):
            m_re, m_im = _cmul(lo_re, lo_im, jnp.broadcast_to(s_re, lo_re.shape), jnp.broadcast_to(s_im, lo_re.shape))
            take = ((expo >> bit) & 1) == 1
            lo_re, lo_im = jnp.where(take, m_re, lo_re), jnp.where(take, m_im, lo_im)
            if (1 << bit) < T // 2:
                s_re, s_im = _cmul(s_re, s_im, s_re, s_im)
        hi_re, hi_im = _cmul(lo_re, lo_im, jnp.broadcast_to(s_re, lo_re.shape), jnp.broadcast_to(s_im, lo_re.shape))
        q_re = jnp.concatenate([jnp.where(fwd_row, lo_re, hi_re), jnp.where(fwd_row, hi_re, lo_re)], axis=1)
        q_im = jnp.concatenate([jnp.where(fwd_row, lo_im, hi_im), jnp.where(fwd_row, hi_im, lo_im)], axis=1)
        tiled = lambda c: _dot_nt_exact(eye_p, c, repeat=T)
        c_re = jnp.concatenate([tiled(cr_ref[d0, e]), tiled(cr_ref[d1, e])], axis=0)
        c_im = jnp.concatenate([tiled(ci_ref[d0, e]), tiled(ci_ref[d1, e])], axis=0)
        o_re, o_im = _cmul(c_re, c_im, q_re, q_im)
        wout_ref[e, 0:LANES] = o_re.astype(_BF16)
        wout_ref[e, LANES:] = (-o_im).astype(_BF16)
        for ri, src in enumerate((c_re, -c_im)):
            for slot, d in enumerate((d0, d1)):
                r0 = ri * LANES + slot * SSM_STATE
                block = src[slot * SSM_STATE:(slot + 1) * SSM_STATE].astype(_BF16)
                cmat_ref[e, r0:r0 + SSM_STATE, d * CHUNK_W:(d + 1) * CHUNK_W] = block
                cmat_ref[e, r0:r0 + SSM_STATE, (1 - d) * CHUNK_W:(2 - d) * CHUNK_W] = zero_half

    m0_re, m1_re = pt_re[T][0:1], pt_re[T][1:2]
    m0_im, m1_im = pt_im[T][0:1], pt_im[T][1:2]
    cf = (jnp.where(left1, m0_re, m1_re), jnp.where(left1, m0_im, m1_im))
    cb = (jnp.where(left1, m1_re, m0_re), jnp.where(left1, m1_im, m0_im))
    coefs = []
    for _ in range(n_steps):
        coefs.append([cf[0], cf[1], cb[0], cb[1]])
        cf = _cmul(cf[0], cf[1], cf[0], cf[1])
        cb = _cmul(cb[0], cb[1], cb[0], cb[1])
    return coefs


def _ssm_kernel(n_lat, n_ctx, pairs_per_step, u_ref, uc_ref, are_ref, aim_ref, ldt_ref,
                br_ref, bi_ref, cr_ref, ci_ref, d_ref, y_ref, *scratch):
    programs = []
    for p in range(pairs_per_step):
        members = slice(p * PAIR, (p + 1) * PAIR)
        programs.append(_ssm_pair(
            n_lat, n_ctx, u_ref.at[members], uc_ref.at[members],
            tuple(r.at[:, p:p + 1] for r in (are_ref, aim_ref, ldt_ref)), br_ref.at[:, members], bi_ref.at[:, members], cr_ref.at[:, members], ci_ref.at[:, members],
            d_ref.at[members], y_ref.at[members], *[s.at[p] for s in scratch]))
    live = []
    while programs or live:
        if programs:
            live.append(programs.pop(0))
        for prog in list(live):
            if next(prog, "done") == "done":
                live.remove(prog)


def _ssm_pair(n_lat, n_ctx, u_ref, uc_ref, par_refs, br_ref, bi_ref, cr_ref, ci_ref, d_ref,
              y_ref, buf_a, buf_b, ends_a, ends_b, wst_ref, wout_ref, cmat_ref, m_scr):
    n_rows = n_lat + n_ctx
    pad = SUBLANES
    coefs = _ssm_prep(_scan_steps(n_rows), par_refs, br_ref, bi_ref, cr_ref, ci_ref,
                      wst_ref, wout_ref, cmat_ref)
    yield

    lane = lax.broadcasted_iota(jnp.int32, (SSM_GROUP, CHUNK_W), 1)
    blk = lane // SSM_GROUP
    on_diag = lane % SSM_GROUP == lax.broadcasted_iota(jnp.int32, (SSM_GROUP, CHUNK_W), 0)
    for e in range(PAIR):
        d_skip = jnp.where(on_diag, jnp.broadcast_to(d_ref[e], (SSM_GROUP, CHUNK_W)), 0.0)
        taps = _dot(wst_ref[e], cmat_ref[e])
        taps_f = taps[:, 0:CHUNK_W]
        taps_b = taps[:, CHUNK_W:]
        for s in range(CHUNK):
            acc = jnp.zeros((SSM_GROUP, CHUNK_W), _F32)
            for t in range(CHUNK):
                val = None
                if s >= t:
                    r0 = (s - t) * SSM_GROUP
                    val = taps_b[r0:r0 + SSM_GROUP]
                if s <= t:
                    r0 = (CHUNK - 1 - t + s) * SSM_GROUP
                    v2 = taps_f[r0:r0 + SSM_GROUP]
                    val = v2 if val is None else val + v2
                if s == t:
                    val = val + d_skip
                acc = jnp.where(blk == t, val, acc)
            m_scr[e, s * SSM_GROUP:(s + 1) * SSM_GROUP, :] = acc.astype(_BF16)
    yield

    def paired(m0, m1):
        left = lax.broadcasted_iota(jnp.int32, (m0.shape[0], PAIR_STATE), 1) < SSM_STATE
        re0, im0 = m0[:, 0:PAIR_STATE], m0[:, PAIR_STATE:]
        re1, im1 = m1[:, 0:PAIR_STATE], m1[:, PAIR_STATE:]
        return [jnp.where(left, re0, re1), jnp.where(left, im0, im1),
                jnp.where(left, re1, re0), jnp.where(left, im1, im0)]

    s_lat = paired(_dot(u_ref[0], wst_ref[0]), _dot(u_ref[1], wst_ref[1]))
    s_ctx = paired(_dot(uc_ref[0], wst_ref[0]), _dot(uc_ref[1], wst_ref[1]))
    yield

    n_tiles = n_rows // SUBLANES
    tile_row = lax.broadcasted_iota(jnp.int32, (1, SUBLANES, PAIR_STATE), 1)

    def tile_scan(x_re, x_im, comp0, backward):
        for k in range(SUBLANES.bit_length() - 1):
            sh = 1 << k
            shift, keep = (SUBLANES - sh, tile_row < SUBLANES - sh) if backward else (sh, tile_row >= sh)
            a_re = jnp.where(keep, coefs[k][comp0], 0.0)
            a_im = jnp.where(keep, coefs[k][comp0 + 1], 0.0)
            p_re, p_im = pltpu.roll(x_re, shift, axis=1), pltpu.roll(x_im, shift, axis=1)
            x_re, x_im = x_re + a_re * p_re - a_im * p_im, x_im + a_re * p_im + a_im * p_re
        return x_re, x_im

    as_tiles = lambda parts: jnp.concatenate(parts, axis=0).reshape(n_tiles, SUBLANES, PAIR_STATE)
    local = list(tile_scan(as_tiles([s_ctx[0], s_lat[0]]), as_tiles([s_ctx[1], s_lat[1]]), 0, False))
    local += tile_scan(as_tiles([s_lat[2], s_ctx[2]]), as_tiles([s_lat[3], s_ctx[3]]), 2, True)
    for comp in range(N_COMP):
        buf_a[comp, pad:pad + n_rows] = local[comp].reshape(n_rows, PAIR_STATE)
    yield

    for buf in (ends_a, ends_b):
        buf[...] = jnp.zeros(buf.shape, _F32)
    epad = _ends_pad(n_tiles)
    for comp in range(N_COMP):
        end_row = pad + (SUBLANES - 1 if comp < 2 else 0)
        ends_a[comp, epad:epad + n_tiles] = buf_a[comp, pl.ds(end_row, n_tiles, stride=SUBLANES), :]
    src, dst = ends_a, ends_b
    for k in range(_scan_steps(n_tiles)):
        sh = 1 << k
        coef = coefs[SUBLANES.bit_length() - 1 + k]
        lo, hi = epad, epad + n_tiles
        xr, xi = src[0, lo - sh:hi - sh], src[1, lo - sh:hi - sh]
        dst[0, lo:hi] = src[0, lo:hi] + coef[0] * xr - coef[1] * xi
        dst[1, lo:hi] = src[1, lo:hi] + coef[0] * xi + coef[1] * xr
        xr, xi = src[2, lo + sh:hi + sh], src[3, lo + sh:hi + sh]
        dst[2, lo:hi] = src[2, lo:hi] + coef[2] * xr - coef[3] * xi
        dst[3, lo:hi] = src[3, lo:hi] + coef[2] * xi + coef[3] * xr
        src, dst = dst, src

    def carry_powers(comp0, backward):
        base = [(coefs[k][comp0], coefs[k][comp0 + 1]) for k in range(SUBLANES.bit_length())]
        pw = []
        for r in range(1, SUBLANES + 1):
            acc = None
            for k, term in enumerate(base):
                if (r >> k) & 1:
                    acc = term if acc is None else _cmul(acc[0], acc[1], term[0], term[1])
            pw.append(acc)
        if backward:
            pw = pw[::-1]
        return (jnp.concatenate([p[0] for p in pw], axis=0), jnp.concatenate([p[1] for p in pw], axis=0))

    for comp0, backward in ((0, False), (2, True)):
        p_re, p_im = carry_powers(comp0, backward)
        first = epad + 1 if backward else epad - 1
        for i in range(n_tiles):
            c_re = src[comp0, first + i:first + i + 1]
            c_im = src[comp0 + 1, first + i:first + i + 1]
            rows = slice(pad + i * SUBLANES, pad + (i + 1) * SUBLANES)
            buf_b[comp0, rows] = buf_a[comp0, rows] + p_re * c_re - p_im * c_im
            buf_b[comp0 + 1, rows] = buf_a[comp0 + 1, rows] + p_re * c_im + p_im * c_re
    yield

    f0 = pad + n_ctx - 1
    b0 = pad + 1
    f_re, f_im = buf_b[0, f0:f0 + n_lat], buf_b[1, f0:f0 + n_lat]
    b_re, b_im = buf_b[2, b0:b0 + n_lat], buf_b[3, b0:b0 + n_lat]
    left = lax.broadcasted_iota(jnp.int32, (n_lat, PAIR_STATE), 1) < SSM_STATE
    h0 = jnp.concatenate([jnp.where(left, f_re, b_re), jnp.where(left, f_im, b_im)], axis=1)
    h1 = jnp.concatenate([jnp.where(left, b_re, f_re), jnp.where(left, b_im, f_im)], axis=1)
    y_ref[0] = _dot(u_ref[0], m_scr[0]) + _dot(h0.astype(_BF16), wout_ref[0])
    y_ref[1] = _dot(u_ref[1], m_scr[1]) + _dot(h1.astype(_BF16), wout_ref[1])


def _ssm(u_g, uc_g, a_re, a_im, log_dt, b_re, b_im, c_re, c_im, d_col):
    n_groups, n_lat, _ = u_g.shape
    n_ctx = uc_g.shape[1]
    n_rows = n_lat + n_ctx
    n_pairs = n_groups // PAIR
    pps = 2
    gps = pps * PAIR
    buf = pltpu.VMEM((pps, N_COMP, n_rows + 2 * SUBLANES, PAIR_STATE), _F32)
    n_tiles = n_rows // SUBLANES
    ends_rows = _ends_pad(n_tiles) + -(-(n_tiles + _ends_pad(n_tiles)) // SUBLANES) * SUBLANES
    ends = pltpu.VMEM((pps, N_COMP, ends_rows, PAIR_STATE), _F32)
    mat = pltpu.VMEM((pps, PAIR, CHUNK_W, CHUNK_W), _BF16)
    pair3 = lambda q: (q, 0, 0)
    by_dir = lambda q: (0, q, 0, 0)
    return pl.pallas_call(
        functools.partial(_ssm_kernel, n_lat, n_ctx, pps),
        grid=(n_pairs // pps,),
        in_specs=[pl.BlockSpec((gps, n_lat, CHUNK_W), pair3),
                  pl.BlockSpec((gps, n_ctx, CHUNK_W), pair3),
                  pl.BlockSpec((2, pps) + a_re.shape[2:], by_dir),
                  pl.BlockSpec((2, pps) + a_im.shape[2:], by_dir),
                  pl.BlockSpec((2, pps) + log_dt.shape[2:], by_dir),
                  pl.BlockSpec((2, gps) + b_re.shape[2:], by_dir),
                  pl.BlockSpec((2, gps) + b_im.shape[2:], by_dir),
                  pl.BlockSpec((2, gps) + c_re.shape[2:], by_dir),
                  pl.BlockSpec((2, gps) + c_im.shape[2:], by_dir),
                  pl.BlockSpec((gps,) + d_col.shape[1:], pair3)],
        out_specs=pl.BlockSpec((gps, n_lat, CHUNK_W), pair3),
        out_shape=jax.ShapeDtypeStruct((n_groups, n_lat, CHUNK_W), _F32),
        scratch_shapes=[buf, buf, ends, ends, mat, mat,
                        pltpu.VMEM((pps, PAIR, CHUNK_W, 2 * CHUNK_W), _BF16), mat],
        compiler_params=pltpu.CompilerParams(dimension_semantics=("arbitrary",),
                                             vmem_limit_bytes=VMEM_LIMIT),
        name="ssm",
    )(u_g, uc_g, a_re, a_im, log_dt, b_re, b_im, c_re, c_im, d_col)


def _out_kernel(x_ref, conv_ref, yg_ref, mods_ref, gssm_ref, g2_ref, gfin_ref,
                wglu32_ref, wout32_ref, w1_ref, w2_ref, o_ref, y_scr, mix_scr, wglu_ref, wout_ref, ones_ref):
    @pl.when(pl.program_id(0) == 0)
    def _():
        wglu_ref[...] = wglu32_ref[...].astype(_BF16)
        wout_ref[...] = wout32_ref[...].astype(_BF16)
        ones_ref[...] = _group_ones(MXU_W, SSM_GROUP)

    gt1 = mods_ref[0:1, 2 * D_MODEL:3 * D_MODEL]
    sh2 = mods_ref[0:1, 3 * D_MODEL:4 * D_MODEL]
    sc2 = mods_ref[0:1, 4 * D_MODEL:5 * D_MODEL]
    gt2 = mods_ref[0:1, 5 * D_MODEL:6 * D_MODEL]
    th = x_ref.shape[0] // 2
    n_split = 2
    ff = D_FF // n_split

    def mixer(r0):
        rows_ = slice(r0, r0 + th)
        chunks = slice(r0 // CHUNK, (r0 + th) // CHUNK)
        for q in range(N_SLABS):
            for half in range(CHUNK // BLOCKS):
                cols = [yg_ref[q * BLOCKS + g8, chunks, half * LANES:(half + 1) * LANES] for g8 in range(BLOCKS)]
                rows = _lane_block_transpose(cols)
                for t in range(BLOCKS):
                    y_scr[q, pl.ds(r0 + half * BLOCKS + t, th // CHUNK, stride=CHUNK), :] = rows[t]
        s = jax.nn.gelu(jnp.concatenate([y_scr[q, rows_] for q in range(N_SLABS)], axis=1))
        s = s * jax.nn.sigmoid(_dot(s.astype(_BF16), wglu_ref[...]))
        sq = (s * s).astype(_BF16)
        ssq = jnp.concatenate([_dot(sq[:, k * MXU_W:(k + 1) * MXU_W], ones_ref[...])
                               for k in range(SSM_W // MXU_W)], axis=1)
        sn = s * lax.rsqrt(ssq * (1.0 / SSM_GROUP) + RMS_EPS) * gssm_ref[...]
        mix = mix_scr[rows_] + _dot(sn.astype(_BF16), wout_ref[CONV_W:])
        x_mid = x_ref[rows_, :] + gt1 * mix
        return x_mid, (_rms_rows(x_mid, g2_ref[...] * (1.0 + sc2)) + sh2).astype(_BF16)

    def mlp_chunk(h2, j):
        a = jnp.maximum(_dot(h2, w1_ref[:, j * ff:(j + 1) * ff]), 0.0)
        return _dot((a * a).astype(_BF16), w2_ref[j * ff:(j + 1) * ff])

    def finish(r0, x_mid, acc):
        o_ref[r0:r0 + th, :] = _rms_rows(x_mid + gt2 * acc, gfin_ref[...])

    mix_scr[...] = _dot(conv_ref[...], wout_ref[0:CONV_W])
    x_a, h_a = mixer(0)
    acc_a = mlp_chunk(h_a, 0)
    x_b, h_b = mixer(th)
    for j in range(1, n_split):
        acc_a = acc_a + mlp_chunk(h_a, j)
    acc_b = mlp_chunk(h_b, 0)
    finish(0, x_a, acc_a)
    for j in range(1, n_split):
        acc_b = acc_b + mlp_chunk(h_b, j)
    finish(th, x_b, acc_b)


def _out_proj(x2d, conv_n, y_g, mods, g_ssm, g2, g_fin, wglu, wout, w1, w2, tm):
    n = x2d.shape[0]
    const = lambda i: (0, 0)
    row = lambda i: (i, 0)
    resident = lambda shape: pl.BlockSpec(shape, const, pipeline_mode=pl.Buffered(1))
    return pl.pallas_call(
        _out_kernel,
        grid=(n // tm,),
        in_specs=[pl.BlockSpec((tm, D_MODEL), row),
                  pl.BlockSpec((tm, CONV_W), row),
                  pl.BlockSpec((SSM_GROUPS, tm // CHUNK, CHUNK_W), lambda i: (0, i, 0)),
                  pl.BlockSpec(mods.shape, const),
                  pl.BlockSpec((1, SSM_W), const),
                  pl.BlockSpec((1, D_MODEL), const),
                  pl.BlockSpec((1, D_MODEL), const),
                  resident((SSM_W, SSM_W)),
                  resident((D_MODEL, D_MODEL)),
                  resident((D_MODEL, D_FF)),
                  resident((D_FF, D_MODEL))],
        out_specs=pl.BlockSpec((tm, D_MODEL), row),
        out_shape=jax.ShapeDtypeStruct((n, D_MODEL), _F32),
        scratch_shapes=[pltpu.VMEM((N_SLABS, tm, LANES), _F32), pltpu.VMEM((tm, D_MODEL), _F32),
                        pltpu.VMEM((SSM_W, SSM_W), _BF16), pltpu.VMEM((D_MODEL, D_MODEL), _BF16),
                        pltpu.VMEM((MXU_W, MXU_W), _BF16)],
        compiler_params=pltpu.CompilerParams(dimension_semantics=("arbitrary",),
                                             vmem_limit_bytes=VMEM_LIMIT),
        name="out_proj",
    )(x2d, conv_n, y_g, mods, g_ssm, g2, g_fin, wglu, wout, w1, w2)


def kernel(x, c, ctx, c_ctx, w_mod, b_mod, g_norm1, w_in, conv_w, ssm_a_re, ssm_a_im, ssm_log_dt,
           ssm_b_re, ssm_b_im, ssm_c_re, ssm_c_im, ssm_d, w_glu, g_conv_out, g_ssm_out, w_out,
           g_norm2, w_mlp1, w_mlp2, g_final):
    bsz, n_lat, d_model = x.shape
    n_ctx = ctx.shape[1]
    assert bsz == 1 and d_model == D_MODEL and w_mod.shape[0] == 1
    assert n_lat % (CHUNK * SUBLANES) == 0 and n_ctx % (CHUNK * CHUNK) == 0 and n_lat % GRID_W == 0
    layer = 0
    x2d = x[0]
    ctx2d = ctx[0]

    mods = _modulation(c, c_ctx[None, :], w_mod[layer], b_mod[layer][None, :])

    g1 = g_norm1[layer][None, :]
    g_conv = g_conv_out[layer][None, :]
    conv_n, u_g, uc_g, w1_bf, w2_bf = _in_proj(x2d, ctx2d, mods, g1, w_in[layer], conv_w[layer], g_conv, tm=1024,
                                                side_weights=(w_mlp1[layer], w_mlp2[layer]))

    n_pairs = SSM_GROUPS // PAIR
    y_g = _ssm(u_g, uc_g,
               ssm_a_re.reshape(2, n_pairs, 1, PAIR_STATE), ssm_a_im.reshape(2, n_pairs, 1, PAIR_STATE),
               ssm_log_dt.reshape(2, n_pairs, 1, PAIR),
               ssm_b_re[layer], ssm_b_im[layer], ssm_c_re[layer], ssm_c_im[layer],
               ssm_d.reshape(SSM_GROUPS, SSM_GROUP, 1))

    out = _out_proj(x2d, conv_n, y_g, mods, g_ssm_out[layer][None, :], g_norm2[layer][None, :],
                    g_final[None, :], w_glu[layer], w_out[layer], w1_bf, w2_bf, tm=1024)
    return out[None]
```

```python
import functools
import math

import jax
import jax.numpy as jnp
from jax import lax
from jax.experimental import pallas as pl
from jax.experimental.pallas import tpu as pltpu

D_MODEL = 1024
GRID_W = 64
CONV_W = 512
CONV_HEADS = 8
SSM_W = 512
SSM_GROUP = 16
SSM_GROUPS = 32
SSM_STATE = 64
IN_W = 3 * CONV_W + SSM_W
D_FF = 4 * D_MODEL
RMS_EPS = 1e-6

LANES = 128
MXU_W = 256
SUBLANES = 8
CHUNK = 16
CHUNK_W = CHUNK * SSM_GROUP
PAIR = 2
PAIR_STATE = PAIR * SSM_STATE
N_COMP = 4
BLOCKS = LANES // SSM_GROUP
N_SLABS = SSM_W // LANES
VMEM_LIMIT = 62 * 1024 * 1024

_BF16 = jnp.bfloat16
_F32 = jnp.float32


def _dot(a, b):
    return jnp.dot(a, b, preferred_element_type=_F32)


def _lane_block_transpose(vs):
    blk = lax.broadcasted_iota(jnp.int32, vs[0].shape, 1) // SSM_GROUP
    dist = BLOCKS // 2
    while dist:
        shift = dist * SSM_GROUP
        upper = (blk & dist) != 0
        out = [None] * BLOCKS
        for a in range(BLOCKS):
            if a & dist:
                continue
            lo, hi = vs[a], vs[a + dist]
            out[a] = jnp.where(upper, pltpu.roll(hi, shift, axis=1), lo)
            out[a + dist] = jnp.where(upper, hi, pltpu.roll(lo, LANES - shift, axis=1))
        vs = out
        dist //= 2
    return vs


def _mod_kernel(c_ref, cctx_ref, w_ref, b_ref, o_ref):
    tk = w_ref.shape[0]
    first = lax.broadcasted_iota(jnp.int32, (SUBLANES, tk), 0) == 0
    s = jnp.where(first, jnp.broadcast_to(c_ref[...], first.shape), jnp.broadcast_to(cctx_ref[...], first.shape))
    act = s * jax.nn.sigmoid(s)
    w = w_ref[...]
    a_hi, w_hi = act.astype(_BF16), w.astype(_BF16)
    a_lo = (act - a_hi.astype(_F32)).astype(_BF16)
    w_lo = (w - w_hi.astype(_F32)).astype(_BF16)
    by_hi = _dot(jnp.concatenate([a_hi, a_lo], axis=0), w_hi)
    part = by_hi[0:SUBLANES] + by_hi[SUBLANES:] + _dot(a_hi, w_lo)

    @pl.when(pl.program_id(0) == 0)
    def _():
        o_ref[...] = part + b_ref[...]

    @pl.when(pl.program_id(0) > 0)
    def _():
        o_ref[...] += part


def _modulation(c_row, cctx_row, w_mod, b_mod, n_out):
    d_in = w_mod.shape[0]
    tk = 256
    return pl.pallas_call(
        _mod_kernel,
        grid=(d_in // tk,),
        in_specs=[pl.BlockSpec((1, tk), lambda k: (0, k)),
                  pl.BlockSpec((1, tk), lambda k: (0, k)),
                  pl.BlockSpec((tk, n_out), lambda k: (k, 0)),
                  pl.BlockSpec((1, n_out), lambda k: (0, 0))],
        out_specs=pl.BlockSpec((SUBLANES, n_out), lambda k: (0, 0)),
        out_shape=jax.ShapeDtypeStruct((SUBLANES, n_out), _F32),
        compiler_params=pltpu.CompilerParams(dimension_semantics=("arbitrary",),
                                             vmem_limit_bytes=VMEM_LIMIT),
        name="mod",
    )(c_row, cctx_row, w_mod, b_mod)


def _rms_rows(x, g):
    ms = jnp.mean(x * x, axis=-1, keepdims=True)
    return x * lax.rsqrt(ms + RMS_EPS) * g


def _group_ones(width, group):
    rows = lax.broadcasted_iota(jnp.int32, (width, width), 0) // group
    cols = lax.broadcasted_iota(jnp.int32, (width, width), 1) // group
    return jnp.where(rows == cols, 1.0, 0.0).astype(_BF16)


def _in_kernel(n_side, x_ref, ctx_ref, mods_ref, g1_ref, w_in_ref, convw_ref, gconv_ref,
               ccol_ref, wmod_a_ref, wmod_b_ref, bmod_ref, *refs):
    side_in, (conv_ref, ug_ref, ucg_ref, mods2_ref) = refs[:n_side], refs[n_side:n_side + 4]
    side_out = refs[n_side + 4:2 * n_side + 4]
    u_scr, w_scr, ones_scr = refs[2 * n_side + 4:]
    step = pl.program_id(0)
    last = pl.num_programs(0) - 1

    def mods2_partial():
        a = ccol_ref[...]
        a = a * jax.nn.sigmoid(a)
        parts = []
        for w_ref in (wmod_a_ref, wmod_b_ref):
            acc = None
            for r0 in range(0, w_ref.shape[0], SUBLANES):
                term = a[r0:r0 + SUBLANES] * w_ref[r0:r0 + SUBLANES, :]
                acc = term if acc is None else acc + term
            parts.append(acc)
        return jnp.concatenate(parts, axis=1)

    @pl.when(step == 0)
    def _():
        mods2_ref[...] = mods2_partial()

    @pl.when(jnp.logical_and(step > 0, step < last))
    def _():
        mods2_ref[...] += mods2_partial()

    @pl.when(step == last - 1)
    def _():
        total = jnp.sum(mods2_ref[...], axis=0, keepdims=True) + bmod_ref[:, 2 * D_MODEL:]
        mods2_ref[...] = jnp.broadcast_to(total, mods2_ref.shape)

    @pl.when(step == 0)
    def _():
        w_scr[:, 0:SSM_W] = w_in_ref[:, 3 * CONV_W:].astype(_BF16)
        for k in range(CONV_W // MXU_W):
            for part in range(3):
                dst = SSM_W + (3 * k + part) * MXU_W
                src = part * CONV_W + k * MXU_W
                w_scr[:, dst:dst + MXU_W] = w_in_ref[:, src:src + MXU_W].astype(_BF16)
        ones_scr[...] = _group_ones(MXU_W, CONV_W // CONV_HEADS)

    def normed(rows_ref, mod_row):
        sh1 = mods_ref[mod_row:mod_row + 1, 0:D_MODEL]
        sc1 = mods_ref[mod_row:mod_row + 1, D_MODEL:2 * D_MODEL]
        return (_rms_rows(rows_ref[...], g1_ref[...] * (1.0 + sc1)) + sh1).astype(_BF16)

    def to_group_major(z_u, out_ref, r0):
        n_rows = z_u.shape[0]
        chunks = slice(r0 // CHUNK, (r0 + n_rows) // CHUNK)
        for q in range(N_SLABS):
            u_scr[q, r0:r0 + n_rows] = z_u[:, q * LANES:(q + 1) * LANES]
        for q in range(N_SLABS):
            for half in range(CHUNK // BLOCKS):
                rows = [u_scr[q, pl.ds(r0 + half * BLOCKS + t, n_rows // CHUNK, stride=CHUNK), :]
                        for t in range(BLOCKS)]
                cols = _lane_block_transpose(rows)
                for g8 in range(BLOCKS):
                    out_ref[q * BLOCKS + g8, chunks, half * LANES:(half + 1) * LANES] = cols[g8].astype(_BF16)

    def conv_branch(z, r0):
        n_rows = z.shape[0]
        pos = lax.broadcasted_iota(jnp.int32, (n_rows, MXU_W), 0) % GRID_W
        for k in range(CONV_W // MXU_W):
            c0 = SSM_W + 3 * k * MXU_W
            lanes = slice(k * MXU_W, (k + 1) * MXU_W)
            b = z[:, c0:c0 + MXU_W]
            cv = z[:, c0 + MXU_W:c0 + 2 * MXU_W] * z[:, c0 + 2 * MXU_W:c0 + 3 * MXU_W]
            prev = jnp.where(pos == 0, 0.0, pltpu.roll(cv, 1, axis=0))
            nxt = jnp.where(pos == GRID_W - 1, 0.0, pltpu.roll(cv, n_rows - 1, axis=0))
            y = b * (convw_ref[0:1, lanes] * prev + convw_ref[1:2, lanes] * cv + convw_ref[2:3, lanes] * nxt)
            ssq = _dot((y * y).astype(_BF16), ones_scr[...])
            yn = y * lax.rsqrt(ssq * (1.0 / (CONV_W // CONV_HEADS)) + RMS_EPS) * gconv_ref[:, lanes]
            conv_ref[r0:r0 + n_rows, lanes] = yn.astype(_BF16)

    @pl.when(step < last)
    def _():
        for src_ref, dst_ref in zip(side_in, side_out):
            dst_ref[...] = src_ref[...].astype(_BF16)
        th = x_ref.shape[0] // 2
        z = [_dot(normed(x_ref.at[r0:r0 + th], 0), w_scr[...]) for r0 in (0, th)]
        for half, r0 in enumerate((0, th)):
            to_group_major(z[half][:, 0:SSM_W], ug_ref, r0)
            conv_branch(z[half], r0)

    @pl.when(step == last)
    def _():
        to_group_major(_dot(normed(ctx_ref, 1), w_scr[:, 0:SSM_W]), ucg_ref, 0)


def _in_proj(x2d, ctx2d, mods, g1, w_in, conv_w, g_conv, c_col, w_mod, b_mod, tm, side_weights):
    n, n_ctx = x2d.shape[0], ctx2d.shape[0]
    steps = n // tm
    const = lambda i: (0, 0)
    row = lambda i: (jnp.minimum(i, steps - 1), 0)
    side_specs = [pl.BlockSpec((w.shape[0] // steps, w.shape[1]), row) for w in side_weights]
    k_rows = D_MODEL // steps
    n_mod2 = w_mod.shape[1] - 2 * D_MODEL
    wmod_specs = [pl.BlockSpec((k_rows, n_mod2 // 2), lambda i, j=j: (jnp.minimum(i, steps - 1), j)) for j in (1, 2)]
    return pl.pallas_call(
        functools.partial(_in_kernel, len(side_weights)),
        grid=(steps + 1,),
        in_specs=[pl.BlockSpec((tm, D_MODEL), row),
                  pl.BlockSpec((n_ctx, D_MODEL), const),
                  pl.BlockSpec(mods.shape, const),
                  pl.BlockSpec((1, D_MODEL), const),
                  pl.BlockSpec((D_MODEL, IN_W), const, pipeline_mode=pl.Buffered(1)),
                  pl.BlockSpec((3, CONV_W), const),
                  pl.BlockSpec((1, CONV_W), const),
                  pl.BlockSpec((k_rows, 1), row)] + wmod_specs + [pl.BlockSpec(b_mod.shape, const)] + side_specs,
        out_specs=[pl.BlockSpec((tm, CONV_W), row),
                   pl.BlockSpec((SSM_GROUPS, tm // CHUNK, CHUNK_W), lambda i: (0, jnp.minimum(i, steps - 1), 0)),
                   pl.BlockSpec((SSM_GROUPS, n_ctx // CHUNK, CHUNK_W), lambda i: (0, 0, 0)),
                   pl.BlockSpec((SUBLANES, n_mod2), const)] + side_specs,
        out_shape=[jax.ShapeDtypeStruct((n, CONV_W), _BF16),
                   jax.ShapeDtypeStruct((SSM_GROUPS, n // CHUNK, CHUNK_W), _BF16),
                   jax.ShapeDtypeStruct((SSM_GROUPS, n_ctx // CHUNK, CHUNK_W), _BF16),
                   jax.ShapeDtypeStruct((SUBLANES, n_mod2), _F32)]
                  + [jax.ShapeDtypeStruct(w.shape, _BF16) for w in side_weights],
        scratch_shapes=[pltpu.VMEM((N_SLABS, tm, LANES), _F32),
                        pltpu.VMEM((D_MODEL, IN_W), _BF16),
                        pltpu.VMEM((MXU_W, MXU_W), _BF16)],
        compiler_params=pltpu.CompilerParams(dimension_semantics=("arbitrary",),
                                             vmem_limit_bytes=VMEM_LIMIT),
        name="in_proj",
    )(x2d, ctx2d, mods, g1, w_in, conv_w, g_conv, c_col, w_mod, w_mod, b_mod, *side_weights)


def _scan_steps(n_rows):
    return max(1, math.ceil(math.log2(n_rows)))


def _ends_pad(n_tiles):
    return max(SUBLANES, 1 << (_scan_steps(n_tiles) - 1))


def _cmul(ar, ai, br, bi):
    return ar * br - ai * bi, ar * bi + ai * br


def _eye(n):
    rows, cols = (lax.broadcasted_iota(jnp.int32, (n, n), axis) for axis in (0, 1))
    return jnp.where(rows == cols, 1.0, 0.0).astype(_BF16)


def _dot_nt_exact(eye, x, repeat=1):
    hi = x.astype(_BF16)
    rest = x - hi.astype(_F32)
    mid = rest.astype(_BF16)
    lo = (rest - mid.astype(_F32)).astype(_BF16)
    nt = lambda term: lax.dot_general(eye, jnp.concatenate([term] * repeat, axis=0),
                                      (((1,), (1,)), ((), ())), preferred_element_type=_F32)
    return nt(hi) + nt(mid) + nt(lo)


def _ssm_prep(n_steps, par_refs, br_ref, bi_ref, cr_ref, ci_ref, wst_ref, wout_ref, cmat_ref):
    T = CHUNK
    left8 = lax.broadcasted_iota(jnp.int32, (SUBLANES, LANES), 1) < SSM_STATE
    row0 = lax.broadcasted_iota(jnp.int32, (SUBLANES, LANES), 0) == 0

    def member_rows(vf, vb):
        f8 = jnp.broadcast_to(vf, (SUBLANES, LANES))
        b8 = pltpu.roll(jnp.broadcast_to(vb, (SUBLANES, LANES)), SSM_STATE, axis=1)
        return jnp.where(row0, jnp.where(left8, f8, b8), jnp.where(left8, b8, f8))

    are_ref, aim_ref, ldt_ref = par_refs
    in_first = lax.broadcasted_iota(jnp.int32, (1, LANES), 1) < SSM_STATE
    dt_lanes = lambda d: jnp.where(in_first, ldt_ref[d, 0][:, 0:1], ldt_ref[d, 0][:, 1:2])
    lr = jnp.minimum(member_rows(are_ref[0, 0], are_ref[1, 0]), -1e-4)
    li = member_rows(aim_ref[0, 0], aim_ref[1, 0])
    dt = jnp.exp(member_rows(dt_lanes(0), dt_lanes(1)))
    mag = jnp.exp(lr * dt)
    ab_re = mag * jnp.cos(li * dt)
    ab_im = mag * jnp.sin(li * dt)
    nr = ab_re - 1.0
    den = lr * lr + li * li
    f_re = (nr * lr + ab_im * li) / den
    f_im = (ab_im * lr - nr * li) / den
    pt_re, pt_im = [jnp.ones_like(ab_re)], [jnp.zeros_like(ab_im)]
    for _ in range(T):
        nxt = _cmul(pt_re[-1], pt_im[-1], ab_re, ab_im)
        pt_re.append(nxt[0])
        pt_im.append(nxt[1])

    cols = jnp.concatenate([ab_re, ab_im, jnp.zeros((LANES - 2 * SUBLANES, LANES), _F32)], axis=0).T
    left1 = lax.broadcasted_iota(jnp.int32, (1, LANES), 1) < SSM_STATE
    eye_i = _eye(SSM_GROUP)
    eye_p = _eye(SSM_STATE)
    rows = lax.broadcasted_iota(jnp.int32, (LANES, LANES), 0)
    tcol = lax.broadcasted_iota(jnp.int32, (LANES, LANES), 1) // SSM_GROUP
    zero_half = jnp.zeros((SSM_STATE, CHUNK_W), _BF16)
    for e in range(PAIR):
        fwd_first = e == 0
        d0, d1 = (0, 1) if fwd_first else (1, 0)
        b_re = _dot_nt_exact(eye_i, jnp.concatenate([br_ref[d0, e], br_ref[d1, e]], axis=0))
        b_im = _dot_nt_exact(eye_i, jnp.concatenate([bi_ref[d0, e], bi_ref[d1, e]], axis=0))
        bbar_re, bbar_im = _cmul(f_re[e:e + 1], f_im[e:e + 1], b_re, b_im)
        fwd_lane = left1 if fwd_first else jnp.logical_not(left1)
        for s in range(T):
            pr = jnp.where(fwd_lane, pt_re[T - 1 - s][e:e + 1], pt_re[s][e:e + 1])
            pi = jnp.where(fwd_lane, pt_im[T - 1 - s][e:e + 1], pt_im[s][e:e + 1])
            w_re, w_im = _cmul(pr, pi, bbar_re, bbar_im)
            wst_ref[e, s * SSM_GROUP:(s + 1) * SSM_GROUP, 0:LANES] = w_re.astype(_BF16)
            wst_ref[e, s * SSM_GROUP:(s + 1) * SSM_GROUP, LANES:] = w_im.astype(_BF16)
        fwd_row = (rows < SSM_STATE) if fwd_first else (rows >= SSM_STATE)
        expo = jnp.where(fwd_row, tcol + 1, T // 2 - tcol)
        lo_re = jnp.ones((LANES, LANES), _F32)
        lo_im = jnp.zeros((LANES, LANES), _F32)
        s_re, s_im = cols[:, e:e + 1], cols[:, SUBLANES + e:SUBLANES + e + 1]
        for bit in range((T // 2).bit_length()):
            m_re, m_im = _cmul(lo_re, lo_im, jnp.broadcast_to(s_re, lo_re.shape), jnp.broadcast_to(s_im, lo_re.shape))
            take = ((expo >> bit) & 1) == 1
            lo_re, lo_im = jnp.where(take, m_re, lo_re), jnp.where(take, m_im, lo_im)
            if (1 << bit) < T // 2:
                s_re, s_im = _cmul(s_re, s_im, s_re, s_im)
        hi_re, hi_im = _cmul(lo_re, lo_im, jnp.broadcast_to(s_re, lo_re.shape), jnp.broadcast_to(s_im, lo_re.shape))
        q_re = jnp.concatenate([jnp.where(fwd_row, lo_re, hi_re), jnp.where(fwd_row, hi_re, lo_re)], axis=1)
        q_im = jnp.concatenate([jnp.where(fwd_row, lo_im, hi_im), jnp.where(fwd_row, hi_im, lo_im)], axis=1)
        tiled = lambda c: _dot_nt_exact(eye_p, c, repeat=T)
        c_re = jnp.concatenate([tiled(cr_ref[d0, e]), tiled(cr_ref[d1, e])], axis=0)
        c_im = jnp.concatenate([tiled(ci_ref[d0, e]), tiled(ci_ref[d1, e])], axis=0)
        o_re, o_im = _cmul(c_re, c_im, q_re, q_im)
        wout_ref[e, 0:LANES] = o_re.astype(_BF16)
        wout_ref[e, LANES:] = (-o_im).astype(_BF16)
        for ri, src in enumerate((c_re, -c_im)):
            for slot, d in enumerate((d0, d1)):
                r0 = ri * LANES + slot * SSM_STATE
                block = src[slot * SSM_STATE:(slot + 1) * SSM_STATE].astype(_BF16)
                cmat_ref[e, r0:r0 + SSM_STATE, d * CHUNK_W:(d + 1) * CHUNK_W] = block
                cmat_ref[e, r0:r0 + SSM_STATE, (1 - d) * CHUNK_W:(2 - d) * CHUNK_W] = zero_half

    m0_re, m1_re = pt_re[T][0:1], pt_re[T][1:2]
    m0_im, m1_im = pt_im[T][0:1], pt_im[T][1:2]
    cf = (jnp.where(left1, m0_re, m1_re), jnp.where(left1, m0_im, m1_im))
    cb = (jnp.where(left1, m1_re, m0_re), jnp.where(left1, m1_im, m0_im))
    coefs = []
    for _ in range(n_steps):
        coefs.append([cf[0], cf[1], cb[0], cb[1]])
        cf = _cmul(cf[0], cf[1], cf[0], cf[1])
        cb = _cmul(cb[0], cb[1], cb[0], cb[1])
    return coefs


def _ssm_kernel(n_lat, n_ctx, pairs_per_step, u_ref, uc_ref, are_ref, aim_ref, ldt_ref,
                br_ref, bi_ref, cr_ref, ci_ref, d_ref, y_ref, *scratch):
    programs = []
    for p in range(pairs_per_step):
        members = slice(p * PAIR, (p + 1) * PAIR)
        programs.append(_ssm_pair(
            n_lat, n_ctx, u_ref.at[members], uc_ref.at[members],
            tuple(r.at[:, p:p + 1] for r in (are_ref, aim_ref, ldt_ref)), br_ref.at[:, members], bi_ref.at[:, members], cr_ref.at[:, members], ci_ref.at[:, members],
            d_ref.at[members], y_ref.at[members], *[s.at[p] for s in scratch]))
    live = []
    while programs or live:
        if programs:
            live.append(programs.pop(0))
        for prog in list(live):
            if next(prog, "done") == "done":
                live.remove(prog)


def _ssm_pair(n_lat, n_ctx, u_ref, uc_ref, par_refs, br_ref, bi_ref, cr_ref, ci_ref, d_ref,
              y_ref, buf_a, buf_b, ends_a, ends_b, wst_ref, wout_ref, cmat_ref, m_scr):
    n_rows = n_lat + n_ctx
    pad = SUBLANES
    coefs = _ssm_prep(_scan_steps(n_rows), par_refs, br_ref, bi_ref, cr_ref, ci_ref,
                      wst_ref, wout_ref, cmat_ref)
    yield

    lane = lax.broadcasted_iota(jnp.int32, (SSM_GROUP, CHUNK_W), 1)
    blk = lane // SSM_GROUP
    on_diag = lane % SSM_GROUP == lax.broadcasted_iota(jnp.int32, (SSM_GROUP, CHUNK_W), 0)
    for e in range(PAIR):
        d_skip = jnp.where(on_diag, jnp.broadcast_to(d_ref[e], (SSM_GROUP, CHUNK_W)), 0.0)
        taps = _dot(wst_ref[e], cmat_ref[e])
        taps_f = taps[:, 0:CHUNK_W]
        taps_b = taps[:, CHUNK_W:]
        for s in range(CHUNK):
            acc = jnp.zeros((SSM_GROUP, CHUNK_W), _F32)
            for t in range(CHUNK):
                val = None
                if s >= t:
                    r0 = (s - t) * SSM_GROUP
                    val = taps_b[r0:r0 + SSM_GROUP]
                if s <= t:
                    r0 = (CHUNK - 1 - t + s) * SSM_GROUP
                    v2 = taps_f[r0:r0 + SSM_GROUP]
                    val = v2 if val is None else val + v2
                if s == t:
                    val = val + d_skip
                acc = jnp.where(blk == t, val, acc)
            m_scr[e, s * SSM_GROUP:(s + 1) * SSM_GROUP, :] = acc.astype(_BF16)
    yield

    def paired(m0, m1):
        left = lax.broadcasted_iota(jnp.int32, (m0.shape[0], PAIR_STATE), 1) < SSM_STATE
        re0, im0 = m0[:, 0:PAIR_STATE], m0[:, PAIR_STATE:]
        re1, im1 = m1[:, 0:PAIR_STATE], m1[:, PAIR_STATE:]
        return [jnp.where(left, re0, re1), jnp.where(left, im0, im1),
                jnp.where(left, re1, re0), jnp.where(left, im1, im0)]

    s_lat = paired(_dot(u_ref[0], wst_ref[0]), _dot(u_ref[1], wst_ref[1]))
    s_ctx = paired(_dot(uc_ref[0], wst_ref[0]), _dot(uc_ref[1], wst_ref[1]))
    yield

    n_tiles = n_rows // SUBLANES
    tile_row = lax.broadcasted_iota(jnp.int32, (1, SUBLANES, PAIR_STATE), 1)

    def tile_scan(x_re, x_im, comp0, backward):
        for k in range(SUBLANES.bit_length() - 1):
            sh = 1 << k
            shift, keep = (SUBLANES - sh, tile_row < SUBLANES - sh) if backward else (sh, tile_row >= sh)
            a_re = jnp.where(keep, coefs[k][comp0], 0.0)
            a_im = jnp.where(keep, coefs[k][comp0 + 1], 0.0)
            p_re, p_im = pltpu.roll(x_re, shift, axis=1), pltpu.roll(x_im, shift, axis=1)
            x_re, x_im = x_re + a_re * p_re - a_im * p_im, x_im + a_re * p_im + a_im * p_re
        return x_re, x_im

    as_tiles = lambda parts: jnp.concatenate(parts, axis=0).reshape(n_tiles, SUBLANES, PAIR_STATE)
    local = list(tile_scan(as_tiles([s_ctx[0], s_lat[0]]), as_tiles([s_ctx[1], s_lat[1]]), 0, False))
    local += tile_scan(as_tiles([s_lat[2], s_ctx[2]]), as_tiles([s_lat[3], s_ctx[3]]), 2, True)
    for comp in range(N_COMP):
        buf_a[comp, pad:pad + n_rows] = local[comp].reshape(n_rows, PAIR_STATE)
    yield

    for buf in (ends_a, ends_b):
        buf[...] = jnp.zeros(buf.shape, _F32)
    epad = _ends_pad(n_tiles)
    for comp in range(N_COMP):
        end_row = pad + (SUBLANES - 1 if comp < 2 else 0)
        ends_a[comp, epad:epad + n_tiles] = buf_a[comp, pl.ds(end_row, n_tiles, stride=SUBLANES), :]
    src, dst = ends_a, ends_b
    for k in range(_scan_steps(n_tiles)):
        sh = 1 << k
        coef = coefs[SUBLANES.bit_length() - 1 + k]
        lo, hi = epad, epad + n_tiles
        xr, xi = src[0, lo - sh:hi - sh], src[1, lo - sh:hi - sh]
        dst[0, lo:hi] = src[0, lo:hi] + coef[0] * xr - coef[1] * xi
        dst[1, lo:hi] = src[1, lo:hi] + coef[0] * xi + coef[1] * xr
        xr, xi = src[2, lo + sh:hi + sh], src[3, lo + sh:hi + sh]
        dst[2, lo:hi] = src[2, lo:hi] + coef[2] * xr - coef[3] * xi
        dst[3, lo:hi] = src[3, lo:hi] + coef[2] * xi + coef[3] * xr
        src, dst = dst, src

    def carry_powers(comp0, backward):
        base = [(coefs[k][comp0], coefs[k][comp0 + 1]) for k in range(SUBLANES.bit_length())]
        pw = []
        for r in range(1, SUBLANES + 1):
            acc = None
            for k, term in enumerate(base):
                if (r >> k) & 1:
                    acc = term if acc is None else _cmul(acc[0], acc[1], term[0], term[1])
            pw.append(acc)
        if backward:
            pw = pw[::-1]
        return (jnp.concatenate([p[0] for p in pw], axis=0), jnp.concatenate([p[1] for p in pw], axis=0))

    for comp0, backward in ((0, False), (2, True)):
        p_re, p_im = carry_powers(comp0, backward)
        first = epad + 1 if backward else epad - 1
        for i in range(n_tiles):
            c_re = src[comp0, first + i:first + i + 1]
            c_im = src[comp0 + 1, first + i:first + i + 1]
            rows = slice(pad + i * SUBLANES, pad + (i + 1) * SUBLANES)
            buf_b[comp0, rows] = buf_a[comp0, rows] + p_re * c_re - p_im * c_im
            buf_b[comp0 + 1, rows] = buf_a[comp0 + 1, rows] + p_re * c_im + p_im * c_re
    yield

    f0 = pad + n_ctx - 1
    b0 = pad + 1
    f_re, f_im = buf_b[0, f0:f0 + n_lat], buf_b[1, f0:f0 + n_lat]
    b_re, b_im = buf_b[2, b0:b0 + n_lat], buf_b[3, b0:b0 + n_lat]
    left = lax.broadcasted_iota(jnp.int32, (n_lat, PAIR_STATE), 1) < SSM_STATE
    h0 = jnp.concatenate([jnp.where(left, f_re, b_re), jnp.where(left, f_im, b_im)], axis=1)
    h1 = jnp.concatenate([jnp.where(left, b_re, f_re), jnp.where(left, b_im, f_im)], axis=1)
    y_ref[0] = _dot(u_ref[0], m_scr[0]) + _dot(h0.astype(_BF16), wout_ref[0])
    y_ref[1] = _dot(u_ref[1], m_scr[1]) + _dot(h1.astype(_BF16), wout_ref[1])


def _ssm(u_g, uc_g, a_re, a_im, log_dt, b_re, b_im, c_re, c_im, d_col):
    n_groups, n_lat, _ = u_g.shape
    n_ctx = uc_g.shape[1]
    n_rows = n_lat + n_ctx
    n_pairs = n_groups // PAIR
    pps = 2
    gps = pps * PAIR
    buf = pltpu.VMEM((pps, N_COMP, n_rows + 2 * SUBLANES, PAIR_STATE), _F32)
    n_tiles = n_rows // SUBLANES
    ends_rows = _ends_pad(n_tiles) + -(-(n_tiles + _ends_pad(n_tiles)) // SUBLANES) * SUBLANES
    ends = pltpu.VMEM((pps, N_COMP, ends_rows, PAIR_STATE), _F32)
    mat = pltpu.VMEM((pps, PAIR, CHUNK_W, CHUNK_W), _BF16)
    pair3 = lambda q: (q, 0, 0)
    by_dir = lambda q: (0, q, 0, 0)
    return pl.pallas_call(
        functools.partial(_ssm_kernel, n_lat, n_ctx, pps),
        grid=(n_pairs // pps,),
        in_specs=[pl.BlockSpec((gps, n_lat, CHUNK_W), pair3),
                  pl.BlockSpec((gps, n_ctx, CHUNK_W), pair3),
                  pl.BlockSpec((2, pps) + a_re.shape[2:], by_dir),
                  pl.BlockSpec((2, pps) + a_im.shape[2:], by_dir),
                  pl.BlockSpec((2, pps) + log_dt.shape[2:], by_dir),
                  pl.BlockSpec((2, gps) + b_re.shape[2:], by_dir),
                  pl.BlockSpec((2, gps) + b_im.shape[2:], by_dir),
                  pl.BlockSpec((2, gps) + c_re.shape[2:], by_dir),
                  pl.BlockSpec((2, gps) + c_im.shape[2:], by_dir),
                  pl.BlockSpec((gps,) + d_col.shape[1:], pair3)],
        out_specs=pl.BlockSpec((gps, n_lat, CHUNK_W), pair3),
        out_shape=jax.ShapeDtypeStruct((n_groups, n_lat, CHUNK_W), _F32),
        scratch_shapes=[buf, buf, ends, ends, mat, mat,
                        pltpu.VMEM((pps, PAIR, CHUNK_W, 2 * CHUNK_W), _BF16), mat],
        compiler_params=pltpu.CompilerParams(dimension_semantics=("arbitrary",),
                                             vmem_limit_bytes=VMEM_LIMIT),
        name="ssm",
    )(u_g, uc_g, a_re, a_im, log_dt, b_re, b_im, c_re, c_im, d_col)


def _out_kernel(x_ref, conv_ref, yg_ref, mods_ref, gssm_ref, g2_ref, gfin_ref,
                wglu32_ref, wout32_ref, w1_ref, w2_ref, o_ref, y_scr, mix_scr, wglu_ref, wout_ref, ones_ref):
    @pl.when(pl.program_id(0) == 0)
    def _():
        wglu_ref[...] = wglu32_ref[...].astype(_BF16)
        wout_ref[...] = wout32_ref[...].astype(_BF16)
        ones_ref[...] = _group_ones(MXU_W, SSM_GROUP)

    gt1 = mods_ref[0:1, 0:D_MODEL]
    sh2 = mods_ref[0:1, D_MODEL:2 * D_MODEL]
    sc2 = mods_ref[0:1, 2 * D_MODEL:3 * D_MODEL]
    gt2 = mods_ref[0:1, 3 * D_MODEL:4 * D_MODEL]
    th = x_ref.shape[0] // 2
    n_split = 2
    ff = D_FF // n_split

    def mixer(r0):
        rows_ = slice(r0, r0 + th)
        chunks = slice(r0 // CHUNK, (r0 + th) // CHUNK)
        for q in range(N_SLABS):
            for half in range(CHUNK // BLOCKS):
                cols = [yg_ref[q * BLOCKS + g8, chunks, half * LANES:(half + 1) * LANES] for g8 in range(BLOCKS)]
                rows = _lane_block_transpose(cols)
                for t in range(BLOCKS):
                    y_scr[q, pl.ds(r0 + half * BLOCKS + t, th // CHUNK, stride=CHUNK), :] = rows[t]
        s = jax.nn.gelu(jnp.concatenate([y_scr[q, rows_] for q in range(N_SLABS)], axis=1))
        s = s * jax.nn.sigmoid(_dot(s.astype(_BF16), wglu_ref[...]))
        sq = (s * s).astype(_BF16)
        ssq = jnp.concatenate([_dot(sq[:, k * MXU_W:(k + 1) * MXU_W], ones_ref[...])
                               for k in range(SSM_W // MXU_W)], axis=1)
        sn = s * lax.rsqrt(ssq * (1.0 / SSM_GROUP) + RMS_EPS) * gssm_ref[...]
        mix = mix_scr[rows_] + _dot(sn.astype(_BF16), wout_ref[CONV_W:])
        x_mid = x_ref[rows_, :] + gt1 * mix
        return x_mid, (_rms_rows(x_mid, g2_ref[...] * (1.0 + sc2)) + sh2).astype(_BF16)

    def mlp_chunk(h2, j):
        a = jnp.maximum(_dot(h2, w1_ref[:, j * ff:(j + 1) * ff]), 0.0)
        return _dot((a * a).astype(_BF16), w2_ref[j * ff:(j + 1) * ff])

    def finish(r0, x_mid, acc):
        o_ref[r0:r0 + th, :] = _rms_rows(x_mid + gt2 * acc, gfin_ref[...])

    mix_scr[...] = _dot(conv_ref[...], wout_ref[0:CONV_W])
    x_a, h_a = mixer(0)
    acc_a = mlp_chunk(h_a, 0)
    x_b, h_b = mixer(th)
    for j in range(1, n_split):
        acc_a = acc_a + mlp_chunk(h_a, j)
    acc_b = mlp_chunk(h_b, 0)
    finish(0, x_a, acc_a)
    for j in range(1, n_split):
        acc_b = acc_b + mlp_chunk(h_b, j)
    finish(th, x_b, acc_b)


def _out_proj(x2d, conv_n, y_g, mods, g_ssm, g2, g_fin, wglu, wout, w1, w2, tm):
    n = x2d.shape[0]
    const = lambda i: (0, 0)
    row = lambda i: (i, 0)
    resident = lambda shape: pl.BlockSpec(shape, const, pipeline_mode=pl.Buffered(1))
    return pl.pallas_call(
        _out_kernel,
        grid=(n // tm,),
        in_specs=[pl.BlockSpec((tm, D_MODEL), row),
                  pl.BlockSpec((tm, CONV_W), row),
                  pl.BlockSpec((SSM_GROUPS, tm // CHUNK, CHUNK_W), lambda i: (0, i, 0)),
                  pl.BlockSpec(mods.shape, const),
                  pl.BlockSpec((1, SSM_W), const),
                  pl.BlockSpec((1, D_MODEL), const),
                  pl.BlockSpec((1, D_MODEL), const),
                  resident((SSM_W, SSM_W)),
                  resident((D_MODEL, D_MODEL)),
                  resident((D_MODEL, D_FF)),
                  resident((D_FF, D_MODEL))],
        out_specs=pl.BlockSpec((tm, D_MODEL), row),
        out_shape=jax.ShapeDtypeStruct((n, D_MODEL), _F32),
        scratch_shapes=[pltpu.VMEM((N_SLABS, tm, LANES), _F32), pltpu.VMEM((tm, D_MODEL), _F32),
                        pltpu.VMEM((SSM_W, SSM_W), _BF16), pltpu.VMEM((D_MODEL, D_MODEL), _BF16),
                        pltpu.VMEM((MXU_W, MXU_W), _BF16)],
        compiler_params=pltpu.CompilerParams(dimension_semantics=("arbitrary",),
                                             vmem_limit_bytes=VMEM_LIMIT),
        name="out_proj",
    )(x2d, conv_n, y_g, mods, g_ssm, g2, g_fin, wglu, wout, w1, w2)


def kernel(x, c, ctx, c_ctx, w_mod, b_mod, g_norm1, w_in, conv_w, ssm_a_re, ssm_a_im, ssm_log_dt,
           ssm_b_re, ssm_b_im, ssm_c_re, ssm_c_im, ssm_d, w_glu, g_conv_out, g_ssm_out, w_out,
           g_norm2, w_mlp1, w_mlp2, g_final):
    bsz, n_lat, d_model = x.shape
    n_ctx = ctx.shape[1]
    assert bsz == 1 and d_model == D_MODEL and w_mod.shape[0] == 1
    assert n_lat % (CHUNK * SUBLANES) == 0 and n_ctx % (CHUNK * CHUNK) == 0 and n_lat % GRID_W == 0
    layer = 0
    x2d = x[0]
    ctx2d = ctx[0]

    b_mod_row = b_mod[layer][None, :]
    mods = _modulation(c, c_ctx[None, :], w_mod[layer], b_mod_row, n_out=2 * D_MODEL)

    g1 = g_norm1[layer][None, :]
    g_conv = g_conv_out[layer][None, :]
    conv_n, u_g, uc_g, mods2, w1_bf, w2_bf = _in_proj(
        x2d, ctx2d, mods, g1, w_in[layer], conv_w[layer], g_conv, c.reshape(D_MODEL, 1), w_mod[layer], b_mod_row,
        tm=1024, side_weights=(w_mlp1[layer], w_mlp2[layer]))

    n_pairs = SSM_GROUPS // PAIR
    y_g = _ssm(u_g, uc_g,
               ssm_a_re.reshape(2, n_pairs, 1, PAIR_STATE), ssm_a_im.reshape(2, n_pairs, 1, PAIR_STATE),
               ssm_log_dt.reshape(2, n_pairs, 1, PAIR),
               ssm_b_re[layer], ssm_b_im[layer], ssm_c_re[layer], ssm_c_im[layer],
               ssm_d.reshape(SSM_GROUPS, SSM_GROUP, 1))

    out = _out_proj(x2d, conv_n, y_g, mods2, g_ssm_out[layer][None, :], g_norm2[layer][None, :],
                    g_final[None, :], w_glu[layer], w_out[layer], w1_bf, w2_bf, tm=1024)
    return out[None]
```

```python
import functools
import math

import jax
import jax.numpy as jnp
from jax import lax
from jax.experimental import pallas as pl
from jax.experimental.pallas import tpu as pltpu

D_MODEL = 1024
GRID_W = 64
CONV_W = 512
CONV_HEADS = 8
SSM_W = 512
SSM_GROUP = 16
SSM_GROUPS = 32
SSM_STATE = 64
IN_W = 3 * CONV_W + SSM_W
D_FF = 4 * D_MODEL
RMS_EPS = 1e-6

LANES = 128
MXU_W = 256
SUBLANES = 8
CHUNK = 16
CHUNK_W = CHUNK * SSM_GROUP
PAIR = 2
PAIR_STATE = PAIR * SSM_STATE
N_COMP = 4
BLOCKS = LANES // SSM_GROUP
N_SLABS = SSM_W // LANES
VMEM_LIMIT = 62 * 1024 * 1024

_BF16 = jnp.bfloat16
_F32 = jnp.float32


def _dot(a, b):
    return jnp.dot(a, b, preferred_element_type=_F32)


def _lane_block_transpose(vs):
    blk = lax.broadcasted_iota(jnp.int32, vs[0].shape, 1) // SSM_GROUP
    dist = BLOCKS // 2
    while dist:
        shift = dist * SSM_GROUP
        upper = (blk & dist) != 0
        out = [None] * BLOCKS
        for a in range(BLOCKS):
            if a & dist:
                continue
            lo, hi = vs[a], vs[a + dist]
            out[a] = jnp.where(upper, pltpu.roll(hi, shift, axis=1), lo)
            out[a + dist] = jnp.where(upper, hi, pltpu.roll(lo, LANES - shift, axis=1))
        vs = out
        dist //= 2
    return vs


def _mod_kernel(c_ref, cctx_ref, w_ref, b_ref, o_ref):
    tk = w_ref.shape[0]
    first = lax.broadcasted_iota(jnp.int32, (SUBLANES, tk), 0) == 0
    s = jnp.where(first, jnp.broadcast_to(c_ref[...], first.shape), jnp.broadcast_to(cctx_ref[...], first.shape))
    act = s * jax.nn.sigmoid(s)
    w = w_ref[...]
    a_hi, w_hi = act.astype(_BF16), w.astype(_BF16)
    a_lo = (act - a_hi.astype(_F32)).astype(_BF16)
    w_lo = (w - w_hi.astype(_F32)).astype(_BF16)
    by_hi = _dot(jnp.concatenate([a_hi, a_lo], axis=0), w_hi)
    part = by_hi[0:SUBLANES] + by_hi[SUBLANES:] + _dot(a_hi, w_lo)

    @pl.when(pl.program_id(0) == 0)
    def _():
        o_ref[...] = part + b_ref[...]

    @pl.when(pl.program_id(0) > 0)
    def _():
        o_ref[...] += part


def _modulation(c_row, cctx_row, w_mod, b_mod):
    d_in, n_out = w_mod.shape
    tk = 256
    return pl.pallas_call(
        _mod_kernel,
        grid=(d_in // tk,),
        in_specs=[pl.BlockSpec((1, tk), lambda k: (0, k)),
                  pl.BlockSpec((1, tk), lambda k: (0, k)),
                  pl.BlockSpec((tk, n_out), lambda k: (k, 0)),
                  pl.BlockSpec((1, n_out), lambda k: (0, 0))],
        out_specs=pl.BlockSpec((SUBLANES, n_out), lambda k: (0, 0)),
        out_shape=jax.ShapeDtypeStruct((SUBLANES, n_out), _F32),
        compiler_params=pltpu.CompilerParams(dimension_semantics=("arbitrary",),
                                             vmem_limit_bytes=VMEM_LIMIT),
        name="mod",
    )(c_row, cctx_row, w_mod, b_mod)


def _rms_rows(x, g):
    ms = jnp.mean(x * x, axis=-1, keepdims=True)
    return x * lax.rsqrt(ms + RMS_EPS) * g


def _group_ones(width, group):
    rows = lax.broadcasted_iota(jnp.int32, (width, width), 0) // group
    cols = lax.broadcasted_iota(jnp.int32, (width, width), 1) // group
    return jnp.where(rows == cols, 1.0, 0.0).astype(_BF16)


def _in_kernel(n_side, x_ref, ctx_ref, mods_ref, g1_ref, w_in_ref, convw_ref, gconv_ref, *refs):
    side_in, (conv_ref, ug_ref, ucg_ref) = refs[:n_side], refs[n_side:n_side + 3]
    side_out = refs[n_side + 3:2 * n_side + 3]
    u_scr, w_scr, ones_scr = refs[2 * n_side + 3:]
    step = pl.program_id(0)
    last = pl.num_programs(0) - 1

    @pl.when(step == 0)
    def _():
        w_scr[:, 0:SSM_W] = w_in_ref[:, 3 * CONV_W:].astype(_BF16)
        for k in range(CONV_W // MXU_W):
            for part in range(3):
                dst = SSM_W + (3 * k + part) * MXU_W
                src = part * CONV_W + k * MXU_W
                w_scr[:, dst:dst + MXU_W] = w_in_ref[:, src:src + MXU_W].astype(_BF16)
        ones_scr[...] = _group_ones(MXU_W, CONV_W // CONV_HEADS)

    def normed(rows_ref, mod_row):
        sh1 = mods_ref[mod_row:mod_row + 1, 0:D_MODEL]
        sc1 = mods_ref[mod_row:mod_row + 1, D_MODEL:2 * D_MODEL]
        return (_rms_rows(rows_ref[...], g1_ref[...] * (1.0 + sc1)) + sh1).astype(_BF16)

    def to_group_major(z_u, out_ref, r0):
        n_rows = z_u.shape[0]
        chunks = slice(r0 // CHUNK, (r0 + n_rows) // CHUNK)
        for q in range(N_SLABS):
            u_scr[q, r0:r0 + n_rows] = z_u[:, q * LANES:(q + 1) * LANES]
        for q in range(N_SLABS):
            for half in range(CHUNK // BLOCKS):
                rows = [u_scr[q, pl.ds(r0 + half * BLOCKS + t, n_rows // CHUNK, stride=CHUNK), :]
                        for t in range(BLOCKS)]
                cols = _lane_block_transpose(rows)
                for g8 in range(BLOCKS):
                    out_ref[q * BLOCKS + g8, chunks, half * LANES:(half + 1) * LANES] = cols[g8].astype(_BF16)

    def conv_branch(z, r0):
        n_rows = z.shape[0]
        pos = lax.broadcasted_iota(jnp.int32, (n_rows, MXU_W), 0) % GRID_W
        for k in range(CONV_W // MXU_W):
            c0 = SSM_W + 3 * k * MXU_W
            lanes = slice(k * MXU_W, (k + 1) * MXU_W)
            b = z[:, c0:c0 + MXU_W]
            cv = z[:, c0 + MXU_W:c0 + 2 * MXU_W] * z[:, c0 + 2 * MXU_W:c0 + 3 * MXU_W]
            prev = jnp.where(pos == 0, 0.0, pltpu.roll(cv, 1, axis=0))
            nxt = jnp.where(pos == GRID_W - 1, 0.0, pltpu.roll(cv, n_rows - 1, axis=0))
            y = b * (convw_ref[0:1, lanes] * prev + convw_ref[1:2, lanes] * cv + convw_ref[2:3, lanes] * nxt)
            ssq = _dot((y * y).astype(_BF16), ones_scr[...])
            yn = y * lax.rsqrt(ssq * (1.0 / (CONV_W // CONV_HEADS)) + RMS_EPS) * gconv_ref[:, lanes]
            conv_ref[r0:r0 + n_rows, lanes] = yn.astype(_BF16)

    @pl.when(step < last)
    def _():
        for src_ref, dst_ref in zip(side_in, side_out):
            dst_ref[...] = src_ref[...].astype(_BF16)
        th = x_ref.shape[0] // 2
        z = [_dot(normed(x_ref.at[r0:r0 + th], 0), w_scr[...]) for r0 in (0, th)]
        for half, r0 in enumerate((0, th)):
            to_group_major(z[half][:, 0:SSM_W], ug_ref, r0)
            conv_branch(z[half], r0)

    @pl.when(step == last)
    def _():
        to_group_major(_dot(normed(ctx_ref, 1), w_scr[:, 0:SSM_W]), ucg_ref, 0)


def _in_proj(x2d, ctx2d, mods, g1, w_in, conv_w, g_conv, tm, side_weights):
    n, n_ctx = x2d.shape[0], ctx2d.shape[0]
    steps = n // tm
    const = lambda i: (0, 0)
    row = lambda i: (jnp.minimum(i, steps - 1), 0)
    side_specs = [pl.BlockSpec((w.shape[0] // steps, w.shape[1]), row) for w in side_weights]
    return pl.pallas_call(
        functools.partial(_in_kernel, len(side_weights)),
        grid=(steps + 1,),
        in_specs=[pl.BlockSpec((tm, D_MODEL), row),
                  pl.BlockSpec((n_ctx, D_MODEL), const),
                  pl.BlockSpec(mods.shape, const),
                  pl.BlockSpec((1, D_MODEL), const),
                  pl.BlockSpec((D_MODEL, IN_W), const, pipeline_mode=pl.Buffered(1)),
                  pl.BlockSpec((3, CONV_W), const),
                  pl.BlockSpec((1, CONV_W), const)] + side_specs,
        out_specs=[pl.BlockSpec((tm, CONV_W), row),
                   pl.BlockSpec((SSM_GROUPS, tm // CHUNK, CHUNK_W), lambda i: (0, jnp.minimum(i, steps - 1), 0)),
                   pl.BlockSpec((SSM_GROUPS, n_ctx // CHUNK, CHUNK_W), lambda i: (0, 0, 0))] + side_specs,
        out_shape=[jax.ShapeDtypeStruct((n, CONV_W), _BF16),
                   jax.ShapeDtypeStruct((SSM_GROUPS, n // CHUNK, CHUNK_W), _BF16),
                   jax.ShapeDtypeStruct((SSM_GROUPS, n_ctx // CHUNK, CHUNK_W), _BF16)]
                  + [jax.ShapeDtypeStruct(w.shape, _BF16) for w in side_weights],
        scratch_shapes=[pltpu.VMEM((N_SLABS, tm, LANES), _F32),
                        pltpu.VMEM((D_MODEL, IN_W), _BF16),
                        pltpu.VMEM((MXU_W, MXU_W), _BF16)],
        compiler_params=pltpu.CompilerParams(dimension_semantics=("arbitrary",),
                                             vmem_limit_bytes=VMEM_LIMIT),
        name="in_proj",
    )(x2d, ctx2d, mods, g1, w_in, conv_w, g_conv, *side_weights)


def _scan_steps(n_rows):
    return max(1, math.ceil(math.log2(n_rows)))


def _ends_pad(n_tiles):
    return max(SUBLANES, 1 << (_scan_steps(n_tiles) - 1))


def _cmul(ar, ai, br, bi):
    return ar * br - ai * bi, ar * bi + ai * br


def _eye(n):
    rows, cols = (lax.broadcasted_iota(jnp.int32, (n, n), axis) for axis in (0, 1))
    return jnp.where(rows == cols, 1.0, 0.0).astype(_BF16)


def _dot_nt_exact(eye, x, repeat=1):
    hi = x.astype(_BF16)
    rest = x - hi.astype(_F32)
    mid = rest.astype(_BF16)
    lo = (rest - mid.astype(_F32)).astype(_BF16)
    nt = lambda term: lax.dot_general(eye, jnp.concatenate([term] * repeat, axis=0),
                                      (((1,), (1,)), ((), ())), preferred_element_type=_F32)
    return nt(hi) + nt(mid) + nt(lo)


def _ssm_prep(n_steps, par_refs, br_ref, bi_ref, cr_ref, ci_ref, wst_ref, wout_ref, cmat_ref):
    T = CHUNK
    left8 = lax.broadcasted_iota(jnp.int32, (SUBLANES, LANES), 1) < SSM_STATE
    row0 = lax.broadcasted_iota(jnp.int32, (SUBLANES, LANES), 0) == 0

    def member_rows(vf, vb):
        f8 = jnp.broadcast_to(vf, (SUBLANES, LANES))
        b8 = pltpu.roll(jnp.broadcast_to(vb, (SUBLANES, LANES)), SSM_STATE, axis=1)
        return jnp.where(row0, jnp.where(left8, f8, b8), jnp.where(left8, b8, f8))

    are_ref, aim_ref, ldt_ref = par_refs
    in_first = lax.broadcasted_iota(jnp.int32, (1, LANES), 1) < SSM_STATE
    dt_lanes = lambda d: jnp.where(in_first, ldt_ref[d, 0][:, 0:1], ldt_ref[d, 0][:, 1:2])
    lr = jnp.minimum(member_rows(are_ref[0, 0], are_ref[1, 0]), -1e-4)
    li = member_rows(aim_ref[0, 0], aim_ref[1, 0])
    dt = jnp.exp(member_rows(dt_lanes(0), dt_lanes(1)))
    mag = jnp.exp(lr * dt)
    ab_re = mag * jnp.cos(li * dt)
    ab_im = mag * jnp.sin(li * dt)
    nr = ab_re - 1.0
    den = lr * lr + li * li
    f_re = (nr * lr + ab_im * li) / den
    f_im = (ab_im * lr - nr * li) / den
    pt_re, pt_im = [jnp.ones_like(ab_re)], [jnp.zeros_like(ab_im)]
    for _ in range(T):
        nxt = _cmul(pt_re[-1], pt_im[-1], ab_re, ab_im)
        pt_re.append(nxt[0])
        pt_im.append(nxt[1])

    cols = jnp.concatenate([ab_re, ab_im, jnp.zeros((LANES - 2 * SUBLANES, LANES), _F32)], axis=0).T
    left1 = lax.broadcasted_iota(jnp.int32, (1, LANES), 1) < SSM_STATE
    eye_i = _eye(SSM_GROUP)
    eye_p = _eye(SSM_STATE)
    rows = lax.broadcasted_iota(jnp.int32, (LANES, LANES), 0)
    tcol = lax.broadcasted_iota(jnp.int32, (LANES, LANES), 1) // SSM_GROUP
    zero_half = jnp.zeros((SSM_STATE, CHUNK_W), _BF16)
    for e in range(PAIR):
        fwd_first = e == 0
        d0, d1 = (0, 1) if fwd_first else (1, 0)
        b_re = _dot_nt_exact(eye_i, jnp.concatenate([br_ref[d0, e], br_ref[d1, e]], axis=0))
        b_im = _dot_nt_exact(eye_i, jnp.concatenate([bi_ref[d0, e], bi_ref[d1, e]], axis=0))
        bbar_re, bbar_im = _cmul(f_re[e:e + 1], f_im[e:e + 1], b_re, b_im)
        fwd_lane = left1 if fwd_first else jnp.logical_not(left1)
        for s in range(T):
            pr = jnp.where(fwd_lane, pt_re[T - 1 - s][e:e + 1], pt_re[s][e:e + 1])
            pi = jnp.where(fwd_lane, pt_im[T - 1 - s][e:e + 1], pt_im[s][e:e + 1])
            w_re, w_im = _cmul(pr, pi, bbar_re, bbar_im)
            wst_ref[e, s * SSM_GROUP:(s + 1) * SSM_GROUP, 0:LANES] = w_re.astype(_BF16)
            wst_ref[e, s * SSM_GROUP:(s + 1) * SSM_GROUP, LANES:] = w_im.astype(_BF16)
        fwd_row = (rows < SSM_STATE) if fwd_first else (rows >= SSM_STATE)
        expo = jnp.where(fwd_row, tcol + 1, T // 2 - tcol)
        lo_re = jnp.ones((LANES, LANES), _F32)
        lo_im = jnp.zeros((LANES, LANES), _F32)
        s_re, s_im = cols[:, e:e + 1], cols[:, SUBLANES + e:SUBLANES + e + 1]
        for bit in range((T // 2).bit_length()):
            m_re, m_im = _cmul(lo_re, lo_im, jnp.broadcast_to(s_re, lo_re.shape), jnp.broadcast_to(s_im, lo_re.shape))
            take = ((expo >> bit) & 1) == 1
            lo_re, lo_im = jnp.where(take, m_re, lo_re), jnp.where(take, m_im, lo_im)
            if (1 << bit) < T // 2:
                s_re, s_im = _cmul(s_re, s_im, s_re, s_im)
        hi_re, hi_im = _cmul(lo_re, lo_im, jnp.broadcast_to(s_re, lo_re.shape), jnp.broadcast_to(s_im, lo_re.shape))
        q_re = jnp.concatenate([jnp.where(fwd_row, lo_re, hi_re), jnp.where(fwd_row, hi_re, lo_re)], axis=1)
        q_im = jnp.concatenate([jnp.where(fwd_row, lo_im, hi_im), jnp.where(fwd_row, hi_im, lo_im)], axis=1)
        tiled = lambda c: _dot_nt_exact(eye_p, c, repeat=T)
        c_re = jnp.concatenate([tiled(cr_ref[d0, e]), tiled(cr_ref[d1, e])], axis=0)
        c_im = jnp.concatenate([tiled(ci_ref[d0, e]), tiled(ci_ref[d1, e])], axis=0)
        o_re, o_im = _cmul(c_re, c_im, q_re, q_im)
        wout_ref[e, 0:LANES] = o_re.astype(_BF16)
        wout_ref[e, LANES:] = (-o_im).astype(_BF16)
        for ri, src in enumerate((c_re, -c_im)):
            for slot, d in enumerate((d0, d1)):
                r0 = ri * LANES + slot * SSM_STATE
                block = src[slot * SSM_STATE:(slot + 1) * SSM_STATE].astype(_BF16)
                cmat_ref[e, r0:r0 + SSM_STATE, d * CHUNK_W:(d + 1) * CHUNK_W] = block
                cmat_ref[e, r0:r0 + SSM_STATE, (1 - d) * CHUNK_W:(2 - d) * CHUNK_W] = zero_half

    m0_re, m1_re = pt_re[T][0:1], pt_re[T][1:2]
    m0_im, m1_im = pt_im[T][0:1], pt_im[T][1:2]
    cf = (jnp.where(left1, m0_re, m1_re), jnp.where(left1, m0_im, m1_im))
    cb = (jnp.where(left1, m1_re, m0_re), jnp.where(left1, m1_im, m0_im))
    coefs = []
    for _ in range(n_steps):
        coefs.append([cf[0], cf[1], cb[0], cb[1]])
        cf = _cmul(cf[0], cf[1], cf[0], cf[1])
        cb = _cmul(cb[0], cb[1], cb[0], cb[1])
    return coefs


def _ssm_kernel(n_lat, n_ctx, pairs_per_step, u_ref, uc_ref, are_ref, aim_ref, ldt_ref,
                br_ref, bi_ref, cr_ref, ci_ref, d_ref, y_ref, *scratch):
    programs = []
    for p in range(pairs_per_step):
        members = slice(p * PAIR, (p + 1) * PAIR)
        programs.append(_ssm_pair(
            n_lat, n_ctx, u_ref.at[members], uc_ref.at[members],
            tuple(r.at[:, p:p + 1] for r in (are_ref, aim_ref, ldt_ref)), br_ref.at[:, members], bi_ref.at[:, members], cr_ref.at[:, members], ci_ref.at[:, members],
            d_ref.at[members], y_ref.at[members], *[s.at[p] for s in scratch]))
    live = []
    while programs or live:
        if programs:
            live.append(programs.pop(0))
        for prog in list(live):
            if next(prog, "done") == "done":
                live.remove(prog)


def _ssm_pair(n_lat, n_ctx, u_ref, uc_ref, par_refs, br_ref, bi_ref, cr_ref, ci_ref, d_ref,
              y_ref, buf_a, buf_b, ends_a, ends_b, wst_ref, wout_ref, cmat_ref, m_scr):
    n_rows = n_lat + n_ctx
    pad = SUBLANES
    coefs = _ssm_prep(_scan_steps(n_rows), par_refs, br_ref, bi_ref, cr_ref, ci_ref,
                      wst_ref, wout_ref, cmat_ref)
    yield

    lane = lax.broadcasted_iota(jnp.int32, (SSM_GROUP, CHUNK_W), 1)
    blk = lane // SSM_GROUP
    on_diag = lane % SSM_GROUP == lax.broadcasted_iota(jnp.int32, (SSM_GROUP, CHUNK_W), 0)
    for e in range(PAIR):
        d_skip = jnp.where(on_diag, jnp.broadcast_to(d_ref[e], (SSM_GROUP, CHUNK_W)), 0.0)
        taps = _dot(wst_ref[e], cmat_ref[e])
        taps_f = taps[:, 0:CHUNK_W]
        taps_b = taps[:, CHUNK_W:]
        for s in range(CHUNK):
            acc = jnp.zeros((SSM_GROUP, CHUNK_W), _F32)
            for t in range(CHUNK):
                val = None
                if s >= t:
                    r0 = (s - t) * SSM_GROUP
                    val = taps_b[r0:r0 + SSM_GROUP]
                if s <= t:
                    r0 = (CHUNK - 1 - t + s) * SSM_GROUP
                    v2 = taps_f[r0:r0 + SSM_GROUP]
                    val = v2 if val is None else val + v2
                if s == t:
                    val = val + d_skip
                acc = jnp.where(blk == t, val, acc)
            m_scr[e, s * SSM_GROUP:(s + 1) * SSM_GROUP, :] = acc.astype(_BF16)
    yield

    def paired(m0, m1):
        left = lax.broadcasted_iota(jnp.int32, (m0.shape[0], PAIR_STATE), 1) < SSM_STATE
        re0, im0 = m0[:, 0:PAIR_STATE], m0[:, PAIR_STATE:]
        re1, im1 = m1[:, 0:PAIR_STATE], m1[:, PAIR_STATE:]
        return [jnp.where(left, re0, re1), jnp.where(left, im0, im1),
                jnp.where(left, re1, re0), jnp.where(left, im1, im0)]

    s_lat = paired(_dot(u_ref[0], wst_ref[0]), _dot(u_ref[1], wst_ref[1]))
    s_ctx = paired(_dot(uc_ref[0], wst_ref[0]), _dot(uc_ref[1], wst_ref[1]))
    yield

    n_tiles = n_rows // SUBLANES
    tile_row = lax.broadcasted_iota(jnp.int32, (1, SUBLANES, PAIR_STATE), 1)

    def tile_scan(x_re, x_im, comp0, backward):
        for k in range(SUBLANES.bit_length() - 1):
            sh = 1 << k
            shift, keep = (SUBLANES - sh, tile_row < SUBLANES - sh) if backward else (sh, tile_row >= sh)
            a_re = jnp.where(keep, coefs[k][comp0], 0.0)
            a_im = jnp.where(keep, coefs[k][comp0 + 1], 0.0)
            p_re, p_im = pltpu.roll(x_re, shift, axis=1), pltpu.roll(x_im, shift, axis=1)
            x_re, x_im = x_re + a_re * p_re - a_im * p_im, x_im + a_re * p_im + a_im * p_re
        return x_re, x_im

    as_tiles = lambda parts: jnp.concatenate(parts, axis=0).reshape(n_tiles, SUBLANES, PAIR_STATE)
    local = list(tile_scan(as_tiles([s_ctx[0], s_lat[0]]), as_tiles([s_ctx[1], s_lat[1]]), 0, False))
    local += tile_scan(as_tiles([s_lat[2], s_ctx[2]]), as_tiles([s_lat[3], s_ctx[3]]), 2, True)
    for comp in range(N_COMP):
        buf_a[comp, pad:pad + n_rows] = local[comp].reshape(n_rows, PAIR_STATE)
    yield

    for buf in (ends_a, ends_b):
        buf[...] = jnp.zeros(buf.shape, _F32)
    epad = _ends_pad(n_tiles)
    for comp in range(N_COMP):
        end_row = pad + (SUBLANES - 1 if comp < 2 else 0)
        ends_a[comp, epad:epad + n_tiles] = buf_a[comp, pl.ds(end_row, n_tiles, stride=SUBLANES), :]
    src, dst = ends_a, ends_b
    for k in range(_scan_steps(n_tiles)):
        sh = 1 << k
        coef = coefs[SUBLANES.bit_length() - 1 + k]
        lo, hi = epad, epad + n_tiles
        xr, xi = src[0, lo - sh:hi - sh], src[1, lo - sh:hi - sh]
        dst[0, lo:hi] = src[0, lo:hi] + coef[0] * xr - coef[1] * xi
        dst[1, lo:hi] = src[1, lo:hi] + coef[0] * xi + coef[1] * xr
        xr, xi = src[2, lo + sh:hi + sh], src[3, lo + sh:hi + sh]
        dst[2, lo:hi] = src[2, lo:hi] + coef[2] * xr - coef[3] * xi
        dst[3, lo:hi] = src[3, lo:hi] + coef[2] * xi + coef[3] * xr
        src, dst = dst, src

    def carry_powers(comp0, backward):
        base = [(coefs[k][comp0], coefs[k][comp0 + 1]) for k in range(SUBLANES.bit_length())]
        pw = []
        for r in range(1, SUBLANES + 1):
            acc = None
            for k, term in enumerate(base):
                if (r >> k) & 1:
                    acc = term if acc is None else _cmul(acc[0], acc[1], term[0], term[1])
            pw.append(acc)
        if backward:
            pw = pw[::-1]
        return (jnp.concatenate([p[0] for p in pw], axis=0), jnp.concatenate([p[1] for p in pw], axis=0))

    for comp0, backward in ((0, False), (2, True)):
        p_re, p_im = carry_powers(comp0, backward)
        first = epad + 1 if backward else epad - 1
        for i in range(n_tiles):
            c_re = src[comp0, first + i:first + i + 1]
            c_im = src[comp0 + 1, first + i:first + i + 1]
            rows = slice(pad + i * SUBLANES, pad + (i + 1) * SUBLANES)
            buf_b[comp0, rows] = buf_a[comp0, rows] + p_re * c_re - p_im * c_im
            buf_b[comp0 + 1, rows] = buf_a[comp0 + 1, rows] + p_re * c_im + p_im * c_re
    yield

    f0 = pad + n_ctx - 1
    b0 = pad + 1
    f_re, f_im = buf_b[0, f0:f0 + n_lat], buf_b[1, f0:f0 + n_lat]
    b_re, b_im = buf_b[2, b0:b0 + n_lat], buf_b[3, b0:b0 + n_lat]
    left = lax.broadcasted_iota(jnp.int32, (n_lat, PAIR_STATE), 1) < SSM_STATE
    h0 = jnp.concatenate([jnp.where(left, f_re, b_re), jnp.where(left, f_im, b_im)], axis=1)
    h1 = jnp.concatenate([jnp.where(left, b_re, f_re), jnp.where(left, b_im, f_im)], axis=1)
    y_ref[0] = _dot(u_ref[0], m_scr[0]) + _dot(h0.astype(_BF16), wout_ref[0])
    y_ref[1] = _dot(u_ref[1], m_scr[1]) + _dot(h1.astype(_BF16), wout_ref[1])


def _ssm(u_g, uc_g, a_re, a_im, log_dt, b_re, b_im, c_re, c_im, d_col):
    n_groups, n_lat, _ = u_g.shape
    n_ctx = uc_g.shape[1]
    n_rows = n_lat + n_ctx
    n_pairs = n_groups // PAIR
    pps = 2
    gps = pps * PAIR
    buf = pltpu.VMEM((pps, N_COMP, n_rows + 2 * SUBLANES, PAIR_STATE), _F32)
    n_tiles = n_rows // SUBLANES
    ends_rows = _ends_pad(n_tiles) + -(-(n_tiles + _ends_pad(n_tiles)) // SUBLANES) * SUBLANES
    ends = pltpu.VMEM((pps, N_COMP, ends_rows, PAIR_STATE), _F32)
    mat = pltpu.VMEM((pps, PAIR, CHUNK_W, CHUNK_W), _BF16)
    pair3 = lambda q: (q, 0, 0)
    by_dir = lambda q: (0, q, 0, 0)
    return pl.pallas_call(
        functools.partial(_ssm_kernel, n_lat, n_ctx, pps),
        grid=(n_pairs // pps,),
        in_specs=[pl.BlockSpec((gps, n_lat, CHUNK_W), pair3),
                  pl.BlockSpec((gps, n_ctx, CHUNK_W), pair3),
                  pl.BlockSpec((2, pps) + a_re.shape[2:], by_dir),
                  pl.BlockSpec((2, pps) + a_im.shape[2:], by_dir),
                  pl.BlockSpec((2, pps) + log_dt.shape[2:], by_dir),
                  pl.BlockSpec((2, gps) + b_re.shape[2:], by_dir),
                  pl.BlockSpec((2, gps) + b_im.shape[2:], by_dir),
                  pl.BlockSpec((2, gps) + c_re.shape[2:], by_dir),
                  pl.BlockSpec((2, gps) + c_im.shape[2:], by_dir),
                  pl.BlockSpec((gps,) + d_col.shape[1:], pair3)],
        out_specs=pl.BlockSpec((gps, n_lat, CHUNK_W), pair3),
        out_shape=jax.ShapeDtypeStruct((n_groups, n_lat, CHUNK_W), _F32),
        scratch_shapes=[buf, buf, ends, ends, mat, mat,
                        pltpu.VMEM((pps, PAIR, CHUNK_W, 2 * CHUNK_W), _BF16), mat],
        compiler_params=pltpu.CompilerParams(dimension_semantics=("arbitrary",),
                                             vmem_limit_bytes=VMEM_LIMIT),
        name="ssm",
    )(u_g, uc_g, a_re, a_im, log_dt, b_re, b_im, c_re, c_im, d_col)


def _out_kernel(x_ref, conv_ref, yg_ref, mods_ref, gssm_ref, g2_ref, gfin_ref,
                wglu_ref, wout_ref, w1_ref, w2_ref, o_ref, y_scr, mix_scr, ones_ref):
    @pl.when(pl.program_id(0) == 0)
    def _():
        ones_ref[...] = _group_ones(MXU_W, SSM_GROUP)

    gt1 = mods_ref[0:1, 2 * D_MODEL:3 * D_MODEL]
    sh2 = mods_ref[0:1, 3 * D_MODEL:4 * D_MODEL]
    sc2 = mods_ref[0:1, 4 * D_MODEL:5 * D_MODEL]
    gt2 = mods_ref[0:1, 5 * D_MODEL:6 * D_MODEL]
    th = x_ref.shape[0] // 2
    n_split = 2
    ff = D_FF // n_split

    def mixer(r0):
        rows_ = slice(r0, r0 + th)
        chunks = slice(r0 // CHUNK, (r0 + th) // CHUNK)
        for q in range(N_SLABS):
            for half in range(CHUNK // BLOCKS):
                cols = [yg_ref[q * BLOCKS + g8, chunks, half * LANES:(half + 1) * LANES] for g8 in range(BLOCKS)]
                rows = _lane_block_transpose(cols)
                for t in range(BLOCKS):
                    y_scr[q, pl.ds(r0 + half * BLOCKS + t, th // CHUNK, stride=CHUNK), :] = rows[t]
        s = jax.nn.gelu(jnp.concatenate([y_scr[q, rows_] for q in range(N_SLABS)], axis=1))
        s = s * jax.nn.sigmoid(_dot(s.astype(_BF16), wglu_ref[...]))
        sq = (s * s).astype(_BF16)
        ssq = jnp.concatenate([_dot(sq[:, k * MXU_W:(k + 1) * MXU_W], ones_ref[...])
                               for k in range(SSM_W // MXU_W)], axis=1)
        sn = s * lax.rsqrt(ssq * (1.0 / SSM_GROUP) + RMS_EPS) * gssm_ref[...]
        mix = mix_scr[rows_] + _dot(sn.astype(_BF16), wout_ref[CONV_W:])
        x_mid = x_ref[rows_, :] + gt1 * mix
        return x_mid, (_rms_rows(x_mid, g2_ref[...] * (1.0 + sc2)) + sh2).astype(_BF16)

    def mlp_chunk(h2, j):
        a = jnp.maximum(_dot(h2, w1_ref[:, j * ff:(j + 1) * ff]), 0.0)
        return _dot((a * a).astype(_BF16), w2_ref[j * ff:(j + 1) * ff])

    def finish(r0, x_mid, acc):
        o_ref[r0:r0 + th, :] = _rms_rows(x_mid + gt2 * acc, gfin_ref[...])

    mix_scr[...] = _dot(conv_ref[...], wout_ref[0:CONV_W])
    x_a, h_a = mixer(0)
    acc_a = mlp_chunk(h_a, 0)
    x_b, h_b = mixer(th)
    for j in range(1, n_split):
        acc_a = acc_a + mlp_chunk(h_a, j)
    acc_b = mlp_chunk(h_b, 0)
    finish(0, x_a, acc_a)
    for j in range(1, n_split):
        acc_b = acc_b + mlp_chunk(h_b, j)
    finish(th, x_b, acc_b)


def _out_proj(x2d, conv_n, y_g, mods, g_ssm, g2, g_fin, wglu, wout, w1, w2, tm):
    n = x2d.shape[0]
    const = lambda i: (0, 0)
    row = lambda i: (i, 0)
    resident = lambda shape: pl.BlockSpec(shape, const, pipeline_mode=pl.Buffered(1))
    return pl.pallas_call(
        _out_kernel,
        grid=(n // tm,),
        in_specs=[pl.BlockSpec((tm, D_MODEL), row),
                  pl.BlockSpec((tm, CONV_W), row),
                  pl.BlockSpec((SSM_GROUPS, tm // CHUNK, CHUNK_W), lambda i: (0, i, 0)),
                  pl.BlockSpec(mods.shape, const),
                  pl.BlockSpec((1, SSM_W), const),
                  pl.BlockSpec((1, D_MODEL), const),
                  pl.BlockSpec((1, D_MODEL), const),
                  resident((SSM_W, SSM_W)),
                  resident((D_MODEL, D_MODEL)),
                  resident((D_MODEL, D_FF)),
                  resident((D_FF, D_MODEL))],
        out_specs=pl.BlockSpec((tm, D_MODEL), row),
        out_shape=jax.ShapeDtypeStruct((n, D_MODEL), _F32),
        scratch_shapes=[pltpu.VMEM((N_SLABS, tm, LANES), _F32), pltpu.VMEM((tm, D_MODEL), _F32),
                        pltpu.VMEM((MXU_W, MXU_W), _BF16)],
        compiler_params=pltpu.CompilerParams(dimension_semantics=("arbitrary",),
                                             vmem_limit_bytes=VMEM_LIMIT),
        name="out_proj",
    )(x2d, conv_n, y_g, mods, g_ssm, g2, g_fin, wglu, wout, w1, w2)


def kernel(x, c, ctx, c_ctx, w_mod, b_mod, g_norm1, w_in, conv_w, ssm_a_re, ssm_a_im, ssm_log_dt,
           ssm_b_re, ssm_b_im, ssm_c_re, ssm_c_im, ssm_d, w_glu, g_conv_out, g_ssm_out, w_out,
           g_norm2, w_mlp1, w_mlp2, g_final):
    bsz, n_lat, d_model = x.shape
    n_ctx = ctx.shape[1]
    assert bsz == 1 and d_model == D_MODEL and w_mod.shape[0] == 1
    assert n_lat % (CHUNK * SUBLANES) == 0 and n_ctx % (CHUNK * CHUNK) == 0 and n_lat % GRID_W == 0
    layer = 0
    x2d = x[0]
    ctx2d = ctx[0]

    mods = _modulation(c, c_ctx[None, :], w_mod[layer], b_mod[layer][None, :])

    g1 = g_norm1[layer][None, :]
    g_conv = g_conv_out[layer][None, :]
    conv_n, u_g, uc_g, w1_bf, w2_bf, wout_bf, wglu_bf = _in_proj(
        x2d, ctx2d, mods, g1, w_in[layer], conv_w[layer], g_conv, tm=1024,
        side_weights=(w_mlp1[layer], w_mlp2[layer], w_out[layer], w_glu[layer]))

    n_pairs = SSM_GROUPS // PAIR
    y_g = _ssm(u_g, uc_g,
               ssm_a_re.reshape(2, n_pairs, 1, PAIR_STATE), ssm_a_im.reshape(2, n_pairs, 1, PAIR_STATE),
               ssm_log_dt.reshape(2, n_pairs, 1, PAIR),
               ssm_b_re[layer], ssm_b_im[layer], ssm_c_re[layer], ssm_c_im[layer],
               ssm_d.reshape(SSM_GROUPS, SSM_GROUP, 1))

    out = _out_proj(x2d, conv_n, y_g, mods, g_ssm_out[layer][None, :], g_norm2[layer][None, :],
                    g_final[None, :], wglu_bf, wout_bf, w1_bf, w2_bf, tm=1024)
    return out[None]
```

```python
import functools
import math

import jax
import jax.numpy as jnp
from jax import lax
from jax.experimental import pallas as pl
from jax.experimental.pallas import tpu as pltpu

D_MODEL = 1024
GRID_W = 64
CONV_W = 512
CONV_HEADS = 8
SSM_W = 512
SSM_GROUP = 16
SSM_GROUPS = 32
SSM_STATE = 64
IN_W = 3 * CONV_W + SSM_W
D_FF = 4 * D_MODEL
RMS_EPS = 1e-6

LANES = 128
MXU_W = 256
SUBLANES = 8
CHUNK = 16
CHUNK_W = CHUNK * SSM_GROUP
PAIR = 2
PAIR_STATE = PAIR * SSM_STATE
N_COMP = 4
BLOCKS = LANES // SSM_GROUP
N_SLABS = SSM_W // LANES
VMEM_LIMIT = 62 * 1024 * 1024

_BF16 = jnp.bfloat16
_F32 = jnp.float32


def _dot(a, b):
    return jnp.dot(a, b, preferred_element_type=_F32)


def _lane_block_transpose(vs):
    blk = lax.broadcasted_iota(jnp.int32, vs[0].shape, 1) // SSM_GROUP
    dist = BLOCKS // 2
    while dist:
        shift = dist * SSM_GROUP
        upper = (blk & dist) != 0
        out = [None] * BLOCKS
        for a in range(BLOCKS):
            if a & dist:
                continue
            lo, hi = vs[a], vs[a + dist]
            out[a] = jnp.where(upper, pltpu.roll(hi, shift, axis=1), lo)
            out[a + dist] = jnp.where(upper, hi, pltpu.roll(lo, LANES - shift, axis=1))
        vs = out
        dist //= 2
    return vs


def _mod_kernel(c_ref, cctx_ref, w_ref, b_ref, o_ref):
    tk = w_ref.shape[0]
    first = lax.broadcasted_iota(jnp.int32, (SUBLANES, tk), 0) == 0
    s = jnp.where(first, jnp.broadcast_to(c_ref[...], first.shape), jnp.broadcast_to(cctx_ref[...], first.shape))
    act = s * jax.nn.sigmoid(s)
    w = w_ref[...]
    a_hi, w_hi = act.astype(_BF16), w.astype(_BF16)
    a_lo = (act - a_hi.astype(_F32)).astype(_BF16)
    w_lo = (w - w_hi.astype(_F32)).astype(_BF16)
    by_hi = _dot(jnp.concatenate([a_hi, a_lo], axis=0), w_hi)
    part = by_hi[0:SUBLANES] + by_hi[SUBLANES:] + _dot(a_hi, w_lo)

    @pl.when(pl.program_id(0) == 0)
    def _():
        o_ref[...] = part + b_ref[...]

    @pl.when(pl.program_id(0) > 0)
    def _():
        o_ref[...] += part


def _modulation(c_row, cctx_row, w_mod, b_mod):
    d_in, n_out = w_mod.shape
    tk = 256
    return pl.pallas_call(
        _mod_kernel,
        grid=(d_in // tk,),
        in_specs=[pl.BlockSpec((1, tk), lambda k: (0, k)),
                  pl.BlockSpec((1, tk), lambda k: (0, k)),
                  pl.BlockSpec((tk, n_out), lambda k: (k, 0)),
                  pl.BlockSpec((1, n_out), lambda k: (0, 0))],
        out_specs=pl.BlockSpec((SUBLANES, n_out), lambda k: (0, 0)),
        out_shape=jax.ShapeDtypeStruct((SUBLANES, n_out), _F32),
        compiler_params=pltpu.CompilerParams(dimension_semantics=("arbitrary",),
                                             vmem_limit_bytes=VMEM_LIMIT),
        name="mod",
    )(c_row, cctx_row, w_mod, b_mod)


def _rms_rows(x, g):
    ms = jnp.mean(x * x, axis=-1, keepdims=True)
    return x * lax.rsqrt(ms + RMS_EPS) * g


def _group_ones(width, group):
    rows = lax.broadcasted_iota(jnp.int32, (width, width), 0) // group
    cols = lax.broadcasted_iota(jnp.int32, (width, width), 1) // group
    return jnp.where(rows == cols, 1.0, 0.0).astype(_BF16)


def _in_kernel(n_side, x_ref, ctx_ref, mods_ref, g1_ref, w_in_ref, convw_ref, gconv_ref, *refs):
    side_in, (conv_ref, ug_ref, ucg_ref) = refs[:n_side], refs[n_side:n_side + 3]
    side_out = refs[n_side + 3:2 * n_side + 3]
    u_scr, w_scr, ones_scr = refs[2 * n_side + 3:]
    step = pl.program_id(0)
    last = pl.num_programs(0) - 1

    @pl.when(step == 0)
    def _():
        w_scr[:, 0:SSM_W] = w_in_ref[:, 3 * CONV_W:].astype(_BF16)
        for k in range(CONV_W // MXU_W):
            for part in range(3):
                dst = SSM_W + (3 * k + part) * MXU_W
                src = part * CONV_W + k * MXU_W
                w_scr[:, dst:dst + MXU_W] = w_in_ref[:, src:src + MXU_W].astype(_BF16)
        ones_scr[...] = _group_ones(MXU_W, CONV_W // CONV_HEADS)

    def normed(rows_ref, mod_row):
        sh1 = mods_ref[mod_row:mod_row + 1, 0:D_MODEL]
        sc1 = mods_ref[mod_row:mod_row + 1, D_MODEL:2 * D_MODEL]
        return (_rms_rows(rows_ref[...], g1_ref[...] * (1.0 + sc1)) + sh1).astype(_BF16)

    def to_group_major(z_u, out_ref, r0):
        n_rows = z_u.shape[0]
        chunks = slice(r0 // CHUNK, (r0 + n_rows) // CHUNK)
        for q in range(N_SLABS):
            u_scr[q, r0:r0 + n_rows] = z_u[:, q * LANES:(q + 1) * LANES]
        for q in range(N_SLABS):
            for half in range(CHUNK // BLOCKS):
                rows = [u_scr[q, pl.ds(r0 + half * BLOCKS + t, n_rows // CHUNK, stride=CHUNK), :]
                        for t in range(BLOCKS)]
                cols = _lane_block_transpose(rows)
                for g8 in range(BLOCKS):
                    out_ref[q * BLOCKS + g8, chunks, half * LANES:(half + 1) * LANES] = cols[g8].astype(_BF16)

    def conv_branch(z, r0):
        n_rows = z.shape[0]
        pos = lax.broadcasted_iota(jnp.int32, (n_rows, MXU_W), 0) % GRID_W
        for k in range(CONV_W // MXU_W):
            c0 = SSM_W + 3 * k * MXU_W
            lanes = slice(k * MXU_W, (k + 1) * MXU_W)
            b = z[:, c0:c0 + MXU_W]
            cv = z[:, c0 + MXU_W:c0 + 2 * MXU_W] * z[:, c0 + 2 * MXU_W:c0 + 3 * MXU_W]
            prev = jnp.where(pos == 0, 0.0, pltpu.roll(cv, 1, axis=0))
            nxt = jnp.where(pos == GRID_W - 1, 0.0, pltpu.roll(cv, n_rows - 1, axis=0))
            y = b * (convw_ref[0, :, lanes] * prev + convw_ref[1, :, lanes] * cv + convw_ref[2, :, lanes] * nxt)
            ssq = _dot((y * y).astype(_BF16), ones_scr[...])
            yn = y * lax.rsqrt(ssq * (1.0 / (CONV_W // CONV_HEADS)) + RMS_EPS) * gconv_ref[:, lanes]
            conv_ref[r0:r0 + n_rows, lanes] = yn.astype(_BF16)

    @pl.when(step < last)
    def _():
        for src_ref, dst_ref in zip(side_in, side_out):
            dst_ref[...] = src_ref[...].astype(_BF16)
        th = x_ref.shape[0] // 2
        z = [_dot(normed(x_ref.at[r0:r0 + th], 0), w_scr[...]) for r0 in (0, th)]
        for half, r0 in enumerate((0, th)):
            to_group_major(z[half][:, 0:SSM_W], ug_ref, r0)
            conv_branch(z[half], r0)

    @pl.when(step == last)
    def _():
        to_group_major(_dot(normed(ctx_ref, 1), w_scr[:, 0:SSM_W]), ucg_ref, 0)


def _in_proj(x2d, ctx2d, mods, g1, w_in, conv_w, g_conv, tm, side_weights):
    n, n_ctx = x2d.shape[0], ctx2d.shape[0]
    steps = n // tm
    const = lambda i: (0, 0)
    row = lambda i: (jnp.minimum(i, steps - 1), 0)
    side_specs = [pl.BlockSpec((w.shape[0] // steps, w.shape[1]), row) for w in side_weights]
    return pl.pallas_call(
        functools.partial(_in_kernel, len(side_weights)),
        grid=(steps + 1,),
        in_specs=[pl.BlockSpec((tm, D_MODEL), row),
                  pl.BlockSpec((n_ctx, D_MODEL), const),
                  pl.BlockSpec(mods.shape, const),
                  pl.BlockSpec((1, D_MODEL), const),
                  pl.BlockSpec((D_MODEL, IN_W), const, pipeline_mode=pl.Buffered(1)),
                  pl.BlockSpec((3, 1, CONV_W), lambda i: (0, 0, 0)),
                  pl.BlockSpec((1, CONV_W), const)] + side_specs,
        out_specs=[pl.BlockSpec((tm, CONV_W), row),
                   pl.BlockSpec((SSM_GROUPS, tm // CHUNK, CHUNK_W), lambda i: (0, jnp.minimum(i, steps - 1), 0)),
                   pl.BlockSpec((SSM_GROUPS, n_ctx // CHUNK, CHUNK_W), lambda i: (0, 0, 0))] + side_specs,
        out_shape=[jax.ShapeDtypeStruct((n, CONV_W), _BF16),
                   jax.ShapeDtypeStruct((SSM_GROUPS, n // CHUNK, CHUNK_W), _BF16),
                   jax.ShapeDtypeStruct((SSM_GROUPS, n_ctx // CHUNK, CHUNK_W), _BF16)]
                  + [jax.ShapeDtypeStruct(w.shape, _BF16) for w in side_weights],
        scratch_shapes=[pltpu.VMEM((N_SLABS, tm, LANES), _F32),
                        pltpu.VMEM((D_MODEL, IN_W), _BF16),
                        pltpu.VMEM((MXU_W, MXU_W), _BF16)],
        compiler_params=pltpu.CompilerParams(dimension_semantics=("arbitrary",),
                                             vmem_limit_bytes=VMEM_LIMIT),
        name="in_proj",
    )(x2d, ctx2d, mods, g1, w_in, conv_w, g_conv, *side_weights)


def _scan_steps(n_rows):
    return max(1, math.ceil(math.log2(n_rows)))


def _ends_pad(n_tiles):
    return max(SUBLANES, 1 << (_scan_steps(n_tiles) - 1))


def _cmul(ar, ai, br, bi):
    return ar * br - ai * bi, ar * bi + ai * br


def _eye(n):
    rows, cols = (lax.broadcasted_iota(jnp.int32, (n, n), axis) for axis in (0, 1))
    return jnp.where(rows == cols, 1.0, 0.0).astype(_BF16)


def _dot_nt_exact(eye, x, repeat=1):
    hi = x.astype(_BF16)
    rest = x - hi.astype(_F32)
    mid = rest.astype(_BF16)
    lo = (rest - mid.astype(_F32)).astype(_BF16)
    nt = lambda term: lax.dot_general(eye, jnp.concatenate([term] * repeat, axis=0),
                                      (((1,), (1,)), ((), ())), preferred_element_type=_F32)
    return nt(hi) + nt(mid) + nt(lo)


def _ssm_prep(n_steps, group0, par_refs, br_ref, bi_ref, cr_ref, ci_ref, wst_ref, wout_ref, cmat_ref):
    T = CHUNK
    row0 = lax.broadcasted_iota(jnp.int32, (SUBLANES, LANES), 0) == 0

    def member_rows(slot):
        m0 = jnp.concatenate([slot(0, 0), slot(1, 0)], axis=1)
        m1 = jnp.concatenate([slot(1, 1), slot(0, 1)], axis=1)
        return jnp.where(row0, jnp.broadcast_to(m0, (SUBLANES, LANES)), jnp.broadcast_to(m1, (SUBLANES, LANES)))

    are_ref, aim_ref, ldt_ref = par_refs
    lr = jnp.minimum(member_rows(lambda d, m: are_ref[d, pl.ds(group0 + m, 1), :]), -1e-4)
    li = member_rows(lambda d, m: aim_ref[d, pl.ds(group0 + m, 1), :])
    dt = jnp.exp(member_rows(lambda d, m: jnp.full((1, SSM_STATE), ldt_ref[d, group0 + m], _F32)))
    mag = jnp.exp(lr * dt)
    ab_re = mag * jnp.cos(li * dt)
    ab_im = mag * jnp.sin(li * dt)
    nr = ab_re - 1.0
    den = lr * lr + li * li
    f_re = (nr * lr + ab_im * li) / den
    f_im = (ab_im * lr - nr * li) / den
    pt_re, pt_im = [jnp.ones_like(ab_re)], [jnp.zeros_like(ab_im)]
    for _ in range(T):
        nxt = _cmul(pt_re[-1], pt_im[-1], ab_re, ab_im)
        pt_re.append(nxt[0])
        pt_im.append(nxt[1])

    cols = jnp.concatenate([ab_re, ab_im, jnp.zeros((LANES - 2 * SUBLANES, LANES), _F32)], axis=0).T
    left1 = lax.broadcasted_iota(jnp.int32, (1, LANES), 1) < SSM_STATE
    eye_p = _eye(SSM_STATE)
    rows = lax.broadcasted_iota(jnp.int32, (LANES, LANES), 0)
    tcol = lax.broadcasted_iota(jnp.int32, (LANES, LANES), 1) // SSM_GROUP
    zero_half = jnp.zeros((SSM_STATE, CHUNK_W), _BF16)
    for e in range(PAIR):
        fwd_first = e == 0
        d0, d1 = (0, 1) if fwd_first else (1, 0)
        b_re = jnp.concatenate([br_ref[d0, e], br_ref[d1, e]], axis=1)
        b_im = jnp.concatenate([bi_ref[d0, e], bi_ref[d1, e]], axis=1)
        bbar_re, bbar_im = _cmul(f_re[e:e + 1], f_im[e:e + 1], b_re, b_im)
        fwd_lane = left1 if fwd_first else jnp.logical_not(left1)
        for s in range(T):
            pr = jnp.where(fwd_lane, pt_re[T - 1 - s][e:e + 1], pt_re[s][e:e + 1])
            pi = jnp.where(fwd_lane, pt_im[T - 1 - s][e:e + 1], pt_im[s][e:e + 1])
            w_re, w_im = _cmul(pr, pi, bbar_re, bbar_im)
            wst_ref[e, s * SSM_GROUP:(s + 1) * SSM_GROUP, 0:LANES] = w_re.astype(_BF16)
            wst_ref[e, s * SSM_GROUP:(s + 1) * SSM_GROUP, LANES:] = w_im.astype(_BF16)
        fwd_row = (rows < SSM_STATE) if fwd_first else (rows >= SSM_STATE)
        expo = jnp.where(fwd_row, tcol + 1, T // 2 - tcol)
        lo_re = jnp.ones((LANES, LANES), _F32)
        lo_im = jnp.zeros((LANES, LANES), _F32)
        s_re, s_im = cols[:, e:e + 1], cols[:, SUBLANES + e:SUBLANES + e + 1]
        for bit in range((T // 2).bit_length()):
            m_re, m_im = _cmul(lo_re, lo_im, jnp.broadcast_to(s_re, lo_re.shape), jnp.broadcast_to(s_im, lo_re.shape))
            take = ((expo >> bit) & 1) == 1
            lo_re, lo_im = jnp.where(take, m_re, lo_re), jnp.where(take, m_im, lo_im)
            if (1 << bit) < T // 2:
                s_re, s_im = _cmul(s_re, s_im, s_re, s_im)
        hi_re, hi_im = _cmul(lo_re, lo_im, jnp.broadcast_to(s_re, lo_re.shape), jnp.broadcast_to(s_im, lo_re.shape))
        q_re = jnp.concatenate([jnp.where(fwd_row, lo_re, hi_re), jnp.where(fwd_row, hi_re, lo_re)], axis=1)
        q_im = jnp.concatenate([jnp.where(fwd_row, lo_im, hi_im), jnp.where(fwd_row, hi_im, lo_im)], axis=1)
        tiled = lambda c: _dot_nt_exact(eye_p, c, repeat=T)
        c_re = jnp.concatenate([tiled(cr_ref[d0, e]), tiled(cr_ref[d1, e])], axis=0)
        c_im = jnp.concatenate([tiled(ci_ref[d0, e]), tiled(ci_ref[d1, e])], axis=0)
        o_re, o_im = _cmul(c_re, c_im, q_re, q_im)
        wout_ref[e, 0:LANES] = o_re.astype(_BF16)
        wout_ref[e, LANES:] = (-o_im).astype(_BF16)
        for ri, src in enumerate((c_re, -c_im)):
            for slot, d in enumerate((d0, d1)):
                r0 = ri * LANES + slot * SSM_STATE
                block = src[slot * SSM_STATE:(slot + 1) * SSM_STATE].astype(_BF16)
                cmat_ref[e, r0:r0 + SSM_STATE, d * CHUNK_W:(d + 1) * CHUNK_W] = block
                cmat_ref[e, r0:r0 + SSM_STATE, (1 - d) * CHUNK_W:(2 - d) * CHUNK_W] = zero_half

    m0_re, m1_re = pt_re[T][0:1], pt_re[T][1:2]
    m0_im, m1_im = pt_im[T][0:1], pt_im[T][1:2]
    cf = (jnp.where(left1, m0_re, m1_re), jnp.where(left1, m0_im, m1_im))
    cb = (jnp.where(left1, m1_re, m0_re), jnp.where(left1, m1_im, m0_im))
    coefs = []
    for _ in range(n_steps):
        coefs.append([cf[0], cf[1], cb[0], cb[1]])
        cf = _cmul(cf[0], cf[1], cf[0], cf[1])
        cb = _cmul(cb[0], cb[1], cb[0], cb[1])
    return coefs


def _ssm_kernel(n_lat, n_ctx, pairs_per_step, u_ref, uc_ref, are_ref, aim_ref, ldt_ref,
                br_ref, bi_ref, cr_ref, ci_ref, d_ref, y_ref, *scratch):
    programs = []
    for p in range(pairs_per_step):
        members = slice(p * PAIR, (p + 1) * PAIR)
        group0 = (pl.program_id(0) * pairs_per_step + p) * PAIR
        programs.append(_ssm_pair(
            n_lat, n_ctx, group0, u_ref.at[members], uc_ref.at[members], (are_ref, aim_ref, ldt_ref),
            br_ref.at[:, members], bi_ref.at[:, members], cr_ref.at[:, members], ci_ref.at[:, members],
            d_ref, y_ref.at[members], *[s.at[p] for s in scratch]))
    live = []
    while programs or live:
        if programs:
            live.append(programs.pop(0))
        for prog in list(live):
            if next(prog, "done") == "done":
                live.remove(prog)


def _ssm_pair(n_lat, n_ctx, group0, u_ref, uc_ref, par_refs, br_ref, bi_ref, cr_ref, ci_ref, d_ref,
              y_ref, buf_a, buf_b, ends_a, ends_b, wst_ref, wout_ref, cmat_ref, m_scr):
    n_rows = n_lat + n_ctx
    pad = SUBLANES
    coefs = _ssm_prep(_scan_steps(n_rows), group0, par_refs, br_ref, bi_ref, cr_ref, ci_ref,
                      wst_ref, wout_ref, cmat_ref)
    yield

    lane = lax.broadcasted_iota(jnp.int32, (SSM_GROUP, CHUNK_W), 1)
    blk = lane // SSM_GROUP
    channel = lax.broadcasted_iota(jnp.int32, (SSM_GROUP, CHUNK_W), 0)
    on_diag = lane % SSM_GROUP == channel
    for e in range(PAIR):
        d_skip = jnp.zeros((SSM_GROUP, CHUNK_W), _F32)
        for i in range(SSM_GROUP):
            d_skip = jnp.where(on_diag & (channel == i), d_ref[0, (group0 + e) * SSM_GROUP + i], d_skip)
        taps = _dot(wst_ref[e], cmat_ref[e])
        taps_f = taps[:, 0:CHUNK_W]
        taps_b = taps[:, CHUNK_W:]
        for s in range(CHUNK):
            acc = jnp.zeros((SSM_GROUP, CHUNK_W), _F32)
            for t in range(CHUNK):
                val = None
                if s >= t:
                    r0 = (s - t) * SSM_GROUP
                    val = taps_b[r0:r0 + SSM_GROUP]
                if s <= t:
                    r0 = (CHUNK - 1 - t + s) * SSM_GROUP
                    v2 = taps_f[r0:r0 + SSM_GROUP]
                    val = v2 if val is None else val + v2
                if s == t:
                    val = val + d_skip
                acc = jnp.where(blk == t, val, acc)
            m_scr[e, s * SSM_GROUP:(s + 1) * SSM_GROUP, :] = acc.astype(_BF16)
    yield

    def paired(m0, m1):
        left = lax.broadcasted_iota(jnp.int32, (m0.shape[0], PAIR_STATE), 1) < SSM_STATE
        re0, im0 = m0[:, 0:PAIR_STATE], m0[:, PAIR_STATE:]
        re1, im1 = m1[:, 0:PAIR_STATE], m1[:, PAIR_STATE:]
        return [jnp.where(left, re0, re1), jnp.where(left, im0, im1),
                jnp.where(left, re1, re0), jnp.where(left, im1, im0)]

    s_lat = paired(_dot(u_ref[0], wst_ref[0]), _dot(u_ref[1], wst_ref[1]))
    s_ctx = paired(_dot(uc_ref[0], wst_ref[0]), _dot(uc_ref[1], wst_ref[1]))
    yield

    n_tiles = n_rows // SUBLANES
    tile_row = lax.broadcasted_iota(jnp.int32, (1, SUBLANES, PAIR_STATE), 1)

    def tile_scan(x_re, x_im, comp0, backward):
        for k in range(SUBLANES.bit_length() - 1):
            sh = 1 << k
            shift, keep = (SUBLANES - sh, tile_row < SUBLANES - sh) if backward else (sh, tile_row >= sh)
            a_re = jnp.where(keep, coefs[k][comp0], 0.0)
            a_im = jnp.where(keep, coefs[k][comp0 + 1], 0.0)
            p_re, p_im = pltpu.roll(x_re, shift, axis=1), pltpu.roll(x_im, shift, axis=1)
            x_re, x_im = x_re + a_re * p_re - a_im * p_im, x_im + a_re * p_im + a_im * p_re
        return x_re, x_im

    as_tiles = lambda parts: jnp.concatenate(parts, axis=0).reshape(n_tiles, SUBLANES, PAIR_STATE)
    local = list(tile_scan(as_tiles([s_ctx[0], s_lat[0]]), as_tiles([s_ctx[1], s_lat[1]]), 0, False))
    local += tile_scan(as_tiles([s_lat[2], s_ctx[2]]), as_tiles([s_lat[3], s_ctx[3]]), 2, True)
    for comp in range(N_COMP):
        buf_a[comp, pad:pad + n_rows] = local[comp].reshape(n_rows, PAIR_STATE)
    yield

    for buf in (ends_a, ends_b):
        buf[...] = jnp.zeros(buf.shape, _F32)
    epad = _ends_pad(n_tiles)
    for comp in range(N_COMP):
        end_row = pad + (SUBLANES - 1 if comp < 2 else 0)
        ends_a[comp, epad:epad + n_tiles] = buf_a[comp, pl.ds(end_row, n_tiles, stride=SUBLANES), :]
    src, dst = ends_a, ends_b
    for k in range(_scan_steps(n_tiles)):
        sh = 1 << k
        coef = coefs[SUBLANES.bit_length() - 1 + k]
        lo, hi = epad, epad + n_tiles
        xr, xi = src[0, lo - sh:hi - sh], src[1, lo - sh:hi - sh]
        dst[0, lo:hi] = src[0, lo:hi] + coef[0] * xr - coef[1] * xi
        dst[1, lo:hi] = src[1, lo:hi] + coef[0] * xi + coef[1] * xr
        xr, xi = src[2, lo + sh:hi + sh], src[3, lo + sh:hi + sh]
        dst[2, lo:hi] = src[2, lo:hi] + coef[2] * xr - coef[3] * xi
        dst[3, lo:hi] = src[3, lo:hi] + coef[2] * xi + coef[3] * xr
        src, dst = dst, src

    def carry_powers(comp0, backward):
        base = [(coefs[k][comp0], coefs[k][comp0 + 1]) for k in range(SUBLANES.bit_length())]
        pw = []
        for r in range(1, SUBLANES + 1):
            acc = None
            for k, term in enumerate(base):
                if (r >> k) & 1:
                    acc = term if acc is None else _cmul(acc[0], acc[1], term[0], term[1])
            pw.append(acc)
        if backward:
            pw = pw[::-1]
        return (jnp.concatenate([p[0] for p in pw], axis=0), jnp.concatenate([p[1] for p in pw], axis=0))

    for comp0, backward in ((0, False), (2, True)):
        p_re, p_im = carry_powers(comp0, backward)
        first = epad + 1 if backward else epad - 1
        for i in range(n_tiles):
            c_re = src[comp0, first + i:first + i + 1]
            c_im = src[comp0 + 1, first + i:first + i + 1]
            rows = slice(pad + i * SUBLANES, pad + (i + 1) * SUBLANES)
            buf_b[comp0, rows] = buf_a[comp0, rows] + p_re * c_re - p_im * c_im
            buf_b[comp0 + 1, rows] = buf_a[comp0 + 1, rows] + p_re * c_im + p_im * c_re
    yield

    f0 = pad + n_ctx - 1
    b0 = pad + 1
    f_re, f_im = buf_b[0, f0:f0 + n_lat], buf_b[1, f0:f0 + n_lat]
    b_re, b_im = buf_b[2, b0:b0 + n_lat], buf_b[3, b0:b0 + n_lat]
    left = lax.broadcasted_iota(jnp.int32, (n_lat, PAIR_STATE), 1) < SSM_STATE
    h0 = jnp.concatenate([jnp.where(left, f_re, b_re), jnp.where(left, f_im, b_im)], axis=1)
    h1 = jnp.concatenate([jnp.where(left, b_re, f_re), jnp.where(left, b_im, f_im)], axis=1)
    y_ref[0] = _dot(u_ref[0], m_scr[0]) + _dot(h0.astype(_BF16), wout_ref[0])
    y_ref[1] = _dot(u_ref[1], m_scr[1]) + _dot(h1.astype(_BF16), wout_ref[1])


def _ssm(u_g, uc_g, a_re, a_im, log_dt, b_re, b_im, c_re, c_im, d_row):
    n_groups, n_lat, _ = u_g.shape
    n_ctx = uc_g.shape[1]
    n_rows = n_lat + n_ctx
    n_pairs = n_groups // PAIR
    pps = 2
    gps = pps * PAIR
    buf = pltpu.VMEM((pps, N_COMP, n_rows + 2 * SUBLANES, PAIR_STATE), _F32)
    n_tiles = n_rows // SUBLANES
    ends_rows = _ends_pad(n_tiles) + -(-(n_tiles + _ends_pad(n_tiles)) // SUBLANES) * SUBLANES
    ends = pltpu.VMEM((pps, N_COMP, ends_rows, PAIR_STATE), _F32)
    mat = pltpu.VMEM((pps, PAIR, CHUNK_W, CHUNK_W), _BF16)
    pair3 = lambda q: (q, 0, 0)
    by_dir = lambda q: (0, q, 0, 0)
    return pl.pallas_call(
        functools.partial(_ssm_kernel, n_lat, n_ctx, pps),
        grid=(n_pairs // pps,),
        in_specs=[pl.BlockSpec((gps, n_lat, CHUNK_W), pair3),
                  pl.BlockSpec((gps, n_ctx, CHUNK_W), pair3),
                  pl.BlockSpec(a_re.shape, lambda q: (0, 0, 0)),
                  pl.BlockSpec(a_im.shape, lambda q: (0, 0, 0)),
                  pl.BlockSpec(memory_space=pltpu.SMEM),
                  pl.BlockSpec((2, gps) + b_re.shape[2:], by_dir),
                  pl.BlockSpec((2, gps) + b_im.shape[2:], by_dir),
                  pl.BlockSpec((2, gps) + c_re.shape[2:], by_dir),
                  pl.BlockSpec((2, gps) + c_im.shape[2:], by_dir),
                  pl.BlockSpec(memory_space=pltpu.SMEM)],
        out_specs=pl.BlockSpec((gps, n_lat, CHUNK_W), pair3),
        out_shape=jax.ShapeDtypeStruct((n_groups, n_lat, CHUNK_W), _F32),
        scratch_shapes=[buf, buf, ends, ends, mat, mat,
                        pltpu.VMEM((pps, PAIR, CHUNK_W, 2 * CHUNK_W), _BF16), mat],
        compiler_params=pltpu.CompilerParams(dimension_semantics=("arbitrary",),
                                             vmem_limit_bytes=VMEM_LIMIT),
        name="ssm",
    )(u_g, uc_g, a_re, a_im, log_dt, b_re, b_im, c_re, c_im, d_row)


def _out_kernel(x_ref, conv_ref, yg_ref, mods_ref, gssm_ref, g2_ref, gfin_ref,
                wglu32_ref, wout32_ref, w1_ref, w2_ref, o_ref, y_scr, mix_scr, wglu_ref, wout_ref, ones_ref):
    @pl.when(pl.program_id(0) == 0)
    def _():
        wglu_ref[...] = wglu32_ref[...].astype(_BF16)
        wout_ref[...] = wout32_ref[...].astype(_BF16)
        ones_ref[...] = _group_ones(MXU_W, SSM_GROUP)

    gt1 = mods_ref[0:1, 2 * D_MODEL:3 * D_MODEL]
    sh2 = mods_ref[0:1, 3 * D_MODEL:4 * D_MODEL]
    sc2 = mods_ref[0:1, 4 * D_MODEL:5 * D_MODEL]
    gt2 = mods_ref[0:1, 5 * D_MODEL:6 * D_MODEL]
    th = x_ref.shape[0] // 2
    n_split = 2
    ff = D_FF // n_split

    def mixer(r0):
        rows_ = slice(r0, r0 + th)
        chunks = slice(r0 // CHUNK, (r0 + th) // CHUNK)
        for q in range(N_SLABS):
            for half in range(CHUNK // BLOCKS):
                cols = [yg_ref[q * BLOCKS + g8, chunks, half * LANES:(half + 1) * LANES] for g8 in range(BLOCKS)]
                rows = _lane_block_transpose(cols)
                for t in range(BLOCKS):
                    y_scr[q, pl.ds(r0 + half * BLOCKS + t, th // CHUNK, stride=CHUNK), :] = rows[t]
        s = jax.nn.gelu(jnp.concatenate([y_scr[q, rows_] for q in range(N_SLABS)], axis=1))
        s = s * jax.nn.sigmoid(_dot(s.astype(_BF16), wglu_ref[...]))
        sq = (s * s).astype(_BF16)
        ssq = jnp.concatenate([_dot(sq[:, k * MXU_W:(k + 1) * MXU_W], ones_ref[...])
                               for k in range(SSM_W // MXU_W)], axis=1)
        sn = s * lax.rsqrt(ssq * (1.0 / SSM_GROUP) + RMS_EPS) * gssm_ref[...]
        mix = mix_scr[rows_] + _dot(sn.astype(_BF16), wout_ref[CONV_W:])
        x_mid = x_ref[rows_, :] + gt1 * mix
        return x_mid, (_rms_rows(x_mid, g2_ref[...] * (1.0 + sc2)) + sh2).astype(_BF16)

    def mlp_chunk(h2, j):
        a = jnp.maximum(_dot(h2, w1_ref[:, j * ff:(j + 1) * ff]), 0.0)
        return _dot((a * a).astype(_BF16), w2_ref[j * ff:(j + 1) * ff])

    def finish(r0, x_mid, acc):
        o_ref[r0:r0 + th, :] = _rms_rows(x_mid + gt2 * acc, gfin_ref[...])

    mix_scr[...] = _dot(conv_ref[...], wout_ref[0:CONV_W])
    x_a, h_a = mixer(0)
    acc_a = mlp_chunk(h_a, 0)
    x_b, h_b = mixer(th)
    for j in range(1, n_split):
        acc_a = acc_a + mlp_chunk(h_a, j)
    acc_b = mlp_chunk(h_b, 0)
    finish(0, x_a, acc_a)
    for j in range(1, n_split):
        acc_b = acc_b + mlp_chunk(h_b, j)
    finish(th, x_b, acc_b)


def _out_proj(x2d, conv_n, y_g, mods, g_ssm, g2, g_fin, wglu, wout, w1, w2, tm):
    n = x2d.shape[0]
    const = lambda i: (0, 0)
    row = lambda i: (i, 0)
    resident = lambda shape: pl.BlockSpec(shape, const, pipeline_mode=pl.Buffered(1))
    return pl.pallas_call(
        _out_kernel,
        grid=(n // tm,),
        in_specs=[pl.BlockSpec((tm, D_MODEL), row),
                  pl.BlockSpec((tm, CONV_W), row),
                  pl.BlockSpec((SSM_GROUPS, tm // CHUNK, CHUNK_W), lambda i: (0, i, 0)),
                  pl.BlockSpec(mods.shape, const),
                  pl.BlockSpec((1, SSM_W), const),
                  pl.BlockSpec((1, D_MODEL), const),
                  pl.BlockSpec((1, D_MODEL), const),
                  resident((SSM_W, SSM_W)),
                  resident((D_MODEL, D_MODEL)),
                  resident((D_MODEL, D_FF)),
                  resident((D_FF, D_MODEL))],
        out_specs=pl.BlockSpec((tm, D_MODEL), row),
        out_shape=jax.ShapeDtypeStruct((n, D_MODEL), _F32),
        scratch_shapes=[pltpu.VMEM((N_SLABS, tm, LANES), _F32), pltpu.VMEM((tm, D_MODEL), _F32),
                        pltpu.VMEM((SSM_W, SSM_W), _BF16), pltpu.VMEM((D_MODEL, D_MODEL), _BF16),
                        pltpu.VMEM((MXU_W, MXU_W), _BF16)],
        compiler_params=pltpu.CompilerParams(dimension_semantics=("arbitrary",),
                                             vmem_limit_bytes=VMEM_LIMIT),
        name="out_proj",
    )(x2d, conv_n, y_g, mods, g_ssm, g2, g_fin, wglu, wout, w1, w2)


def kernel(x, c, ctx, c_ctx, w_mod, b_mod, g_norm1, w_in, conv_w, ssm_a_re, ssm_a_im, ssm_log_dt,
           ssm_b_re, ssm_b_im, ssm_c_re, ssm_c_im, ssm_d, w_glu, g_conv_out, g_ssm_out, w_out,
           g_norm2, w_mlp1, w_mlp2, g_final):
    bsz, n_lat, d_model = x.shape
    n_ctx = ctx.shape[1]
    assert bsz == 1 and d_model == D_MODEL and w_mod.shape[0] == 1
    assert n_lat % (CHUNK * SUBLANES) == 0 and n_ctx % (CHUNK * CHUNK) == 0 and n_lat % GRID_W == 0
    layer = 0
    x2d = x[0]
    ctx2d = ctx[0]

    mods = _modulation(c, c_ctx[None, :], w_mod[layer], b_mod[layer][None, :])

    g1 = g_norm1[layer][None, :]
    g_conv = g_conv_out[layer][None, :]
    conv_n, u_g, uc_g, w1_bf, w2_bf = _in_proj(x2d, ctx2d, mods, g1, w_in[layer], conv_w[layer][:, None, :], g_conv,
                                                tm=1024, side_weights=(w_mlp1[layer], w_mlp2[layer]))

    y_g = _ssm(u_g, uc_g, ssm_a_re[layer], ssm_a_im[layer], ssm_log_dt[layer],
               jnp.swapaxes(ssm_b_re[layer], -1, -2), jnp.swapaxes(ssm_b_im[layer], -1, -2),
               ssm_c_re[layer], ssm_c_im[layer], ssm_d[layer:layer + 1])

    out = _out_proj(x2d, conv_n, y_g, mods, g_ssm_out[layer][None, :], g_norm2[layer][None, :],
                    g_final[None, :], w_glu[layer], w_out[layer], w1_bf, w2_bf, tm=1024)
    return out[None]
```

```python
import functools
import math

import jax
import jax.numpy as jnp
from jax import lax
from jax.experimental import pallas as pl
from jax.experimental.pallas import tpu as pltpu

D_MODEL = 1024
GRID_W = 64
CONV_W = 512
CONV_HEADS = 8
SSM_W = 512
SSM_GROUP = 16
SSM_GROUPS = 32
SSM_STATE = 64
IN_W = 3 * CONV_W + SSM_W
D_FF = 4 * D_MODEL
RMS_EPS = 1e-6

LANES = 128
MXU_W = 256
SUBLANES = 8
CHUNK = 16
CHUNK_W = CHUNK * SSM_GROUP
PAIR = 2
PAIR_STATE = PAIR * SSM_STATE
N_COMP = 4
BLOCKS = LANES // SSM_GROUP
N_SLABS = SSM_W // LANES
VMEM_LIMIT = 62 * 1024 * 1024

_BF16 = jnp.bfloat16
_F32 = jnp.float32


def _dot(a, b):
    return jnp.dot(a, b, preferred_element_type=_F32)


def _lane_block_transpose(vs):
    blk = lax.broadcasted_iota(jnp.int32, vs[0].shape, 1) // SSM_GROUP
    dist = BLOCKS // 2
    while dist:
        shift = dist * SSM_GROUP
        upper = (blk & dist) != 0
        out = [None] * BLOCKS
        for a in range(BLOCKS):
            if a & dist:
                continue
            lo, hi = vs[a], vs[a + dist]
            out[a] = jnp.where(upper, pltpu.roll(hi, shift, axis=1), lo)
            out[a + dist] = jnp.where(upper, hi, pltpu.roll(lo, LANES - shift, axis=1))
        vs = out
        dist //= 2
    return vs


def _mod_kernel(c_ref, cctx_ref, w_ref, b_ref, o_ref):
    tk = w_ref.shape[0]
    first = lax.broadcasted_iota(jnp.int32, (SUBLANES, tk), 0) == 0
    s = jnp.where(first, jnp.broadcast_to(c_ref[...], first.shape), jnp.broadcast_to(cctx_ref[...], first.shape))
    act = s * jax.nn.sigmoid(s)
    w = w_ref[...]
    a_hi, w_hi = act.astype(_BF16), w.astype(_BF16)
    a_lo = (act - a_hi.astype(_F32)).astype(_BF16)
    w_lo = (w - w_hi.astype(_F32)).astype(_BF16)
    by_hi = _dot(jnp.concatenate([a_hi, a_lo], axis=0), w_hi)
    part = by_hi[0:SUBLANES] + by_hi[SUBLANES:] + _dot(a_hi, w_lo)

    @pl.when(pl.program_id(0) == 0)
    def _():
        o_ref[...] = part + b_ref[...]

    @pl.when(pl.program_id(0) > 0)
    def _():
        o_ref[...] += part


def _modulation(c_row, cctx_row, w_mod, b_mod):
    d_in, n_out = w_mod.shape
    tk = 256
    return pl.pallas_call(
        _mod_kernel,
        grid=(d_in // tk,),
        in_specs=[pl.BlockSpec((1, tk), lambda k: (0, k)),
                  pl.BlockSpec((1, tk), lambda k: (0, k)),
                  pl.BlockSpec((tk, n_out), lambda k: (k, 0)),
                  pl.BlockSpec((1, n_out), lambda k: (0, 0))],
        out_specs=pl.BlockSpec((SUBLANES, n_out), lambda k: (0, 0)),
        out_shape=jax.ShapeDtypeStruct((SUBLANES, n_out), _F32),
        compiler_params=pltpu.CompilerParams(dimension_semantics=("arbitrary",),
                                             vmem_limit_bytes=VMEM_LIMIT),
        name="mod",
    )(c_row, cctx_row, w_mod, b_mod)


def _rms_rows(x, g):
    ms = jnp.mean(x * x, axis=-1, keepdims=True)
    return x * lax.rsqrt(ms + RMS_EPS) * g


def _group_ones(width, group):
    rows = lax.broadcasted_iota(jnp.int32, (width, width), 0) // group
    cols = lax.broadcasted_iota(jnp.int32, (width, width), 1) // group
    return jnp.where(rows == cols, 1.0, 0.0).astype(_BF16)


def _in_kernel(n_side, x_ref, ctx_ref, mods_ref, g1_ref, w_in_ref, convw_ref, gconv_ref, *refs):
    side_in, (conv_ref, ug_ref, ucg_ref) = refs[:n_side], refs[n_side:n_side + 3]
    side_out = refs[n_side + 3:2 * n_side + 3]
    u_scr, w_scr, ones_scr = refs[2 * n_side + 3:]
    step = pl.program_id(0)
    last = pl.num_programs(0) - 1

    @pl.when(step == 0)
    def _():
        w_scr[:, 0:SSM_W] = w_in_ref[:, 3 * CONV_W:].astype(_BF16)
        for k in range(CONV_W // MXU_W):
            for part in range(3):
                dst = SSM_W + (3 * k + part) * MXU_W
                src = part * CONV_W + k * MXU_W
                w_scr[:, dst:dst + MXU_W] = w_in_ref[:, src:src + MXU_W].astype(_BF16)
        ones_scr[...] = _group_ones(MXU_W, CONV_W // CONV_HEADS)

    def normed(rows_ref, mod_row):
        sh1 = mods_ref[mod_row:mod_row + 1, 0:D_MODEL]
        sc1 = mods_ref[mod_row:mod_row + 1, D_MODEL:2 * D_MODEL]
        return (_rms_rows(rows_ref[...], g1_ref[...] * (1.0 + sc1)) + sh1).astype(_BF16)

    def to_group_major(z_u, out_ref, r0):
        n_rows = z_u.shape[0]
        chunks = slice(r0 // CHUNK, (r0 + n_rows) // CHUNK)
        for q in range(N_SLABS):
            u_scr[q, r0:r0 + n_rows] = z_u[:, q * LANES:(q + 1) * LANES]
        for q in range(N_SLABS):
            for half in range(CHUNK // BLOCKS):
                rows = [u_scr[q, pl.ds(r0 + half * BLOCKS + t, n_rows // CHUNK, stride=CHUNK), :]
                        for t in range(BLOCKS)]
                cols = _lane_block_transpose(rows)
                for g8 in range(BLOCKS):
                    out_ref[q * BLOCKS + g8, chunks, half * LANES:(half + 1) * LANES] = cols[g8].astype(_BF16)

    def conv_branch(z, r0):
        n_rows = z.shape[0]
        pos = lax.broadcasted_iota(jnp.int32, (n_rows, MXU_W), 0) % GRID_W
        for k in range(CONV_W // MXU_W):
            c0 = SSM_W + 3 * k * MXU_W
            lanes = slice(k * MXU_W, (k + 1) * MXU_W)
            b = z[:, c0:c0 + MXU_W]
            cv = z[:, c0 + MXU_W:c0 + 2 * MXU_W] * z[:, c0 + 2 * MXU_W:c0 + 3 * MXU_W]
            prev = jnp.where(pos == 0, 0.0, pltpu.roll(cv, 1, axis=0))
            nxt = jnp.where(pos == GRID_W - 1, 0.0, pltpu.roll(cv, n_rows - 1, axis=0))
            y = b * (convw_ref[0, :, lanes] * prev + convw_ref[1, :, lanes] * cv + convw_ref[2, :, lanes] * nxt)
            ssq = _dot((y * y).astype(_BF16), ones_scr[...])
            yn = y * lax.rsqrt(ssq * (1.0 / (CONV_W // CONV_HEADS)) + RMS_EPS) * gconv_ref[:, lanes]
            conv_ref[r0:r0 + n_rows, lanes] = yn.astype(_BF16)

    @pl.when(step < last)
    def _():
        for src_ref, dst_ref in zip(side_in, side_out):
            dst_ref[...] = src_ref[...].astype(_BF16)
        th = x_ref.shape[0] // 2
        z = [_dot(normed(x_ref.at[r0:r0 + th], 0), w_scr[...]) for r0 in (0, th)]
        for half, r0 in enumerate((0, th)):
            to_group_major(z[half][:, 0:SSM_W], ug_ref, r0)
            conv_branch(z[half], r0)

    @pl.when(step == last)
    def _():
        to_group_major(_dot(normed(ctx_ref, 1), w_scr[:, 0:SSM_W]), ucg_ref, 0)


def _in_proj(x2d, ctx2d, mods, g1, w_in, conv_w, g_conv, tm, side_weights):
    n, n_ctx = x2d.shape[0], ctx2d.shape[0]
    steps = n // tm
    const = lambda i: (0, 0)
    row = lambda i: (jnp.minimum(i, steps - 1), 0)
    side_specs = [pl.BlockSpec((w.shape[0] // steps, w.shape[1]), row) for w in side_weights]
    return pl.pallas_call(
        functools.partial(_in_kernel, len(side_weights)),
        grid=(steps + 1,),
        in_specs=[pl.BlockSpec((tm, D_MODEL), row),
                  pl.BlockSpec((n_ctx, D_MODEL), const),
                  pl.BlockSpec(mods.shape, const),
                  pl.BlockSpec((1, D_MODEL), const),
                  pl.BlockSpec((D_MODEL, IN_W), const, pipeline_mode=pl.Buffered(1)),
                  pl.BlockSpec((3, 1, CONV_W), lambda i: (0, 0, 0)),
                  pl.BlockSpec((1, CONV_W), const)] + side_specs,
        out_specs=[pl.BlockSpec((tm, CONV_W), row),
                   pl.BlockSpec((SSM_GROUPS, tm // CHUNK, CHUNK_W), lambda i: (0, jnp.minimum(i, steps - 1), 0)),
                   pl.BlockSpec((SSM_GROUPS, n_ctx // CHUNK, CHUNK_W), lambda i: (0, 0, 0))] + side_specs,
        out_shape=[jax.ShapeDtypeStruct((n, CONV_W), _BF16),
                   jax.ShapeDtypeStruct((SSM_GROUPS, n // CHUNK, CHUNK_W), _BF16),
                   jax.ShapeDtypeStruct((SSM_GROUPS, n_ctx // CHUNK, CHUNK_W), _BF16)]
                  + [jax.ShapeDtypeStruct(w.shape, _BF16) for w in side_weights],
        scratch_shapes=[pltpu.VMEM((N_SLABS, tm, LANES), _F32),
                        pltpu.VMEM((D_MODEL, IN_W), _BF16),
                        pltpu.VMEM((MXU_W, MXU_W), _BF16)],
        compiler_params=pltpu.CompilerParams(dimension_semantics=("arbitrary",),
                                             vmem_limit_bytes=VMEM_LIMIT),
        name="in_proj",
    )(x2d, ctx2d, mods, g1, w_in, conv_w, g_conv, *side_weights)


def _scan_steps(n_rows):
    return max(1, math.ceil(math.log2(n_rows)))


def _ends_pad(n_tiles):
    return max(SUBLANES, 1 << (_scan_steps(n_tiles) - 1))


def _cmul(ar, ai, br, bi):
    return ar * br - ai * bi, ar * bi + ai * br


def _eye(n):
    rows, cols = (lax.broadcasted_iota(jnp.int32, (n, n), axis) for axis in (0, 1))
    return jnp.where(rows == cols, 1.0, 0.0).astype(_BF16)


def _dot_nt_exact(eye, x, repeat=1):
    hi = x.astype(_BF16)
    rest = x - hi.astype(_F32)
    mid = rest.astype(_BF16)
    lo = (rest - mid.astype(_F32)).astype(_BF16)
    nt = lambda term: lax.dot_general(eye, jnp.concatenate([term] * repeat, axis=0),
                                      (((1,), (1,)), ((), ())), preferred_element_type=_F32)
    return nt(hi) + nt(mid) + nt(lo)


def _ssm_prep(n_steps, group0, par_refs, br_ref, bi_ref, cr_ref, ci_ref, wst_ref, wout_ref):
    T = CHUNK
    row0 = lax.broadcasted_iota(jnp.int32, (SUBLANES, LANES), 0) == 0

    def member_rows(slot):
        m0 = jnp.concatenate([slot(0, 0), slot(1, 0)], axis=1)
        m1 = jnp.concatenate([slot(1, 1), slot(0, 1)], axis=1)
        return jnp.where(row0, jnp.broadcast_to(m0, (SUBLANES, LANES)), jnp.broadcast_to(m1, (SUBLANES, LANES)))

    are_ref, aim_ref, ldt_ref = par_refs
    lr = jnp.minimum(member_rows(lambda d, m: are_ref[d, pl.ds(group0 + m, 1), :]), -1e-4)
    li = member_rows(lambda d, m: aim_ref[d, pl.ds(group0 + m, 1), :])
    dt = jnp.exp(member_rows(lambda d, m: jnp.full((1, SSM_STATE), ldt_ref[d, group0 + m], _F32)))
    mag = jnp.exp(lr * dt)
    ab_re = mag * jnp.cos(li * dt)
    ab_im = mag * jnp.sin(li * dt)
    nr = ab_re - 1.0
    den = lr * lr + li * li
    f_re = (nr * lr + ab_im * li) / den
    f_im = (ab_im * lr - nr * li) / den
    pt_re, pt_im = [jnp.ones_like(ab_re)], [jnp.zeros_like(ab_im)]
    for _ in range(T):
        nxt = _cmul(pt_re[-1], pt_im[-1], ab_re, ab_im)
        pt_re.append(nxt[0])
        pt_im.append(nxt[1])

    cols = jnp.concatenate([ab_re, ab_im, jnp.zeros((LANES - 2 * SUBLANES, LANES), _F32)], axis=0).T
    left1 = lax.broadcasted_iota(jnp.int32, (1, LANES), 1) < SSM_STATE
    eye_p = _eye(SSM_STATE)
    rows = lax.broadcasted_iota(jnp.int32, (LANES, LANES), 0)
    tcol = lax.broadcasted_iota(jnp.int32, (LANES, LANES), 1) // SSM_GROUP
    taps = []
    for e in range(PAIR):
        fwd_first = e == 0
        d0, d1 = (0, 1) if fwd_first else (1, 0)
        b_re = jnp.concatenate([br_ref[d0, e], br_ref[d1, e]], axis=1)
        b_im = jnp.concatenate([bi_ref[d0, e], bi_ref[d1, e]], axis=1)
        bbar_re, bbar_im = _cmul(f_re[e:e + 1], f_im[e:e + 1], b_re, b_im)
        fwd_lane = left1 if fwd_first else jnp.logical_not(left1)
        for s in range(T):
            pr = jnp.where(fwd_lane, pt_re[T - 1 - s][e:e + 1], pt_re[s][e:e + 1])
            pi = jnp.where(fwd_lane, pt_im[T - 1 - s][e:e + 1], pt_im[s][e:e + 1])
            w_re, w_im = _cmul(pr, pi, bbar_re, bbar_im)
            wst_ref[e, s * SSM_GROUP:(s + 1) * SSM_GROUP, 0:LANES] = w_re.astype(_BF16)
            wst_ref[e, s * SSM_GROUP:(s + 1) * SSM_GROUP, LANES:] = w_im.astype(_BF16)
        fwd_row = (rows < SSM_STATE) if fwd_first else (rows >= SSM_STATE)
        expo = jnp.where(fwd_row, tcol + 1, T // 2 - tcol)
        lo_re = jnp.ones((LANES, LANES), _F32)
        lo_im = jnp.zeros((LANES, LANES), _F32)
        s_re, s_im = cols[:, e:e + 1], cols[:, SUBLANES + e:SUBLANES + e + 1]
        for bit in range((T // 2).bit_length()):
            m_re, m_im = _cmul(lo_re, lo_im, jnp.broadcast_to(s_re, lo_re.shape), jnp.broadcast_to(s_im, lo_re.shape))
            take = ((expo >> bit) & 1) == 1
            lo_re, lo_im = jnp.where(take, m_re, lo_re), jnp.where(take, m_im, lo_im)
            if (1 << bit) < T // 2:
                s_re, s_im = _cmul(s_re, s_im, s_re, s_im)
        hi_re, hi_im = _cmul(lo_re, lo_im, jnp.broadcast_to(s_re, lo_re.shape), jnp.broadcast_to(s_im, lo_re.shape))
        q_re = jnp.concatenate([jnp.where(fwd_row, lo_re, hi_re), jnp.where(fwd_row, hi_re, lo_re)], axis=1)
        q_im = jnp.concatenate([jnp.where(fwd_row, lo_im, hi_im), jnp.where(fwd_row, hi_im, lo_im)], axis=1)
        tiled = lambda c: _dot_nt_exact(eye_p, c, repeat=T)
        c_re = jnp.concatenate([tiled(cr_ref[d0, e]), tiled(cr_ref[d1, e])], axis=0)
        c_im = jnp.concatenate([tiled(ci_ref[d0, e]), tiled(ci_ref[d1, e])], axis=0)
        o_re, o_im = _cmul(c_re, c_im, q_re, q_im)
        wout = jnp.concatenate([o_re.astype(_BF16), (-o_im).astype(_BF16)], axis=0)
        wout_ref[e] = wout
        bbar = jnp.concatenate([bbar_re, bbar_im], axis=1)
        fwd_slot = jnp.concatenate([fwd_lane, fwd_lane], axis=1)
        one_way = jnp.concatenate([jnp.where(fwd_slot, bbar, 0.0), jnp.where(fwd_slot, 0.0, bbar)], axis=0)
        k_fb = _dot(one_way.astype(_BF16), wout)
        k_0 = _dot(bbar.astype(_BF16), jnp.concatenate([c_re.astype(_BF16), (-c_im).astype(_BF16)], axis=0))
        taps.append((k_fb[0:SSM_GROUP], k_fb[SSM_GROUP:], k_0))

    m0_re, m1_re = pt_re[T][0:1], pt_re[T][1:2]
    m0_im, m1_im = pt_im[T][0:1], pt_im[T][1:2]
    cf = (jnp.where(left1, m0_re, m1_re), jnp.where(left1, m0_im, m1_im))
    cb = (jnp.where(left1, m1_re, m0_re), jnp.where(left1, m1_im, m0_im))
    coefs = []
    for _ in range(n_steps):
        coefs.append([cf[0], cf[1], cb[0], cb[1]])
        cf = _cmul(cf[0], cf[1], cf[0], cf[1])
        cb = _cmul(cb[0], cb[1], cb[0], cb[1])
    return coefs, taps


def _ssm_kernel(n_lat, n_ctx, pairs_per_step, u_ref, uc_ref, are_ref, aim_ref, ldt_ref,
                br_ref, bi_ref, cr_ref, ci_ref, d_ref, y_ref, *scratch):
    programs = []
    for p in range(pairs_per_step):
        members = slice(p * PAIR, (p + 1) * PAIR)
        group0 = (pl.program_id(0) * pairs_per_step + p) * PAIR
        programs.append(_ssm_pair(
            n_lat, n_ctx, group0, u_ref.at[members], uc_ref.at[members], (are_ref, aim_ref, ldt_ref),
            br_ref.at[:, members], bi_ref.at[:, members], cr_ref.at[:, members], ci_ref.at[:, members],
            d_ref, y_ref.at[members], *[s.at[p] for s in scratch]))
    live = []
    while programs or live:
        if programs:
            live.append(programs.pop(0))
        for prog in list(live):
            if next(prog, "done") == "done":
                live.remove(prog)


def _ssm_pair(n_lat, n_ctx, group0, u_ref, uc_ref, par_refs, br_ref, bi_ref, cr_ref, ci_ref, d_ref,
              y_ref, buf_a, buf_b, ends_a, ends_b, wst_ref, wout_ref, m_scr):
    n_rows = n_lat + n_ctx
    pad = SUBLANES
    coefs, taps = _ssm_prep(_scan_steps(n_rows), group0, par_refs, br_ref, bi_ref, cr_ref, ci_ref,
                            wst_ref, wout_ref)
    yield

    lane = lax.broadcasted_iota(jnp.int32, (SSM_GROUP, LANES), 1)
    channel = lax.broadcasted_iota(jnp.int32, (SSM_GROUP, LANES), 0)
    on_diag = lane % SSM_GROUP == channel
    shifted = lane < LANES - SSM_GROUP
    for e in range(PAIR):
        k_f, k_b, k_0 = taps[e]
        d_skip = jnp.zeros((SSM_GROUP, LANES), _F32)
        for i in range(SSM_GROUP):
            d_skip = jnp.where(on_diag & (channel == i), d_ref[0, (group0 + e) * SSM_GROUP + i], d_skip)
        kb_lo, kb_hi = (pltpu.roll(k_b[:, q * LANES:(q + 1) * LANES], LANES - SSM_GROUP, axis=1) for q in range(2))
        strip = [jnp.where(shifted, kb_lo, kb_hi),
                 jnp.where(shifted, kb_hi, k_0[:, LANES:] + d_skip),
                 k_f[:, 0:LANES], k_f[:, LANES:]]
        for s in range(CHUNK):
            first, offset = divmod((CHUNK - 1 - s) * SSM_GROUP, LANES)
            if offset == 0:
                window = strip[first:first + 2]
            else:
                turned = [pltpu.roll(strip[first + q], LANES - offset, axis=1) for q in range(3)]
                window = [jnp.where(lane < LANES - offset, turned[q], turned[q + 1]) for q in range(2)]
            m_scr[e, s * SSM_GROUP:(s + 1) * SSM_GROUP, :] = jnp.concatenate(window, axis=1).astype(_BF16)
    yield

    def paired(m0, m1):
        left = lax.broadcasted_iota(jnp.int32, (m0.shape[0], PAIR_STATE), 1) < SSM_STATE
        re0, im0 = m0[:, 0:PAIR_STATE], m0[:, PAIR_STATE:]
        re1, im1 = m1[:, 0:PAIR_STATE], m1[:, PAIR_STATE:]
        return [jnp.where(left, re0, re1), jnp.where(left, im0, im1),
                jnp.where(left, re1, re0), jnp.where(left, im1, im0)]

    s_lat = paired(_dot(u_ref[0], wst_ref[0]), _dot(u_ref[1], wst_ref[1]))
    s_ctx = paired(_dot(uc_ref[0], wst_ref[0]), _dot(uc_ref[1], wst_ref[1]))
    yield

    n_tiles = n_rows // SUBLANES
    tile_row = lax.broadcasted_iota(jnp.int32, (1, SUBLANES, PAIR_STATE), 1)

    def tile_scan(x_re, x_im, comp0, backward):
        for k in range(SUBLANES.bit_length() - 1):
            sh = 1 << k
            shift, keep = (SUBLANES - sh, tile_row < SUBLANES - sh) if backward else (sh, tile_row >= sh)
            a_re = jnp.where(keep, coefs[k][comp0], 0.0)
            a_im = jnp.where(keep, coefs[k][comp0 + 1], 0.0)
            p_re, p_im = pltpu.roll(x_re, shift, axis=1), pltpu.roll(x_im, shift, axis=1)
            x_re, x_im = x_re + a_re * p_re - a_im * p_im, x_im + a_re * p_im + a_im * p_re
        return x_re, x_im

    as_tiles = lambda parts: jnp.concatenate(parts, axis=0).reshape(n_tiles, SUBLANES, PAIR_STATE)
    local = list(tile_scan(as_tiles([s_ctx[0], s_lat[0]]), as_tiles([s_ctx[1], s_lat[1]]), 0, False))
    local += tile_scan(as_tiles([s_lat[2], s_ctx[2]]), as_tiles([s_lat[3], s_ctx[3]]), 2, True)
    for comp in range(N_COMP):
        buf_a[comp, pad:pad + n_rows] = local[comp].reshape(n_rows, PAIR_STATE)
    yield

    for buf in (ends_a, ends_b):
        buf[...] = jnp.zeros(buf.shape, _F32)
    epad = _ends_pad(n_tiles)
    for comp in range(N_COMP):
        end_row = pad + (SUBLANES - 1 if comp < 2 else 0)
        ends_a[comp, epad:epad + n_tiles] = buf_a[comp, pl.ds(end_row, n_tiles, stride=SUBLANES), :]
    src, dst = ends_a, ends_b
    for k in range(_scan_steps(n_tiles)):
        sh = 1 << k
        coef = coefs[SUBLANES.bit_length() - 1 + k]
        lo, hi = epad, epad + n_tiles
        xr, xi = src[0, lo - sh:hi - sh], src[1, lo - sh:hi - sh]
        dst[0, lo:hi] = src[0, lo:hi] + coef[0] * xr - coef[1] * xi
        dst[1, lo:hi] = src[1, lo:hi] + coef[0] * xi + coef[1] * xr
        xr, xi = src[2, lo + sh:hi + sh], src[3, lo + sh:hi + sh]
        dst[2, lo:hi] = src[2, lo:hi] + coef[2] * xr - coef[3] * xi
        dst[3, lo:hi] = src[3, lo:hi] + coef[2] * xi + coef[3] * xr
        src, dst = dst, src

    def carry_powers(comp0, backward):
        base = [(coefs[k][comp0], coefs[k][comp0 + 1]) for k in range(SUBLANES.bit_length())]
        pw = []
        for r in range(1, SUBLANES + 1):
            acc = None
            for k, term in enumerate(base):
                if (r >> k) & 1:
                    acc = term if acc is None else _cmul(acc[0], acc[1], term[0], term[1])
            pw.append(acc)
        if backward:
            pw = pw[::-1]
        return (jnp.concatenate([p[0] for p in pw], axis=0), jnp.concatenate([p[1] for p in pw], axis=0))

    for comp0, backward in ((0, False), (2, True)):
        p_re, p_im = carry_powers(comp0, backward)
        first = epad + 1 if backward else epad - 1
        for i in range(n_tiles):
            c_re = src[comp0, first + i:first + i + 1]
            c_im = src[comp0 + 1, first + i:first + i + 1]
            rows = slice(pad + i * SUBLANES, pad + (i + 1) * SUBLANES)
            buf_b[comp0, rows] = buf_a[comp0, rows] + p_re * c_re - p_im * c_im
            buf_b[comp0 + 1, rows] = buf_a[comp0 + 1, rows] + p_re * c_im + p_im * c_re
    yield

    f0 = pad + n_ctx - 1
    b0 = pad + 1
    f_re, f_im = buf_b[0, f0:f0 + n_lat], buf_b[1, f0:f0 + n_lat]
    b_re, b_im = buf_b[2, b0:b0 + n_lat], buf_b[3, b0:b0 + n_lat]
    left = lax.broadcasted_iota(jnp.int32, (n_lat, PAIR_STATE), 1) < SSM_STATE
    h0 = jnp.concatenate([jnp.where(left, f_re, b_re), jnp.where(left, f_im, b_im)], axis=1)
    h1 = jnp.concatenate([jnp.where(left, b_re, f_re), jnp.where(left, b_im, f_im)], axis=1)
    y_ref[0] = _dot(u_ref[0], m_scr[0]) + _dot(h0.astype(_BF16), wout_ref[0])
    y_ref[1] = _dot(u_ref[1], m_scr[1]) + _dot(h1.astype(_BF16), wout_ref[1])


def _ssm(u_g, uc_g, a_re, a_im, log_dt, b_re, b_im, c_re, c_im, d_row):
    n_groups, n_lat, _ = u_g.shape
    n_ctx = uc_g.shape[1]
    n_rows = n_lat + n_ctx
    n_pairs = n_groups // PAIR
    pps = 2
    gps = pps * PAIR
    buf = pltpu.VMEM((pps, N_COMP, n_rows + 2 * SUBLANES, PAIR_STATE), _F32)
    n_tiles = n_rows // SUBLANES
    ends_rows = _ends_pad(n_tiles) + -(-(n_tiles + _ends_pad(n_tiles)) // SUBLANES) * SUBLANES
    ends = pltpu.VMEM((pps, N_COMP, ends_rows, PAIR_STATE), _F32)
    mat = pltpu.VMEM((pps, PAIR, CHUNK_W, CHUNK_W), _BF16)
    pair3 = lambda q: (q, 0, 0)
    by_dir = lambda q: (0, q, 0, 0)
    return pl.pallas_call(
        functools.partial(_ssm_kernel, n_lat, n_ctx, pps),
        grid=(n_pairs // pps,),
        in_specs=[pl.BlockSpec((gps, n_lat, CHUNK_W), pair3),
                  pl.BlockSpec((gps, n_ctx, CHUNK_W), pair3),
                  pl.BlockSpec(a_re.shape, lambda q: (0, 0, 0)),
                  pl.BlockSpec(a_im.shape, lambda q: (0, 0, 0)),
                  pl.BlockSpec(memory_space=pltpu.SMEM),
                  pl.BlockSpec((2, gps) + b_re.shape[2:], by_dir),
                  pl.BlockSpec((2, gps) + b_im.shape[2:], by_dir),
                  pl.BlockSpec((2, gps) + c_re.shape[2:], by_dir),
                  pl.BlockSpec((2, gps) + c_im.shape[2:], by_dir),
                  pl.BlockSpec(memory_space=pltpu.SMEM)],
        out_specs=pl.BlockSpec((gps, n_lat, CHUNK_W), pair3),
        out_shape=jax.ShapeDtypeStruct((n_groups, n_lat, CHUNK_W), _F32),
        scratch_shapes=[buf, buf, ends, ends, mat, mat, mat],
        compiler_params=pltpu.CompilerParams(dimension_semantics=("arbitrary",),
                                             vmem_limit_bytes=VMEM_LIMIT),
        name="ssm",
    )(u_g, uc_g, a_re, a_im, log_dt, b_re, b_im, c_re, c_im, d_row)


def _out_kernel(x_ref, conv_ref, yg_ref, mods_ref, gssm_ref, g2_ref, gfin_ref,
                wglu32_ref, wout32_ref, w1_ref, w2_ref, o_ref, y_scr, mix_scr, wglu_ref, wout_ref, ones_ref):
    @pl.when(pl.program_id(0) == 0)
    def _():
        wglu_ref[...] = wglu32_ref[...].astype(_BF16)
        wout_ref[...] = wout32_ref[...].astype(_BF16)
        ones_ref[...] = _group_ones(MXU_W, SSM_GROUP)

    gt1 = mods_ref[0:1, 2 * D_MODEL:3 * D_MODEL]
    sh2 = mods_ref[0:1, 3 * D_MODEL:4 * D_MODEL]
    sc2 = mods_ref[0:1, 4 * D_MODEL:5 * D_MODEL]
    gt2 = mods_ref[0:1, 5 * D_MODEL:6 * D_MODEL]
    th = x_ref.shape[0] // 2
    n_split = 2
    ff = D_FF // n_split

    def mixer(r0):
        rows_ = slice(r0, r0 + th)
        chunks = slice(r0 // CHUNK, (r0 + th) // CHUNK)
        for q in range(N_SLABS):
            for half in range(CHUNK // BLOCKS):
                cols = [yg_ref[q * BLOCKS + g8, chunks, half * LANES:(half + 1) * LANES] for g8 in range(BLOCKS)]
                rows = _lane_block_transpose(cols)
                for t in range(BLOCKS):
                    y_scr[q, pl.ds(r0 + half * BLOCKS + t, th // CHUNK, stride=CHUNK), :] = rows[t]
        s = jax.nn.gelu(jnp.concatenate([y_scr[q, rows_] for q in range(N_SLABS)], axis=1))
        s = s * jax.nn.sigmoid(_dot(s.astype(_BF16), wglu_ref[...]))
        sq = (s * s).astype(_BF16)
        ssq = jnp.concatenate([_dot(sq[:, k * MXU_W:(k + 1) * MXU_W], ones_ref[...])
                               for k in range(SSM_W // MXU_W)], axis=1)
        sn = s * lax.rsqrt(ssq * (1.0 / SSM_GROUP) + RMS_EPS) * gssm_ref[...]
        mix = mix_scr[rows_] + _dot(sn.astype(_BF16), wout_ref[CONV_W:])
        x_mid = x_ref[rows_, :] + gt1 * mix
        return x_mid, (_rms_rows(x_mid, g2_ref[...] * (1.0 + sc2)) + sh2).astype(_BF16)

    def mlp_chunk(h2, j):
        a = jnp.maximum(_dot(h2, w1_ref[:, j * ff:(j + 1) * ff]), 0.0)
        return _dot((a * a).astype(_BF16), w2_ref[j * ff:(j + 1) * ff])

    def finish(r0, x_mid, acc):
        o_ref[r0:r0 + th, :] = _rms_rows(x_mid + gt2 * acc, gfin_ref[...])

    mix_scr[...] = _dot(conv_ref[...], wout_ref[0:CONV_W])
    x_a, h_a = mixer(0)
    acc_a = mlp_chunk(h_a, 0)
    x_b, h_b = mixer(th)
    for j in range(1, n_split):
        acc_a = acc_a + mlp_chunk(h_a, j)
    acc_b = mlp_chunk(h_b, 0)
    finish(0, x_a, acc_a)
    for j in range(1, n_split):
        acc_b = acc_b + mlp_chunk(h_b, j)
    finish(th, x_b, acc_b)


def _out_proj(x2d, conv_n, y_g, mods, g_ssm, g2, g_fin, wglu, wout, w1, w2, tm):
    n = x2d.shape[0]
    const = lambda i: (0, 0)
    row = lambda i: (i, 0)
    resident = lambda shape: pl.BlockSpec(shape, const, pipeline_mode=pl.Buffered(1))
    return pl.pallas_call(
        _out_kernel,
        grid=(n // tm,),
        in_specs=[pl.BlockSpec((tm, D_MODEL), row),
                  pl.BlockSpec((tm, CONV_W), row),
                  pl.BlockSpec((SSM_GROUPS, tm // CHUNK, CHUNK_W), lambda i: (0, i, 0)),
                  pl.BlockSpec(mods.shape, const),
                  pl.BlockSpec((1, SSM_W), const),
                  pl.BlockSpec((1, D_MODEL), const),
                  pl.BlockSpec((1, D_MODEL), const),
                  resident((SSM_W, SSM_W)),
                  resident((D_MODEL, D_MODEL)),
                  resident((D_MODEL, D_FF)),
                  resident((D_FF, D_MODEL))],
        out_specs=pl.BlockSpec((tm, D_MODEL), row),
        out_shape=jax.ShapeDtypeStruct((n, D_MODEL), _F32),
        scratch_shapes=[pltpu.VMEM((N_SLABS, tm, LANES), _F32), pltpu.VMEM((tm, D_MODEL), _F32),
                        pltpu.VMEM((SSM_W, SSM_W), _BF16), pltpu.VMEM((D_MODEL, D_MODEL), _BF16),
                        pltpu.VMEM((MXU_W, MXU_W), _BF16)],
        compiler_params=pltpu.CompilerParams(dimension_semantics=("arbitrary",),
                                             vmem_limit_bytes=VMEM_LIMIT),
        name="out_proj",
    )(x2d, conv_n, y_g, mods, g_ssm, g2, g_fin, wglu, wout, w1, w2)


def kernel(x, c, ctx, c_ctx, w_mod, b_mod, g_norm1, w_in, conv_w, ssm_a_re, ssm_a_im, ssm_log_dt,
           ssm_b_re, ssm_b_im, ssm_c_re, ssm_c_im, ssm_d, w_glu, g_conv_out, g_ssm_out, w_out,
           g_norm2, w_mlp1, w_mlp2, g_final):
    bsz, n_lat, d_model = x.shape
    n_ctx = ctx.shape[1]
    assert bsz == 1 and d_model == D_MODEL and w_mod.shape[0] == 1
    assert n_lat % (CHUNK * SUBLANES) == 0 and n_ctx % (CHUNK * CHUNK) == 0 and n_lat % GRID_W == 0
    layer = 0
    x2d = x[0]
    ctx2d = ctx[0]

    mods = _modulation(c, c_ctx[None, :], w_mod[layer], b_mod[layer][None, :])

    g1 = g_norm1[layer][None, :]
    g_conv = g_conv_out[layer][None, :]
    conv_n, u_g, uc_g, w1_bf, w2_bf = _in_proj(x2d, ctx2d, mods, g1, w_in[layer], conv_w[layer][:, None, :], g_conv,
                                                tm=1024, side_weights=(w_mlp1[layer], w_mlp2[layer]))

    y_g = _ssm(u_g, uc_g, ssm_a_re[layer], ssm_a_im[layer], ssm_log_dt[layer],
               jnp.swapaxes(ssm_b_re[layer], -1, -2), jnp.swapaxes(ssm_b_im[layer], -1, -2),
               ssm_c_re[layer], ssm_c_im[layer], ssm_d[layer:layer + 1])

    out = _out_proj(x2d, conv_n, y_g, mods, g_ssm_out[layer][None, :], g_norm2[layer][None, :],
                    g_final[None, :], w_glu[layer], w_out[layer], w1_bf, w2_bf, tm=1024)
    return out[None]
```

```python
import functools
import math

import jax
import jax.numpy as jnp
from jax import lax
from jax.experimental import pallas as pl
from jax.experimental.pallas import tpu as pltpu

D_MODEL = 1024
GRID_W = 64
CONV_W = 512
CONV_HEADS = 8
SSM_W = 512
SSM_GROUP = 16
SSM_GROUPS = 32
SSM_STATE = 64
IN_W = 3 * CONV_W + SSM_W
D_FF = 4 * D_MODEL
RMS_EPS = 1e-6

LANES = 128
MXU_W = 256
SUBLANES = 8
CHUNK = 32
CHUNK_W = CHUNK * SSM_GROUP
CHUNK_SLABS = CHUNK_W // LANES
PART = 2 * SUBLANES
CHUNK_PARTS = CHUNK // PART
PAIR = 2
PAIR_STATE = PAIR * SSM_STATE
N_COMP = 4
BLOCKS = LANES // SSM_GROUP
N_SLABS = SSM_W // LANES
VMEM_LIMIT = 62 * 1024 * 1024

_BF16 = jnp.bfloat16
_F32 = jnp.float32


def _dot(a, b):
    return jnp.dot(a, b, preferred_element_type=_F32)


def _part_row(part, n_rows):
    return part * (n_rows // CHUNK_PARTS)


def _lane_block_transpose(vs):
    blk = lax.broadcasted_iota(jnp.int32, vs[0].shape, 1) // SSM_GROUP
    dist = BLOCKS // 2
    while dist:
        shift = dist * SSM_GROUP
        upper = (blk & dist) != 0
        out = [None] * BLOCKS
        for a in range(BLOCKS):
            if a & dist:
                continue
            lo, hi = vs[a], vs[a + dist]
            out[a] = jnp.where(upper, pltpu.roll(hi, shift, axis=1), lo)
            out[a + dist] = jnp.where(upper, hi, pltpu.roll(lo, LANES - shift, axis=1))
        vs = out
        dist //= 2
    return vs


def _mod_kernel(c_ref, cctx_ref, w_ref, b_ref, o_ref):
    tk = w_ref.shape[0]
    first = lax.broadcasted_iota(jnp.int32, (SUBLANES, tk), 0) == 0
    s = jnp.where(first, jnp.broadcast_to(c_ref[...], first.shape), jnp.broadcast_to(cctx_ref[...], first.shape))
    act = s * jax.nn.sigmoid(s)
    w = w_ref[...]
    a_hi, w_hi = act.astype(_BF16), w.astype(_BF16)
    a_lo = (act - a_hi.astype(_F32)).astype(_BF16)
    w_lo = (w - w_hi.astype(_F32)).astype(_BF16)
    by_hi = _dot(jnp.concatenate([a_hi, a_lo], axis=0), w_hi)
    part = by_hi[0:SUBLANES] + by_hi[SUBLANES:] + _dot(a_hi, w_lo)

    @pl.when(pl.program_id(0) == 0)
    def _():
        o_ref[...] = part + b_ref[...]

    @pl.when(pl.program_id(0) > 0)
    def _():
        o_ref[...] += part


def _modulation(c_row, cctx_row, w_mod, b_mod):
    d_in, n_out = w_mod.shape
    tk = 256
    return pl.pallas_call(
        _mod_kernel,
        grid=(d_in // tk,),
        in_specs=[pl.BlockSpec((1, tk), lambda k: (0, k)),
                  pl.BlockSpec((1, tk), lambda k: (0, k)),
                  pl.BlockSpec((tk, n_out), lambda k: (k, 0)),
                  pl.BlockSpec((1, n_out), lambda k: (0, 0))],
        out_specs=pl.BlockSpec((SUBLANES, n_out), lambda k: (0, 0)),
        out_shape=jax.ShapeDtypeStruct((SUBLANES, n_out), _F32),
        compiler_params=pltpu.CompilerParams(dimension_semantics=("arbitrary",),
                                             vmem_limit_bytes=VMEM_LIMIT),
        name="mod",
    )(c_row, cctx_row, w_mod, b_mod)


def _rms_rows(x, g):
    ms = jnp.mean(x * x, axis=-1, keepdims=True)
    return x * lax.rsqrt(ms + RMS_EPS) * g


def _group_ones(width, group):
    rows = lax.broadcasted_iota(jnp.int32, (width, width), 0) // group
    cols = lax.broadcasted_iota(jnp.int32, (width, width), 1) // group
    return jnp.where(rows == cols, 1.0, 0.0).astype(_BF16)


def _in_kernel(n_side, x_ref, ctx_ref, mods_ref, g1_ref, w_in_ref, convw_ref, gconv_ref, *refs):
    side_in, (conv_ref, ug_ref, ucg_ref) = refs[:n_side], refs[n_side:n_side + 3]
    side_out = refs[n_side + 3:2 * n_side + 3]
    u_scr, w_scr, ones_scr = refs[2 * n_side + 3:]
    step = pl.program_id(0)
    last = pl.num_programs(0) - 1

    @pl.when(step == 0)
    def _():
        w_scr[:, 0:SSM_W] = w_in_ref[:, 3 * CONV_W:].astype(_BF16)
        for k in range(CONV_W // MXU_W):
            for part in range(3):
                dst = SSM_W + (3 * k + part) * MXU_W
                src = part * CONV_W + k * MXU_W
                w_scr[:, dst:dst + MXU_W] = w_in_ref[:, src:src + MXU_W].astype(_BF16)
        ones_scr[...] = _group_ones(MXU_W, CONV_W // CONV_HEADS)

    def normed(rows_ref, mod_row):
        sh1 = mods_ref[mod_row:mod_row + 1, 0:D_MODEL]
        sc1 = mods_ref[mod_row:mod_row + 1, D_MODEL:2 * D_MODEL]
        return (_rms_rows(rows_ref[...], g1_ref[...] * (1.0 + sc1)) + sh1).astype(_BF16)

    def to_group_major(z_u, out_ref, r0):
        n_rows = z_u.shape[0]
        chunks = slice(r0 // CHUNK, (r0 + n_rows) // CHUNK)
        for q in range(N_SLABS):
            for c in range(n_rows // CHUNK):
                for part in range(CHUNK_PARTS):
                    dst = r0 + _part_row(part, n_rows) + c * PART
                    src = c * CHUNK + part * PART
                    u_scr[q, dst:dst + PART] = z_u[src:src + PART, q * LANES:(q + 1) * LANES]
        for q in range(N_SLABS):
            for half in range(CHUNK // BLOCKS):
                rows = [u_scr[q, pl.ds(r0 + _part_row((half * BLOCKS + t) // PART, n_rows) + (half * BLOCKS + t) % PART,
                                       n_rows // CHUNK, stride=PART), :] for t in range(BLOCKS)]
                cols = _lane_block_transpose(rows)
                for g8 in range(BLOCKS):
                    out_ref[q * BLOCKS + g8, chunks, half * LANES:(half + 1) * LANES] = cols[g8].astype(_BF16)

    def conv_branch(z, r0):
        n_rows = z.shape[0]
        pos = lax.broadcasted_iota(jnp.int32, (n_rows, MXU_W), 0) % GRID_W
        for k in range(CONV_W // MXU_W):
            c0 = SSM_W + 3 * k * MXU_W
            lanes = slice(k * MXU_W, (k + 1) * MXU_W)
            b = z[:, c0:c0 + MXU_W]
            cv = z[:, c0 + MXU_W:c0 + 2 * MXU_W] * z[:, c0 + 2 * MXU_W:c0 + 3 * MXU_W]
            prev = jnp.where(pos == 0, 0.0, pltpu.roll(cv, 1, axis=0))
            nxt = jnp.where(pos == GRID_W - 1, 0.0, pltpu.roll(cv, n_rows - 1, axis=0))
            y = b * (convw_ref[0, :, lanes] * prev + convw_ref[1, :, lanes] * cv + convw_ref[2, :, lanes] * nxt)
            ssq = _dot((y * y).astype(_BF16), ones_scr[...])
            yn = y * lax.rsqrt(ssq * (1.0 / (CONV_W // CONV_HEADS)) + RMS_EPS) * gconv_ref[:, lanes]
            conv_ref[r0:r0 + n_rows, lanes] = yn.astype(_BF16)

    @pl.when(step < last)
    def _():
        for src_ref, dst_ref in zip(side_in, side_out):
            dst_ref[...] = src_ref[...].astype(_BF16)
        th = x_ref.shape[0] // 2
        z = [_dot(normed(x_ref.at[r0:r0 + th], 0), w_scr[...]) for r0 in (0, th)]
        for half, r0 in enumerate((0, th)):
            to_group_major(z[half][:, 0:SSM_W], ug_ref, r0)
            conv_branch(z[half], r0)

    @pl.when(step == last)
    def _():
        to_group_major(_dot(normed(ctx_ref, 1), w_scr[:, 0:SSM_W]), ucg_ref, 0)


def _in_proj(x2d, ctx2d, mods, g1, w_in, conv_w, g_conv, tm, side_weights):
    n, n_ctx = x2d.shape[0], ctx2d.shape[0]
    steps = n // tm
    const = lambda i: (0, 0)
    row = lambda i: (jnp.minimum(i, steps - 1), 0)
    side_specs = [pl.BlockSpec((w.shape[0] // steps, w.shape[1]), row) for w in side_weights]
    return pl.pallas_call(
        functools.partial(_in_kernel, len(side_weights)),
        grid=(steps + 1,),
        in_specs=[pl.BlockSpec((tm, D_MODEL), row),
                  pl.BlockSpec((n_ctx, D_MODEL), const),
                  pl.BlockSpec(mods.shape, const),
                  pl.BlockSpec((1, D_MODEL), const),
                  pl.BlockSpec((D_MODEL, IN_W), const, pipeline_mode=pl.Buffered(1)),
                  pl.BlockSpec((3, 1, CONV_W), lambda i: (0, 0, 0)),
                  pl.BlockSpec((1, CONV_W), const)] + side_specs,
        out_specs=[pl.BlockSpec((tm, CONV_W), row),
                   pl.BlockSpec((SSM_GROUPS, tm // CHUNK, CHUNK_W), lambda i: (0, jnp.minimum(i, steps - 1), 0)),
                   pl.BlockSpec((SSM_GROUPS, n_ctx // CHUNK, CHUNK_W), lambda i: (0, 0, 0))] + side_specs,
        out_shape=[jax.ShapeDtypeStruct((n, CONV_W), _BF16),
                   jax.ShapeDtypeStruct((SSM_GROUPS, n // CHUNK, CHUNK_W), _BF16),
                   jax.ShapeDtypeStruct((SSM_GROUPS, n_ctx // CHUNK, CHUNK_W), _BF16)]
                  + [jax.ShapeDtypeStruct(w.shape, _BF16) for w in side_weights],
        scratch_shapes=[pltpu.VMEM((N_SLABS, tm, LANES), _F32),
                        pltpu.VMEM((D_MODEL, IN_W), _BF16),
                        pltpu.VMEM((MXU_W, MXU_W), _BF16)],
        compiler_params=pltpu.CompilerParams(dimension_semantics=("arbitrary",),
                                             vmem_limit_bytes=VMEM_LIMIT),
        name="in_proj",
    )(x2d, ctx2d, mods, g1, w_in, conv_w, g_conv, *side_weights)


def _scan_steps(n_rows):
    return max(1, math.ceil(math.log2(n_rows)))


def _ends_pad(n_tiles):
    return max(SUBLANES, 1 << (_scan_steps(n_tiles) - 1))


def _cmul(ar, ai, br, bi):
    return ar * br - ai * bi, ar * bi + ai * br


def _eye(n):
    rows, cols = (lax.broadcasted_iota(jnp.int32, (n, n), axis) for axis in (0, 1))
    return jnp.where(rows == cols, 1.0, 0.0).astype(_BF16)


def _dot_nt_exact(eye, x, repeat=1):
    hi = x.astype(_BF16)
    rest = x - hi.astype(_F32)
    mid = rest.astype(_BF16)
    lo = (rest - mid.astype(_F32)).astype(_BF16)
    nt = lambda term: lax.dot_general(eye, jnp.concatenate([term] * repeat, axis=0),
                                      (((1,), (1,)), ((), ())), preferred_element_type=_F32)
    return nt(hi) + nt(mid) + nt(lo)


def _ssm_prep(n_steps, group0, par_refs, br_ref, bi_ref, cr_ref, ci_ref, wst_ref, wout_ref):
    T = CHUNK
    row0 = lax.broadcasted_iota(jnp.int32, (SUBLANES, LANES), 0) == 0

    def member_rows(slot):
        m0 = jnp.concatenate([slot(0, 0), slot(1, 0)], axis=1)
        m1 = jnp.concatenate([slot(1, 1), slot(0, 1)], axis=1)
        return jnp.where(row0, jnp.broadcast_to(m0, (SUBLANES, LANES)), jnp.broadcast_to(m1, (SUBLANES, LANES)))

    are_ref, aim_ref, ldt_ref = par_refs
    lr = jnp.minimum(member_rows(lambda d, m: are_ref[d, pl.ds(group0 + m, 1), :]), -1e-4)
    li = member_rows(lambda d, m: aim_ref[d, pl.ds(group0 + m, 1), :])
    dt = jnp.exp(member_rows(lambda d, m: jnp.full((1, SSM_STATE), ldt_ref[d, group0 + m], _F32)))
    mag = jnp.exp(lr * dt)
    ab_re = mag * jnp.cos(li * dt)
    ab_im = mag * jnp.sin(li * dt)
    nr = ab_re - 1.0
    den = lr * lr + li * li
    f_re = (nr * lr + ab_im * li) / den
    f_im = (ab_im * lr - nr * li) / den
    pt_re, pt_im = [jnp.ones_like(ab_re)], [jnp.zeros_like(ab_im)]
    for _ in range(T):
        nxt = _cmul(pt_re[-1], pt_im[-1], ab_re, ab_im)
        pt_re.append(nxt[0])
        pt_im.append(nxt[1])

    cols = jnp.concatenate([ab_re, ab_im, jnp.zeros((LANES - 2 * SUBLANES, LANES), _F32)], axis=0).T
    left1 = lax.broadcasted_iota(jnp.int32, (1, LANES), 1) < SSM_STATE
    eye_p = _eye(SSM_STATE)
    rows = lax.broadcasted_iota(jnp.int32, (LANES, LANES), 0)
    tcol = lax.broadcasted_iota(jnp.int32, (LANES, LANES), 1) // SSM_GROUP
    taps = []
    for e in range(PAIR):
        fwd_first = e == 0
        d0, d1 = (0, 1) if fwd_first else (1, 0)
        b_re = jnp.concatenate([br_ref[d0, e], br_ref[d1, e]], axis=1)
        b_im = jnp.concatenate([bi_ref[d0, e], bi_ref[d1, e]], axis=1)
        bbar_re, bbar_im = _cmul(f_re[e:e + 1], f_im[e:e + 1], b_re, b_im)
        fwd_lane = left1 if fwd_first else jnp.logical_not(left1)
        for s in range(T):
            pr = jnp.where(fwd_lane, pt_re[T - 1 - s][e:e + 1], pt_re[s][e:e + 1])
            pi = jnp.where(fwd_lane, pt_im[T - 1 - s][e:e + 1], pt_im[s][e:e + 1])
            w_re, w_im = _cmul(pr, pi, bbar_re, bbar_im)
            wst_ref[e, s * SSM_GROUP:(s + 1) * SSM_GROUP, 0:LANES] = w_re.astype(_BF16)
            wst_ref[e, s * SSM_GROUP:(s + 1) * SSM_GROUP, LANES:] = w_im.astype(_BF16)
        fwd_row = (rows < SSM_STATE) if fwd_first else (rows >= SSM_STATE)
        expo = jnp.where(fwd_row, tcol + 1, BLOCKS - tcol)
        lo_re = jnp.ones((LANES, LANES), _F32)
        lo_im = jnp.zeros((LANES, LANES), _F32)
        s_re, s_im = cols[:, e:e + 1], cols[:, SUBLANES + e:SUBLANES + e + 1]
        for bit in range(BLOCKS.bit_length()):
            m_re, m_im = _cmul(lo_re, lo_im, jnp.broadcast_to(s_re, lo_re.shape), jnp.broadcast_to(s_im, lo_re.shape))
            take = ((expo >> bit) & 1) == 1
            lo_re, lo_im = jnp.where(take, m_re, lo_re), jnp.where(take, m_im, lo_im)
            if (1 << bit) < BLOCKS:
                s_re, s_im = _cmul(s_re, s_im, s_re, s_im)
        slab = [(lo_re, lo_im)]
        for _ in range(1, CHUNK_SLABS):
            slab.append(_cmul(slab[-1][0], slab[-1][1],
                              jnp.broadcast_to(s_re, lo_re.shape), jnp.broadcast_to(s_im, lo_re.shape)))
        q_re, q_im = (jnp.concatenate([jnp.where(fwd_row, slab[q][part], slab[CHUNK_SLABS - 1 - q][part])
                                       for q in range(CHUNK_SLABS)], axis=1) for part in range(2))
        tiled = lambda c: _dot_nt_exact(eye_p, c, repeat=T)
        c_re = jnp.concatenate([tiled(cr_ref[d0, e]), tiled(cr_ref[d1, e])], axis=0)
        c_im = jnp.concatenate([tiled(ci_ref[d0, e]), tiled(ci_ref[d1, e])], axis=0)
        o_re, o_im = _cmul(c_re, c_im, q_re, q_im)
        wout = jnp.concatenate([o_re.astype(_BF16), (-o_im).astype(_BF16)], axis=0)
        wout_ref[e] = wout
        bbar = jnp.concatenate([bbar_re, bbar_im], axis=1)
        fwd_slot = jnp.concatenate([fwd_lane, fwd_lane], axis=1)
        one_way = jnp.concatenate([jnp.where(fwd_slot, bbar, 0.0), jnp.where(fwd_slot, 0.0, bbar)], axis=0)
        k_fb = _dot(one_way.astype(_BF16), wout)
        k_0 = _dot(bbar.astype(_BF16), jnp.concatenate([c_re.astype(_BF16), (-c_im).astype(_BF16)], axis=0))
        taps.append((k_fb[0:SSM_GROUP], k_fb[SSM_GROUP:], k_0))

    m0_re, m1_re = pt_re[T][0:1], pt_re[T][1:2]
    m0_im, m1_im = pt_im[T][0:1], pt_im[T][1:2]
    cf = (jnp.where(left1, m0_re, m1_re), jnp.where(left1, m0_im, m1_im))
    cb = (jnp.where(left1, m1_re, m0_re), jnp.where(left1, m1_im, m0_im))
    coefs = []
    for _ in range(n_steps):
        coefs.append([cf[0], cf[1], cb[0], cb[1]])
        cf = _cmul(cf[0], cf[1], cf[0], cf[1])
        cb = _cmul(cb[0], cb[1], cb[0], cb[1])
    return coefs, taps


def _ssm_kernel(n_lat, n_ctx, pairs_per_step, u_ref, uc_ref, are_ref, aim_ref, ldt_ref,
                br_ref, bi_ref, cr_ref, ci_ref, d_ref, y_ref, *scratch):
    programs = []
    for p in range(pairs_per_step):
        members = slice(p * PAIR, (p + 1) * PAIR)
        group0 = (pl.program_id(0) * pairs_per_step + p) * PAIR
        programs.append(_ssm_pair(
            n_lat, n_ctx, group0, u_ref.at[members], uc_ref.at[members], (are_ref, aim_ref, ldt_ref),
            br_ref.at[:, members], bi_ref.at[:, members], cr_ref.at[:, members], ci_ref.at[:, members],
            d_ref, y_ref.at[members], *[s.at[p] for s in scratch]))
    live = []
    while programs or live:
        if programs:
            live.append(programs.pop(0))
        for prog in list(live):
            if next(prog, "done") == "done":
                live.remove(prog)


def _ssm_pair(n_lat, n_ctx, group0, u_ref, uc_ref, par_refs, br_ref, bi_ref, cr_ref, ci_ref, d_ref,
              y_ref, buf_a, buf_b, ends_a, ends_b, wst_ref, wout_ref, m_scr):
    n_rows = n_lat + n_ctx
    pad = SUBLANES
    coefs, taps = _ssm_prep(_scan_steps(n_rows), group0, par_refs, br_ref, bi_ref, cr_ref, ci_ref,
                            wst_ref, wout_ref)
    yield

    lane = lax.broadcasted_iota(jnp.int32, (SSM_GROUP, LANES), 1)
    channel = lax.broadcasted_iota(jnp.int32, (SSM_GROUP, LANES), 0)
    on_diag = lane % SSM_GROUP == channel
    shifted = lane < LANES - SSM_GROUP
    for e in range(PAIR):
        k_f, k_b, k_0 = taps[e]
        d_skip = jnp.zeros((SSM_GROUP, LANES), _F32)
        for i in range(SSM_GROUP):
            d_skip = jnp.where(on_diag & (channel == i), d_ref[0, (group0 + e) * SSM_GROUP + i], d_skip)
        slabs = lambda k: [k[:, q * LANES:(q + 1) * LANES] for q in range(CHUNK_SLABS)]
        kb = [pltpu.roll(v, LANES - SSM_GROUP, axis=1) for v in slabs(k_b)] + [k_0[:, 0:LANES] + d_skip]
        strip = [jnp.where(shifted, kb[q], kb[q + 1]) for q in range(CHUNK_SLABS)] + slabs(k_f)
        for s in range(CHUNK):
            first, offset = divmod((CHUNK - 1 - s) * SSM_GROUP, LANES)
            if offset == 0:
                window = strip[first:first + CHUNK_SLABS]
            else:
                turned = [pltpu.roll(strip[first + q], LANES - offset, axis=1) for q in range(CHUNK_SLABS + 1)]
                window = [jnp.where(lane < LANES - offset, turned[q], turned[q + 1]) for q in range(CHUNK_SLABS)]
            m_scr[e, s * SSM_GROUP:(s + 1) * SSM_GROUP, :] = jnp.concatenate(window, axis=1).astype(_BF16)
    yield

    def paired(m0, m1):
        left = lax.broadcasted_iota(jnp.int32, (m0.shape[0], PAIR_STATE), 1) < SSM_STATE
        re0, im0 = m0[:, 0:PAIR_STATE], m0[:, PAIR_STATE:]
        re1, im1 = m1[:, 0:PAIR_STATE], m1[:, PAIR_STATE:]
        return [jnp.where(left, re0, re1), jnp.where(left, im0, im1),
                jnp.where(left, re1, re0), jnp.where(left, im1, im0)]

    s_lat = paired(_dot(u_ref[0], wst_ref[0]), _dot(u_ref[1], wst_ref[1]))
    s_ctx = paired(_dot(uc_ref[0], wst_ref[0]), _dot(uc_ref[1], wst_ref[1]))
    yield

    n_tiles = n_rows // SUBLANES
    tile_row = lax.broadcasted_iota(jnp.int32, (1, SUBLANES, PAIR_STATE), 1)

    def tile_scan(x_re, x_im, comp0, backward):
        for k in range(SUBLANES.bit_length() - 1):
            sh = 1 << k
            shift, keep = (SUBLANES - sh, tile_row < SUBLANES - sh) if backward else (sh, tile_row >= sh)
            a_re = jnp.where(keep, coefs[k][comp0], 0.0)
            a_im = jnp.where(keep, coefs[k][comp0 + 1], 0.0)
            p_re, p_im = pltpu.roll(x_re, shift, axis=1), pltpu.roll(x_im, shift, axis=1)
            x_re, x_im = x_re + a_re * p_re - a_im * p_im, x_im + a_re * p_im + a_im * p_re
        return x_re, x_im

    as_tiles = lambda parts: jnp.concatenate(parts, axis=0).reshape(n_tiles, SUBLANES, PAIR_STATE)
    local = list(tile_scan(as_tiles([s_ctx[0], s_lat[0]]), as_tiles([s_ctx[1], s_lat[1]]), 0, False))
    local += tile_scan(as_tiles([s_lat[2], s_ctx[2]]), as_tiles([s_lat[3], s_ctx[3]]), 2, True)
    for comp in range(N_COMP):
        buf_a[comp, pad:pad + n_rows] = local[comp].reshape(n_rows, PAIR_STATE)
    yield

    for buf in (ends_a, ends_b):
        buf[...] = jnp.zeros(buf.shape, _F32)
    epad = _ends_pad(n_tiles)
    for comp in range(N_COMP):
        end_row = pad + (SUBLANES - 1 if comp < 2 else 0)
        ends_a[comp, epad:epad + n_tiles] = buf_a[comp, pl.ds(end_row, n_tiles, stride=SUBLANES), :]
    src, dst = ends_a, ends_b
    for k in range(_scan_steps(n_tiles)):
        sh = 1 << k
        coef = coefs[SUBLANES.bit_length() - 1 + k]
        lo, hi = epad, epad + n_tiles
        xr, xi = src[0, lo - sh:hi - sh], src[1, lo - sh:hi - sh]
        dst[0, lo:hi] = src[0, lo:hi] + coef[0] * xr - coef[1] * xi
        dst[1, lo:hi] = src[1, lo:hi] + coef[0] * xi + coef[1] * xr
        xr, xi = src[2, lo + sh:hi + sh], src[3, lo + sh:hi + sh]
        dst[2, lo:hi] = src[2, lo:hi] + coef[2] * xr - coef[3] * xi
        dst[3, lo:hi] = src[3, lo:hi] + coef[2] * xi + coef[3] * xr
        src, dst = dst, src

    def carry_powers(comp0, backward):
        base = [(coefs[k][comp0], coefs[k][comp0 + 1]) for k in range(SUBLANES.bit_length())]
        pw = []
        for r in range(1, SUBLANES + 1):
            acc = None
            for k, term in enumerate(base):
                if (r >> k) & 1:
                    acc = term if acc is None else _cmul(acc[0], acc[1], term[0], term[1])
            pw.append(acc)
        if backward:
            pw = pw[::-1]
        return (jnp.concatenate([p[0] for p in pw], axis=0), jnp.concatenate([p[1] for p in pw], axis=0))

    for comp0, backward in ((0, False), (2, True)):
        p_re, p_im = carry_powers(comp0, backward)
        first = epad + 1 if backward else epad - 1
        for i in range(n_tiles):
            c_re = src[comp0, first + i:first + i + 1]
            c_im = src[comp0 + 1, first + i:first + i + 1]
            rows = slice(pad + i * SUBLANES, pad + (i + 1) * SUBLANES)
            buf_b[comp0, rows] = buf_a[comp0, rows] + p_re * c_re - p_im * c_im
            buf_b[comp0 + 1, rows] = buf_a[comp0 + 1, rows] + p_re * c_im + p_im * c_re
    yield

    f0 = pad + n_ctx - 1
    b0 = pad + 1
    f_re, f_im = buf_b[0, f0:f0 + n_lat], buf_b[1, f0:f0 + n_lat]
    b_re, b_im = buf_b[2, b0:b0 + n_lat], buf_b[3, b0:b0 + n_lat]
    left = lax.broadcasted_iota(jnp.int32, (n_lat, PAIR_STATE), 1) < SSM_STATE
    h0 = jnp.concatenate([jnp.where(left, f_re, b_re), jnp.where(left, f_im, b_im)], axis=1)
    h1 = jnp.concatenate([jnp.where(left, b_re, f_re), jnp.where(left, b_im, f_im)], axis=1)
    y_ref[0] = _dot(u_ref[0], m_scr[0]) + _dot(h0.astype(_BF16), wout_ref[0])
    y_ref[1] = _dot(u_ref[1], m_scr[1]) + _dot(h1.astype(_BF16), wout_ref[1])


def _ssm(u_g, uc_g, a_re, a_im, log_dt, b_re, b_im, c_re, c_im, d_row):
    n_groups, n_lat, _ = u_g.shape
    n_ctx = uc_g.shape[1]
    n_rows = n_lat + n_ctx
    n_pairs = n_groups // PAIR
    pps = 2
    gps = pps * PAIR
    buf = pltpu.VMEM((pps, N_COMP, n_rows + 2 * SUBLANES, PAIR_STATE), _F32)
    n_tiles = n_rows // SUBLANES
    ends_rows = _ends_pad(n_tiles) + -(-(n_tiles + _ends_pad(n_tiles)) // SUBLANES) * SUBLANES
    ends = pltpu.VMEM((pps, N_COMP, ends_rows, PAIR_STATE), _F32)
    slots = 2 * PAIR_STATE
    mat = lambda rows, cols: pltpu.VMEM((pps, PAIR, rows, cols), _BF16)
    pair3 = lambda q: (q, 0, 0)
    by_dir = lambda q: (0, q, 0, 0)
    return pl.pallas_call(
        functools.partial(_ssm_kernel, n_lat, n_ctx, pps),
        grid=(n_pairs // pps,),
        in_specs=[pl.BlockSpec((gps, n_lat, CHUNK_W), pair3),
                  pl.BlockSpec((gps, n_ctx, CHUNK_W), pair3),
                  pl.BlockSpec(a_re.shape, lambda q: (0, 0, 0)),
                  pl.BlockSpec(a_im.shape, lambda q: (0, 0, 0)),
                  pl.BlockSpec(memory_space=pltpu.SMEM),
                  pl.BlockSpec((2, gps) + b_re.shape[2:], by_dir),
                  pl.BlockSpec((2, gps) + b_im.shape[2:], by_dir),
                  pl.BlockSpec((2, gps) + c_re.shape[2:], by_dir),
                  pl.BlockSpec((2, gps) + c_im.shape[2:], by_dir),
                  pl.BlockSpec(memory_space=pltpu.SMEM)],
        out_specs=pl.BlockSpec((gps, n_lat, CHUNK_W), pair3),
        out_shape=jax.ShapeDtypeStruct((n_groups, n_lat, CHUNK_W), _F32),
        scratch_shapes=[buf, buf, ends, ends, mat(CHUNK_W, slots), mat(slots, CHUNK_W), mat(CHUNK_W, CHUNK_W)],
        compiler_params=pltpu.CompilerParams(dimension_semantics=("arbitrary",),
                                             vmem_limit_bytes=VMEM_LIMIT),
        name="ssm",
    )(u_g, uc_g, a_re, a_im, log_dt, b_re, b_im, c_re, c_im, d_row)


def _out_kernel(x_ref, conv_ref, yg_ref, mods_ref, gssm_ref, g2_ref, gfin_ref,
                wglu32_ref, wout32_ref, w1_ref, w2_ref, o_ref, y_scr, mix_scr, wglu_ref, wout_ref, ones_ref):
    @pl.when(pl.program_id(0) == 0)
    def _():
        wglu_ref[...] = wglu32_ref[...].astype(_BF16)
        wout_ref[...] = wout32_ref[...].astype(_BF16)
        ones_ref[...] = _group_ones(MXU_W, SSM_GROUP)

    gt1 = mods_ref[0:1, 2 * D_MODEL:3 * D_MODEL]
    sh2 = mods_ref[0:1, 3 * D_MODEL:4 * D_MODEL]
    sc2 = mods_ref[0:1, 4 * D_MODEL:5 * D_MODEL]
    gt2 = mods_ref[0:1, 5 * D_MODEL:6 * D_MODEL]
    th = x_ref.shape[0] // 2
    n_split = 2
    ff = D_FF // n_split

    def mixer(r0):
        rows_ = slice(r0, r0 + th)
        chunks = slice(r0 // CHUNK, (r0 + th) // CHUNK)
        for q in range(N_SLABS):
            for half in range(CHUNK // BLOCKS):
                cols = [yg_ref[q * BLOCKS + g8, chunks, half * LANES:(half + 1) * LANES] for g8 in range(BLOCKS)]
                rows = _lane_block_transpose(cols)
                for t in range(BLOCKS):
                    part, tok = divmod(half * BLOCKS + t, PART)
                    y_scr[q, pl.ds(r0 + _part_row(part, th) + tok, th // CHUNK, stride=PART), :] = rows[t]
        in_order = [r0 + _part_row(part, th) + c * PART for c in range(th // CHUNK) for part in range(CHUNK_PARTS)]
        s = jax.nn.gelu(jnp.concatenate(
            [jnp.concatenate([y_scr[q, p0:p0 + PART] for p0 in in_order], axis=0) for q in range(N_SLABS)], axis=1))
        s = s * jax.nn.sigmoid(_dot(s.astype(_BF16), wglu_ref[...]))
        sq = (s * s).astype(_BF16)
        ssq = jnp.concatenate([_dot(sq[:, k * MXU_W:(k + 1) * MXU_W], ones_ref[...])
                               for k in range(SSM_W // MXU_W)], axis=1)
        sn = s * lax.rsqrt(ssq * (1.0 / SSM_GROUP) + RMS_EPS) * gssm_ref[...]
        mix = mix_scr[rows_] + _dot(sn.astype(_BF16), wout_ref[CONV_W:])
        x_mid = x_ref[rows_, :] + gt1 * mix
        return x_mid, (_rms_rows(x_mid, g2_ref[...] * (1.0 + sc2)) + sh2).astype(_BF16)

    def mlp_chunk(h2, j):
        a = jnp.maximum(_dot(h2, w1_ref[:, j * ff:(j + 1) * ff]), 0.0)
        return _dot((a * a).astype(_BF16), w2_ref[j * ff:(j + 1) * ff])

    def finish(r0, x_mid, acc):
        o_ref[r0:r0 + th, :] = _rms_rows(x_mid + gt2 * acc, gfin_ref[...])

    mix_scr[...] = _dot(conv_ref[...], wout_ref[0:CONV_W])
    x_a, h_a = mixer(0)
    acc_a = mlp_chunk(h_a, 0)
    x_b, h_b = mixer(th)
    for j in range(1, n_split):
        acc_a = acc_a + mlp_chunk(h_a, j)
    acc_b = mlp_chunk(h_b, 0)
    finish(0, x_a, acc_a)
    for j in range(1, n_split):
        acc_b = acc_b + mlp_chunk(h_b, j)
    finish(th, x_b, acc_b)


def _out_proj(x2d, conv_n, y_g, mods, g_ssm, g2, g_fin, wglu, wout, w1, w2, tm):
    n = x2d.shape[0]
    const = lambda i: (0, 0)
    row = lambda i: (i, 0)
    resident = lambda shape: pl.BlockSpec(shape, const, pipeline_mode=pl.Buffered(1))
    return pl.pallas_call(
        _out_kernel,
        grid=(n // tm,),
        in_specs=[pl.BlockSpec((tm, D_MODEL), row),
                  pl.BlockSpec((tm, CONV_W), row),
                  pl.BlockSpec((SSM_GROUPS, tm // CHUNK, CHUNK_W), lambda i: (0, i, 0)),
                  pl.BlockSpec(mods.shape, const),
                  pl.BlockSpec((1, SSM_W), const),
                  pl.BlockSpec((1, D_MODEL), const),
                  pl.BlockSpec((1, D_MODEL), const),
                  resident((SSM_W, SSM_W)),
                  resident((D_MODEL, D_MODEL)),
                  resident((D_MODEL, D_FF)),
                  resident((D_FF, D_MODEL))],
        out_specs=pl.BlockSpec((tm, D_MODEL), row),
        out_shape=jax.ShapeDtypeStruct((n, D_MODEL), _F32),
        scratch_shapes=[pltpu.VMEM((N_SLABS, tm, LANES), _F32), pltpu.VMEM((tm, D_MODEL), _F32),
                        pltpu.VMEM((SSM_W, SSM_W), _BF16), pltpu.VMEM((D_MODEL, D_MODEL), _BF16),
                        pltpu.VMEM((MXU_W, MXU_W), _BF16)],
        compiler_params=pltpu.CompilerParams(dimension_semantics=("arbitrary",),
                                             vmem_limit_bytes=VMEM_LIMIT),
        name="out_proj",
    )(x2d, conv_n, y_g, mods, g_ssm, g2, g_fin, wglu, wout, w1, w2)


def kernel(x, c, ctx, c_ctx, w_mod, b_mod, g_norm1, w_in, conv_w, ssm_a_re, ssm_a_im, ssm_log_dt,
           ssm_b_re, ssm_b_im, ssm_c_re, ssm_c_im, ssm_d, w_glu, g_conv_out, g_ssm_out, w_out,
           g_norm2, w_mlp1, w_mlp2, g_final):
    bsz, n_lat, d_model = x.shape
    n_ctx = ctx.shape[1]
    assert bsz == 1 and d_model == D_MODEL and w_mod.shape[0] == 1
    assert n_lat % (CHUNK * SUBLANES) == 0 and n_ctx % (CHUNK * SUBLANES) == 0 and n_lat % GRID_W == 0
    layer = 0
    x2d = x[0]
    ctx2d = ctx[0]

    mods = _modulation(c, c_ctx[None, :], w_mod[layer], b_mod[layer][None, :])

    g1 = g_norm1[layer][None, :]
    g_conv = g_conv_out[layer][None, :]
    conv_n, u_g, uc_g, w1_bf, w2_bf = _in_proj(x2d, ctx2d, mods, g1, w_in[layer], conv_w[layer][:, None, :], g_conv,
                                                tm=1024, side_weights=(w_mlp1[layer], w_mlp2[layer]))

    y_g = _ssm(u_g, uc_g, ssm_a_re[layer], ssm_a_im[layer], ssm_log_dt[layer],
               jnp.swapaxes(ssm_b_re[layer], -1, -2), jnp.swapaxes(ssm_b_im[layer], -1, -2),
               ssm_c_re[layer], ssm_c_im[layer], ssm_d[layer:layer + 1])

    out = _out_proj(x2d, conv_n, y_g, mods, g_ssm_out[layer][None, :], g_norm2[layer][None, :],
                    g_final[None, :], w_glu[layer], w_out[layer], w1_bf, w2_bf, tm=1024)
    return out[None]
```

```python
import functools
import math

import jax
import jax.numpy as jnp
from jax import lax
from jax.experimental import pallas as pl
from jax.experimental.pallas import tpu as pltpu

D_MODEL = 1024
GRID_W = 64
CONV_W = 512
CONV_HEADS = 8
SSM_W = 512
SSM_GROUP = 16
SSM_GROUPS = 32
SSM_STATE = 64
IN_W = 3 * CONV_W + SSM_W
D_FF = 4 * D_MODEL
RMS_EPS = 1e-6

LANES = 128
MXU_W = 256
SUBLANES = 8
CHUNK = 32
CHUNK_W = CHUNK * SSM_GROUP
CHUNK_SLABS = CHUNK_W // LANES
PART = 2 * SUBLANES
CHUNK_PARTS = CHUNK // PART
PAIR = 2
PAIR_STATE = PAIR * SSM_STATE
N_COMP = 4
BLOCKS = LANES // SSM_GROUP
N_SLABS = SSM_W // LANES
VMEM_LIMIT = 62 * 1024 * 1024

_BF16 = jnp.bfloat16
_F32 = jnp.float32


def _dot(a, b):
    return jnp.dot(a, b, preferred_element_type=_F32)


def _part_row(part, n_rows):
    return part * (n_rows // CHUNK_PARTS)


def _lane_block_transpose(vs):
    blk = lax.broadcasted_iota(jnp.int32, vs[0].shape, 1) // SSM_GROUP
    dist = BLOCKS // 2
    while dist:
        shift = dist * SSM_GROUP
        upper = (blk & dist) != 0
        out = [None] * BLOCKS
        for a in range(BLOCKS):
            if a & dist:
                continue
            lo, hi = vs[a], vs[a + dist]
            out[a] = jnp.where(upper, pltpu.roll(hi, shift, axis=1), lo)
            out[a + dist] = jnp.where(upper, hi, pltpu.roll(lo, LANES - shift, axis=1))
        vs = out
        dist //= 2
    return vs


def _mod_kernel(c_ref, cctx_ref, w_ref, b_ref, o_ref):
    tk = w_ref.shape[0]
    first = lax.broadcasted_iota(jnp.int32, (SUBLANES, tk), 0) == 0
    s = jnp.where(first, jnp.broadcast_to(c_ref[...], first.shape), jnp.broadcast_to(cctx_ref[...], first.shape))
    act = s * jax.nn.sigmoid(s)
    w = w_ref[...]
    a_hi, w_hi = act.astype(_BF16), w.astype(_BF16)
    a_lo = (act - a_hi.astype(_F32)).astype(_BF16)
    w_lo = (w - w_hi.astype(_F32)).astype(_BF16)
    by_hi = _dot(jnp.concatenate([a_hi, a_lo], axis=0), w_hi)
    part = by_hi[0:SUBLANES] + by_hi[SUBLANES:] + _dot(a_hi, w_lo)

    @pl.when(pl.program_id(0) == 0)
    def _():
        o_ref[...] = part + b_ref[...]

    @pl.when(pl.program_id(0) > 0)
    def _():
        o_ref[...] += part


def _modulation(c_row, cctx_row, w_mod, b_mod):
    d_in, n_out = w_mod.shape
    tk = 256
    return pl.pallas_call(
        _mod_kernel,
        grid=(d_in // tk,),
        in_specs=[pl.BlockSpec((1, tk), lambda k: (0, k)),
                  pl.BlockSpec((1, tk), lambda k: (0, k)),
                  pl.BlockSpec((tk, n_out), lambda k: (k, 0)),
                  pl.BlockSpec((1, n_out), lambda k: (0, 0))],
        out_specs=pl.BlockSpec((SUBLANES, n_out), lambda k: (0, 0)),
        out_shape=jax.ShapeDtypeStruct((SUBLANES, n_out), _F32),
        compiler_params=pltpu.CompilerParams(dimension_semantics=("arbitrary",),
                                             vmem_limit_bytes=VMEM_LIMIT),
        name="mod",
    )(c_row, cctx_row, w_mod, b_mod)


def _rms_rows(x, g):
    ms = jnp.mean(x * x, axis=-1, keepdims=True)
    return x * lax.rsqrt(ms + RMS_EPS) * g


def _group_ones(width, group):
    rows = lax.broadcasted_iota(jnp.int32, (width, width), 0) // group
    cols = lax.broadcasted_iota(jnp.int32, (width, width), 1) // group
    return jnp.where(rows == cols, 1.0, 0.0).astype(_BF16)


def _in_kernel(n_side, x_ref, ctx_ref, mods_ref, g1_ref, w_in_ref, convw_ref, gconv_ref, *refs):
    side_in, (conv_ref, ug_ref, ucg_ref) = refs[:n_side], refs[n_side:n_side + 3]
    side_out = refs[n_side + 3:2 * n_side + 3]
    u_scr, w_scr, ones_scr = refs[2 * n_side + 3:]
    step = pl.program_id(0)

    @pl.when(step == 0)
    def _():
        w_scr[:, 0:SSM_W] = w_in_ref[:, 3 * CONV_W:].astype(_BF16)
        for k in range(CONV_W // MXU_W):
            for part in range(3):
                dst = SSM_W + (3 * k + part) * MXU_W
                src = part * CONV_W + k * MXU_W
                w_scr[:, dst:dst + MXU_W] = w_in_ref[:, src:src + MXU_W].astype(_BF16)
        ones_scr[...] = _group_ones(MXU_W, CONV_W // CONV_HEADS)

    def normed(rows_ref, mod_row):
        sh1 = mods_ref[mod_row:mod_row + 1, 0:D_MODEL]
        sc1 = mods_ref[mod_row:mod_row + 1, D_MODEL:2 * D_MODEL]
        return (_rms_rows(rows_ref[...], g1_ref[...] * (1.0 + sc1)) + sh1).astype(_BF16)

    def to_group_major(z_u, out_ref, r0):
        n_rows = z_u.shape[0]
        chunks = slice(r0 // CHUNK, (r0 + n_rows) // CHUNK)
        for q in range(N_SLABS):
            for c in range(n_rows // CHUNK):
                for part in range(CHUNK_PARTS):
                    dst = r0 + _part_row(part, n_rows) + c * PART
                    src = c * CHUNK + part * PART
                    u_scr[q, dst:dst + PART] = z_u[src:src + PART, q * LANES:(q + 1) * LANES]
        for q in range(N_SLABS):
            for half in range(CHUNK // BLOCKS):
                rows = [u_scr[q, pl.ds(r0 + _part_row((half * BLOCKS + t) // PART, n_rows) + (half * BLOCKS + t) % PART,
                                       n_rows // CHUNK, stride=PART), :] for t in range(BLOCKS)]
                cols = _lane_block_transpose(rows)
                for g8 in range(BLOCKS):
                    out_ref[q * BLOCKS + g8, chunks, half * LANES:(half + 1) * LANES] = cols[g8].astype(_BF16)

    def conv_branch(z, r0):
        n_rows = z.shape[0]
        pos = lax.broadcasted_iota(jnp.int32, (n_rows, MXU_W), 0) % GRID_W
        for k in range(CONV_W // MXU_W):
            c0 = SSM_W + 3 * k * MXU_W
            lanes = slice(k * MXU_W, (k + 1) * MXU_W)
            b = z[:, c0:c0 + MXU_W]
            cv = z[:, c0 + MXU_W:c0 + 2 * MXU_W] * z[:, c0 + 2 * MXU_W:c0 + 3 * MXU_W]
            prev = jnp.where(pos == 0, 0.0, pltpu.roll(cv, 1, axis=0))
            nxt = jnp.where(pos == GRID_W - 1, 0.0, pltpu.roll(cv, n_rows - 1, axis=0))
            y = b * (convw_ref[0, :, lanes] * prev + convw_ref[1, :, lanes] * cv + convw_ref[2, :, lanes] * nxt)
            ssq = _dot((y * y).astype(_BF16), ones_scr[...])
            yn = y * lax.rsqrt(ssq * (1.0 / (CONV_W // CONV_HEADS)) + RMS_EPS) * gconv_ref[:, lanes]
            conv_ref[r0:r0 + n_rows, lanes] = yn.astype(_BF16)

    th = x_ref.shape[0] // 2
    z = [_dot(normed(x_ref.at[r0:r0 + th], 0), w_scr[...]) for r0 in (0, th)]
    for src_ref, dst_ref in zip(side_in, side_out):
        dst_ref[...] = src_ref[...].astype(_BF16)
    for half, r0 in enumerate((0, th)):
        to_group_major(z[half][:, 0:SSM_W], ug_ref, r0)
        conv_branch(z[half], r0)

    @pl.when(step == pl.num_programs(0) - 1)
    def _():
        to_group_major(_dot(normed(ctx_ref, 1), w_scr[:, 0:SSM_W]), ucg_ref, 0)


def _in_proj(x2d, ctx2d, mods, g1, w_in, conv_w, g_conv, tm, side_weights):
    n, n_ctx = x2d.shape[0], ctx2d.shape[0]
    steps = n // tm
    const = lambda i: (0, 0)
    row = lambda i: (i, 0)
    side_specs = [pl.BlockSpec((w.shape[0] // steps, w.shape[1]), row) for w in side_weights]
    return pl.pallas_call(
        functools.partial(_in_kernel, len(side_weights)),
        grid=(steps,),
        in_specs=[pl.BlockSpec((tm, D_MODEL), row),
                  pl.BlockSpec((n_ctx, D_MODEL), const),
                  pl.BlockSpec(mods.shape, const),
                  pl.BlockSpec((1, D_MODEL), const),
                  pl.BlockSpec((D_MODEL, IN_W), const, pipeline_mode=pl.Buffered(1)),
                  pl.BlockSpec((3, 1, CONV_W), lambda i: (0, 0, 0)),
                  pl.BlockSpec((1, CONV_W), const)] + side_specs,
        out_specs=[pl.BlockSpec((tm, CONV_W), row),
                   pl.BlockSpec((SSM_GROUPS, tm // CHUNK, CHUNK_W), lambda i: (0, i, 0)),
                   pl.BlockSpec((SSM_GROUPS, n_ctx // CHUNK, CHUNK_W), lambda i: (0, 0, 0))] + side_specs,
        out_shape=[jax.ShapeDtypeStruct((n, CONV_W), _BF16),
                   jax.ShapeDtypeStruct((SSM_GROUPS, n // CHUNK, CHUNK_W), _BF16),
                   jax.ShapeDtypeStruct((SSM_GROUPS, n_ctx // CHUNK, CHUNK_W), _BF16)]
                  + [jax.ShapeDtypeStruct(w.shape, _BF16) for w in side_weights],
        scratch_shapes=[pltpu.VMEM((N_SLABS, tm, LANES), _F32),
                        pltpu.VMEM((D_MODEL, IN_W), _BF16),
                        pltpu.VMEM((MXU_W, MXU_W), _BF16)],
        compiler_params=pltpu.CompilerParams(dimension_semantics=("arbitrary",),
                                             vmem_limit_bytes=VMEM_LIMIT),
        name="in_proj",
    )(x2d, ctx2d, mods, g1, w_in, conv_w, g_conv, *side_weights)


def _scan_steps(n_rows):
    return max(1, math.ceil(math.log2(n_rows)))


def _ends_pad(n_tiles):
    return max(SUBLANES, 1 << (_scan_steps(n_tiles) - 1))


def _cmul(ar, ai, br, bi):
    return ar * br - ai * bi, ar * bi + ai * br


def _eye(n):
    rows, cols = (lax.broadcasted_iota(jnp.int32, (n, n), axis) for axis in (0, 1))
    return jnp.where(rows == cols, 1.0, 0.0).astype(_BF16)


def _dot_nt_exact(eye, x, repeat=1):
    hi = x.astype(_BF16)
    rest = x - hi.astype(_F32)
    mid = rest.astype(_BF16)
    lo = (rest - mid.astype(_F32)).astype(_BF16)
    nt = lambda term: lax.dot_general(eye, jnp.concatenate([term] * repeat, axis=0),
                                      (((1,), (1,)), ((), ())), preferred_element_type=_F32)
    return nt(hi) + nt(mid) + nt(lo)


def _ssm_prep(n_steps, group0, par_refs, br_ref, bi_ref, cr_ref, ci_ref, wst_ref, wout_ref):
    T = CHUNK
    row0 = lax.broadcasted_iota(jnp.int32, (SUBLANES, LANES), 0) == 0

    def member_rows(slot):
        m0 = jnp.concatenate([slot(0, 0), slot(1, 0)], axis=1)
        m1 = jnp.concatenate([slot(1, 1), slot(0, 1)], axis=1)
        return jnp.where(row0, jnp.broadcast_to(m0, (SUBLANES, LANES)), jnp.broadcast_to(m1, (SUBLANES, LANES)))

    are_ref, aim_ref, ldt_ref = par_refs
    lr = jnp.minimum(member_rows(lambda d, m: are_ref[d, pl.ds(group0 + m, 1), :]), -1e-4)
    li = member_rows(lambda d, m: aim_ref[d, pl.ds(group0 + m, 1), :])
    dt = jnp.exp(member_rows(lambda d, m: jnp.full((1, SSM_STATE), ldt_ref[d, group0 + m], _F32)))
    mag = jnp.exp(lr * dt)
    ab_re = mag * jnp.cos(li * dt)
    ab_im = mag * jnp.sin(li * dt)
    nr = ab_re - 1.0
    den = lr * lr + li * li
    f_re = (nr * lr + ab_im * li) / den
    f_im = (ab_im * lr - nr * li) / den
    pt_re, pt_im = [jnp.ones_like(ab_re)], [jnp.zeros_like(ab_im)]
    for _ in range(T):
        nxt = _cmul(pt_re[-1], pt_im[-1], ab_re, ab_im)
        pt_re.append(nxt[0])
        pt_im.append(nxt[1])

    cols = jnp.concatenate([ab_re, ab_im, jnp.zeros((LANES - 2 * SUBLANES, LANES), _F32)], axis=0).T
    left1 = lax.broadcasted_iota(jnp.int32, (1, LANES), 1) < SSM_STATE
    eye_p = _eye(SSM_STATE)
    tcol = lax.broadcasted_iota(jnp.int32, (SSM_STATE, LANES), 1) // SSM_GROUP
    takes = [[((expo >> bit) & 1) == 1 for expo in (tcol + 1, BLOCKS - tcol)] for bit in range(BLOCKS.bit_length())]
    taps = []
    for e in range(PAIR):
        fwd_first = e == 0
        d0, d1 = (0, 1) if fwd_first else (1, 0)
        b_re = jnp.concatenate([br_ref[d0, e], br_ref[d1, e]], axis=1)
        b_im = jnp.concatenate([bi_ref[d0, e], bi_ref[d1, e]], axis=1)
        bbar_re, bbar_im = _cmul(f_re[e:e + 1], f_im[e:e + 1], b_re, b_im)
        fwd_lane = left1 if fwd_first else jnp.logical_not(left1)
        for s in range(T):
            pr = jnp.where(fwd_lane, pt_re[T - 1 - s][e:e + 1], pt_re[s][e:e + 1])
            pi = jnp.where(fwd_lane, pt_im[T - 1 - s][e:e + 1], pt_im[s][e:e + 1])
            w_re, w_im = _cmul(pr, pi, bbar_re, bbar_im)
            wst_ref[e, s * SSM_GROUP:(s + 1) * SSM_GROUP, 0:LANES] = w_re.astype(_BF16)
            wst_ref[e, s * SSM_GROUP:(s + 1) * SSM_GROUP, LANES:] = w_im.astype(_BF16)
        upper, lower = (0, 1) if fwd_first else (1, 0)
        by_rows = lambda up, low: jnp.concatenate([up[0:SSM_STATE], low[SSM_STATE:]], axis=0)
        lo_re = jnp.ones((LANES, LANES), _F32)
        lo_im = jnp.zeros((LANES, LANES), _F32)
        s_re, s_im = cols[:, e:e + 1], cols[:, SUBLANES + e:SUBLANES + e + 1]
        for bit in range(BLOCKS.bit_length()):
            m_re, m_im = _cmul(lo_re, lo_im, jnp.broadcast_to(s_re, lo_re.shape), jnp.broadcast_to(s_im, lo_re.shape))
            pick = lambda m, lo: jnp.concatenate([jnp.where(takes[bit][upper], m[0:SSM_STATE], lo[0:SSM_STATE]),
                                                  jnp.where(takes[bit][lower], m[SSM_STATE:], lo[SSM_STATE:])], axis=0)
            lo_re, lo_im = pick(m_re, lo_re), pick(m_im, lo_im)
            if (1 << bit) < BLOCKS:
                s_re, s_im = _cmul(s_re, s_im, s_re, s_im)
        slab = [(lo_re, lo_im)]
        for _ in range(1, CHUNK_SLABS):
            slab.append(_cmul(slab[-1][0], slab[-1][1],
                              jnp.broadcast_to(s_re, lo_re.shape), jnp.broadcast_to(s_im, lo_re.shape)))
        rising = lambda part: [slab[q][part] for q in range(CHUNK_SLABS)]
        q_re, q_im = (jnp.concatenate([by_rows(up, low) if fwd_first else by_rows(low, up)
                                       for up, low in zip(rising(part), rising(part)[::-1])], axis=1)
                      for part in range(2))
        tiled = lambda c: _dot_nt_exact(eye_p, c, repeat=T)
        c_re = jnp.concatenate([tiled(cr_ref[d0, e]), tiled(cr_ref[d1, e])], axis=0)
        c_im = jnp.concatenate([tiled(ci_ref[d0, e]), tiled(ci_ref[d1, e])], axis=0)
        o_re, o_im = _cmul(c_re, c_im, q_re, q_im)
        wout = jnp.concatenate([o_re.astype(_BF16), (-o_im).astype(_BF16)], axis=0)
        wout_ref[e] = wout
        bbar = jnp.concatenate([bbar_re, bbar_im], axis=1)
        fwd_slot = jnp.concatenate([fwd_lane, fwd_lane], axis=1)
        one_way = jnp.concatenate([jnp.where(fwd_slot, bbar, 0.0), jnp.where(fwd_slot, 0.0, bbar)], axis=0)
        k_fb = _dot(one_way.astype(_BF16), wout)
        k_0 = _dot(bbar.astype(_BF16), jnp.concatenate([c_re.astype(_BF16), (-c_im).astype(_BF16)], axis=0))
        taps.append((k_fb[0:SSM_GROUP], k_fb[SSM_GROUP:], k_0))

    m0_re, m1_re = pt_re[T][0:1], pt_re[T][1:2]
    m0_im, m1_im = pt_im[T][0:1], pt_im[T][1:2]
    cf = (jnp.where(left1, m0_re, m1_re), jnp.where(left1, m0_im, m1_im))
    cb = (jnp.where(left1, m1_re, m0_re), jnp.where(left1, m1_im, m0_im))
    coefs = []
    for _ in range(n_steps):
        coefs.append([cf[0], cf[1], cb[0], cb[1]])
        cf = _cmul(cf[0], cf[1], cf[0], cf[1])
        cb = _cmul(cb[0], cb[1], cb[0], cb[1])
    return coefs, taps


def _ssm_kernel(n_lat, n_ctx, pairs_per_step, u_ref, uc_ref, are_ref, aim_ref, ldt_ref,
                br_ref, bi_ref, cr_ref, ci_ref, d_ref, y_ref, *scratch):
    programs = []
    for p in range(pairs_per_step):
        members = slice(p * PAIR, (p + 1) * PAIR)
        group0 = (pl.program_id(0) * pairs_per_step + p) * PAIR
        programs.append(_ssm_pair(
            n_lat, n_ctx, group0, u_ref.at[members], uc_ref.at[members], (are_ref, aim_ref, ldt_ref),
            br_ref.at[:, members], bi_ref.at[:, members], cr_ref.at[:, members], ci_ref.at[:, members],
            d_ref, y_ref.at[members], *[s.at[p] for s in scratch]))
    while programs:
        programs = [prog for prog in programs if next(prog, "done") != "done"]


def _ssm_pair(n_lat, n_ctx, group0, u_ref, uc_ref, par_refs, br_ref, bi_ref, cr_ref, ci_ref, d_ref,
              y_ref, buf_a, buf_b, ends_a, ends_b, wst_ref, wout_ref, m_scr):
    n_rows = n_lat + n_ctx
    pad = SUBLANES
    coefs, taps = _ssm_prep(_scan_steps(n_rows), group0, par_refs, br_ref, bi_ref, cr_ref, ci_ref,
                            wst_ref, wout_ref)
    yield

    lane = lax.broadcasted_iota(jnp.int32, (SSM_GROUP, LANES), 1)
    channel = lax.broadcasted_iota(jnp.int32, (SSM_GROUP, LANES), 0)
    on_diag = lane % SSM_GROUP == channel
    shifted = lane < LANES - SSM_GROUP
    for e in range(PAIR):
        k_f, k_b, k_0 = taps[e]
        d_skip = jnp.zeros((SSM_GROUP, LANES), _F32)
        for i in range(SSM_GROUP):
            d_skip = jnp.where(on_diag & (channel == i), d_ref[0, (group0 + e) * SSM_GROUP + i], d_skip)
        slabs = lambda k: [k[:, q * LANES:(q + 1) * LANES] for q in range(CHUNK_SLABS)]
        kb = [pltpu.roll(v, LANES - SSM_GROUP, axis=1) for v in slabs(k_b)] + [k_0[:, 0:LANES] + d_skip]
        strip = [jnp.where(shifted, kb[q], kb[q + 1]) for q in range(CHUNK_SLABS)] + slabs(k_f)
        for s in range(CHUNK):
            first, offset = divmod((CHUNK - 1 - s) * SSM_GROUP, LANES)
            if offset == 0:
                window = strip[first:first + CHUNK_SLABS]
            else:
                turned = [pltpu.roll(strip[first + q], LANES - offset, axis=1) for q in range(CHUNK_SLABS + 1)]
                window = [jnp.where(lane < LANES - offset, turned[q], turned[q + 1]) for q in range(CHUNK_SLABS)]
            m_scr[e, s * SSM_GROUP:(s + 1) * SSM_GROUP, :] = jnp.concatenate(window, axis=1).astype(_BF16)
    yield

    def paired(m0, m1):
        left = lax.broadcasted_iota(jnp.int32, (m0.shape[0], PAIR_STATE), 1) < SSM_STATE
        re0, im0 = m0[:, 0:PAIR_STATE], m0[:, PAIR_STATE:]
        re1, im1 = m1[:, 0:PAIR_STATE], m1[:, PAIR_STATE:]
        return [jnp.where(left, re0, re1), jnp.where(left, im0, im1),
                jnp.where(left, re1, re0), jnp.where(left, im1, im0)]

    s_lat = paired(_dot(u_ref[0], wst_ref[0]), _dot(u_ref[1], wst_ref[1]))
    s_ctx = paired(_dot(uc_ref[0], wst_ref[0]), _dot(uc_ref[1], wst_ref[1]))
    yield

    n_tiles = n_rows // SUBLANES
    tile_row = lax.broadcasted_iota(jnp.int32, (1, SUBLANES, PAIR_STATE), 1)

    def tile_scan(x_re, x_im, comp0, backward):
        for k in range(SUBLANES.bit_length() - 1):
            sh = 1 << k
            shift, keep = (SUBLANES - sh, tile_row < SUBLANES - sh) if backward else (sh, tile_row >= sh)
            a_re = jnp.where(keep, coefs[k][comp0], 0.0)
            a_im = jnp.where(keep, coefs[k][comp0 + 1], 0.0)
            p_re, p_im = pltpu.roll(x_re, shift, axis=1), pltpu.roll(x_im, shift, axis=1)
            x_re, x_im = x_re + a_re * p_re - a_im * p_im, x_im + a_re * p_im + a_im * p_re
        return x_re, x_im

    as_tiles = lambda parts: jnp.concatenate(parts, axis=0).reshape(n_tiles, SUBLANES, PAIR_STATE)
    local = list(tile_scan(as_tiles([s_ctx[0], s_lat[0]]), as_tiles([s_ctx[1], s_lat[1]]), 0, False))
    local += tile_scan(as_tiles([s_lat[2], s_ctx[2]]), as_tiles([s_lat[3], s_ctx[3]]), 2, True)
    for comp in range(N_COMP):
        buf_a[comp, pad:pad + n_rows] = local[comp].reshape(n_rows, PAIR_STATE)
    yield

    for buf in (ends_a, ends_b):
        buf[...] = jnp.zeros(buf.shape, _F32)
    epad = _ends_pad(n_tiles)
    for comp in range(N_COMP):
        end_row = pad + (SUBLANES - 1 if comp < 2 else 0)
        ends_a[comp, epad:epad + n_tiles] = buf_a[comp, pl.ds(end_row, n_tiles, stride=SUBLANES), :]
    src, dst = ends_a, ends_b
    for k in range(_scan_steps(n_tiles)):
        sh = 1 << k
        coef = coefs[SUBLANES.bit_length() - 1 + k]
        lo, hi = epad, epad + n_tiles
        xr, xi = src[0, lo - sh:hi - sh], src[1, lo - sh:hi - sh]
        dst[0, lo:hi] = src[0, lo:hi] + coef[0] * xr - coef[1] * xi
        dst[1, lo:hi] = src[1, lo:hi] + coef[0] * xi + coef[1] * xr
        xr, xi = src[2, lo + sh:hi + sh], src[3, lo + sh:hi + sh]
        dst[2, lo:hi] = src[2, lo:hi] + coef[2] * xr - coef[3] * xi
        dst[3, lo:hi] = src[3, lo:hi] + coef[2] * xi + coef[3] * xr
        src, dst = dst, src

    def carry_powers(comp0, backward):
        base = [(coefs[k][comp0], coefs[k][comp0 + 1]) for k in range(SUBLANES.bit_length())]
        pw = []
        for r in range(1, SUBLANES + 1):
            acc = None
            for k, term in enumerate(base):
                if (r >> k) & 1:
                    acc = term if acc is None else _cmul(acc[0], acc[1], term[0], term[1])
            pw.append(acc)
        if backward:
            pw = pw[::-1]
        return (jnp.concatenate([p[0] for p in pw], axis=0), jnp.concatenate([p[1] for p in pw], axis=0))

    for comp0, backward in ((0, False), (2, True)):
        p_re, p_im = carry_powers(comp0, backward)
        first = epad + 1 if backward else epad - 1
        for i in range(n_tiles):
            c_re = src[comp0, first + i:first + i + 1]
            c_im = src[comp0 + 1, first + i:first + i + 1]
            rows = slice(pad + i * SUBLANES, pad + (i + 1) * SUBLANES)
            buf_b[comp0, rows] = buf_a[comp0, rows] + p_re * c_re - p_im * c_im
            buf_b[comp0 + 1, rows] = buf_a[comp0 + 1, rows] + p_re * c_im + p_im * c_re
    yield

    f0 = pad + n_ctx - 1
    b0 = pad + 1
    f_re, f_im = buf_b[0, f0:f0 + n_lat], buf_b[1, f0:f0 + n_lat]
    b_re, b_im = buf_b[2, b0:b0 + n_lat], buf_b[3, b0:b0 + n_lat]
    left = lax.broadcasted_iota(jnp.int32, (n_lat, PAIR_STATE), 1) < SSM_STATE
    h0 = jnp.concatenate([jnp.where(left, f_re, b_re), jnp.where(left, f_im, b_im)], axis=1)
    h1 = jnp.concatenate([jnp.where(left, b_re, f_re), jnp.where(left, b_im, f_im)], axis=1)
    y_ref[0] = _dot(u_ref[0], m_scr[0]) + _dot(h0.astype(_BF16), wout_ref[0])
    y_ref[1] = _dot(u_ref[1], m_scr[1]) + _dot(h1.astype(_BF16), wout_ref[1])


def _ssm(u_g, uc_g, a_re, a_im, log_dt, b_re, b_im, c_re, c_im, d_row):
    n_groups, n_lat, _ = u_g.shape
    n_ctx = uc_g.shape[1]
    n_rows = n_lat + n_ctx
    n_pairs = n_groups // PAIR
    pps = 4
    gps = pps * PAIR
    buf = pltpu.VMEM((pps, N_COMP, n_rows + 2 * SUBLANES, PAIR_STATE), _F32)
    n_tiles = n_rows // SUBLANES
    ends_rows = _ends_pad(n_tiles) + -(-(n_tiles + _ends_pad(n_tiles)) // SUBLANES) * SUBLANES
    ends = pltpu.VMEM((pps, N_COMP, ends_rows, PAIR_STATE), _F32)
    slots = 2 * PAIR_STATE
    mat = lambda rows, cols: pltpu.VMEM((pps, PAIR, rows, cols), _BF16)
    pair3 = lambda q: (q, 0, 0)
    by_dir = lambda q: (0, q, 0, 0)
    return pl.pallas_call(
        functools.partial(_ssm_kernel, n_lat, n_ctx, pps),
        grid=(n_pairs // pps,),
        in_specs=[pl.BlockSpec((gps, n_lat, CHUNK_W), pair3),
                  pl.BlockSpec((gps, n_ctx, CHUNK_W), pair3),
                  pl.BlockSpec(a_re.shape, lambda q: (0, 0, 0)),
                  pl.BlockSpec(a_im.shape, lambda q: (0, 0, 0)),
                  pl.BlockSpec(memory_space=pltpu.SMEM),
                  pl.BlockSpec((2, gps) + b_re.shape[2:], by_dir),
                  pl.BlockSpec((2, gps) + b_im.shape[2:], by_dir),
                  pl.BlockSpec((2, gps) + c_re.shape[2:], by_dir),
                  pl.BlockSpec((2, gps) + c_im.shape[2:], by_dir),
                  pl.BlockSpec(memory_space=pltpu.SMEM)],
        out_specs=pl.BlockSpec((gps, n_lat, CHUNK_W), pair3),
        out_shape=jax.ShapeDtypeStruct((n_groups, n_lat, CHUNK_W), _F32),
        scratch_shapes=[buf, buf, ends, ends, mat(CHUNK_W, slots), mat(slots, CHUNK_W), mat(CHUNK_W, CHUNK_W)],
        compiler_params=pltpu.CompilerParams(dimension_semantics=("arbitrary",),
                                             vmem_limit_bytes=VMEM_LIMIT),
        name="ssm",
    )(u_g, uc_g, a_re, a_im, log_dt, b_re, b_im, c_re, c_im, d_row)


def _out_kernel(x_ref, conv_ref, yg_ref, mods_ref, gssm_ref, g2_ref, gfin_ref,
                wglu32_ref, wout32_ref, w1_ref, w2_ref, o_ref, y_scr, mix_scr, wglu_ref, wout_ref, ones_ref):
    @pl.when(pl.program_id(0) == 0)
    def _():
        wglu_ref[...] = wglu32_ref[...].astype(_BF16)
        wout_ref[...] = wout32_ref[...].astype(_BF16)
        ones_ref[...] = _group_ones(MXU_W, SSM_GROUP)

    gt1 = mods_ref[0:1, 2 * D_MODEL:3 * D_MODEL]
    sh2 = mods_ref[0:1, 3 * D_MODEL:4 * D_MODEL]
    sc2 = mods_ref[0:1, 4 * D_MODEL:5 * D_MODEL]
    gt2 = mods_ref[0:1, 5 * D_MODEL:6 * D_MODEL]
    th = x_ref.shape[0] // 2
    n_split = 2
    ff = D_FF // n_split

    def mixer(r0):
        rows_ = slice(r0, r0 + th)
        chunks = slice(r0 // CHUNK, (r0 + th) // CHUNK)
        for q in range(N_SLABS):
            for half in range(CHUNK // BLOCKS):
                cols = [yg_ref[q * BLOCKS + g8, chunks, half * LANES:(half + 1) * LANES] for g8 in range(BLOCKS)]
                rows = _lane_block_transpose(cols)
                for t in range(BLOCKS):
                    part, tok = divmod(half * BLOCKS + t, PART)
                    y_scr[q, pl.ds(r0 + _part_row(part, th) + tok, th // CHUNK, stride=PART), :] = rows[t]
        in_order = [r0 + _part_row(part, th) + c * PART for c in range(th // CHUNK) for part in range(CHUNK_PARTS)]
        s = jax.nn.gelu(jnp.concatenate(
            [jnp.concatenate([y_scr[q, p0:p0 + PART] for p0 in in_order], axis=0) for q in range(N_SLABS)], axis=1))
        s = s * jax.nn.sigmoid(_dot(s.astype(_BF16), wglu_ref[...]))
        sq = (s * s).astype(_BF16)
        ssq = jnp.concatenate([_dot(sq[:, k * MXU_W:(k + 1) * MXU_W], ones_ref[...])
                               for k in range(SSM_W // MXU_W)], axis=1)
        sn = s * lax.rsqrt(ssq * (1.0 / SSM_GROUP) + RMS_EPS) * gssm_ref[...]
        mix = mix_scr[rows_] + _dot(sn.astype(_BF16), wout_ref[CONV_W:])
        x_mid = x_ref[rows_, :] + gt1 * mix
        return x_mid, (_rms_rows(x_mid, g2_ref[...] * (1.0 + sc2)) + sh2).astype(_BF16)

    def mlp_chunk(h2, j):
        a = jnp.maximum(_dot(h2, w1_ref[:, j * ff:(j + 1) * ff]), 0.0)
        return _dot((a * a).astype(_BF16), w2_ref[j * ff:(j + 1) * ff])

    def finish(r0, x_mid, acc):
        o_ref[r0:r0 + th, :] = _rms_rows(x_mid + gt2 * acc, gfin_ref[...])

    mix_scr[...] = _dot(conv_ref[...], wout_ref[0:CONV_W])
    x_a, h_a = mixer(0)
    acc_a = mlp_chunk(h_a, 0)
    x_b, h_b = mixer(th)
    for j in range(1, n_split):
        acc_a = acc_a + mlp_chunk(h_a, j)
    acc_b = mlp_chunk(h_b, 0)
    finish(0, x_a, acc_a)
    for j in range(1, n_split):
        acc_b = acc_b + mlp_chunk(h_b, j)
    finish(th, x_b, acc_b)


def _out_proj(x2d, conv_n, y_g, mods, g_ssm, g2, g_fin, wglu, wout, w1, w2, tm):
    n = x2d.shape[0]
    const = lambda i: (0, 0)
    row = lambda i: (i, 0)
    resident = lambda shape: pl.BlockSpec(shape, const, pipeline_mode=pl.Buffered(1))
    return pl.pallas_call(
        _out_kernel,
        grid=(n // tm,),
        in_specs=[pl.BlockSpec((tm, D_MODEL), row),
                  pl.BlockSpec((tm, CONV_W), row),
                  pl.BlockSpec((SSM_GROUPS, tm // CHUNK, CHUNK_W), lambda i: (0, i, 0)),
                  pl.BlockSpec(mods.shape, const),
                  pl.BlockSpec((1, SSM_W), const),
                  pl.BlockSpec((1, D_MODEL), const),
                  pl.BlockSpec((1, D_MODEL), const),
                  resident((SSM_W, SSM_W)),
                  resident((D_MODEL, D_MODEL)),
                  resident((D_MODEL, D_FF)),
                  resident((D_FF, D_MODEL))],
        out_specs=pl.BlockSpec((tm, D_MODEL), row),
        out_shape=jax.ShapeDtypeStruct((n, D_MODEL), _F32),
        scratch_shapes=[pltpu.VMEM((N_SLABS, tm, LANES), _F32), pltpu.VMEM((tm, D_MODEL), _F32),
                        pltpu.VMEM((SSM_W, SSM_W), _BF16), pltpu.VMEM((D_MODEL, D_MODEL), _BF16),
                        pltpu.VMEM((MXU_W, MXU_W), _BF16)],
        compiler_params=pltpu.CompilerParams(dimension_semantics=("arbitrary",),
                                             vmem_limit_bytes=VMEM_LIMIT),
        name="out_proj",
    )(x2d, conv_n, y_g, mods, g_ssm, g2, g_fin, wglu, wout, w1, w2)


def kernel(x, c, ctx, c_ctx, w_mod, b_mod, g_norm1, w_in, conv_w, ssm_a_re, ssm_a_im, ssm_log_dt,
           ssm_b_re, ssm_b_im, ssm_c_re, ssm_c_im, ssm_d, w_glu, g_conv_out, g_ssm_out, w_out,
           g_norm2, w_mlp1, w_mlp2, g_final):
    bsz, n_lat, d_model = x.shape
    n_ctx = ctx.shape[1]
    assert bsz == 1 and d_model == D_MODEL and w_mod.shape[0] == 1
    assert n_lat % (CHUNK * SUBLANES) == 0 and n_ctx % (CHUNK * SUBLANES) == 0 and n_lat % GRID_W == 0
    layer = 0
    x2d = x[0]
    ctx2d = ctx[0]

    mods = _modulation(c, c_ctx[None, :], w_mod[layer], b_mod[layer][None, :])

    g1 = g_norm1[layer][None, :]
    g_conv = g_conv_out[layer][None, :]
    conv_n, u_g, uc_g, w1_bf, w2_bf = _in_proj(x2d, ctx2d, mods, g1, w_in[layer], conv_w[layer][:, None, :], g_conv,
                                                tm=1024, side_weights=(w_mlp1[layer], w_mlp2[layer]))

    y_g = _ssm(u_g, uc_g, ssm_a_re[layer], ssm_a_im[layer], ssm_log_dt[layer],
               jnp.swapaxes(ssm_b_re[layer], -1, -2), jnp.swapaxes(ssm_b_im[layer], -1, -2),
               ssm_c_re[layer], ssm_c_im[layer], ssm_d[layer:layer + 1])

    out = _out_proj(x2d, conv_n, y_g, mods, g_ssm_out[layer][None, :], g_norm2[layer][None, :],
                    g_final[None, :], w_glu[layer], w_out[layer], w1_bf, w2_bf, tm=1024)
    return out[None]
```

```python
import functools
import math

import jax
import jax.numpy as jnp
from jax import lax
from jax.experimental import pallas as pl
from jax.experimental.pallas import tpu as pltpu

D_MODEL = 1024
GRID_W = 64
CONV_W = 512
CONV_HEADS = 8
SSM_W = 512
SSM_GROUP = 16
SSM_GROUPS = 32
SSM_STATE = 64
IN_W = 3 * CONV_W + SSM_W
D_FF = 4 * D_MODEL
RMS_EPS = 1e-6

LANES = 128
MXU_W = 256
SUBLANES = 8
CHUNK = 32
CHUNK_W = CHUNK * SSM_GROUP
CHUNK_SLABS = CHUNK_W // LANES
PART = 2 * SUBLANES
CHUNK_PARTS = CHUNK // PART
PAIR = 2
PAIR_STATE = PAIR * SSM_STATE
N_COMP = 4
BLOCKS = LANES // SSM_GROUP
N_SLABS = SSM_W // LANES
VMEM_LIMIT = 62 * 1024 * 1024

_BF16 = jnp.bfloat16
_F32 = jnp.float32


def _dot(a, b):
    return jnp.dot(a, b, preferred_element_type=_F32)


def _part_row(part, n_rows):
    return part * (n_rows // CHUNK_PARTS)


def _lane_block_transpose(vs):
    blk = lax.broadcasted_iota(jnp.int32, vs[0].shape, 1) // SSM_GROUP
    dist = BLOCKS // 2
    while dist:
        shift = dist * SSM_GROUP
        upper = (blk & dist) != 0
        out = [None] * BLOCKS
        for a in range(BLOCKS):
            if a & dist:
                continue
            lo, hi = vs[a], vs[a + dist]
            out[a] = jnp.where(upper, pltpu.roll(hi, shift, axis=1), lo)
            out[a + dist] = jnp.where(upper, hi, pltpu.roll(lo, LANES - shift, axis=1))
        vs = out
        dist //= 2
    return vs


def _mod_kernel(c_ref, cctx_ref, w_hbm, b_ref, o_ref, w_buf, sem):
    n_chunks, tk, _ = w_buf.shape
    copies = [pltpu.make_async_copy(w_hbm.at[k * tk:(k + 1) * tk], w_buf.at[k], sem.at[k]) for k in range(n_chunks)]
    for copy in copies:
        copy.start()
    first = lax.broadcasted_iota(jnp.int32, (SUBLANES, tk), 0) == 0
    for k, copy in enumerate(copies):
        cols = slice(k * tk, (k + 1) * tk)
        s = jnp.where(first, jnp.broadcast_to(c_ref[:, cols], first.shape),
                      jnp.broadcast_to(cctx_ref[:, cols], first.shape))
        act = s * jax.nn.sigmoid(s)
        copy.wait()
        w = w_buf[k]
        a_hi, w_hi = act.astype(_BF16), w.astype(_BF16)
        a_lo = (act - a_hi.astype(_F32)).astype(_BF16)
        w_lo = (w - w_hi.astype(_F32)).astype(_BF16)
        by_hi = _dot(jnp.concatenate([a_hi, a_lo], axis=0), w_hi)
        part = by_hi[0:SUBLANES] + by_hi[SUBLANES:] + _dot(a_hi, w_lo)
        if k == 0:
            o_ref[...] = part + b_ref[...]
        else:
            o_ref[...] += part


def _modulation(c_row, cctx_row, w_mod, b_mod):
    d_in, n_out = w_mod.shape
    tk = LANES
    in_vmem = pl.BlockSpec(memory_space=pltpu.VMEM)
    return pl.pallas_call(
        _mod_kernel,
        in_specs=[in_vmem, in_vmem, pl.BlockSpec(memory_space=pl.ANY), in_vmem],
        out_specs=pl.BlockSpec(memory_space=pltpu.VMEM),
        out_shape=jax.ShapeDtypeStruct((SUBLANES, n_out), _F32),
        scratch_shapes=[pltpu.VMEM((d_in // tk, tk, n_out), _F32), pltpu.SemaphoreType.DMA((d_in // tk,))],
        compiler_params=pltpu.CompilerParams(vmem_limit_bytes=VMEM_LIMIT),
        name="mod",
    )(c_row, cctx_row, w_mod, b_mod)


def _rms_rows(x, g):
    ms = jnp.mean(x * x, axis=-1, keepdims=True)
    return x * lax.rsqrt(ms + RMS_EPS) * g


def _group_ones(width, group):
    rows = lax.broadcasted_iota(jnp.int32, (width, width), 0) // group
    cols = lax.broadcasted_iota(jnp.int32, (width, width), 1) // group
    return jnp.where(rows == cols, 1.0, 0.0).astype(_BF16)


def _in_kernel(n_side, x_ref, ctx_ref, mods_ref, g1_ref, w_in_ref, convw_ref, gconv_ref, *refs):
    side_in, (conv_ref, ug_ref, ucg_ref) = refs[:n_side], refs[n_side:n_side + 3]
    side_out = refs[n_side + 3:2 * n_side + 3]
    u_scr, w_scr, ones_scr = refs[2 * n_side + 3:]
    step = pl.program_id(0)

    @pl.when(step == 0)
    def _():
        w_scr[:, 0:SSM_W] = w_in_ref[:, 3 * CONV_W:].astype(_BF16)
        for k in range(CONV_W // MXU_W):
            for part in range(3):
                dst = SSM_W + (3 * k + part) * MXU_W
                src = part * CONV_W + k * MXU_W
                w_scr[:, dst:dst + MXU_W] = w_in_ref[:, src:src + MXU_W].astype(_BF16)
        ones_scr[...] = _group_ones(MXU_W, CONV_W // CONV_HEADS)

    def normed(rows_ref, mod_row):
        sh1 = mods_ref[mod_row:mod_row + 1, 0:D_MODEL]
        sc1 = mods_ref[mod_row:mod_row + 1, D_MODEL:2 * D_MODEL]
        return (_rms_rows(rows_ref[...], g1_ref[...] * (1.0 + sc1)) + sh1).astype(_BF16)

    def to_group_major(z_u, out_ref, r0):
        n_rows = z_u.shape[0]
        chunks = slice(r0 // CHUNK, (r0 + n_rows) // CHUNK)
        for q in range(N_SLABS):
            for c in range(n_rows // CHUNK):
                for part in range(CHUNK_PARTS):
                    dst = r0 + _part_row(part, n_rows) + c * PART
                    src = c * CHUNK + part * PART
                    u_scr[q, dst:dst + PART] = z_u[src:src + PART, q * LANES:(q + 1) * LANES]
        for q in range(N_SLABS):
            for half in range(CHUNK // BLOCKS):
                rows = [u_scr[q, pl.ds(r0 + _part_row((half * BLOCKS + t) // PART, n_rows) + (half * BLOCKS + t) % PART,
                                       n_rows // CHUNK, stride=PART), :] for t in range(BLOCKS)]
                cols = _lane_block_transpose(rows)
                for g8 in range(BLOCKS):
                    out_ref[q * BLOCKS + g8, chunks, half * LANES:(half + 1) * LANES] = cols[g8].astype(_BF16)

    def conv_branch(z, r0):
        n_rows = z.shape[0]
        pos = lax.broadcasted_iota(jnp.int32, (n_rows, MXU_W), 0) % GRID_W
        for k in range(CONV_W // MXU_W):
            c0 = SSM_W + 3 * k * MXU_W
            lanes = slice(k * MXU_W, (k + 1) * MXU_W)
            b = z[:, c0:c0 + MXU_W]
            cv = z[:, c0 + MXU_W:c0 + 2 * MXU_W] * z[:, c0 + 2 * MXU_W:c0 + 3 * MXU_W]
            prev = jnp.where(pos == 0, 0.0, pltpu.roll(cv, 1, axis=0))
            nxt = jnp.where(pos == GRID_W - 1, 0.0, pltpu.roll(cv, n_rows - 1, axis=0))
            y = b * (convw_ref[0, :, lanes] * prev + convw_ref[1, :, lanes] * cv + convw_ref[2, :, lanes] * nxt)
            ssq = _dot((y * y).astype(_BF16), ones_scr[...])
            yn = y * lax.rsqrt(ssq * (1.0 / (CONV_W // CONV_HEADS)) + RMS_EPS) * gconv_ref[:, lanes]
            conv_ref[r0:r0 + n_rows, lanes] = yn.astype(_BF16)

    th = x_ref.shape[0] // 2
    z = [_dot(normed(x_ref.at[r0:r0 + th], 0), w_scr[...]) for r0 in (0, th)]
    for src_ref, dst_ref in zip(side_in, side_out):
        dst_ref[...] = src_ref[...].astype(_BF16)
    for half, r0 in enumerate((0, th)):
        to_group_major(z[half][:, 0:SSM_W], ug_ref, r0)
        conv_branch(z[half], r0)

    @pl.when(step == pl.num_programs(0) - 1)
    def _():
        to_group_major(_dot(normed(ctx_ref, 1), w_scr[:, 0:SSM_W]), ucg_ref, 0)


def _in_proj(x2d, ctx2d, mods, g1, w_in, conv_w, g_conv, tm, side_weights):
    n, n_ctx = x2d.shape[0], ctx2d.shape[0]
    steps = n // tm
    const = lambda i: (0, 0)
    row = lambda i: (i, 0)
    side_specs = [pl.BlockSpec((w.shape[0] // steps, w.shape[1]), row) for w in side_weights]
    return pl.pallas_call(
        functools.partial(_in_kernel, len(side_weights)),
        grid=(steps,),
        in_specs=[pl.BlockSpec((tm, D_MODEL), row),
                  pl.BlockSpec((n_ctx, D_MODEL), const),
                  pl.BlockSpec(mods.shape, const),
                  pl.BlockSpec((1, D_MODEL), const),
                  pl.BlockSpec((D_MODEL, IN_W), const, pipeline_mode=pl.Buffered(1)),
                  pl.BlockSpec((3, 1, CONV_W), lambda i: (0, 0, 0)),
                  pl.BlockSpec((1, CONV_W), const)] + side_specs,
        out_specs=[pl.BlockSpec((tm, CONV_W), row),
                   pl.BlockSpec((SSM_GROUPS, tm // CHUNK, CHUNK_W), lambda i: (0, i, 0)),
                   pl.BlockSpec((SSM_GROUPS, n_ctx // CHUNK, CHUNK_W), lambda i: (0, 0, 0))] + side_specs,
        out_shape=[jax.ShapeDtypeStruct((n, CONV_W), _BF16),
                   jax.ShapeDtypeStruct((SSM_GROUPS, n // CHUNK, CHUNK_W), _BF16),
                   jax.ShapeDtypeStruct((SSM_GROUPS, n_ctx // CHUNK, CHUNK_W), _BF16)]
                  + [jax.ShapeDtypeStruct(w.shape, _BF16) for w in side_weights],
        scratch_shapes=[pltpu.VMEM((N_SLABS, tm, LANES), _F32),
                        pltpu.VMEM((D_MODEL, IN_W), _BF16),
                        pltpu.VMEM((MXU_W, MXU_W), _BF16)],
        compiler_params=pltpu.CompilerParams(dimension_semantics=("arbitrary",),
                                             vmem_limit_bytes=VMEM_LIMIT),
        name="in_proj",
    )(x2d, ctx2d, mods, g1, w_in, conv_w, g_conv, *side_weights)


def _scan_steps(n_rows):
    return max(1, math.ceil(math.log2(n_rows)))


def _ends_pad(n_tiles):
    return max(SUBLANES, 1 << (_scan_steps(n_tiles) - 1))


def _cmul(ar, ai, br, bi):
    return ar * br - ai * bi, ar * bi + ai * br


def _eye(n):
    rows, cols = (lax.broadcasted_iota(jnp.int32, (n, n), axis) for axis in (0, 1))
    return jnp.where(rows == cols, 1.0, 0.0).astype(_BF16)


def _dot_nt_exact(eye, x, repeat=1):
    hi = x.astype(_BF16)
    rest = x - hi.astype(_F32)
    mid = rest.astype(_BF16)
    lo = (rest - mid.astype(_F32)).astype(_BF16)
    nt = lambda term: lax.dot_general(eye, jnp.concatenate([term] * repeat, axis=0),
                                      (((1,), (1,)), ((), ())), preferred_element_type=_F32)
    return nt(hi) + nt(mid) + nt(lo)


def _ssm_prep(n_steps, group0, par_refs, br_ref, bi_ref, cr_ref, ci_ref, wst_ref, wout_ref):
    T = CHUNK
    row0 = lax.broadcasted_iota(jnp.int32, (SUBLANES, LANES), 0) == 0

    def member_rows(slot):
        m0 = jnp.concatenate([slot(0, 0), slot(1, 0)], axis=1)
        m1 = jnp.concatenate([slot(1, 1), slot(0, 1)], axis=1)
        return jnp.where(row0, jnp.broadcast_to(m0, (SUBLANES, LANES)), jnp.broadcast_to(m1, (SUBLANES, LANES)))

    are_ref, aim_ref, ldt_ref = par_refs
    lr = jnp.minimum(member_rows(lambda d, m: are_ref[d, pl.ds(group0 + m, 1), :]), -1e-4)
    li = member_rows(lambda d, m: aim_ref[d, pl.ds(group0 + m, 1), :])
    dt = jnp.exp(member_rows(lambda d, m: jnp.full((1, SSM_STATE), ldt_ref[d, group0 + m], _F32)))
    mag = jnp.exp(lr * dt)
    ab_re = mag * jnp.cos(li * dt)
    ab_im = mag * jnp.sin(li * dt)
    nr = ab_re - 1.0
    den = lr * lr + li * li
    f_re = (nr * lr + ab_im * li) / den
    f_im = (ab_im * lr - nr * li) / den
    pt_re, pt_im = [jnp.ones_like(ab_re)], [jnp.zeros_like(ab_im)]
    for _ in range(T):
        nxt = _cmul(pt_re[-1], pt_im[-1], ab_re, ab_im)
        pt_re.append(nxt[0])
        pt_im.append(nxt[1])

    cols = jnp.concatenate([ab_re, ab_im, jnp.zeros((LANES - 2 * SUBLANES, LANES), _F32)], axis=0).T
    left1 = lax.broadcasted_iota(jnp.int32, (1, LANES), 1) < SSM_STATE
    eye_p = _eye(SSM_STATE)
    tcol = lax.broadcasted_iota(jnp.int32, (SSM_STATE, LANES), 1) // SSM_GROUP
    takes = [[((expo >> bit) & 1) == 1 for expo in (tcol + 1, BLOCKS - tcol)] for bit in range(BLOCKS.bit_length())]
    taps = []
    for e in range(PAIR):
        fwd_first = e == 0
        d0, d1 = (0, 1) if fwd_first else (1, 0)
        b_re = jnp.concatenate([br_ref[d0, e], br_ref[d1, e]], axis=1)
        b_im = jnp.concatenate([bi_ref[d0, e], bi_ref[d1, e]], axis=1)
        bbar_re, bbar_im = _cmul(f_re[e:e + 1], f_im[e:e + 1], b_re, b_im)
        fwd_lane = left1 if fwd_first else jnp.logical_not(left1)
        for s in range(T):
            pr = jnp.where(fwd_lane, pt_re[T - 1 - s][e:e + 1], pt_re[s][e:e + 1])
            pi = jnp.where(fwd_lane, pt_im[T - 1 - s][e:e + 1], pt_im[s][e:e + 1])
            w_re, w_im = _cmul(pr, pi, bbar_re, bbar_im)
            wst_ref[e, s * SSM_GROUP:(s + 1) * SSM_GROUP, 0:LANES] = w_re.astype(_BF16)
            wst_ref[e, s * SSM_GROUP:(s + 1) * SSM_GROUP, LANES:] = w_im.astype(_BF16)
        upper, lower = (0, 1) if fwd_first else (1, 0)
        by_rows = lambda up, low: jnp.concatenate([up[0:SSM_STATE], low[SSM_STATE:]], axis=0)
        lo_re = jnp.ones((LANES, LANES), _F32)
        lo_im = jnp.zeros((LANES, LANES), _F32)
        s_re, s_im = cols[:, e:e + 1], cols[:, SUBLANES + e:SUBLANES + e + 1]
        for bit in range(BLOCKS.bit_length()):
            m_re, m_im = _cmul(lo_re, lo_im, jnp.broadcast_to(s_re, lo_re.shape), jnp.broadcast_to(s_im, lo_re.shape))
            pick = lambda m, lo: jnp.concatenate([jnp.where(takes[bit][upper], m[0:SSM_STATE], lo[0:SSM_STATE]),
                                                  jnp.where(takes[bit][lower], m[SSM_STATE:], lo[SSM_STATE:])], axis=0)
            lo_re, lo_im = pick(m_re, lo_re), pick(m_im, lo_im)
            if (1 << bit) < BLOCKS:
                s_re, s_im = _cmul(s_re, s_im, s_re, s_im)
        slab = [(lo_re, lo_im)]
        for _ in range(1, CHUNK_SLABS):
            slab.append(_cmul(slab[-1][0], slab[-1][1],
                              jnp.broadcast_to(s_re, lo_re.shape), jnp.broadcast_to(s_im, lo_re.shape)))
        rising = lambda part: [slab[q][part] for q in range(CHUNK_SLABS)]
        q_re, q_im = (jnp.concatenate([by_rows(up, low) if fwd_first else by_rows(low, up)
                                       for up, low in zip(rising(part), rising(part)[::-1])], axis=1)
                      for part in range(2))
        tiled = lambda c: _dot_nt_exact(eye_p, c, repeat=T)
        c_re = jnp.concatenate([tiled(cr_ref[d0, e]), tiled(cr_ref[d1, e])], axis=0)
        c_im = jnp.concatenate([tiled(ci_ref[d0, e]), tiled(ci_ref[d1, e])], axis=0)
        o_re, o_im = _cmul(c_re, c_im, q_re, q_im)
        wout = jnp.concatenate([o_re.astype(_BF16), (-o_im).astype(_BF16)], axis=0)
        wout_ref[e] = wout
        bbar = jnp.concatenate([bbar_re, bbar_im], axis=1)
        fwd_slot = jnp.concatenate([fwd_lane, fwd_lane], axis=1)
        one_way = jnp.concatenate([jnp.where(fwd_slot, bbar, 0.0), jnp.where(fwd_slot, 0.0, bbar)], axis=0)
        k_fb = _dot(one_way.astype(_BF16), wout)
        k_0 = _dot(bbar.astype(_BF16), jnp.concatenate([c_re.astype(_BF16), (-c_im).astype(_BF16)], axis=0))
        taps.append((k_fb[0:SSM_GROUP], k_fb[SSM_GROUP:], k_0))

    m0_re, m1_re = pt_re[T][0:1], pt_re[T][1:2]
    m0_im, m1_im = pt_im[T][0:1], pt_im[T][1:2]
    cf = (jnp.where(left1, m0_re, m1_re), jnp.where(left1, m0_im, m1_im))
    cb = (jnp.where(left1, m1_re, m0_re), jnp.where(left1, m1_im, m0_im))
    coefs = []
    for _ in range(n_steps):
        coefs.append([cf[0], cf[1], cb[0], cb[1]])
        cf = _cmul(cf[0], cf[1], cf[0], cf[1])
        cb = _cmul(cb[0], cb[1], cb[0], cb[1])
    return coefs, taps


def _ssm_kernel(n_lat, n_ctx, pairs_per_step, u_ref, uc_ref, are_ref, aim_ref, ldt_ref,
                br_ref, bi_ref, cr_ref, ci_ref, d_ref, y_ref, *scratch):
    programs = []
    for p in range(pairs_per_step):
        members = slice(p * PAIR, (p + 1) * PAIR)
        group0 = (pl.program_id(0) * pairs_per_step + p) * PAIR
        programs.append(_ssm_pair(
            n_lat, n_ctx, group0, u_ref.at[members], uc_ref.at[members], (are_ref, aim_ref, ldt_ref),
            br_ref.at[:, members], bi_ref.at[:, members], cr_ref.at[:, members], ci_ref.at[:, members],
            d_ref, y_ref.at[members], *[s.at[p] for s in scratch]))
    while programs:
        programs = [prog for prog in programs if next(prog, "done") != "done"]


def _ssm_pair(n_lat, n_ctx, group0, u_ref, uc_ref, par_refs, br_ref, bi_ref, cr_ref, ci_ref, d_ref,
              y_ref, buf_a, buf_b, ends_a, ends_b, wst_ref, wout_ref, m_scr):
    n_rows = n_lat + n_ctx
    pad = SUBLANES
    coefs, taps = _ssm_prep(_scan_steps(n_rows), group0, par_refs, br_ref, bi_ref, cr_ref, ci_ref,
                            wst_ref, wout_ref)
    yield

    lane = lax.broadcasted_iota(jnp.int32, (SSM_GROUP, LANES), 1)
    channel = lax.broadcasted_iota(jnp.int32, (SSM_GROUP, LANES), 0)
    on_diag = lane % SSM_GROUP == channel
    shifted = lane < LANES - SSM_GROUP
    for e in range(PAIR):
        k_f, k_b, k_0 = taps[e]
        d_skip = jnp.zeros((SSM_GROUP, LANES), _F32)
        for i in range(SSM_GROUP):
            d_skip = jnp.where(on_diag & (channel == i), d_ref[0, (group0 + e) * SSM_GROUP + i], d_skip)
        slabs = lambda k: [k[:, q * LANES:(q + 1) * LANES] for q in range(CHUNK_SLABS)]
        kb = [pltpu.roll(v, LANES - SSM_GROUP, axis=1) for v in slabs(k_b)] + [k_0[:, 0:LANES] + d_skip]
        strip = [jnp.where(shifted, kb[q], kb[q + 1]) for q in range(CHUNK_SLABS)] + slabs(k_f)
        for s in range(CHUNK):
            first, offset = divmod((CHUNK - 1 - s) * SSM_GROUP, LANES)
            if offset == 0:
                window = strip[first:first + CHUNK_SLABS]
            else:
                turned = [pltpu.roll(strip[first + q], LANES - offset, axis=1) for q in range(CHUNK_SLABS + 1)]
                window = [jnp.where(lane < LANES - offset, turned[q], turned[q + 1]) for q in range(CHUNK_SLABS)]
            m_scr[e, s * SSM_GROUP:(s + 1) * SSM_GROUP, :] = jnp.concatenate(window, axis=1).astype(_BF16)
    yield

    def paired(m0, m1):
        left = lax.broadcasted_iota(jnp.int32, (m0.shape[0], PAIR_STATE), 1) < SSM_STATE
        re0, im0 = m0[:, 0:PAIR_STATE], m0[:, PAIR_STATE:]
        re1, im1 = m1[:, 0:PAIR_STATE], m1[:, PAIR_STATE:]
        return [jnp.where(left, re0, re1), jnp.where(left, im0, im1),
                jnp.where(left, re1, re0), jnp.where(left, im1, im0)]

    s_lat = paired(_dot(u_ref[0], wst_ref[0]), _dot(u_ref[1], wst_ref[1]))
    s_ctx = paired(_dot(uc_ref[0], wst_ref[0]), _dot(uc_ref[1], wst_ref[1]))
    yield

    n_tiles = n_rows // SUBLANES
    tile_row = lax.broadcasted_iota(jnp.int32, (1, SUBLANES, PAIR_STATE), 1)

    def tile_scan(x_re, x_im, comp0, backward):
        for k in range(SUBLANES.bit_length() - 1):
            sh = 1 << k
            shift, keep = (SUBLANES - sh, tile_row < SUBLANES - sh) if backward else (sh, tile_row >= sh)
            a_re = jnp.where(keep, coefs[k][comp0], 0.0)
            a_im = jnp.where(keep, coefs[k][comp0 + 1], 0.0)
            p_re, p_im = pltpu.roll(x_re, shift, axis=1), pltpu.roll(x_im, shift, axis=1)
            x_re, x_im = x_re + a_re * p_re - a_im * p_im, x_im + a_re * p_im + a_im * p_re
        return x_re, x_im

    as_tiles = lambda parts: jnp.concatenate(parts, axis=0).reshape(n_tiles, SUBLANES, PAIR_STATE)
    local = list(tile_scan(as_tiles([s_ctx[0], s_lat[0]]), as_tiles([s_ctx[1], s_lat[1]]), 0, False))
    local += tile_scan(as_tiles([s_lat[2], s_ctx[2]]), as_tiles([s_lat[3], s_ctx[3]]), 2, True)
    for comp in range(N_COMP):
        buf_a[comp, pad:pad + n_rows] = local[comp].reshape(n_rows, PAIR_STATE)
    yield

    for buf in (ends_a, ends_b):
        buf[...] = jnp.zeros(buf.shape, _F32)
    epad = _ends_pad(n_tiles)
    for comp in range(N_COMP):
        end_row = pad + (SUBLANES - 1 if comp < 2 else 0)
        ends_a[comp, epad:epad + n_tiles] = buf_a[comp, pl.ds(end_row, n_tiles, stride=SUBLANES), :]
    src, dst = ends_a, ends_b
    for k in range(_scan_steps(n_tiles)):
        sh = 1 << k
        coef = coefs[SUBLANES.bit_length() - 1 + k]
        lo, hi = epad, epad + n_tiles
        xr, xi = src[0, lo - sh:hi - sh], src[1, lo - sh:hi - sh]
        dst[0, lo:hi] = src[0, lo:hi] + coef[0] * xr - coef[1] * xi
        dst[1, lo:hi] = src[1, lo:hi] + coef[0] * xi + coef[1] * xr
        xr, xi = src[2, lo + sh:hi + sh], src[3, lo + sh:hi + sh]
        dst[2, lo:hi] = src[2, lo:hi] + coef[2] * xr - coef[3] * xi
        dst[3, lo:hi] = src[3, lo:hi] + coef[2] * xi + coef[3] * xr
        src, dst = dst, src

    def carry_powers(comp0, backward):
        base = [(coefs[k][comp0], coefs[k][comp0 + 1]) for k in range(SUBLANES.bit_length())]
        pw = []
        for r in range(1, SUBLANES + 1):
            acc = None
            for k, term in enumerate(base):
                if (r >> k) & 1:
                    acc = term if acc is None else _cmul(acc[0], acc[1], term[0], term[1])
            pw.append(acc)
        if backward:
            pw = pw[::-1]
        return (jnp.concatenate([p[0] for p in pw], axis=0), jnp.concatenate([p[1] for p in pw], axis=0))

    for comp0, backward in ((0, False), (2, True)):
        p_re, p_im = carry_powers(comp0, backward)
        first = epad + 1 if backward else epad - 1
        for i in range(n_tiles):
            c_re = src[comp0, first + i:first + i + 1]
            c_im = src[comp0 + 1, first + i:first + i + 1]
            rows = slice(pad + i * SUBLANES, pad + (i + 1) * SUBLANES)
            buf_b[comp0, rows] = buf_a[comp0, rows] + p_re * c_re - p_im * c_im
            buf_b[comp0 + 1, rows] = buf_a[comp0 + 1, rows] + p_re * c_im + p_im * c_re
    yield

    f0 = pad + n_ctx - 1
    b0 = pad + 1
    f_re, f_im = buf_b[0, f0:f0 + n_lat], buf_b[1, f0:f0 + n_lat]
    b_re, b_im = buf_b[2, b0:b0 + n_lat], buf_b[3, b0:b0 + n_lat]
    left = lax.broadcasted_iota(jnp.int32, (n_lat, PAIR_STATE), 1) < SSM_STATE
    h0 = jnp.concatenate([jnp.where(left, f_re, b_re), jnp.where(left, f_im, b_im)], axis=1)
    h1 = jnp.concatenate([jnp.where(left, b_re, f_re), jnp.where(left, b_im, f_im)], axis=1)
    y_ref[0] = _dot(u_ref[0], m_scr[0]) + _dot(h0.astype(_BF16), wout_ref[0])
    y_ref[1] = _dot(u_ref[1], m_scr[1]) + _dot(h1.astype(_BF16), wout_ref[1])


def _ssm(u_g, uc_g, a_re, a_im, log_dt, b_re, b_im, c_re, c_im, d_row):
    n_groups, n_lat, _ = u_g.shape
    n_ctx = uc_g.shape[1]
    n_rows = n_lat + n_ctx
    n_pairs = n_groups // PAIR
    pps = 4
    gps = pps * PAIR
    buf = pltpu.VMEM((pps, N_COMP, n_rows + 2 * SUBLANES, PAIR_STATE), _F32)
    n_tiles = n_rows // SUBLANES
    ends_rows = _ends_pad(n_tiles) + -(-(n_tiles + _ends_pad(n_tiles)) // SUBLANES) * SUBLANES
    ends = pltpu.VMEM((pps, N_COMP, ends_rows, PAIR_STATE), _F32)
    slots = 2 * PAIR_STATE
    mat = lambda rows, cols: pltpu.VMEM((pps, PAIR, rows, cols), _BF16)
    pair3 = lambda q: (q, 0, 0)
    by_dir = lambda q: (0, q, 0, 0)
    return pl.pallas_call(
        functools.partial(_ssm_kernel, n_lat, n_ctx, pps),
        grid=(n_pairs // pps,),
        in_specs=[pl.BlockSpec((gps, n_lat, CHUNK_W), pair3),
                  pl.BlockSpec((gps, n_ctx, CHUNK_W), pair3),
                  pl.BlockSpec(a_re.shape, lambda q: (0, 0, 0)),
                  pl.BlockSpec(a_im.shape, lambda q: (0, 0, 0)),
                  pl.BlockSpec(memory_space=pltpu.SMEM),
                  pl.BlockSpec((2, gps) + b_re.shape[2:], by_dir),
                  pl.BlockSpec((2, gps) + b_im.shape[2:], by_dir),
                  pl.BlockSpec((2, gps) + c_re.shape[2:], by_dir),
                  pl.BlockSpec((2, gps) + c_im.shape[2:], by_dir),
                  pl.BlockSpec(memory_space=pltpu.SMEM)],
        out_specs=pl.BlockSpec((gps, n_lat, CHUNK_W), pair3),
        out_shape=jax.ShapeDtypeStruct((n_groups, n_lat, CHUNK_W), _F32),
        scratch_shapes=[buf, buf, ends, ends, mat(CHUNK_W, slots), mat(slots, CHUNK_W), mat(CHUNK_W, CHUNK_W)],
        compiler_params=pltpu.CompilerParams(dimension_semantics=("arbitrary",),
                                             vmem_limit_bytes=VMEM_LIMIT),
        name="ssm",
    )(u_g, uc_g, a_re, a_im, log_dt, b_re, b_im, c_re, c_im, d_row)


def _out_kernel(x_ref, conv_ref, yg_ref, mods_ref, gssm_ref, g2_ref, gfin_ref,
                wglu32_ref, wout32_ref, w1_ref, w2_ref, o_ref, y_scr, mix_scr, wglu_ref, wout_ref, ones_ref):
    @pl.when(pl.program_id(0) == 0)
    def _():
        wglu_ref[...] = wglu32_ref[...].astype(_BF16)
        wout_ref[...] = wout32_ref[...].astype(_BF16)
        ones_ref[...] = _group_ones(MXU_W, SSM_GROUP)

    gt1 = mods_ref[0:1, 2 * D_MODEL:3 * D_MODEL]
    sh2 = mods_ref[0:1, 3 * D_MODEL:4 * D_MODEL]
    sc2 = mods_ref[0:1, 4 * D_MODEL:5 * D_MODEL]
    gt2 = mods_ref[0:1, 5 * D_MODEL:6 * D_MODEL]
    th = x_ref.shape[0] // 2
    n_split = 2
    ff = D_FF // n_split

    def mixer(r0):
        rows_ = slice(r0, r0 + th)
        chunks = slice(r0 // CHUNK, (r0 + th) // CHUNK)
        for q in range(N_SLABS):
            for half in range(CHUNK // BLOCKS):
                cols = [yg_ref[q * BLOCKS + g8, chunks, half * LANES:(half + 1) * LANES] for g8 in range(BLOCKS)]
                rows = _lane_block_transpose(cols)
                for t in range(BLOCKS):
                    part, tok = divmod(half * BLOCKS + t, PART)
                    y_scr[q, pl.ds(r0 + _part_row(part, th) + tok, th // CHUNK, stride=PART), :] = rows[t]
        in_order = [r0 + _part_row(part, th) + c * PART for c in range(th // CHUNK) for part in range(CHUNK_PARTS)]
        s = jax.nn.gelu(jnp.concatenate(
            [jnp.concatenate([y_scr[q, p0:p0 + PART] for p0 in in_order], axis=0) for q in range(N_SLABS)], axis=1))
        s = s * jax.nn.sigmoid(_dot(s.astype(_BF16), wglu_ref[...]))
        sq = (s * s).astype(_BF16)
        ssq = jnp.concatenate([_dot(sq[:, k * MXU_W:(k + 1) * MXU_W], ones_ref[...])
                               for k in range(SSM_W // MXU_W)], axis=1)
        sn = s * lax.rsqrt(ssq * (1.0 / SSM_GROUP) + RMS_EPS) * gssm_ref[...]
        mix = mix_scr[rows_] + _dot(sn.astype(_BF16), wout_ref[CONV_W:])
        x_mid = x_ref[rows_, :] + gt1 * mix
        return x_mid, (_rms_rows(x_mid, g2_ref[...] * (1.0 + sc2)) + sh2).astype(_BF16)

    def mlp_chunk(h2, j):
        a = jnp.maximum(_dot(h2, w1_ref[:, j * ff:(j + 1) * ff]), 0.0)
        return _dot((a * a).astype(_BF16), w2_ref[j * ff:(j + 1) * ff])

    def finish(r0, x_mid, acc):
        o_ref[r0:r0 + th, :] = _rms_rows(x_mid + gt2 * acc, gfin_ref[...])

    mix_scr[...] = _dot(conv_ref[...], wout_ref[0:CONV_W])
    x_a, h_a = mixer(0)
    acc_a = mlp_chunk(h_a, 0)
    x_b, h_b = mixer(th)
    for j in range(1, n_split):
        acc_a = acc_a + mlp_chunk(h_a, j)
    acc_b = mlp_chunk(h_b, 0)
    finish(0, x_a, acc_a)
    for j in range(1, n_split):
        acc_b = acc_b + mlp_chunk(h_b, j)
    finish(th, x_b, acc_b)


def _out_proj(x2d, conv_n, y_g, mods, g_ssm, g2, g_fin, wglu, wout, w1, w2, tm):
    n = x2d.shape[0]
    const = lambda i: (0, 0)
    row = lambda i: (i, 0)
    resident = lambda shape: pl.BlockSpec(shape, const, pipeline_mode=pl.Buffered(1))
    return pl.pallas_call(
        _out_kernel,
        grid=(n // tm,),
        in_specs=[pl.BlockSpec((tm, D_MODEL), row),
                  pl.BlockSpec((tm, CONV_W), row),
                  pl.BlockSpec((SSM_GROUPS, tm // CHUNK, CHUNK_W), lambda i: (0, i, 0)),
                  pl.BlockSpec(mods.shape, const),
                  pl.BlockSpec((1, SSM_W), const),
                  pl.BlockSpec((1, D_MODEL), const),
                  pl.BlockSpec((1, D_MODEL), const),
                  resident((SSM_W, SSM_W)),
                  resident((D_MODEL, D_MODEL)),
                  resident((D_MODEL, D_FF)),
                  resident((D_FF, D_MODEL))],
        out_specs=pl.BlockSpec((tm, D_MODEL), row),
        out_shape=jax.ShapeDtypeStruct((n, D_MODEL), _F32),
        scratch_shapes=[pltpu.VMEM((N_SLABS, tm, LANES), _F32), pltpu.VMEM((tm, D_MODEL), _F32),
                        pltpu.VMEM((SSM_W, SSM_W), _BF16), pltpu.VMEM((D_MODEL, D_MODEL), _BF16),
                        pltpu.VMEM((MXU_W, MXU_W), _BF16)],
        compiler_params=pltpu.CompilerParams(dimension_semantics=("arbitrary",),
                                             vmem_limit_bytes=VMEM_LIMIT),
        name="out_proj",
    )(x2d, conv_n, y_g, mods, g_ssm, g2, g_fin, wglu, wout, w1, w2)


def kernel(x, c, ctx, c_ctx, w_mod, b_mod, g_norm1, w_in, conv_w, ssm_a_re, ssm_a_im, ssm_log_dt,
           ssm_b_re, ssm_b_im, ssm_c_re, ssm_c_im, ssm_d, w_glu, g_conv_out, g_ssm_out, w_out,
           g_norm2, w_mlp1, w_mlp2, g_final):
    bsz, n_lat, d_model = x.shape
    n_ctx = ctx.shape[1]
    assert bsz == 1 and d_model == D_MODEL and w_mod.shape[0] == 1
    assert n_lat % (CHUNK * SUBLANES) == 0 and n_ctx % (CHUNK * SUBLANES) == 0 and n_lat % GRID_W == 0
    layer = 0
    x2d = x[0]
    ctx2d = ctx[0]

    mods = _modulation(c, c_ctx[None, :], w_mod[layer], b_mod[layer][None, :])

    g1 = g_norm1[layer][None, :]
    g_conv = g_conv_out[layer][None, :]
    conv_n, u_g, uc_g, w1_bf, w2_bf = _in_proj(x2d, ctx2d, mods, g1, w_in[layer], conv_w[layer][:, None, :], g_conv,
                                                tm=1024, side_weights=(w_mlp1[layer], w_mlp2[layer]))

    y_g = _ssm(u_g, uc_g, ssm_a_re[layer], ssm_a_im[layer], ssm_log_dt[layer],
               jnp.swapaxes(ssm_b_re[layer], -1, -2), jnp.swapaxes(ssm_b_im[layer], -1, -2),
               ssm_c_re[layer], ssm_c_im[layer], ssm_d[layer:layer + 1])

    out = _out_proj(x2d, conv_n, y_g, mods, g_ssm_out[layer][None, :], g_norm2[layer][None, :],
                    g_final[None, :], w_glu[layer], w_out[layer], w1_bf, w2_bf, tm=1024)
    return out[None]
```
